```python
import math
import jax, jax.numpy as jnp
from jax import lax
import numpy as np

D_MODEL = 1024
BATCH = 32
SEQ = 256
DEPTH = 2
DEC_BATCH = 2
DEC_SEQ = 1024
PAST_LEN = 512

GRID_W = 64
HEAD_DIM = 64
N_Q_HEADS = 8
N_KV_HEADS = 2
Q_PER_KV = N_Q_HEADS // N_KV_HEADS
ATTN_W = N_Q_HEADS * HEAD_DIM
KV_W = N_KV_HEADS * HEAD_DIM
Q_BLOCK = 128
ROPE_THETA = 10000.0
ATTN_SCALE = HEAD_DIM ** -0.5
HY_W = 512
HY_SHORT = 3
HY_BANDS = 16
HY_EMB = 2 * HY_BANDS + 1
HY_FH = 64
HY_DECAY_MIN = math.log(100.0) / 1.5
HY_DECAY_MAX = math.log(100.0) / 0.3
LRU_W = 512
LRU_BLOCKS = 8
LRU_BS = LRU_W // LRU_BLOCKS
LRU_CONV = 4
LRU_C = 8.0
D_MIX = ATTN_W + HY_W + LRU_W
PROJ_SIZES = (ATTN_W, KV_W, KV_W, ATTN_W, HY_W, HY_W, HY_W, HY_W, LRU_W, LRU_W)
D_PROJ = sum(PROJ_SIZES)
PROJ_SPLITS = tuple(int(s) for s in np.cumsum(PROJ_SIZES)[:-1])
EPS = 1e-6
F32 = jnp.float32

kernel_name = 'hymba_flow_hybrid_step'


def rmsnorm(x, g):
    xf = x.astype(F32)
    y = xf * lax.rsqrt(jnp.mean(xf * xf, axis=-1, keepdims=True) + EPS)
    return (y * g.astype(F32)).astype(x.dtype)


def dwconv_centred(x, w, b):
    K = w.shape[0]
    L = x.shape[1]
    left = (K - 1) // 2
    xp = jnp.pad(x, ((0, 0), (left, K - 1 - left), (0, 0)))
    y = b
    for j in range(K):
        y = y + xp[:, j:j + L] * w[j]
    return y


def axial_rope(x):
    L = x.shape[1]
    rows = L // GRID_W
    row = jnp.repeat(jnp.arange(rows), GRID_W).astype(F32)
    col = jnp.tile(jnp.arange(GRID_W), rows).astype(F32)
    n_freq = HEAD_DIM // 4
    inv = ROPE_THETA ** (-jnp.arange(n_freq, dtype=F32) / n_freq)
    ang = jnp.concatenate([row[:, None] * inv, col[:, None] * inv], axis=-1)
    cos = jnp.cos(ang)[None, :, None, :]
    sin = jnp.sin(ang)[None, :, None, :]
    xf = x.astype(F32).reshape(x.shape[:-1] + (HEAD_DIM // 2, 2))
    x1, x2 = xf[..., 0], xf[..., 1]
    out = jnp.stack([x1 * cos - x2 * sin, x1 * sin + x2 * cos], axis=-1)
    return out.reshape(x.shape).astype(x.dtype)


def blocked_attention(q, k, v):
    B, Lq, H, Dh = q.shape
    nb = Lq // Q_BLOCK
    qb = q.reshape(B, nb, Q_BLOCK, N_KV_HEADS, Q_PER_KV, Dh).transpose(1, 0, 2, 3, 4, 5)

    def one_block(qblk):
        s = jnp.einsum('bqkgd,bskd->bkgqs', qblk, k).astype(F32) * ATTN_SCALE
        p = jax.nn.softmax(s, axis=-1).astype(v.dtype)
        return jnp.einsum('bkgqs,bskd->bqkgd', p, v)

    o = lax.map(one_block, qb)
    return o.transpose(1, 0, 2, 3, 4, 5).reshape(B, Lq, H * Dh)


def hyena_filter(L, p):
    t = jnp.linspace(0.0, 1.0, L, dtype=F32)[:, None]
    w = 2.0 * math.pi * jnp.arange(L, dtype=F32)[:, None] / L
    f = jnp.linspace(1e-4, HY_BANDS - 1, HY_BANDS, dtype=F32)[None, :]
    feats = jnp.concatenate([t, jnp.cos(f * w), -jnp.sin(f * w)], axis=-1)
    freq = p['hy_filt_freq'].astype(F32)
    hdn = jnp.sin(freq[0] * (feats @ p['hy_filt_w1'].astype(F32) + p['hy_filt_b1'].astype(F32)))
    hdn = jnp.sin(freq[1] * (hdn @ p['hy_filt_w2'].astype(F32) + p['hy_filt_b2'].astype(F32)))
    h = (hdn @ p['hy_filt_w3'].astype(F32)) * jnp.exp(-t * p['hy_filt_decay'].astype(F32))
    h_fwd, h_bwd = h[:, :HY_W], h[:, HY_W:]
    filt = jnp.concatenate([h_fwd, jnp.zeros((1, HY_W), F32), h_bwd[:0:-1]], axis=0)
    return filt / jnp.sum(jnp.abs(filt), axis=0, keepdims=True)


def long_conv(z, filt, bias):
    L = z.shape[1]
    zf = z.astype(F32)
    y = jnp.fft.irfft(jnp.fft.rfft(zf, n=2 * L, axis=1) * jnp.fft.rfft(filt, n=2 * L, axis=0)[None],
                      n=2 * L, axis=1)[:, :L]
    return (y + zf * bias.astype(F32)).astype(z.dtype)


def _lin_combine(e1, e2):
    a1, b1 = e1
    a2, b2 = e2
    return a1 * a2, a2 * b1 + b2


def rg_lru(xc, wa, ba, wx, bx, lam, h0, reverse):
    B, L, W = xc.shape
    xf = xc.astype(F32)
    xb = xf.reshape(B, L, LRU_BLOCKS, LRU_BS)
    r = jax.nn.sigmoid(jnp.einsum('blnc,ncd->blnd', xb, wa.astype(F32)).reshape(B, L, W) + ba.astype(F32))
    i = jax.nn.sigmoid(jnp.einsum('blnc,ncd->blnd', xb, wx.astype(F32)).reshape(B, L, W) + bx.astype(F32))
    log_a = -LRU_C * r * jax.nn.softplus(-lam.astype(F32))
    a = jnp.exp(log_a)
    b = jnp.sqrt(-jnp.expm1(2.0 * log_a)) * (i * xf)
    if reverse:
        a, b = a[:, ::-1], b[:, ::-1]
    b = b.at[:, 0].add(a[:, 0] * h0.astype(F32))
    _, hs = lax.associative_scan(_lin_combine, (a, b), axis=1)
    h_last = hs[:, -1]
    if reverse:
        hs = hs[:, ::-1]
    return hs, h_last


def mixer_layer(x, cvec, p, ctx_k=None, ctx_v=None, ctx_h=None):
    B, L, _ = x.shape
    latent = ctx_k is not None
    mod = jnp.dot(jax.nn.silu(cvec.astype(F32)), p['w_ada'].astype(F32)) + p['b_ada'].astype(F32)
    shift, scale, gate = jnp.split(mod[:, None, :], 3, axis=-1)
    h = (rmsnorm(x, p['norm_g']).astype(F32) * (1.0 + scale) + shift).astype(x.dtype)
    u = jnp.einsum('bld,de->ble', h, p['w_in'])
    q, k, v, g_attn, hy_x0, hy_x1, hy_v, g_hy, lru_x, g_lru = jnp.split(u, PROJ_SPLITS, axis=-1)

    q = rmsnorm(q.reshape(B, L, N_Q_HEADS, HEAD_DIM), p['q_norm_g'])
    k = rmsnorm(k.reshape(B, L, N_KV_HEADS, HEAD_DIM), p['k_norm_g'])
    v = v.reshape(B, L, N_KV_HEADS, HEAD_DIM)
    if latent:
        q = axial_rope(q)
        k_all = jnp.concatenate([ctx_k.astype(k.dtype), axial_rope(k)], axis=1)
        v_all = jnp.concatenate([ctx_v.astype(v.dtype), v], axis=1)
    else:
        k_all, v_all = k, v
    y_attn = blocked_attention(q, k_all, v_all)

    hz = dwconv_centred(jnp.concatenate([hy_x0, hy_x1, hy_v], axis=-1), p['hy_short_w'], p['hy_short_b'])
    x0, x1, hv = jnp.split(hz, 3, axis=-1)
    y_hy = x0 * long_conv(x1 * hv, hyena_filter(L, p), p['hy_bias'])

    xc = dwconv_centred(lru_x, p['lru_conv_w'], p['lru_conv_b'])
    h0 = ctx_h if latent else jnp.zeros((B, 2, LRU_W), F32)
    y_f, h_f = rg_lru(xc, p['lru_wa'][0], p['lru_ba'][0], p['lru_wx'][0], p['lru_bx'][0],
                      p['lru_lambda'][0], h0[:, 0], False)
    y_b, h_b = rg_lru(xc, p['lru_wa'][1], p['lru_ba'][1], p['lru_wx'][1], p['lru_bx'][1],
                      p['lru_lambda'][1], h0[:, 1], True)
    y_lru = (y_f + y_b).astype(x.dtype)

    mix = jnp.concatenate([y_attn * jax.nn.silu(g_attn), y_hy * jax.nn.silu(g_hy),
                           y_lru * jax.nn.silu(g_lru)], axis=-1)
    out = jnp.einsum('ble,ed->bld', mix, p['w_out'])
    x_new = (x.astype(F32) + gate * out.astype(F32)).astype(x.dtype)
    return x_new, k, v, jnp.stack([h_f, h_b], axis=1).astype(x.dtype)


def setup_inputs(seed: int = 0) -> dict:
    key = jax.random.key(seed)
    ks = iter(jax.random.split(key, 48))

    def nrm(shape, s):
        return jax.random.normal(next(ks), shape, F32) * s

    a_c = jax.random.uniform(next(ks), (DEPTH, 2, LRU_W), F32, 0.9, 0.999)
    sig = a_c ** (1.0 / LRU_C)
    return {
        'x_prompt': nrm((BATCH, SEQ, D_MODEL), 1.0),
        'x_sample': nrm((DEC_BATCH, DEC_SEQ, D_MODEL), 1.0),
        'cache_k': nrm((DEC_BATCH, DEPTH, PAST_LEN, N_KV_HEADS, HEAD_DIM), 1.0),
        'cache_v': nrm((DEC_BATCH, DEPTH, PAST_LEN, N_KV_HEADS, HEAD_DIM), 1.0),
        'state_lru': nrm((DEC_BATCH, DEPTH, 2, LRU_W), 0.5),
        'c': nrm((DEC_BATCH, D_MODEL), 1.0),
        'c_ctx': nrm((D_MODEL,), 1.0),
        'norm_g': 1.0 + nrm((DEPTH, D_MODEL), 0.02),
        'w_ada': nrm((DEPTH, D_MODEL, 3 * D_MODEL), D_MODEL ** -0.5),
        'b_ada': nrm((DEPTH, 3 * D_MODEL), 0.02),
        'w_in': nrm((DEPTH, D_MODEL, D_PROJ), D_MODEL ** -0.5),
        'q_norm_g': 1.0 + nrm((DEPTH, HEAD_DIM), 0.02),
        'k_norm_g': 1.0 + nrm((DEPTH, HEAD_DIM), 0.02),
        'hy_short_w': nrm((DEPTH, HY_SHORT, 3 * HY_W), HY_SHORT ** -0.5),
        'hy_short_b': nrm((DEPTH, 3 * HY_W), 0.02),
        'hy_filt_w1': nrm((DEPTH, HY_EMB, HY_FH), HY_EMB ** -0.5),
        'hy_filt_b1': nrm((DEPTH, HY_FH), 0.02),
        'hy_filt_w2': nrm((DEPTH, HY_FH, HY_FH), HY_FH ** -0.5),
        'hy_filt_b2': nrm((DEPTH, HY_FH), 0.02),
        'hy_filt_w3': nrm((DEPTH, HY_FH, 2 * HY_W), HY_FH ** -0.5),
        'hy_filt_freq': 1.0 + nrm((DEPTH, 2, HY_FH), 0.02),
        'hy_filt_decay': jax.random.uniform(next(ks), (DEPTH, 2 * HY_W), F32, HY_DECAY_MIN, HY_DECAY_MAX),
        'hy_bias': nrm((DEPTH, HY_W), 0.1),
        'lru_conv_w': nrm((DEPTH, LRU_CONV, LRU_W), LRU_CONV ** -0.5),
        'lru_conv_b': nrm((DEPTH, LRU_W), 0.02),
        'lru_wa': nrm((DEPTH, 2, LRU_BLOCKS, LRU_BS, LRU_BS), LRU_BS ** -0.5),
        'lru_ba': nrm((DEPTH, 2, LRU_W), 0.02),
        'lru_wx': nrm((DEPTH, 2, LRU_BLOCKS, LRU_BS, LRU_BS), LRU_BS ** -0.5),
        'lru_bx': nrm((DEPTH, 2, LRU_W), 0.02),
        'lru_lambda': jnp.log(sig) - jnp.log1p(-sig),
        'w_out': nrm((DEPTH, D_MIX, D_MODEL), D_MIX ** -0.5),
        'final_g': 1.0 + nrm((D_MODEL,), 0.02),
    }


def reference(x_prompt, x_sample, cache_k, cache_v, state_lru, c, c_ctx, norm_g, w_ada, b_ada, w_in,
              q_norm_g, k_norm_g, hy_short_w, hy_short_b, hy_filt_w1, hy_filt_b1, hy_filt_w2, hy_filt_b2,
              hy_filt_w3, hy_filt_freq, hy_filt_decay, hy_bias, lru_conv_w, lru_conv_b, lru_wa, lru_ba,
              lru_wx, lru_bx, lru_lambda, w_out, final_g):
    def layer_params(l):
        return {
            'norm_g': norm_g[l], 'w_ada': w_ada[l], 'b_ada': b_ada[l], 'w_in': w_in[l],
            'q_norm_g': q_norm_g[l], 'k_norm_g': k_norm_g[l],
            'hy_short_w': hy_short_w[l], 'hy_short_b': hy_short_b[l],
            'hy_filt_w1': hy_filt_w1[l], 'hy_filt_b1': hy_filt_b1[l],
            'hy_filt_w2': hy_filt_w2[l], 'hy_filt_b2': hy_filt_b2[l],
            'hy_filt_w3': hy_filt_w3[l], 'hy_filt_freq': hy_filt_freq[l],
            'hy_filt_decay': hy_filt_decay[l], 'hy_bias': hy_bias[l],
            'lru_conv_w': lru_conv_w[l], 'lru_conv_b': lru_conv_b[l],
            'lru_wa': lru_wa[l], 'lru_ba': lru_ba[l], 'lru_wx': lru_wx[l], 'lru_bx': lru_bx[l],
            'lru_lambda': lru_lambda[l], 'w_out': w_out[l],
        }

    y_p = x_prompt
    k_list, v_list, h_list = [], [], []
    for l in range(DEPTH):
        y_p, k_l, v_l, h_l = mixer_layer(y_p, c_ctx[None, :], layer_params(l))
        k_list.append(k_l)
        v_list.append(v_l)
        h_list.append(h_l)
    y_prompt = rmsnorm(y_p, final_g)
    new_k = jnp.stack(k_list, axis=1)
    new_v = jnp.stack(v_list, axis=1)
    new_lru = jnp.stack(h_list, axis=1)

    y_s = x_sample
    for l in range(DEPTH):
        y_s, _, _, _ = mixer_layer(y_s, c, layer_params(l), cache_k[:, l], cache_v[:, l], state_lru[:, l])
    y_sample = rmsnorm(y_s, final_g)
    return (y_prompt, y_sample, new_k, new_v, new_lru)
```

```python
import functools
import math

import numpy as np
import jax
import jax.numpy as jnp
from jax import lax
from jax.experimental import pallas as pl
from jax.experimental.pallas import tpu as pltpu

F32 = jnp.float32
BF16 = jnp.bfloat16

D_MODEL = 1024
DEPTH = 2
GRID_W = 64
HEAD_DIM = 64
N_Q_HEADS = 8
N_KV_HEADS = 2
Q_PER_KV = N_Q_HEADS // N_KV_HEADS
ATTN_W = N_Q_HEADS * HEAD_DIM
KV_W = N_KV_HEADS * HEAD_DIM
ROPE_THETA = 10000.0
ATTN_SCALE = HEAD_DIM ** -0.5
HY_W = 512
HY_BANDS = 16
HY_EMB = 2 * HY_BANDS + 1
HY_FH = 64
LRU_W = 512
LRU_BLOCKS = 8
LRU_BS = LRU_W // LRU_BLOCKS
LRU_C = 8.0
EPS = 1e-6

ATTN_COLS = ATTN_W + 2 * KV_W + ATTN_W
HY_COL0 = ATTN_COLS
HY_COLS = 4 * HY_W
LRU_COL0 = HY_COL0 + HY_COLS
LRU_COLS = 2 * LRU_W

LANES = 128
MXU_DIM = 256
Q_CHUNK = 256
FEAT_PAD = 128
MOD_ROWS = 8
VMEM_LIMIT = 56 * 1024 * 1024


def _cparams(n_axes):
    return pltpu.CompilerParams(dimension_semantics=("arbitrary",) * n_axes, vmem_limit_bytes=VMEM_LIMIT)


def _split_bf16(a):
    hi = a.astype(BF16)
    lo = (a - hi.astype(F32)).astype(BF16)
    return hi, lo


def _dot(a, b):
    return jnp.dot(a, b, preferred_element_type=F32)


def _dot3(a, b):
    a_hi, a_lo = _split_bf16(a)
    b_hi, b_lo = _split_bf16(b)
    return _dot(a_hi, b_hi) + _dot(a_hi, b_lo) + _dot(a_lo, b_hi)


def _dot3_const(m_hi, m_lo, b):
    b_hi, b_lo = _split_bf16(b)
    return _dot(m_hi, b_hi) + _dot(m_hi, b_lo) + _dot(m_lo, b_hi)


def _silu(x):
    return x * jax.nn.sigmoid(x)


@functools.lru_cache(maxsize=None)
def _dft_tables(L):
    n = 2 * L
    t = np.arange(L)
    kt = (t[:, None] * t[None, :]) % n
    ang = 2.0 * np.pi * kt / n
    cosm, sinm = np.cos(ang), np.sin(ang)
    alt = (-1.0) ** t
    f_s = -sinm
    f_s[0, :] = alt
    fwd = np.concatenate([cosm, f_s], axis=0)
    g_c = 2.0 * cosm.T / n
    g_c[:, 0] = 1.0 / n
    g_s = -2.0 * sinm.T / n
    g_s[:, 0] = alt / n
    inv = np.concatenate([g_c, g_s], axis=1)
    return fwd.astype(np.float32), inv.astype(np.float32)


@functools.lru_cache(maxsize=None)
def _hyena_feats(L):
    t = np.linspace(0.0, 1.0, L)[:, None]
    w = 2.0 * math.pi * np.arange(L)[:, None] / L
    f = np.linspace(1e-4, HY_BANDS - 1, HY_BANDS)[None, :]
    out = np.zeros((L, FEAT_PAD), np.float32)
    out[:, :HY_EMB] = np.concatenate([t, np.cos(f * w), -np.sin(f * w)], axis=-1)
    return out


@functools.lru_cache(maxsize=None)
def _rope_tables(L):
    rows = L // GRID_W
    row = np.repeat(np.arange(rows), GRID_W).astype(np.float64)
    col = np.tile(np.arange(GRID_W), rows).astype(np.float64)
    n_freq = HEAD_DIM // 4
    inv = ROPE_THETA ** (-np.arange(n_freq) / n_freq)
    ang = np.concatenate([row[:, None] * inv, col[:, None] * inv], axis=-1)
    cos = np.repeat(np.cos(ang), 2, axis=-1)
    sin = np.repeat(np.sin(ang), 2, axis=-1) * np.tile(np.array([-1.0, 1.0]), HEAD_DIM // 2)
    reps = LANES // HEAD_DIM
    return np.tile(cos, (1, reps)).astype(np.float32), np.tile(sin, (1, reps)).astype(np.float32)


def _mod_kernel(c_ref, w_ref, b_ref, o_ref):
    s = _silu(c_ref[...])
    o_ref[0] = _dot3(s, w_ref[0]) + b_ref[0]


def _modulation(cvecs, w_ada, b_ada):
    tn = D_MODEL
    return pl.pallas_call(
        _mod_kernel,
        grid=(DEPTH, 3 * D_MODEL // tn),
        in_specs=[
            pl.BlockSpec((MOD_ROWS, D_MODEL), lambda l, j: (0, 0)),
            pl.BlockSpec((1, D_MODEL, tn), lambda l, j: (l, 0, j)),
            pl.BlockSpec((1, 1, tn), lambda l, j: (l, 0, j)),
        ],
        out_specs=pl.BlockSpec((1, MOD_ROWS, tn), lambda l, j: (l, 0, j)),
        out_shape=jax.ShapeDtypeStruct((DEPTH, MOD_ROWS, 3 * D_MODEL), F32),
        compiler_params=_cparams(2),
        name="adaln_mod",
    )(cvecs, w_ada, b_ada.reshape(DEPTH, 1, 3 * D_MODEL))


def _filter_kernel(L, feats_ref, w1_ref, b1_ref, w2_ref, b2_ref, w3_ref, freq_ref, decay_ref,
                   fhi_ref, flo_ref, a_ref, b_ref, d_ref):
    feats = feats_ref[...]
    t = feats[:, 0:1]
    freq = freq_ref[0]
    hdn = jnp.sin(freq[0:1] * (_dot3(feats, w1_ref[0]) + b1_ref[0]))
    hdn = jnp.sin(freq[1:2] * (_dot3(hdn, w2_ref[0]) + b2_ref[0]))
    h = _dot3(hdn, w3_ref[0]) * jnp.exp(-t * decay_ref[0])
    row = lax.broadcasted_iota(jnp.int32, (L, HY_W), 0)
    h_fwd = h[:, :HY_W]
    h_bwd = jnp.where(row >= 1, h[:, HY_W:], 0.0)
    inv_norm = 1.0 / jnp.sum(jnp.abs(h_fwd) + jnp.abs(h_bwd), axis=0, keepdims=True)
    even = h_fwd + h_bwd
    odd = h_fwd - h_bwd
    h_re = _dot3_const(fhi_ref[0:L, :], flo_ref[0:L, :], even) * inv_norm
    h_im = _dot3_const(fhi_ref[L:2 * L, :], flo_ref[L:2 * L, :], odd) * inv_norm
    alt = jnp.where(row % 2 == 0, 1.0, -1.0)
    nyq = jnp.sum(even * alt, axis=0, keepdims=True) * inv_norm
    a_ref[0] = h_re
    b_ref[0] = jnp.where(row >= 1, h_im, 0.0)
    d_ref[0] = jnp.where(row >= 1, h_re, nyq)


def _filter_spectrum(L, feats, f_hi, f_lo, w1p, b1, w2, b2, w3, freq, decay):
    full = lambda shape: pl.BlockSpec(shape, lambda l: (0,) * len(shape))
    per_layer = lambda shape: pl.BlockSpec((1,) + shape, lambda l: (l,) + (0,) * len(shape))
    out = jax.ShapeDtypeStruct((DEPTH, L, HY_W), F32)
    return pl.pallas_call(
        functools.partial(_filter_kernel, L),
        grid=(DEPTH,),
        in_specs=[
            full((L, FEAT_PAD)),
            per_layer((FEAT_PAD, HY_FH)), per_layer((1, HY_FH)),
            per_layer((HY_FH, HY_FH)), per_layer((1, HY_FH)),
            per_layer((HY_FH, 2 * HY_W)), per_layer((2, HY_FH)), per_layer((1, 2 * HY_W)),
            full((2 * L, L)), full((2 * L, L)),
        ],
        out_specs=[per_layer((L, HY_W))] * 3,
        out_shape=[out, out, out],
        compiler_params=_cparams(1),
        name=f"hyena_filter_L{L}",
    )(feats, w1p, b1, w2, b2, w3, freq, decay, f_hi, f_lo)


def _rope(x, cos, sin):
    lane = lax.broadcasted_iota(jnp.int32, x.shape, 1)
    partner = jnp.where(lane % 2 == 0, pltpu.roll(x, LANES - 1, 1), pltpu.roll(x, 1, 1))
    return x * cos + partner * sin


def _head_mean_square(x, e):
    sq = (x * x).astype(BF16)
    width = x.shape[1]
    if width <= MXU_DIM:
        return _dot(sq, e[:width, :width])
    parts = [_dot(sq[:, i:i + MXU_DIM], e) for i in range(0, width, MXU_DIM)]
    return jnp.concatenate(parts, axis=1)


def _attn_kernel(latent, L, *refs):
    if latent:
        (x_ref, mod_ref, ng_ref, w_ref, gq_ref, gk_ref, e_ref, cos_ref, sin_ref, ck_ref, cv_ref,
         h_out, mix_out, k_out, v_out, q_s, k_s, v_s, g_s, o_s, ck_s, cv_s) = refs
    else:
        (x_ref, mod_ref, ng_ref, w_ref, gq_ref, gk_ref, e_ref,
         h_out, mix_out, k_out, v_out, q_s, k_s, v_s, g_s, o_s) = refs

    x = x_ref[0]
    m = mod_ref[0]
    shift, scale = m[0:1], m[1:2]
    ms = jnp.mean(x * x, axis=-1, keepdims=True)
    h = (x * lax.rsqrt(ms + EPS) * ng_ref[...]) * (1.0 + scale) + shift
    hb = h.astype(BF16)
    h_out[0] = hb

    u = _dot(hb, w_ref[...])
    q = u[:, :ATTN_W]
    k = u[:, ATTN_W:ATTN_W + KV_W]
    v = u[:, ATTN_W + KV_W:ATTN_W + 2 * KV_W]
    g_s[...] = u[:, ATTN_W + 2 * KV_W:]
    e = e_ref[...]
    qn = q * lax.rsqrt(_head_mean_square(q, e) + EPS) * gq_ref[...]
    kn = k * lax.rsqrt(_head_mean_square(k, e) + EPS) * gk_ref[...]
    k_out[0] = kn
    v_out[0] = v
    if latent:
        cos, sin = cos_ref[...], sin_ref[...]
        qn = jnp.concatenate([_rope(qn[:, i:i + LANES], cos, sin) for i in range(0, ATTN_W, LANES)], axis=1)
        kn = _rope(kn, cos, sin)
        ck_s[...] = ck_ref[0, 0].astype(BF16)
        cv_s[...] = cv_ref[0, 0].astype(BF16)
    q_s[...] = (qn * ATTN_SCALE).astype(BF16)
    k_s[...] = kn.astype(BF16)
    v_s[...] = v.astype(BF16)

    contract_last = (((1,), (1,)), ((), ()))

    def chunk(c, carry):
        r0 = pl.multiple_of(c * Q_CHUNK, Q_CHUNK)
        for hd in range(N_Q_HEADS):
            j = hd // Q_PER_KV
            hs = slice(hd * HEAD_DIM, (hd + 1) * HEAD_DIM)
            js = slice(j * HEAD_DIM, (j + 1) * HEAD_DIM)
            qh = q_s[pl.ds(r0, Q_CHUNK), hs]
            s = lax.dot_general(qh, k_s[:, js], contract_last, preferred_element_type=F32)
            mx = jnp.max(s, axis=-1, keepdims=True)
            if latent:
                s0 = lax.dot_general(qh, ck_s[:, js], contract_last, preferred_element_type=F32)
                mx = jnp.maximum(mx, jnp.max(s0, axis=-1, keepdims=True))
                p0 = jnp.exp(s0 - mx)
            p = jnp.exp(s - mx)
            den = jnp.sum(p, axis=-1, keepdims=True)
            o = _dot(p.astype(BF16), v_s[:, js])
            if latent:
                den = den + jnp.sum(p0, axis=-1, keepdims=True)
                o = o + _dot(p0.astype(BF16), cv_s[:, js])
            o_s[pl.ds(r0, Q_CHUNK), hs] = o / den
        return carry

    lax.fori_loop(0, L // Q_CHUNK, chunk, 0)
    mix_out[0] = (o_s[...] * _silu(g_s[...])).astype(BF16)


def _attention_branch(latent, layer, x, mod, norm_g, w_attn, gq, gk, e_avg, rope=None, cache=None):
    B, L, _ = x.shape
    full = lambda shape: pl.BlockSpec(shape, lambda b: (0,) * len(shape))
    per_b = lambda shape: pl.BlockSpec((1,) + shape, lambda b: (b,) + (0,) * len(shape))
    mod_spec = per_b((3, D_MODEL)) if latent else pl.BlockSpec((1, 3, D_MODEL), lambda b: (0, 0, 0))
    in_specs = [per_b((L, D_MODEL)), mod_spec, full((1, D_MODEL)), full((D_MODEL, ATTN_COLS)),
                full((1, ATTN_W)), full((1, KV_W)), full((MXU_DIM, MXU_DIM))]
    args = [x, mod, norm_g, w_attn, gq, gk, e_avg]
    scratch = [pltpu.VMEM((L, ATTN_W), BF16), pltpu.VMEM((L, KV_W), BF16), pltpu.VMEM((L, KV_W), BF16),
               pltpu.VMEM((L, ATTN_W), F32), pltpu.VMEM((L, ATTN_W), F32)]
    if latent:
        cache_k, cache_v = cache
        past = cache_k.shape[2]
        cache_spec = pl.BlockSpec((1, 1, past, KV_W), lambda b: (b, layer, 0, 0))
        in_specs += [full((L, LANES)), full((L, LANES)), cache_spec, cache_spec]
        args += [rope[0], rope[1], cache_k, cache_v]
        scratch += [pltpu.VMEM((past, KV_W), BF16), pltpu.VMEM((past, KV_W), BF16)]
    return pl.pallas_call(
        functools.partial(_attn_kernel, latent, L),
        grid=(B,),
        in_specs=in_specs,
        out_specs=[per_b((L, D_MODEL)), per_b((L, ATTN_W)), per_b((L, KV_W)), per_b((L, KV_W))],
        out_shape=[jax.ShapeDtypeStruct((B, L, D_MODEL), BF16), jax.ShapeDtypeStruct((B, L, ATTN_W), BF16),
                   jax.ShapeDtypeStruct((B, L, KV_W), F32), jax.ShapeDtypeStruct((B, L, KV_W), F32)],
        scratch_shapes=scratch,
        compiler_params=_cparams(1),
        name=f"attn_branch_L{L}",
    )(*args)


def _shifted(x, offset):
    L = x.shape[0]
    row = lax.broadcasted_iota(jnp.int32, x.shape, 0)
    rolled = pltpu.roll(x, (-offset) % L, 0)
    valid = (row >= -offset) if offset < 0 else (row < L - offset)
    return jnp.where(valid, rolled, 0.0)


def _hyena_kernel(L, ct, h_ref, w_ref, sw_ref, sb_ref, fa_ref, fb_ref, fd_ref, bias_ref, fwd_ref, inv_ref, mix_out):
    u = _dot(h_ref[0], w_ref[0])
    sw = sw_ref[0]
    sb = sb_ref[0]

    def short_conv(i):
        xs = u[:, i * ct:(i + 1) * ct]
        w = sw[:, i * ct:(i + 1) * ct]
        return sb[:, i * ct:(i + 1) * ct] + _shifted(xs, -1) * w[0:1] + xs * w[1:2] + _shifted(xs, 1) * w[2:3]

    x0, x1, hv = short_conv(0), short_conv(1), short_conv(2)
    gate = u[:, 3 * ct:]
    z = x1 * hv
    zf = _dot(fwd_ref[...], z.astype(BF16))
    re, im = zf[:L], zf[L:]
    fa, fb, fd = fa_ref[0], fb_ref[0], fd_ref[0]
    y_re = re * fa - im * fb
    y_im = re * fb + im * fd
    y = _dot(inv_ref[:, :L], y_re.astype(BF16)) + _dot(inv_ref[:, L:], y_im.astype(BF16))
    y = y + z * bias_ref[0]
    mix_out[0] = (x0 * y * _silu(gate)).astype(BF16)


def _hyena_branch(layer, hmod, w_hy, sw, sb, filt, bias, fwd, inv, ct):
    B, L, _ = hmod.shape
    nt = HY_W // ct
    fa, fb, fd = filt
    filt_spec = pl.BlockSpec((1, L, ct), lambda b, i: (layer, 0, i))
    return pl.pallas_call(
        functools.partial(_hyena_kernel, L, ct),
        grid=(B, nt),
        in_specs=[
            pl.BlockSpec((1, L, D_MODEL), lambda b, i: (b, 0, 0)),
            pl.BlockSpec((1, D_MODEL, 4 * ct), lambda b, i: (i, 0, 0)),
            pl.BlockSpec((1, 3, 3 * ct), lambda b, i: (i, 0, 0)),
            pl.BlockSpec((1, 1, 3 * ct), lambda b, i: (i, 0, 0)),
            filt_spec, filt_spec, filt_spec,
            pl.BlockSpec((1, 1, ct), lambda b, i: (layer, 0, i)),
            pl.BlockSpec((2 * L, L), lambda b, i: (0, 0)),
            pl.BlockSpec((L, 2 * L), lambda b, i: (0, 0)),
        ],
        out_specs=pl.BlockSpec((1, L, ct), lambda b, i: (b, 0, i)),
        out_shape=jax.ShapeDtypeStruct((B, L, HY_W), BF16),
        compiler_params=_cparams(2),
        name=f"hyena_branch_L{L}",
    )(hmod, w_hy, sw, sb, fa, fb, fd, bias, fwd, inv)


def _lru_kernel(L, h_ref, w_ref, cw_ref, cb_ref, wg_ref, bg_ref, lam_ref, h0_ref, mix_out, hl_out, a_s, b_s, g_s):
    u = _dot(h_ref[0], w_ref[...])
    xs = u[:, :LRU_W]
    g_s[...] = u[:, LRU_W:]
    cw = cw_ref[...]
    xc = (cb_ref[...] + _shifted(xs, -1) * cw[0:1] + xs * cw[1:2] + _shifted(xs, 1) * cw[2:3]
          + _shifted(xs, 2) * cw[3:4])
    xcb = xc.astype(BF16)
    sp = jax.nn.softplus(-lam_ref[...])
    half_w = LRU_W // 2
    for half in range(2):
        cols = slice(half * half_w, (half + 1) * half_w)
        gates = _dot(xcb[:, cols], wg_ref[half]) + bg_ref[half]
        xh = xc[:, cols]
        for d in range(2):
            r = jax.nn.sigmoid(gates[:, (2 * d) * half_w:(2 * d + 1) * half_w])
            i = jax.nn.sigmoid(gates[:, (2 * d + 1) * half_w:(2 * d + 2) * half_w])
            log_a = -LRU_C * r * sp[d:d + 1, cols]
            a = jnp.exp(log_a)
            a_s[d, :, cols] = a
            b_s[d, :, cols] = jnp.sqrt(-jnp.tanh(log_a) * (a * a + 1.0)) * (i * xh)

    h0 = h0_ref[0]

    def step(t, carry):
        hf, hb = carry
        hf = a_s[0, pl.ds(t, 1), :] * hf + b_s[0, pl.ds(t, 1), :]
        b_s[0, pl.ds(t, 1), :] = hf
        tb = L - 1 - t
        hb = a_s[1, pl.ds(tb, 1), :] * hb + b_s[1, pl.ds(tb, 1), :]
        b_s[1, pl.ds(tb, 1), :] = hb
        return hf, hb

    hf, hb = lax.fori_loop(0, L, step, (h0[0:1], h0[1:2]), unroll=8)
    hl_out[0, 0:1, :] = hf
    hl_out[0, 1:2, :] = hb
    mix_out[0] = ((b_s[0] + b_s[1]) * _silu(g_s[...])).astype(BF16)


def _lru_branch(hmod, w_lru, cw, cb, wg, bg, lam, h0):
    B, L, _ = hmod.shape
    full = lambda shape: pl.BlockSpec(shape, lambda b: (0,) * len(shape))
    per_b = lambda shape: pl.BlockSpec((1,) + shape, lambda b: (b,) + (0,) * len(shape))
    return pl.pallas_call(
        functools.partial(_lru_kernel, L),
        grid=(B,),
        in_specs=[per_b((L, D_MODEL)), full((D_MODEL, LRU_COLS)), full((4, LRU_W)), full((1, LRU_W)),
                  full((2, LRU_W // 2, 2 * LRU_W)), full((2, 1, 2 * LRU_W)), full((2, LRU_W)), per_b((2, LRU_W))],
        out_specs=[per_b((L, LRU_W)), per_b((2, LRU_W))],
        out_shape=[jax.ShapeDtypeStruct((B, L, LRU_W), BF16), jax.ShapeDtypeStruct((B, 2, LRU_W), F32)],
        scratch_shapes=[pltpu.VMEM((2, L, LRU_W), F32), pltpu.VMEM((2, L, LRU_W), F32), pltpu.VMEM((L, LRU_W), F32)],
        compiler_params=_cparams(1),
        name=f"lru_branch_L{L}",
    )(hmod, w_lru, cw, cb, wg, bg, lam, h0)


def _out_kernel(final, x_ref, ma_ref, mh_ref, ml_ref, w_ref, mod_ref, fg_ref, o_ref):
    out = (_dot(ma_ref[0], w_ref[0:ATTN_W, :]) + _dot(mh_ref[0], w_ref[ATTN_W:ATTN_W + HY_W, :])
           + _dot(ml_ref[0], w_ref[ATTN_W + HY_W:, :]))
    gate = mod_ref[0][2:3]
    xn = x_ref[0] + gate * out
    if final:
        ms = jnp.mean(xn * xn, axis=-1, keepdims=True)
        xn = xn * lax.rsqrt(ms + EPS) * fg_ref[...]
    o_ref[0] = xn


def _out_projection(final, latent, x, mix_a, mix_h, mix_l, w_out, mod, final_g):
    B, L, _ = x.shape
    tm = min(L, 512)
    tile = lambda w: pl.BlockSpec((1, tm, w), lambda b, i: (b, i, 0))
    mod_spec = pl.BlockSpec((1, 3, D_MODEL), (lambda b, i: (b, 0, 0)) if latent else (lambda b, i: (0, 0, 0)))
    return pl.pallas_call(
        functools.partial(_out_kernel, final),
        grid=(B, L // tm),
        in_specs=[tile(D_MODEL), tile(ATTN_W), tile(HY_W), tile(LRU_W),
                  pl.BlockSpec((ATTN_W + HY_W + LRU_W, D_MODEL), lambda b, i: (0, 0)),
                  mod_spec, pl.BlockSpec((1, D_MODEL), lambda b, i: (0, 0))],
        out_specs=tile(D_MODEL),
        out_shape=jax.ShapeDtypeStruct((B, L, D_MODEL), F32),
        compiler_params=_cparams(2),
        name=f"out_proj_L{L}",
    )(x, mix_a, mix_h, mix_l, w_out, mod, final_g)


def _block_diag(blocks):
    n, s, _ = blocks.shape
    eye = jnp.eye(n, dtype=blocks.dtype)
    return jnp.einsum('ncd,nm->ncmd', blocks, eye).reshape(n * s, n * s)


def _lru_gate_weights(wa, ba, wx, bx):
    half_w = LRU_W // 2
    mats = [_block_diag(m) for m in (wa[0], wx[0], wa[1], wx[1])]
    biases = (ba[0], bx[0], ba[1], bx[1])
    wg, bg = [], []
    for half in range(2):
        cols = slice(half * half_w, (half + 1) * half_w)
        wg.append(jnp.concatenate([m[cols, cols] for m in mats], axis=1))
        bg.append(jnp.concatenate([b[cols] for b in biases], axis=0)[None, :])
    return jnp.stack(wg).astype(BF16), jnp.stack(bg)


def _mixer_pass(latent, x, mod, layer_weights, filt, dft, rope, final_g, ct, cache=None, state=None):
    B, L, _ = x.shape
    fwd, inv = dft
    ks, vs, hs = [], [], []
    for l, lw in enumerate(layer_weights):
        mod_l = mod[l]
        hmod, mix_a, k_l, v_l = _attention_branch(latent, l, x, mod_l, lw['norm_g'], lw['w_attn'], lw['gq'], lw['gk'],
                                                  lw['e_avg'], rope, cache)
        mix_h = _hyena_branch(l, hmod, lw['w_hy'][ct], lw['sw'][ct], lw['sb'][ct], filt, lw['hy_bias'], fwd, inv, ct)
        h0 = state[:, l] if latent else jnp.zeros((B, 2, LRU_W), F32)
        mix_l, h_l = _lru_branch(hmod, lw['w_lru'], lw['cw'], lw['cb'], lw['wg'], lw['bg'], lw['lam'], h0)
        x = _out_projection(l == DEPTH - 1, latent, x, mix_a, mix_h, mix_l, lw['w_out'], mod_l, final_g)
        ks.append(k_l)
        vs.append(v_l)
        hs.append(h_l)
    return x, ks, vs, hs


def kernel(x_prompt, x_sample, cache_k, cache_v, state_lru, c, c_ctx, norm_g, w_ada, b_ada, w_in, q_norm_g, k_norm_g,
           hy_short_w, hy_short_b, hy_filt_w1, hy_filt_b1, hy_filt_w2, hy_filt_b2, hy_filt_w3, hy_filt_freq,
           hy_filt_decay, hy_bias, lru_conv_w, lru_conv_b, lru_wa, lru_ba, lru_wx, lru_bx, lru_lambda, w_out, final_g):
    batch, seq, _ = x_prompt.shape
    dec_batch, dec_seq, _ = x_sample.shape
    past = cache_k.shape[2]
    ct_of = {seq: HY_W, dec_seq: MXU_DIM}

    cvecs = jnp.zeros((MOD_ROWS, D_MODEL), F32).at[0].set(c_ctx).at[1:1 + dec_batch].set(c)
    mod = _modulation(cvecs, w_ada, b_ada).reshape(DEPTH, MOD_ROWS, 3, D_MODEL)
    mod_ctx = mod[:, 0:1]
    mod_lat = mod[:, 1:1 + dec_batch]

    w1p = jnp.zeros((DEPTH, FEAT_PAD, HY_FH), F32).at[:, :HY_EMB].set(hy_filt_w1)
    dft, filt = {}, {}
    for L in (seq, dec_seq):
        fwd, inv = (jnp.asarray(m) for m in _dft_tables(L))
        f_hi, f_lo = _split_bf16(fwd)
        dft[L] = (f_hi, inv.astype(BF16))
        filt[L] = _filter_spectrum(L, jnp.asarray(_hyena_feats(L)), f_hi, f_lo, w1p, hy_filt_b1[:, None, :],
                                   hy_filt_w2, hy_filt_b2[:, None, :], hy_filt_w3, hy_filt_freq,
                                   hy_filt_decay[:, None, :])

    e_avg = jnp.asarray(np.kron(np.eye(MXU_DIM // HEAD_DIM), np.full((HEAD_DIM, HEAD_DIM), 1.0 / HEAD_DIM)), BF16)
    w_in_b = w_in.astype(BF16)
    w_out_b = w_out.astype(BF16)
    layer_weights = []
    for l in range(DEPTH):
        w_hy_cols = w_in_b[l][:, HY_COL0:HY_COL0 + HY_COLS]
        w_hy, sw, sb = {}, {}, {}
        for ct in set(ct_of.values()):
            nt = HY_W // ct
            w_hy[ct] = w_hy_cols.reshape(D_MODEL, 4, nt, ct).transpose(2, 0, 1, 3).reshape(nt, D_MODEL, 4 * ct)
            sw[ct] = hy_short_w[l].reshape(3, 3, nt, ct).transpose(2, 0, 1, 3).reshape(nt, 3, 3 * ct)
            sb[ct] = hy_short_b[l].reshape(3, nt, ct).transpose(1, 0, 2).reshape(nt, 1, 3 * ct)
        wg, bg = _lru_gate_weights(lru_wa[l], lru_ba[l], lru_wx[l], lru_bx[l])
        layer_weights.append({
            'norm_g': norm_g[l][None, :],
            'w_attn': w_in_b[l][:, :ATTN_COLS],
            'gq': jnp.tile(q_norm_g[l], N_Q_HEADS)[None, :],
            'gk': jnp.tile(k_norm_g[l], N_KV_HEADS)[None, :],
            'e_avg': e_avg,
            'w_hy': w_hy, 'sw': sw, 'sb': sb,
            'hy_bias': hy_bias[:, None, :],
            'w_lru': w_in_b[l][:, LRU_COL0:LRU_COL0 + LRU_COLS],
            'cw': lru_conv_w[l], 'cb': lru_conv_b[l][None, :],
            'wg': wg, 'bg': bg, 'lam': lru_lambda[l],
            'w_out': w_out_b[l],
        })
    fg = final_g[None, :]

    y_prompt, ks, vs, hs = _mixer_pass(False, x_prompt, mod_ctx, layer_weights, filt[seq], dft[seq], None, fg,
                                       ct_of[seq])
    new_k = jnp.stack(ks, axis=1).reshape(batch, DEPTH, seq, N_KV_HEADS, HEAD_DIM)
    new_v = jnp.stack(vs, axis=1).reshape(batch, DEPTH, seq, N_KV_HEADS, HEAD_DIM)
    new_lru = jnp.stack(hs, axis=1)

    rope = tuple(jnp.asarray(t) for t in _rope_tables(dec_seq))
    cache = (cache_k.reshape(dec_batch, DEPTH, past, KV_W), cache_v.reshape(dec_batch, DEPTH, past, KV_W))
    y_sample, _, _, _ = _mixer_pass(True, x_sample, mod_lat, layer_weights, filt[dec_seq], dft[dec_seq], rope, fg,
                                    ct_of[dec_seq], cache, state_lru)
    return (y_prompt, y_sample, new_k, new_v, new_lru)
```

```python
import functools
import math

import numpy as np
import jax
import jax.numpy as jnp
from jax import lax
from jax.experimental import pallas as pl
from jax.experimental.pallas import tpu as pltpu

F32 = jnp.float32
BF16 = jnp.bfloat16

D_MODEL = 1024
DEPTH = 2
GRID_W = 64
HEAD_DIM = 64
N_Q_HEADS = 8
N_KV_HEADS = 2
Q_PER_KV = N_Q_HEADS // N_KV_HEADS
ATTN_W = N_Q_HEADS * HEAD_DIM
KV_W = N_KV_HEADS * HEAD_DIM
ROPE_THETA = 10000.0
ATTN_SCALE = HEAD_DIM ** -0.5
HY_W = 512
HY_BANDS = 16
HY_EMB = 2 * HY_BANDS + 1
HY_FH = 64
LRU_W = 512
LRU_BLOCKS = 8
LRU_BS = LRU_W // LRU_BLOCKS
LRU_C = 8.0
EPS = 1e-6

ATTN_COLS = ATTN_W + 2 * KV_W + ATTN_W
D_MIX = ATTN_W + HY_W + LRU_W

LANES = 128
SUBLANES = 8
MXU_DIM = 256
COL_BLOCK = MXU_DIM
HY_BLOCK0 = ATTN_COLS // COL_BLOCK
LRU_BLOCK0 = (ATTN_COLS + 4 * HY_W) // COL_BLOCK
Q_CHUNK = 256
OUT_ROWS = 512
FEAT_PAD = 128
MOD_ROWS = 8
VMEM_LIMIT = 56 * 1024 * 1024


def _cparams(n_axes):
    return pltpu.CompilerParams(dimension_semantics=("arbitrary",) * n_axes, vmem_limit_bytes=VMEM_LIMIT)


def _split_bf16(a):
    hi = a.astype(BF16)
    lo = (a - hi.astype(F32)).astype(BF16)
    return hi, lo


def _dot(a, b):
    return jnp.dot(a, b, preferred_element_type=F32)


def _dot3(a, b):
    a_hi, a_lo = _split_bf16(a)
    b_hi, b_lo = _split_bf16(b)
    return _dot(a_hi, b_hi) + _dot(a_hi, b_lo) + _dot(a_lo, b_hi)


def _dot3_const(m_hi, m_lo, b):
    b_hi, b_lo = _split_bf16(b)
    return _dot(m_hi, b_hi) + _dot(m_hi, b_lo) + _dot(m_lo, b_hi)


def _silu(x):
    return x * jax.nn.sigmoid(x)


@functools.lru_cache(maxsize=None)
def _dft_tables(L):
    n = 2 * L
    t = np.arange(L)
    kt = (t[:, None] * t[None, :]) % n
    ang = 2.0 * np.pi * kt / n
    cosm, sinm = np.cos(ang), np.sin(ang)
    alt = (-1.0) ** t
    f_s = -sinm
    f_s[0, :] = alt
    fwd = np.concatenate([cosm, f_s], axis=0)
    g_c = 2.0 * cosm.T / n
    g_c[:, 0] = 1.0 / n
    g_s = -2.0 * sinm.T / n
    g_s[:, 0] = alt / n
    inv = np.concatenate([g_c, g_s], axis=1)
    return fwd.astype(np.float32), inv.astype(np.float32)


@functools.lru_cache(maxsize=None)
def _hyena_feats(L):
    t = np.linspace(0.0, 1.0, L)[:, None]
    w = 2.0 * math.pi * np.arange(L)[:, None] / L
    f = np.linspace(1e-4, HY_BANDS - 1, HY_BANDS)[None, :]
    out = np.zeros((L, FEAT_PAD), np.float32)
    out[:, :HY_EMB] = np.concatenate([t, np.cos(f * w), -np.sin(f * w)], axis=-1)
    return out


@functools.lru_cache(maxsize=None)
def _rope_tables(L):
    rows = L // GRID_W
    row = np.repeat(np.arange(rows), GRID_W).astype(np.float64)
    col = np.tile(np.arange(GRID_W), rows).astype(np.float64)
    n_freq = HEAD_DIM // 4
    inv = ROPE_THETA ** (-np.arange(n_freq) / n_freq)
    ang = np.concatenate([row[:, None] * inv, col[:, None] * inv], axis=-1)
    cos = np.repeat(np.cos(ang), 2, axis=-1)
    sin = np.repeat(np.sin(ang), 2, axis=-1) * np.tile(np.array([-1.0, 1.0]), HEAD_DIM // 2)
    reps = LANES // HEAD_DIM
    return np.tile(cos, (1, reps)).astype(np.float32), np.tile(sin, (1, reps)).astype(np.float32)


@functools.lru_cache(maxsize=None)
def _head_average_matrix():
    return np.kron(np.eye(MXU_DIM // HEAD_DIM), np.full((HEAD_DIM, HEAD_DIM), 1.0 / HEAD_DIM)).astype(np.float32)


def _mod_kernel(c_ref, w_ref, b_ref, o_ref):
    s = _silu(c_ref[...])
    o_ref[0] = _dot3(s, w_ref[0]) + b_ref[0]


def _modulation(cvecs, w_ada, b_ada):
    tn = D_MODEL
    return pl.pallas_call(
        _mod_kernel,
        grid=(DEPTH, 3 * D_MODEL // tn),
        in_specs=[
            pl.BlockSpec((MOD_ROWS, D_MODEL), lambda l, j: (0, 0)),
            pl.BlockSpec((1, D_MODEL, tn), lambda l, j: (l, 0, j)),
            pl.BlockSpec((1, 1, tn), lambda l, j: (l, 0, j)),
        ],
        out_specs=pl.BlockSpec((1, MOD_ROWS, tn), lambda l, j: (l, 0, j)),
        out_shape=jax.ShapeDtypeStruct((DEPTH, MOD_ROWS, 3 * D_MODEL), F32),
        compiler_params=_cparams(2),
        name="adaln_mod",
    )(cvecs, w_ada, b_ada.reshape(DEPTH, 1, 3 * D_MODEL))


def _filter_kernel(L, feats_ref, w1_ref, b1_ref, w2_ref, b2_ref, w3_ref, freq_ref, decay_ref,
                   fhi_ref, flo_ref, a_ref, b_ref, d_ref):
    feats = feats_ref[...]
    t = feats[:, 0:1]
    freq = freq_ref[0]
    hdn = jnp.sin(freq[0:1] * (_dot3(feats, w1_ref[0]) + b1_ref[0]))
    hdn = jnp.sin(freq[1:2] * (_dot3(hdn, w2_ref[0]) + b2_ref[0]))
    h = _dot3(hdn, w3_ref[0]) * jnp.exp(-t * decay_ref[0])
    row = lax.broadcasted_iota(jnp.int32, (L, HY_W), 0)
    h_fwd = h[:, :HY_W]
    h_bwd = jnp.where(row >= 1, h[:, HY_W:], 0.0)
    inv_norm = 1.0 / jnp.sum(jnp.abs(h_fwd) + jnp.abs(h_bwd), axis=0, keepdims=True)
    even = h_fwd + h_bwd
    odd = h_fwd - h_bwd
    h_re = _dot3_const(fhi_ref[0:L, :], flo_ref[0:L, :], even) * inv_norm
    h_im = _dot3_const(fhi_ref[L:2 * L, :], flo_ref[L:2 * L, :], odd) * inv_norm
    alt = jnp.where(row % 2 == 0, 1.0, -1.0)
    nyq = jnp.sum(even * alt, axis=0, keepdims=True) * inv_norm
    a_ref[0] = h_re
    b_ref[0] = jnp.where(row >= 1, h_im, 0.0)
    d_ref[0] = jnp.where(row >= 1, h_re, nyq)


def _filter_spectrum(L, feats, f_hi, f_lo, w1p, b1, w2, b2, w3, freq, decay):
    full = lambda shape: pl.BlockSpec(shape, lambda l: (0,) * len(shape))
    per_layer = lambda shape: pl.BlockSpec((1,) + shape, lambda l: (l,) + (0,) * len(shape))
    out = jax.ShapeDtypeStruct((DEPTH, L, HY_W), F32)
    return pl.pallas_call(
        functools.partial(_filter_kernel, L),
        grid=(DEPTH,),
        in_specs=[
            full((L, FEAT_PAD)),
            per_layer((FEAT_PAD, HY_FH)), per_layer((1, HY_FH)),
            per_layer((HY_FH, HY_FH)), per_layer((1, HY_FH)),
            per_layer((HY_FH, 2 * HY_W)), per_layer((2, HY_FH)), per_layer((1, 2 * HY_W)),
            full((2 * L, L)), full((2 * L, L)),
        ],
        out_specs=[per_layer((L, HY_W))] * 3,
        out_shape=[out, out, out],
        compiler_params=_cparams(1),
        name=f"hyena_filter_L{L}",
    )(feats, w1p, b1, w2, b2, w3, freq, decay, f_hi, f_lo)


def _rope(x, cos, sin):
    lane = lax.broadcasted_iota(jnp.int32, x.shape, 1)
    partner = jnp.where(lane % 2 == 0, pltpu.roll(x, LANES - 1, 1), pltpu.roll(x, 1, 1))
    return x * cos + partner * sin


def _head_mean_square(x, e):
    sq = (x * x).astype(BF16)
    width = x.shape[1]
    if width <= MXU_DIM:
        return _dot(sq, e[:width, :width])
    parts = [_dot(sq[:, i:i + MXU_DIM], e) for i in range(0, width, MXU_DIM)]
    return jnp.concatenate(parts, axis=1)


def _attn_kernel(latent, aliased, layer, L, *refs):
    x_ref, mod_ref, ng_ref, w_ref, gq_ref, gk_ref, e_ref = refs[:7]
    refs = refs[7:]
    if latent:
        cos_ref, sin_ref, ck_ref, cv_ref, h_out, mix_out, q_s, k_s, v_s, g_s, o_s, ck_s, cv_s = refs
    else:
        if aliased:
            refs = refs[2:]
        h_out, mix_out, k_out, v_out, q_s, k_s, v_s, g_s, o_s = refs

    x = x_ref[0]
    m = mod_ref[0, 0]
    shift, scale = m[0:1], m[1:2]
    ms = jnp.mean(x * x, axis=-1, keepdims=True)
    h = (x * lax.rsqrt(ms + EPS) * ng_ref[0]) * (1.0 + scale) + shift
    hb = h.astype(BF16)
    h_out[0] = hb

    u = _dot(hb, w_ref[0])
    q = u[:, :ATTN_W]
    k = u[:, ATTN_W:ATTN_W + KV_W]
    v = u[:, ATTN_W + KV_W:ATTN_W + 2 * KV_W]
    g_s[...] = u[:, ATTN_W + 2 * KV_W:]
    e = e_ref[...]
    qn = q * lax.rsqrt(_head_mean_square(q, e) + EPS) * gq_ref[0]
    kn = k * lax.rsqrt(_head_mean_square(k, e) + EPS) * gk_ref[0]
    if latent:
        cos, sin = cos_ref[...], sin_ref[...]
        qn = jnp.concatenate([_rope(qn[:, i:i + LANES], cos, sin) for i in range(0, ATTN_W, LANES)], axis=1)
        kn = _rope(kn, cos, sin)
        ck_s[...] = ck_ref[0, 0].astype(BF16)
        cv_s[...] = cv_ref[0, 0].astype(BF16)
    elif aliased:
        k_out[0, 0] = kn
        v_out[0, 0] = v
    else:
        for slot in range(DEPTH):
            k_out[0, slot] = kn if slot == layer else jnp.zeros_like(kn)
            v_out[0, slot] = v if slot == layer else jnp.zeros_like(v)
    q_s[...] = (qn * ATTN_SCALE).astype(BF16)
    k_s[...] = kn.astype(BF16)
    v_s[...] = v.astype(BF16)

    contract_last = (((1,), (1,)), ((), ()))

    def chunk(c, carry):
        r0 = pl.multiple_of(c * Q_CHUNK, Q_CHUNK)
        for hd in range(N_Q_HEADS):
            j = hd // Q_PER_KV
            hs = slice(hd * HEAD_DIM, (hd + 1) * HEAD_DIM)
            js = slice(j * HEAD_DIM, (j + 1) * HEAD_DIM)
            qh = q_s[pl.ds(r0, Q_CHUNK), hs]
            s = lax.dot_general(qh, k_s[:, js], contract_last, preferred_element_type=F32)
            mx = jnp.max(s, axis=-1, keepdims=True)
            if latent:
                s0 = lax.dot_general(qh, ck_s[:, js], contract_last, preferred_element_type=F32)
                mx = jnp.maximum(mx, jnp.max(s0, axis=-1, keepdims=True))
                p0 = jnp.exp(s0 - mx)
            p = jnp.exp(s - mx)
            den = jnp.sum(p, axis=-1, keepdims=True)
            o = _dot(p.astype(BF16), v_s[:, js])
            if latent:
                den = den + jnp.sum(p0, axis=-1, keepdims=True)
                o = o + _dot(p0.astype(BF16), cv_s[:, js])
            o_s[pl.ds(r0, Q_CHUNK), hs] = o / den
        return carry

    lax.fori_loop(0, L // Q_CHUNK, chunk, 0)
    mix_out[0] = (o_s[...] * _silu(g_s[...])).astype(BF16)


def _attention_branch(latent, layer, x, mod, norm_g, w_in_b, gq, gk, e_avg, rope=None, cache=None, kv_prev=None):
    B, L, _ = x.shape
    full = lambda shape: pl.BlockSpec(shape, lambda b: (0,) * len(shape))
    per_b = lambda shape: pl.BlockSpec((1,) + shape, lambda b: (b,) + (0,) * len(shape))
    per_layer = lambda shape: pl.BlockSpec((1,) + shape, lambda b: (layer,) + (0,) * len(shape))
    mod_spec = pl.BlockSpec((1, 1, 3, D_MODEL), (lambda b: (layer, 1 + b, 0, 0)) if latent else (lambda b: (layer, 0, 0, 0)))
    in_specs = [per_b((L, D_MODEL)), mod_spec, per_layer((1, D_MODEL)), per_layer((D_MODEL, ATTN_COLS)),
                per_layer((1, ATTN_W)), per_layer((1, KV_W)), full((MXU_DIM, MXU_DIM))]
    args = [x, mod, norm_g, w_in_b, gq, gk, e_avg]
    out_specs = [per_b((L, D_MODEL)), per_b((L, ATTN_W))]
    out_shape = [jax.ShapeDtypeStruct((B, L, D_MODEL), BF16), jax.ShapeDtypeStruct((B, L, ATTN_W), BF16)]
    scratch = [pltpu.VMEM((L, ATTN_W), BF16), pltpu.VMEM((L, KV_W), BF16), pltpu.VMEM((L, KV_W), BF16),
               pltpu.VMEM((L, ATTN_W), F32), pltpu.VMEM((L, ATTN_W), F32)]
    aliases = {}
    if latent:
        cache_k, cache_v = cache
        past = cache_k.shape[2]
        cache_spec = pl.BlockSpec((1, 1, past, KV_W), lambda b: (b, layer, 0, 0))
        in_specs += [full((L, LANES)), full((L, LANES)), cache_spec, cache_spec]
        args += [rope[0], rope[1], cache_k, cache_v]
        scratch += [pltpu.VMEM((past, KV_W), BF16), pltpu.VMEM((past, KV_W), BF16)]
    else:
        if kv_prev is not None:
            kv_spec = pl.BlockSpec((1, 1, L, KV_W), lambda b: (b, layer, 0, 0))
        else:
            kv_spec = pl.BlockSpec((1, DEPTH, L, KV_W), lambda b: (b, 0, 0, 0))
        kv_shape = jax.ShapeDtypeStruct((B, DEPTH, L, KV_W), F32)
        out_specs += [kv_spec, kv_spec]
        out_shape += [kv_shape, kv_shape]
        if kv_prev is not None:
            in_specs += [pl.BlockSpec(memory_space=pl.ANY)] * 2
            aliases = {len(args): 2, len(args) + 1: 3}
            args += list(kv_prev)
    return pl.pallas_call(
        functools.partial(_attn_kernel, latent, kv_prev is not None, layer, L),
        grid=(B,),
        in_specs=in_specs,
        out_specs=out_specs,
        out_shape=out_shape,
        scratch_shapes=scratch,
        input_output_aliases=aliases,
        compiler_params=_cparams(1),
        name=f"attn_branch_L{L}",
    )(*args)


def _shifted(x, offset, period):
    rows = x.shape[0]
    t = lax.broadcasted_iota(jnp.int32, x.shape, 0)
    if rows != period:
        t = t % period
    rolled = pltpu.roll(x, (-offset) % rows, 0)
    valid = (t >= -offset) if offset < 0 else (t < period - offset)
    return jnp.where(valid, rolled, 0.0)


def _hyena_kernel(L, h_ref, w0_ref, w1_ref, w2_ref, wg_ref, sw0_ref, sw1_ref, sw2_ref, sb0_ref, sb1_ref, sb2_ref,
                  fa_ref, fb_ref, fd_ref, bias_ref, fwd_ref, inv_ref, mix_out):
    hb = h_ref[0]

    def short_conv(w_ref, sw_ref, sb_ref):
        xs = _dot(hb, w_ref[0])
        w = sw_ref[0]
        return sb_ref[0] + _shifted(xs, -1, L) * w[0:1] + xs * w[1:2] + _shifted(xs, 1, L) * w[2:3]

    x0 = short_conv(w0_ref, sw0_ref, sb0_ref)
    x1 = short_conv(w1_ref, sw1_ref, sb1_ref)
    hv = short_conv(w2_ref, sw2_ref, sb2_ref)
    gate = _dot(hb, wg_ref[0])
    z = x1 * hv
    zf = _dot(fwd_ref[...], z.astype(BF16))
    re, im = zf[:L], zf[L:]
    fa, fb, fd = fa_ref[0], fb_ref[0], fd_ref[0]
    y_re = re * fa - im * fb
    y_im = re * fb + im * fd
    y = _dot(inv_ref[:, :L], y_re.astype(BF16)) + _dot(inv_ref[:, L:], y_im.astype(BF16))
    y = y + z * bias_ref[0]
    mix_out[0] = (x0 * y * _silu(gate)).astype(BF16)


def _hyena_branch(layer, hmod, w_in_b, short_w, short_b, filt, bias, fwd, inv):
    B, L, _ = hmod.shape
    cb = COL_BLOCK
    nt = HY_W // cb
    fa, fb, fd = filt
    w_spec = lambda part: pl.BlockSpec((1, D_MODEL, cb), lambda i, b: (layer, 0, HY_BLOCK0 + part * nt + i))
    sw_spec = lambda part: pl.BlockSpec((1, 3, cb), lambda i, b: (layer, 0, part * nt + i))
    sb_spec = lambda part: pl.BlockSpec((1, 1, cb), lambda i, b: (layer, 0, part * nt + i))
    filt_spec = pl.BlockSpec((1, L, cb), lambda i, b: (layer, 0, i))
    return pl.pallas_call(
        functools.partial(_hyena_kernel, L),
        grid=(nt, B),
        in_specs=[
            pl.BlockSpec((1, L, D_MODEL), lambda i, b: (b, 0, 0)),
            w_spec(0), w_spec(1), w_spec(2), w_spec(3),
            sw_spec(0), sw_spec(1), sw_spec(2),
            sb_spec(0), sb_spec(1), sb_spec(2),
            filt_spec, filt_spec, filt_spec,
            pl.BlockSpec((1, 1, cb), lambda i, b: (layer, 0, i)),
            pl.BlockSpec((2 * L, L), lambda i, b: (0, 0)),
            pl.BlockSpec((L, 2 * L), lambda i, b: (0, 0)),
        ],
        out_specs=pl.BlockSpec((1, L, cb), lambda i, b: (b, 0, i)),
        out_shape=jax.ShapeDtypeStruct((B, L, HY_W), BF16),
        compiler_params=_cparams(2),
        name=f"hyena_branch_L{L}",
    )(hmod, w_in_b, w_in_b, w_in_b, w_in_b, short_w, short_w, short_w, short_b, short_b, short_b,
      fa, fb, fd, bias, fwd, inv)


def _lru_kernel(latent, aliased, layer, L, nb, *refs):
    (h_ref, wx0_ref, wx1_ref, wg0_ref, wg1_ref, cw_ref, cb_ref, wa_ref, ba_ref, wxg_ref, bx_ref, lam_ref) = refs[:12]
    refs = refs[12:]
    if latent:
        h0_ref, mix_out, gate_w_s, a_s, b_s, y_s, g_s = refs
    else:
        if aliased:
            refs = refs[1:]
        mix_out, hl_out, gate_w_s, a_s, b_s, y_s, g_s = refs
    half_w = LRU_W // 2
    blocks_per_half = half_w // LRU_BS
    n_tiles = L // SUBLANES

    @pl.when(pl.program_id(0) == 0)
    def _build_gate_weights():
        gate_w_s[...] = jnp.zeros(gate_w_s.shape, BF16)
        for half in range(2):
            for kind, (w_ref, d) in enumerate(((wa_ref, 0), (wxg_ref, 0), (wa_ref, 1), (wxg_ref, 1))):
                for j in range(blocks_per_half):
                    blk = w_ref[0, d, half * blocks_per_half + j].astype(BF16)
                    gate_w_s[half, j * LRU_BS:(j + 1) * LRU_BS,
                             kind * half_w + j * LRU_BS:kind * half_w + (j + 1) * LRU_BS] = blk

    hb = h_ref[...].reshape(nb * L, D_MODEL)
    xs = jnp.concatenate([_dot(hb, wx0_ref[0]), _dot(hb, wx1_ref[0])], axis=1)
    g_s[:, :half_w] = _dot(hb, wg0_ref[0])
    g_s[:, half_w:] = _dot(hb, wg1_ref[0])
    cw = cw_ref[0]
    xc = (cb_ref[0] + _shifted(xs, -1, L) * cw[0:1] + xs * cw[1:2] + _shifted(xs, 1, L) * cw[2:3]
          + _shifted(xs, 2, L) * cw[3:4])
    xcb = xc.astype(BF16)
    sp = jax.nn.softplus(-lam_ref[0])
    ba, bx = ba_ref[0], bx_ref[0]
    for half in range(2):
        cols = slice(half * half_w, (half + 1) * half_w)
        bias = jnp.concatenate([ba[0:1, cols], bx[0:1, cols], ba[1:2, cols], bx[1:2, cols]], axis=1)
        gates = _dot(xcb[:, cols], gate_w_s[half]) + bias
        xh = xc[:, cols]
        for d in range(2):
            r = jax.nn.sigmoid(gates[:, (2 * d) * half_w:(2 * d + 1) * half_w])
            i = jax.nn.sigmoid(gates[:, (2 * d + 1) * half_w:(2 * d + 2) * half_w])
            log_a = -LRU_C * r * sp[d:d + 1, cols]
            a = jnp.exp(log_a)
            bcoef = jnp.sqrt(-jnp.tanh(log_a) * (a * a + 1.0)) * (i * xh)
            a_s[d, :, :, cols] = a.reshape(nb * n_tiles, SUBLANES, half_w)
            b_s[d, :, :, cols] = bcoef.reshape(nb * n_tiles, SUBLANES, half_w)

    if latent:
        init = []
        for b in range(nb):
            h0 = h0_ref[b, 0]
            init += [h0[0:1], h0[1:2]]
    else:
        init = [jnp.zeros((1, LRU_W), F32)] * (2 * nb)

    def tile_step(i, carry):
        carry = list(carry)
        for j in range(SUBLANES):
            for b in range(nb):
                tf = b * n_tiles + i
                tb = b * n_tiles + (n_tiles - 1 - i)
                jb = SUBLANES - 1 - j
                hf = a_s[0, tf, j:j + 1, :] * carry[2 * b] + b_s[0, tf, j:j + 1, :]
                y_s[0, tf, j:j + 1, :] = hf
                hbk = a_s[1, tb, jb:jb + 1, :] * carry[2 * b + 1] + b_s[1, tb, jb:jb + 1, :]
                y_s[1, tb, jb:jb + 1, :] = hbk
                carry[2 * b], carry[2 * b + 1] = hf, hbk
        return tuple(carry)

    last = lax.fori_loop(0, n_tiles, tile_step, tuple(init))
    if not latent:
        if not aliased:
            hl_out[...] = jnp.zeros(hl_out.shape, F32)
        slot = 0 if aliased else layer
        for b in range(nb):
            hl_out[b, slot, 0:1, :] = last[2 * b]
            hl_out[b, slot, 1:2, :] = last[2 * b + 1]
    y = (y_s[0] + y_s[1]).reshape(nb * L, LRU_W)
    mix_out[...] = (y * _silu(g_s[...])).astype(BF16).reshape(nb, L, LRU_W)


def _lru_branch(latent, layer, nb, hmod, w_in_b, conv_w, conv_b, wa, ba, wx, bx, lam, state=None, hl_prev=None):
    B, L, _ = hmod.shape
    cb = COL_BLOCK
    half_w = LRU_W // 2
    rows = nb * L
    per_layer = lambda shape: pl.BlockSpec((1,) + shape, lambda b: (layer,) + (0,) * len(shape))
    w_spec = lambda j: pl.BlockSpec((1, D_MODEL, cb), lambda b: (layer, 0, LRU_BLOCK0 + j))
    gate_blocks = (2, LRU_BLOCKS, LRU_BS, LRU_BS)
    in_specs = [pl.BlockSpec((nb, L, D_MODEL), lambda b: (b, 0, 0)), w_spec(0), w_spec(1), w_spec(2), w_spec(3),
                per_layer((4, LRU_W)), per_layer((1, LRU_W)), per_layer(gate_blocks), per_layer((2, LRU_W)),
                per_layer(gate_blocks), per_layer((2, LRU_W)), per_layer((2, LRU_W))]
    args = [hmod, w_in_b, w_in_b, w_in_b, w_in_b, conv_w, conv_b, wa, ba, wx, bx, lam]
    out_specs = [pl.BlockSpec((nb, L, LRU_W), lambda b: (b, 0, 0))]
    out_shape = [jax.ShapeDtypeStruct((B, L, LRU_W), BF16)]
    aliases = {}
    if latent:
        in_specs.append(pl.BlockSpec((nb, 1, 2, LRU_W), lambda b: (b, layer, 0, 0)))
        args.append(state)
    else:
        if hl_prev is not None:
            out_specs.append(pl.BlockSpec((nb, 1, 2, LRU_W), lambda b: (b, layer, 0, 0)))
        else:
            out_specs.append(pl.BlockSpec((nb, DEPTH, 2, LRU_W), lambda b: (b, 0, 0, 0)))
        out_shape.append(jax.ShapeDtypeStruct((B, DEPTH, 2, LRU_W), F32))
        if hl_prev is not None:
            in_specs.append(pl.BlockSpec(memory_space=pl.ANY))
            aliases = {len(args): 1}
            args.append(hl_prev)
    scan_buf = pltpu.VMEM((2, rows // SUBLANES, SUBLANES, LRU_W), F32)
    return pl.pallas_call(
        functools.partial(_lru_kernel, latent, hl_prev is not None, layer, L, nb),
        grid=(B // nb,),
        in_specs=in_specs,
        out_specs=out_specs,
        out_shape=out_shape,
        scratch_shapes=[pltpu.VMEM((2, half_w, 4 * half_w), BF16), scan_buf, scan_buf, scan_buf,
                        pltpu.VMEM((rows, LRU_W), F32)],
        input_output_aliases=aliases,
        compiler_params=_cparams(1),
        name=f"lru_branch_L{L}",
    )(*args)


def _out_kernel(final, x_ref, ma_ref, mh_ref, ml_ref, w_ref, mod_ref, fg_ref, o_ref):
    out = (_dot(ma_ref[...], w_ref[0, 0:ATTN_W, :]) + _dot(mh_ref[...], w_ref[0, ATTN_W:ATTN_W + HY_W, :])
           + _dot(ml_ref[...], w_ref[0, ATTN_W + HY_W:, :]))
    gate = mod_ref[0, 0][2:3]
    xn = x_ref[...] + gate * out
    if final:
        ms = jnp.mean(xn * xn, axis=-1, keepdims=True)
        xn = xn * lax.rsqrt(ms + EPS) * fg_ref[...]
    o_ref[...] = xn


def _out_projection(final, latent, layer, x, mix_a, mix_h, mix_l, w_out_b, mod, final_g):
    B, L, _ = x.shape
    tm = OUT_ROWS
    rows = B * L
    tile = lambda w: pl.BlockSpec((tm, w), lambda i: (i, 0))
    mod_row = (lambda i: (layer, 1 + (i * tm) // L, 0, 0)) if latent else (lambda i: (layer, 0, 0, 0))
    y = pl.pallas_call(
        functools.partial(_out_kernel, final),
        grid=(rows // tm,),
        in_specs=[tile(D_MODEL), tile(ATTN_W), tile(HY_W), tile(LRU_W),
                  pl.BlockSpec((1, D_MIX, D_MODEL), lambda i: (layer, 0, 0)),
                  pl.BlockSpec((1, 1, 3, D_MODEL), mod_row), pl.BlockSpec((1, D_MODEL), lambda i: (0, 0))],
        out_specs=tile(D_MODEL),
        out_shape=jax.ShapeDtypeStruct((rows, D_MODEL), F32),
        compiler_params=_cparams(1),
        name=f"out_proj_L{L}",
    )(x.reshape(rows, D_MODEL), mix_a.reshape(rows, ATTN_W), mix_h.reshape(rows, HY_W), mix_l.reshape(rows, LRU_W),
      w_out_b, mod, final_g)
    return y.reshape(B, L, D_MODEL)


def _mixer_pass(latent, x, p, filt, dft, rope=None, cache=None, state=None):
    B, L, _ = x.shape
    fwd, inv = dft
    nb = 1 if latent else 4
    kv, hl = None, None
    for l in range(DEPTH):
        outs = _attention_branch(latent, l, x, p['mod'], p['norm_g'], p['w_in'], p['gq'], p['gk'], p['e_avg'],
                                 rope, cache, kv)
        hmod, mix_a = outs[0], outs[1]
        if not latent:
            kv = (outs[2], outs[3])
        mix_h = _hyena_branch(l, hmod, p['w_in'], p['short_w'], p['short_b'], filt, p['hy_bias'], fwd, inv)
        lru = _lru_branch(latent, l, nb, hmod, p['w_in'], p['conv_w'], p['conv_b'], p['wa'], p['ba'], p['wx'],
                          p['bx'], p['lam'], state, hl)
        if not latent:
            hl = lru[1]
        x = _out_projection(l == DEPTH - 1, latent, l, x, mix_a, mix_h, lru[0], p['w_out'], p['mod'], p['final_g'])
    return x, kv, hl


def kernel(x_prompt, x_sample, cache_k, cache_v, state_lru, c, c_ctx, norm_g, w_ada, b_ada, w_in, q_norm_g, k_norm_g,
           hy_short_w, hy_short_b, hy_filt_w1, hy_filt_b1, hy_filt_w2, hy_filt_b2, hy_filt_w3, hy_filt_freq,
           hy_filt_decay, hy_bias, lru_conv_w, lru_conv_b, lru_wa, lru_ba, lru_wx, lru_bx, lru_lambda, w_out, final_g):
    batch, seq, _ = x_prompt.shape
    dec_batch, dec_seq, _ = x_sample.shape
    past = cache_k.shape[2]

    cvecs = jnp.concatenate([c_ctx[None, :], c, jnp.zeros((MOD_ROWS - 1 - dec_batch, D_MODEL), F32)], axis=0)
    mod = _modulation(cvecs, w_ada, b_ada).reshape(DEPTH, MOD_ROWS, 3, D_MODEL)

    w1p = jnp.pad(hy_filt_w1, ((0, 0), (0, FEAT_PAD - HY_EMB), (0, 0)))
    dft, filt = {}, {}
    for L in (seq, dec_seq):
        fwd, inv = (jnp.asarray(m) for m in _dft_tables(L))
        f_hi, f_lo = _split_bf16(fwd)
        dft[L] = (f_hi, inv.astype(BF16))
        filt[L] = _filter_spectrum(L, jnp.asarray(_hyena_feats(L)), f_hi, f_lo, w1p, hy_filt_b1[:, None, :],
                                   hy_filt_w2, hy_filt_b2[:, None, :], hy_filt_w3, hy_filt_freq,
                                   hy_filt_decay[:, None, :])

    params = {
        'mod': mod,
        'norm_g': norm_g[:, None, :],
        'w_in': w_in.astype(BF16),
        'gq': jnp.tile(q_norm_g, (1, N_Q_HEADS))[:, None, :],
        'gk': jnp.tile(k_norm_g, (1, N_KV_HEADS))[:, None, :],
        'e_avg': jnp.asarray(_head_average_matrix()).astype(BF16),
        'short_w': hy_short_w, 'short_b': hy_short_b[:, None, :], 'hy_bias': hy_bias[:, None, :],
        'conv_w': lru_conv_w, 'conv_b': lru_conv_b[:, None, :],
        'wa': lru_wa, 'ba': lru_ba, 'wx': lru_wx, 'bx': lru_bx, 'lam': lru_lambda,
        'w_out': w_out.astype(BF16),
        'final_g': final_g[None, :],
    }

    y_prompt, kv, new_lru = _mixer_pass(False, x_prompt, params, filt[seq], dft[seq])
    new_k = kv[0].reshape(batch, DEPTH, seq, N_KV_HEADS, HEAD_DIM)
    new_v = kv[1].reshape(batch, DEPTH, seq, N_KV_HEADS, HEAD_DIM)

    rope = tuple(jnp.asarray(t) for t in _rope_tables(dec_seq))
    cache = (cache_k.reshape(dec_batch, DEPTH, past, KV_W), cache_v.reshape(dec_batch, DEPTH, past, KV_W))
    y_sample, _, _ = _mixer_pass(True, x_sample, params, filt[dec_seq], dft[dec_seq], rope, cache, state_lru)
    return (y_prompt, y_sample, new_k, new_v, new_lru)
```

```python
import functools
import math

import numpy as np
import jax
import jax.numpy as jnp
from jax import lax
from jax.experimental import pallas as pl
from jax.experimental.pallas import tpu as pltpu

F32 = jnp.float32
BF16 = jnp.bfloat16

D_MODEL = 1024
DEPTH = 2
GRID_W = 64
HEAD_DIM = 64
N_Q_HEADS = 8
N_KV_HEADS = 2
Q_PER_KV = N_Q_HEADS // N_KV_HEADS
ATTN_W = N_Q_HEADS * HEAD_DIM
KV_W = N_KV_HEADS * HEAD_DIM
ROPE_THETA = 10000.0
ATTN_SCALE = HEAD_DIM ** -0.5
HY_W = 512
HY_BANDS = 16
HY_EMB = 2 * HY_BANDS + 1
HY_FH = 64
LRU_W = 512
LRU_BLOCKS = 8
LRU_BS = LRU_W // LRU_BLOCKS
LRU_C = 8.0
EPS = 1e-6

ATTN_COLS = ATTN_W + 2 * KV_W + ATTN_W
D_MIX = ATTN_W + HY_W + LRU_W

LANES = 128
SUBLANES = 8
MXU_DIM = 256
COL_BLOCK = MXU_DIM
HY_BLOCK0 = ATTN_COLS // COL_BLOCK
LRU_BLOCK0 = (ATTN_COLS + 4 * HY_W) // COL_BLOCK
Q_CHUNK = 256
OUT_ROWS = 1024
FEAT_PAD = 128
MOD_ROWS = 8
VMEM_LIMIT = 56 * 1024 * 1024


def _cparams(n_axes):
    return pltpu.CompilerParams(dimension_semantics=("arbitrary",) * n_axes, vmem_limit_bytes=VMEM_LIMIT)


def _split_bf16(a):
    hi = a.astype(BF16)
    lo = (a - hi.astype(F32)).astype(BF16)
    return hi, lo


def _dot(a, b):
    return jnp.dot(a, b, preferred_element_type=F32)


def _dot3(a, b):
    a_hi, a_lo = _split_bf16(a)
    b_hi, b_lo = _split_bf16(b)
    return _dot(a_hi, b_hi) + _dot(a_hi, b_lo) + _dot(a_lo, b_hi)


def _dot3_const(m_hi, m_lo, b):
    b_hi, b_lo = _split_bf16(b)
    return _dot(m_hi, b_hi) + _dot(m_hi, b_lo) + _dot(m_lo, b_hi)


def _silu(x):
    return x * jax.nn.sigmoid(x)


@functools.lru_cache(maxsize=None)
def _dft_tables(L):
    n = 2 * L
    t = np.arange(L)
    kt = (t[:, None] * t[None, :]) % n
    ang = 2.0 * np.pi * kt / n
    cosm, sinm = np.cos(ang), np.sin(ang)
    alt = (-1.0) ** t
    f_s = -sinm
    f_s[0, :] = alt
    fwd = np.concatenate([cosm, f_s], axis=0)
    g_c = 2.0 * cosm.T / n
    g_c[:, 0] = 1.0 / n
    g_s = -2.0 * sinm.T / n
    g_s[:, 0] = alt / n
    inv = np.concatenate([g_c, g_s], axis=1)
    return fwd.astype(np.float32), inv.astype(np.float32)


@functools.lru_cache(maxsize=None)
def _hyena_feats(L):
    t = np.linspace(0.0, 1.0, L)[:, None]
    w = 2.0 * math.pi * np.arange(L)[:, None] / L
    f = np.linspace(1e-4, HY_BANDS - 1, HY_BANDS)[None, :]
    out = np.zeros((L, FEAT_PAD), np.float32)
    out[:, :HY_EMB] = np.concatenate([t, np.cos(f * w), -np.sin(f * w)], axis=-1)
    return out


@functools.lru_cache(maxsize=None)
def _rope_tables(L):
    rows = L // GRID_W
    row = np.repeat(np.arange(rows), GRID_W).astype(np.float64)
    col = np.tile(np.arange(GRID_W), rows).astype(np.float64)
    n_freq = HEAD_DIM // 4
    inv = ROPE_THETA ** (-np.arange(n_freq) / n_freq)
    ang = np.concatenate([row[:, None] * inv, col[:, None] * inv], axis=-1)
    cos = np.repeat(np.cos(ang), 2, axis=-1)
    sin = np.repeat(np.sin(ang), 2, axis=-1) * np.tile(np.array([-1.0, 1.0]), HEAD_DIM // 2)
    reps = LANES // HEAD_DIM
    return np.tile(cos, (1, reps)).astype(np.float32), np.tile(sin, (1, reps)).astype(np.float32)


@functools.lru_cache(maxsize=None)
def _head_average_matrix():
    return np.kron(np.eye(MXU_DIM // HEAD_DIM), np.full((HEAD_DIM, HEAD_DIM), 1.0 / HEAD_DIM)).astype(np.float32)


def _mod_kernel(c_ref, w_ref, b_ref, o_ref):
    s = _silu(c_ref[...])
    o_ref[0] = _dot3(s, w_ref[0]) + b_ref[0]


def _modulation(cvecs, w_ada, b_ada):
    tn = D_MODEL
    return pl.pallas_call(
        _mod_kernel,
        grid=(DEPTH, 3 * D_MODEL // tn),
        in_specs=[
            pl.BlockSpec((MOD_ROWS, D_MODEL), lambda l, j: (0, 0)),
            pl.BlockSpec((1, D_MODEL, tn), lambda l, j: (l, 0, j)),
            pl.BlockSpec((1, 1, tn), lambda l, j: (l, 0, j)),
        ],
        out_specs=pl.BlockSpec((1, MOD_ROWS, tn), lambda l, j: (l, 0, j)),
        out_shape=jax.ShapeDtypeStruct((DEPTH, MOD_ROWS, 3 * D_MODEL), F32),
        compiler_params=_cparams(2),
        name="adaln_mod",
    )(cvecs, w_ada, b_ada.reshape(DEPTH, 1, 3 * D_MODEL))


def _filter_kernel(L, feats_ref, w1_ref, b1_ref, w2_ref, b2_ref, w3_ref, freq_ref, decay_ref,
                   fhi_ref, flo_ref, a_ref, b_ref, d_ref):
    feats = feats_ref[...]
    t = feats[:, 0:1]
    freq = freq_ref[0]
    hdn = jnp.sin(freq[0:1] * (_dot3(feats, w1_ref[0]) + b1_ref[0]))
    hdn = jnp.sin(freq[1:2] * (_dot3(hdn, w2_ref[0]) + b2_ref[0]))
    h = _dot3(hdn, w3_ref[0]) * jnp.exp(-t * decay_ref[0])
    row = lax.broadcasted_iota(jnp.int32, (L, HY_W), 0)
    h_fwd = h[:, :HY_W]
    h_bwd = jnp.where(row >= 1, h[:, HY_W:], 0.0)
    inv_norm = 1.0 / jnp.sum(jnp.abs(h_fwd) + jnp.abs(h_bwd), axis=0, keepdims=True)
    even = h_fwd + h_bwd
    odd = h_fwd - h_bwd
    h_re = _dot3_const(fhi_ref[0:L, :], flo_ref[0:L, :], even) * inv_norm
    h_im = _dot3_const(fhi_ref[L:2 * L, :], flo_ref[L:2 * L, :], odd) * inv_norm
    alt = jnp.where(row % 2 == 0, 1.0, -1.0)
    nyq = jnp.sum(even * alt, axis=0, keepdims=True) * inv_norm
    a_ref[0] = h_re
    b_ref[0] = jnp.where(row >= 1, h_im, 0.0)
    d_ref[0] = jnp.where(row >= 1, h_re, nyq)


def _filter_spectrum(L, feats, f_hi, f_lo, w1p, b1, w2, b2, w3, freq, decay):
    full = lambda shape: pl.BlockSpec(shape, lambda l: (0,) * len(shape))
    per_layer = lambda shape: pl.BlockSpec((1,) + shape, lambda l: (l,) + (0,) * len(shape))
    out = jax.ShapeDtypeStruct((DEPTH, L, HY_W), F32)
    return pl.pallas_call(
        functools.partial(_filter_kernel, L),
        grid=(DEPTH,),
        in_specs=[
            full((L, FEAT_PAD)),
            per_layer((FEAT_PAD, HY_FH)), per_layer((1, HY_FH)),
            per_layer((HY_FH, HY_FH)), per_layer((1, HY_FH)),
            per_layer((HY_FH, 2 * HY_W)), per_layer((2, HY_FH)), per_layer((1, 2 * HY_W)),
            full((2 * L, L)), full((2 * L, L)),
        ],
        out_specs=[per_layer((L, HY_W))] * 3,
        out_shape=[out, out, out],
        compiler_params=_cparams(1),
        name=f"hyena_filter_L{L}",
    )(feats, w1p, b1, w2, b2, w3, freq, decay, f_hi, f_lo)


def _rope(x, cos, sin):
    lane = lax.broadcasted_iota(jnp.int32, x.shape, 1)
    partner = jnp.where(lane % 2 == 0, pltpu.roll(x, LANES - 1, 1), pltpu.roll(x, 1, 1))
    return x * cos + partner * sin


def _head_mean_square(x, e):
    sq = (x * x).astype(BF16)
    width = x.shape[1]
    if width <= MXU_DIM:
        return _dot(sq, e[:width, :width])
    parts = [_dot(sq[:, i:i + MXU_DIM], e) for i in range(0, width, MXU_DIM)]
    return jnp.concatenate(parts, axis=1)


def _attn_kernel(latent, aliased, layer, L, nb, *refs):
    x_ref, mod_ref, ng_ref, w_ref, gq_ref, gk_ref, e_ref = refs[:7]
    refs = refs[7:]
    if latent:
        cos_ref, sin_ref, ck_ref, cv_ref, h_out, mix_out, q_s, k_s, v_s, g_s, o_s, ck_s, cv_s = refs
    else:
        if aliased:
            refs = refs[2:]
        h_out, mix_out, k_out, v_out, q_s, k_s, v_s, g_s, o_s = refs
    rows = nb * L

    x = x_ref[...].reshape(rows, D_MODEL)
    m = mod_ref[0, 0]
    shift, scale = m[0:1], m[1:2]
    ms = jnp.mean(x * x, axis=-1, keepdims=True)
    h = (x * lax.rsqrt(ms + EPS) * ng_ref[0]) * (1.0 + scale) + shift
    hb = h.astype(BF16)
    h_out[...] = hb.reshape(nb, L, D_MODEL)

    u = _dot(hb, w_ref[0])
    q = u[:, :ATTN_W]
    k = u[:, ATTN_W:ATTN_W + KV_W]
    v = u[:, ATTN_W + KV_W:ATTN_W + 2 * KV_W]
    g_s[...] = u[:, ATTN_W + 2 * KV_W:]
    e = e_ref[...]
    qn = q * lax.rsqrt(_head_mean_square(q, e) + EPS) * gq_ref[0]
    kn = k * lax.rsqrt(_head_mean_square(k, e) + EPS) * gk_ref[0]
    if latent:
        cos, sin = cos_ref[...], sin_ref[...]
        qn = jnp.concatenate([_rope(qn[:, i:i + LANES], cos, sin) for i in range(0, ATTN_W, LANES)], axis=1)
        kn = _rope(kn, cos, sin)
        ck_s[...] = ck_ref[0, 0].astype(BF16)
        cv_s[...] = cv_ref[0, 0].astype(BF16)
    else:
        for b in range(nb):
            kb, vb = kn[b * L:(b + 1) * L], v[b * L:(b + 1) * L]
            if aliased:
                k_out[b, 0] = kb
                v_out[b, 0] = vb
            else:
                for slot in range(DEPTH):
                    k_out[b, slot] = kb if slot == layer else jnp.zeros_like(kb)
                    v_out[b, slot] = vb if slot == layer else jnp.zeros_like(vb)
    q_s[...] = (qn * ATTN_SCALE).astype(BF16)
    k_s[...] = kn.astype(BF16)
    v_s[...] = v.astype(BF16)

    contract_last = (((1,), (1,)), ((), ()))
    chunks_per_seq = L // Q_CHUNK

    def chunk(c, carry):
        r0 = pl.multiple_of(c * Q_CHUNK, Q_CHUNK)
        k0 = pl.multiple_of((c // chunks_per_seq) * L, L)
        for hd in range(N_Q_HEADS):
            j = hd // Q_PER_KV
            hs = slice(hd * HEAD_DIM, (hd + 1) * HEAD_DIM)
            js = slice(j * HEAD_DIM, (j + 1) * HEAD_DIM)
            qh = q_s[pl.ds(r0, Q_CHUNK), hs]
            s = lax.dot_general(qh, k_s[pl.ds(k0, L), js], contract_last, preferred_element_type=F32)
            mx = jnp.max(s, axis=-1, keepdims=True)
            if latent:
                s0 = lax.dot_general(qh, ck_s[:, js], contract_last, preferred_element_type=F32)
                mx = jnp.maximum(mx, jnp.max(s0, axis=-1, keepdims=True))
                p0 = jnp.exp(s0 - mx)
            p = jnp.exp(s - mx)
            den = jnp.sum(p, axis=-1, keepdims=True)
            o = _dot(p.astype(BF16), v_s[pl.ds(k0, L), js])
            if latent:
                den = den + jnp.sum(p0, axis=-1, keepdims=True)
                o = o + _dot(p0.astype(BF16), cv_s[:, js])
            o_s[pl.ds(r0, Q_CHUNK), hs] = o / den
        return carry

    lax.fori_loop(0, nb * chunks_per_seq, chunk, 0)
    mix_out[...] = (o_s[...] * _silu(g_s[...])).astype(BF16).reshape(nb, L, ATTN_W)


def _attention_branch(latent, layer, nb, x, mod, norm_g, w_in_b, gq, gk, e_avg, rope=None, cache=None, kv_prev=None):
    B, L, _ = x.shape
    rows = nb * L
    assert nb == 1 or not latent
    full = lambda shape: pl.BlockSpec(shape, lambda b: (0,) * len(shape))
    per_b = lambda shape: pl.BlockSpec((nb,) + shape, lambda b: (b,) + (0,) * len(shape))
    per_layer = lambda shape: pl.BlockSpec((1,) + shape, lambda b: (layer,) + (0,) * len(shape))
    mod_spec = pl.BlockSpec((1, 1, 3, D_MODEL), (lambda b: (layer, 1 + b, 0, 0)) if latent else (lambda b: (layer, 0, 0, 0)))
    in_specs = [per_b((L, D_MODEL)), mod_spec, per_layer((1, D_MODEL)), per_layer((D_MODEL, ATTN_COLS)),
                per_layer((1, ATTN_W)), per_layer((1, KV_W)), full((MXU_DIM, MXU_DIM))]
    args = [x, mod, norm_g, w_in_b, gq, gk, e_avg]
    out_specs = [per_b((L, D_MODEL)), per_b((L, ATTN_W))]
    out_shape = [jax.ShapeDtypeStruct((B, L, D_MODEL), BF16), jax.ShapeDtypeStruct((B, L, ATTN_W), BF16)]
    scratch = [pltpu.VMEM((rows, ATTN_W), BF16), pltpu.VMEM((rows, KV_W), BF16), pltpu.VMEM((rows, KV_W), BF16),
               pltpu.VMEM((rows, ATTN_W), F32), pltpu.VMEM((rows, ATTN_W), F32)]
    aliases = {}
    if latent:
        cache_k, cache_v = cache
        past = cache_k.shape[2]
        cache_spec = pl.BlockSpec((1, 1, past, KV_W), lambda b: (b, layer, 0, 0))
        in_specs += [full((L, LANES)), full((L, LANES)), cache_spec, cache_spec]
        args += [rope[0], rope[1], cache_k, cache_v]
        scratch += [pltpu.VMEM((past, KV_W), BF16), pltpu.VMEM((past, KV_W), BF16)]
    else:
        if kv_prev is not None:
            kv_spec = pl.BlockSpec((nb, 1, L, KV_W), lambda b: (b, layer, 0, 0))
        else:
            kv_spec = pl.BlockSpec((nb, DEPTH, L, KV_W), lambda b: (b, 0, 0, 0))
        kv_shape = jax.ShapeDtypeStruct((B, DEPTH, L, KV_W), F32)
        out_specs += [kv_spec, kv_spec]
        out_shape += [kv_shape, kv_shape]
        if kv_prev is not None:
            in_specs += [pl.BlockSpec(memory_space=pl.ANY)] * 2
            aliases = {len(args): 2, len(args) + 1: 3}
            args += list(kv_prev)
    return pl.pallas_call(
        functools.partial(_attn_kernel, latent, kv_prev is not None, layer, L, nb),
        grid=(B // nb,),
        in_specs=in_specs,
        out_specs=out_specs,
        out_shape=out_shape,
        scratch_shapes=scratch,
        input_output_aliases=aliases,
        compiler_params=_cparams(1),
        name=f"attn_branch_L{L}",
    )(*args)


def _shifted(x, offset, period):
    rows = x.shape[0]
    t = lax.broadcasted_iota(jnp.int32, x.shape, 0)
    if rows != period:
        t = t % period
    rolled = pltpu.roll(x, (-offset) % rows, 0)
    valid = (t >= -offset) if offset < 0 else (t < period - offset)
    return jnp.where(valid, rolled, 0.0)


def _hyena_kernel(L, nb, h_ref, w0_ref, w1_ref, w2_ref, wg_ref, sw0_ref, sw1_ref, sw2_ref, sb0_ref, sb1_ref, sb2_ref,
                  fa_ref, fb_ref, fd_ref, bias_ref, fwd_ref, inv_ref, mix_out):
    hb = h_ref[...].reshape(nb * L, D_MODEL)

    def short_conv(w_ref, sw_ref, sb_ref):
        xs = _dot(hb, w_ref[0])
        w = sw_ref[0]
        return sb_ref[0] + _shifted(xs, -1, L) * w[0:1] + xs * w[1:2] + _shifted(xs, 1, L) * w[2:3]

    x0 = short_conv(w0_ref, sw0_ref, sb0_ref)
    x1 = short_conv(w1_ref, sw1_ref, sb1_ref)
    hv = short_conv(w2_ref, sw2_ref, sb2_ref)
    gated = x0 * _silu(_dot(hb, wg_ref[0]))
    z = x1 * hv
    zb = z.astype(BF16)
    fa, fb, fd = fa_ref[0], fb_ref[0], fd_ref[0]
    for b in range(nb):
        rs = slice(b * L, (b + 1) * L)
        zf = _dot(fwd_ref[...], zb[rs])
        re, im = zf[:L], zf[L:]
        y_re = re * fa - im * fb
        y_im = re * fb + im * fd
        y = _dot(inv_ref[:, :L], y_re.astype(BF16)) + _dot(inv_ref[:, L:], y_im.astype(BF16))
        y = y + z[rs] * bias_ref[0]
        mix_out[b] = (y * gated[rs]).astype(BF16)


def _hyena_branch(layer, nb, hmod, w_in_b, short_w, short_b, filt, bias, fwd, inv):
    B, L, _ = hmod.shape
    cb = COL_BLOCK
    nt = HY_W // cb
    fa, fb, fd = filt
    w_spec = lambda part: pl.BlockSpec((1, D_MODEL, cb), lambda i, b: (layer, 0, HY_BLOCK0 + part * nt + i))
    sw_spec = lambda part: pl.BlockSpec((1, 3, cb), lambda i, b: (layer, 0, part * nt + i))
    sb_spec = lambda part: pl.BlockSpec((1, 1, cb), lambda i, b: (layer, 0, part * nt + i))
    filt_spec = pl.BlockSpec((1, L, cb), lambda i, b: (layer, 0, i))
    return pl.pallas_call(
        functools.partial(_hyena_kernel, L, nb),
        grid=(nt, B // nb),
        in_specs=[
            pl.BlockSpec((nb, L, D_MODEL), lambda i, b: (b, 0, 0)),
            w_spec(0), w_spec(1), w_spec(2), w_spec(3),
            sw_spec(0), sw_spec(1), sw_spec(2),
            sb_spec(0), sb_spec(1), sb_spec(2),
            filt_spec, filt_spec, filt_spec,
            pl.BlockSpec((1, 1, cb), lambda i, b: (layer, 0, i)),
            pl.BlockSpec((2 * L, L), lambda i, b: (0, 0)),
            pl.BlockSpec((L, 2 * L), lambda i, b: (0, 0)),
        ],
        out_specs=pl.BlockSpec((nb, L, cb), lambda i, b: (b, 0, i)),
        out_shape=jax.ShapeDtypeStruct((B, L, HY_W), BF16),
        compiler_params=_cparams(2),
        name=f"hyena_branch_L{L}",
    )(hmod, w_in_b, w_in_b, w_in_b, w_in_b, short_w, short_w, short_w, short_b, short_b, short_b,
      fa, fb, fd, bias, fwd, inv)


def _lru_kernel(latent, aliased, layer, L, nb, *refs):
    (h_ref, wx0_ref, wx1_ref, wg0_ref, wg1_ref, cw_ref, cb_ref, wa_ref, ba_ref, wxg_ref, bx_ref, lam_ref) = refs[:12]
    refs = refs[12:]
    if latent:
        h0_ref, mix_out, gate_w_s, a_s, b_s, y_s, g_s = refs
    else:
        if aliased:
            refs = refs[1:]
        mix_out, hl_out, gate_w_s, a_s, b_s, y_s, g_s = refs
    half_w = LRU_W // 2
    blocks_per_half = half_w // LRU_BS
    n_tiles = L // SUBLANES

    @pl.when(pl.program_id(0) == 0)
    def _build_gate_weights():
        gate_w_s[...] = jnp.zeros(gate_w_s.shape, BF16)
        for half in range(2):
            for kind, (w_ref, d) in enumerate(((wa_ref, 0), (wxg_ref, 0), (wa_ref, 1), (wxg_ref, 1))):
                for j in range(blocks_per_half):
                    blk = w_ref[0, d, half * blocks_per_half + j].astype(BF16)
                    gate_w_s[half, j * LRU_BS:(j + 1) * LRU_BS,
                             kind * half_w + j * LRU_BS:kind * half_w + (j + 1) * LRU_BS] = blk

    hb = h_ref[...].reshape(nb * L, D_MODEL)
    xs = jnp.concatenate([_dot(hb, wx0_ref[0]), _dot(hb, wx1_ref[0])], axis=1)
    g_s[:, :half_w] = _dot(hb, wg0_ref[0])
    g_s[:, half_w:] = _dot(hb, wg1_ref[0])
    cw = cw_ref[0]
    xc = (cb_ref[0] + _shifted(xs, -1, L) * cw[0:1] + xs * cw[1:2] + _shifted(xs, 1, L) * cw[2:3]
          + _shifted(xs, 2, L) * cw[3:4])
    xcb = xc.astype(BF16)
    sp = jax.nn.softplus(-lam_ref[0])
    ba, bx = ba_ref[0], bx_ref[0]
    for half in range(2):
        cols = slice(half * half_w, (half + 1) * half_w)
        bias = jnp.concatenate([ba[0:1, cols], bx[0:1, cols], ba[1:2, cols], bx[1:2, cols]], axis=1)
        gates = _dot(xcb[:, cols], gate_w_s[half]) + bias
        xh = xc[:, cols]
        for d in range(2):
            r = jax.nn.sigmoid(gates[:, (2 * d) * half_w:(2 * d + 1) * half_w])
            i = jax.nn.sigmoid(gates[:, (2 * d + 1) * half_w:(2 * d + 2) * half_w])
            log_a = -LRU_C * r * sp[d:d + 1, cols]
            a = jnp.exp(log_a)
            bcoef = jnp.sqrt(-jnp.tanh(log_a) * (a * a + 1.0)) * (i * xh)
            a_s[d, :, :, cols] = a.reshape(nb * n_tiles, SUBLANES, half_w)
            b_s[d, :, :, cols] = bcoef.reshape(nb * n_tiles, SUBLANES, half_w)

    if latent:
        init = []
        for b in range(nb):
            h0 = h0_ref[b, 0]
            init += [h0[0:1], h0[1:2]]
    else:
        init = [jnp.zeros((1, LRU_W), F32)] * (2 * nb)

    def tile_step(i, carry):
        carry = list(carry)
        for j in range(SUBLANES):
            for b in range(nb):
                tf = b * n_tiles + i
                tb = b * n_tiles + (n_tiles - 1 - i)
                jb = SUBLANES - 1 - j
                hf = a_s[0, tf, j:j + 1, :] * carry[2 * b] + b_s[0, tf, j:j + 1, :]
                y_s[0, tf, j:j + 1, :] = hf
                hbk = a_s[1, tb, jb:jb + 1, :] * carry[2 * b + 1] + b_s[1, tb, jb:jb + 1, :]
                y_s[1, tb, jb:jb + 1, :] = hbk
                carry[2 * b], carry[2 * b + 1] = hf, hbk
        return tuple(carry)

    last = lax.fori_loop(0, n_tiles, tile_step, tuple(init))
    if not latent:
        if not aliased:
            hl_out[...] = jnp.zeros(hl_out.shape, F32)
        slot = 0 if aliased else layer
        for b in range(nb):
            hl_out[b, slot, 0:1, :] = last[2 * b]
            hl_out[b, slot, 1:2, :] = last[2 * b + 1]
    y = (y_s[0] + y_s[1]).reshape(nb * L, LRU_W)
    mix_out[...] = (y * _silu(g_s[...])).astype(BF16).reshape(nb, L, LRU_W)


def _lru_branch(latent, layer, nb, hmod, w_in_b, conv_w, conv_b, wa, ba, wx, bx, lam, state=None, hl_prev=None):
    B, L, _ = hmod.shape
    cb = COL_BLOCK
    half_w = LRU_W // 2
    rows = nb * L
    per_layer = lambda shape: pl.BlockSpec((1,) + shape, lambda b: (layer,) + (0,) * len(shape))
    w_spec = lambda j: pl.BlockSpec((1, D_MODEL, cb), lambda b: (layer, 0, LRU_BLOCK0 + j))
    gate_blocks = (2, LRU_BLOCKS, LRU_BS, LRU_BS)
    in_specs = [pl.BlockSpec((nb, L, D_MODEL), lambda b: (b, 0, 0)), w_spec(0), w_spec(1), w_spec(2), w_spec(3),
                per_layer((4, LRU_W)), per_layer((1, LRU_W)), per_layer(gate_blocks), per_layer((2, LRU_W)),
                per_layer(gate_blocks), per_layer((2, LRU_W)), per_layer((2, LRU_W))]
    args = [hmod, w_in_b, w_in_b, w_in_b, w_in_b, conv_w, conv_b, wa, ba, wx, bx, lam]
    out_specs = [pl.BlockSpec((nb, L, LRU_W), lambda b: (b, 0, 0))]
    out_shape = [jax.ShapeDtypeStruct((B, L, LRU_W), BF16)]
    aliases = {}
    if latent:
        in_specs.append(pl.BlockSpec((nb, 1, 2, LRU_W), lambda b: (b, layer, 0, 0)))
        args.append(state)
    else:
        if hl_prev is not None:
            out_specs.append(pl.BlockSpec((nb, 1, 2, LRU_W), lambda b: (b, layer, 0, 0)))
        else:
            out_specs.append(pl.BlockSpec((nb, DEPTH, 2, LRU_W), lambda b: (b, 0, 0, 0)))
        out_shape.append(jax.ShapeDtypeStruct((B, DEPTH, 2, LRU_W), F32))
        if hl_prev is not None:
            in_specs.append(pl.BlockSpec(memory_space=pl.ANY))
            aliases = {len(args): 1}
            args.append(hl_prev)
    scan_buf = pltpu.VMEM((2, rows // SUBLANES, SUBLANES, LRU_W), F32)
    return pl.pallas_call(
        functools.partial(_lru_kernel, latent, hl_prev is not None, layer, L, nb),
        grid=(B // nb,),
        in_specs=in_specs,
        out_specs=out_specs,
        out_shape=out_shape,
        scratch_shapes=[pltpu.VMEM((2, half_w, 4 * half_w), BF16), scan_buf, scan_buf, scan_buf,
                        pltpu.VMEM((rows, LRU_W), F32)],
        input_output_aliases=aliases,
        compiler_params=_cparams(1),
        name=f"lru_branch_L{L}",
    )(*args)


def _out_kernel(final, x_ref, ma_ref, mh_ref, ml_ref, w_ref, mod_ref, fg_ref, o_ref):
    out = (_dot(ma_ref[...], w_ref[0, 0:ATTN_W, :]) + _dot(mh_ref[...], w_ref[0, ATTN_W:ATTN_W + HY_W, :])
           + _dot(ml_ref[...], w_ref[0, ATTN_W + HY_W:, :]))
    gate = mod_ref[0, 0][2:3]
    xn = x_ref[...] + gate * out
    if final:
        ms = jnp.mean(xn * xn, axis=-1, keepdims=True)
        xn = xn * lax.rsqrt(ms + EPS) * fg_ref[...]
    o_ref[...] = xn


def _out_projection(final, latent, layer, x, mix_a, mix_h, mix_l, w_out_b, mod, final_g):
    B, L, _ = x.shape
    tm = OUT_ROWS
    rows = B * L
    tile = lambda w: pl.BlockSpec((tm, w), lambda i: (i, 0))
    mod_row = (lambda i: (layer, 1 + (i * tm) // L, 0, 0)) if latent else (lambda i: (layer, 0, 0, 0))
    y = pl.pallas_call(
        functools.partial(_out_kernel, final),
        grid=(rows // tm,),
        in_specs=[tile(D_MODEL), tile(ATTN_W), tile(HY_W), tile(LRU_W),
                  pl.BlockSpec((1, D_MIX, D_MODEL), lambda i: (layer, 0, 0)),
                  pl.BlockSpec((1, 1, 3, D_MODEL), mod_row), pl.BlockSpec((1, D_MODEL), lambda i: (0, 0))],
        out_specs=tile(D_MODEL),
        out_shape=jax.ShapeDtypeStruct((rows, D_MODEL), F32),
        compiler_params=_cparams(1),
        name=f"out_proj_L{L}",
    )(x.reshape(rows, D_MODEL), mix_a.reshape(rows, ATTN_W), mix_h.reshape(rows, HY_W), mix_l.reshape(rows, LRU_W),
      w_out_b, mod, final_g)
    return y.reshape(B, L, D_MODEL)


def _mixer_pass(latent, x, p, filt, dft, rope=None, cache=None, state=None):
    B, L, _ = x.shape
    fwd, inv = dft
    nb = 1 if latent else 4
    kv, hl = None, None
    for l in range(DEPTH):
        outs = _attention_branch(latent, l, nb, x, p['mod'], p['norm_g'], p['w_in'], p['gq'], p['gk'], p['e_avg'],
                                 rope, cache, kv)
        hmod, mix_a = outs[0], outs[1]
        if not latent:
            kv = (outs[2], outs[3])
        mix_h = _hyena_branch(l, nb, hmod, p['w_in'], p['short_w'], p['short_b'], filt, p['hy_bias'], fwd, inv)
        lru = _lru_branch(latent, l, nb, hmod, p['w_in'], p['conv_w'], p['conv_b'], p['wa'], p['ba'], p['wx'],
                          p['bx'], p['lam'], state, hl)
        if not latent:
            hl = lru[1]
        x = _out_projection(l == DEPTH - 1, latent, l, x, mix_a, mix_h, lru[0], p['w_out'], p['mod'], p['final_g'])
    return x, kv, hl


def kernel(x_prompt, x_sample, cache_k, cache_v, state_lru, c, c_ctx, norm_g, w_ada, b_ada, w_in, q_norm_g, k_norm_g,
           hy_short_w, hy_short_b, hy_filt_w1, hy_filt_b1, hy_filt_w2, hy_filt_b2, hy_filt_w3, hy_filt_freq,
           hy_filt_decay, hy_bias, lru_conv_w, lru_conv_b, lru_wa, lru_ba, lru_wx, lru_bx, lru_lambda, w_out, final_g):
    batch, seq, _ = x_prompt.shape
    dec_batch, dec_seq, _ = x_sample.shape
    past = cache_k.shape[2]

    cvecs = jnp.concatenate([c_ctx[None, :], c, jnp.zeros((MOD_ROWS - 1 - dec_batch, D_MODEL), F32)], axis=0)
    mod = _modulation(cvecs, w_ada, b_ada).reshape(DEPTH, MOD_ROWS, 3, D_MODEL)

    w1p = jnp.pad(hy_filt_w1, ((0, 0), (0, FEAT_PAD - HY_EMB), (0, 0)))
    dft, filt = {}, {}
    for L in (seq, dec_seq):
        fwd, inv = (jnp.asarray(m) for m in _dft_tables(L))
        f_hi, f_lo = _split_bf16(fwd)
        dft[L] = (f_hi, inv.astype(BF16))
        filt[L] = _filter_spectrum(L, jnp.asarray(_hyena_feats(L)), f_hi, f_lo, w1p, hy_filt_b1[:, None, :],
                                   hy_filt_w2, hy_filt_b2[:, None, :], hy_filt_w3, hy_filt_freq,
                                   hy_filt_decay[:, None, :])

    params = {
        'mod': mod,
        'norm_g': norm_g[:, None, :],
        'w_in': w_in.astype(BF16),
        'gq': jnp.tile(q_norm_g, (1, N_Q_HEADS))[:, None, :],
        'gk': jnp.tile(k_norm_g, (1, N_KV_HEADS))[:, None, :],
        'e_avg': jnp.asarray(_head_average_matrix()).astype(BF16),
        'short_w': hy_short_w, 'short_b': hy_short_b[:, None, :], 'hy_bias': hy_bias[:, None, :],
        'conv_w': lru_conv_w, 'conv_b': lru_conv_b[:, None, :],
        'wa': lru_wa, 'ba': lru_ba, 'wx': lru_wx, 'bx': lru_bx, 'lam': lru_lambda,
        'w_out': w_out.astype(BF16),
        'final_g': final_g[None, :],
    }

    y_prompt, kv, new_lru = _mixer_pass(False, x_prompt, params, filt[seq], dft[seq])
    new_k = kv[0].reshape(batch, DEPTH, seq, N_KV_HEADS, HEAD_DIM)
    new_v = kv[1].reshape(batch, DEPTH, seq, N_KV_HEADS, HEAD_DIM)

    rope = tuple(jnp.asarray(t) for t in _rope_tables(dec_seq))
    cache = (cache_k.reshape(dec_batch, DEPTH, past, KV_W), cache_v.reshape(dec_batch, DEPTH, past, KV_W))
    y_sample, _, _ = _mixer_pass(True, x_sample, params, filt[dec_seq], dft[dec_seq], rope, cache, state_lru)
    return (y_prompt, y_sample, new_k, new_v, new_lru)
```

```python
import functools
import math

import numpy as np
import jax
import jax.numpy as jnp
from jax import lax
from jax.experimental import pallas as pl
from jax.experimental.pallas import tpu as pltpu

F32 = jnp.float32
BF16 = jnp.bfloat16

D_MODEL = 1024
DEPTH = 2
GRID_W = 64
HEAD_DIM = 64
N_Q_HEADS = 8
N_KV_HEADS = 2
Q_PER_KV = N_Q_HEADS // N_KV_HEADS
ATTN_W = N_Q_HEADS * HEAD_DIM
KV_W = N_KV_HEADS * HEAD_DIM
ROPE_THETA = 10000.0
ATTN_SCALE = HEAD_DIM ** -0.5
HY_W = 512
HY_BANDS = 16
HY_EMB = 2 * HY_BANDS + 1
HY_FH = 64
LRU_W = 512
LRU_BLOCKS = 8
LRU_BS = LRU_W // LRU_BLOCKS
LRU_C = 8.0
EPS = 1e-6

ATTN_COLS = ATTN_W + 2 * KV_W + ATTN_W
D_MIX = ATTN_W + HY_W + LRU_W

LANES = 128
SUBLANES = 8
MXU_DIM = 256
COL_BLOCK = MXU_DIM
HY_BLOCK0 = ATTN_COLS // COL_BLOCK
LRU_BLOCK0 = (ATTN_COLS + 4 * HY_W) // COL_BLOCK
Q_CHUNK = 256
Q_CHUNK_LATENT = 128
HEAD_PAIRS = Q_PER_KV // 2
OUT_ROWS = 1024
FEAT_PAD = 128
MOD_ROWS = 8
VMEM_LIMIT = 56 * 1024 * 1024


def _cparams(n_axes):
    return pltpu.CompilerParams(dimension_semantics=("arbitrary",) * n_axes, vmem_limit_bytes=VMEM_LIMIT)


def _split_bf16(a):
    hi = a.astype(BF16)
    lo = (a - hi.astype(F32)).astype(BF16)
    return hi, lo


def _dot(a, b):
    return jnp.dot(a, b, preferred_element_type=F32)


def _dot3(a, b):
    a_hi, a_lo = _split_bf16(a)
    b_hi, b_lo = _split_bf16(b)
    return _dot(a_hi, b_hi) + _dot(a_hi, b_lo) + _dot(a_lo, b_hi)


def _dot3_const(m_hi, m_lo, b):
    b_hi, b_lo = _split_bf16(b)
    return _dot(m_hi, b_hi) + _dot(m_hi, b_lo) + _dot(m_lo, b_hi)


def _silu(x):
    return x * jax.nn.sigmoid(x)


@functools.lru_cache(maxsize=None)
def _dft_tables(L):
    n = 2 * L
    t = np.arange(L)
    kt = (t[:, None] * t[None, :]) % n
    ang = 2.0 * np.pi * kt / n
    cosm, sinm = np.cos(ang), np.sin(ang)
    alt = (-1.0) ** t
    f_s = -sinm
    f_s[0, :] = alt
    fwd = np.concatenate([cosm, f_s], axis=0)
    g_c = 2.0 * cosm.T / n
    g_c[:, 0] = 1.0 / n
    g_s = -2.0 * sinm.T / n
    g_s[:, 0] = alt / n
    inv = np.concatenate([g_c, g_s], axis=1)
    return fwd.astype(np.float32), inv.astype(np.float32)


@functools.lru_cache(maxsize=None)
def _hyena_feats(L):
    t = np.linspace(0.0, 1.0, L)[:, None]
    w = 2.0 * math.pi * np.arange(L)[:, None] / L
    f = np.linspace(1e-4, HY_BANDS - 1, HY_BANDS)[None, :]
    out = np.zeros((L, FEAT_PAD), np.float32)
    out[:, :HY_EMB] = np.concatenate([t, np.cos(f * w), -np.sin(f * w)], axis=-1)
    return out


@functools.lru_cache(maxsize=None)
def _rope_tables(L):
    rows = L // GRID_W
    row = np.repeat(np.arange(rows), GRID_W).astype(np.float64)
    col = np.tile(np.arange(GRID_W), rows).astype(np.float64)
    n_freq = HEAD_DIM // 4
    inv = ROPE_THETA ** (-np.arange(n_freq) / n_freq)
    ang = np.concatenate([row[:, None] * inv, col[:, None] * inv], axis=-1)
    cos = np.repeat(np.cos(ang), 2, axis=-1)
    sin = np.repeat(np.sin(ang), 2, axis=-1) * np.tile(np.array([-1.0, 1.0]), HEAD_DIM // 2)
    reps = LANES // HEAD_DIM
    return np.tile(cos, (1, reps)).astype(np.float32), np.tile(sin, (1, reps)).astype(np.float32)


@functools.lru_cache(maxsize=None)
def _head_average_matrix():
    return np.kron(np.eye(MXU_DIM // HEAD_DIM), np.full((HEAD_DIM, HEAD_DIM), 1.0 / HEAD_DIM)).astype(np.float32)


def _mod_kernel(c_ref, w_ref, b_ref, o_ref):
    s = _silu(c_ref[...])
    o_ref[0] = _dot3(s, w_ref[0]) + b_ref[0]


def _modulation(cvecs, w_ada, b_ada):
    tn = D_MODEL
    return pl.pallas_call(
        _mod_kernel,
        grid=(DEPTH, 3 * D_MODEL // tn),
        in_specs=[
            pl.BlockSpec((MOD_ROWS, D_MODEL), lambda l, j: (0, 0)),
            pl.BlockSpec((1, D_MODEL, tn), lambda l, j: (l, 0, j)),
            pl.BlockSpec((1, 1, tn), lambda l, j: (l, 0, j)),
        ],
        out_specs=pl.BlockSpec((1, MOD_ROWS, tn), lambda l, j: (l, 0, j)),
        out_shape=jax.ShapeDtypeStruct((DEPTH, MOD_ROWS, 3 * D_MODEL), F32),
        compiler_params=_cparams(2),
        name="adaln_mod",
    )(cvecs, w_ada, b_ada.reshape(DEPTH, 1, 3 * D_MODEL))


def _filter_kernel(L, feats_ref, w1_ref, b1_ref, w2_ref, b2_ref, w3_ref, freq_ref, decay_ref,
                   fhi_ref, flo_ref, a_ref, b_ref, d_ref):
    feats = feats_ref[...]
    t = feats[:, 0:1]
    freq = freq_ref[0]
    hdn = jnp.sin(freq[0:1] * (_dot3(feats, w1_ref[0]) + b1_ref[0]))
    hdn = jnp.sin(freq[1:2] * (_dot3(hdn, w2_ref[0]) + b2_ref[0]))
    h = _dot3(hdn, w3_ref[0]) * jnp.exp(-t * decay_ref[0])
    row = lax.broadcasted_iota(jnp.int32, (L, HY_W), 0)
    h_fwd = h[:, :HY_W]
    h_bwd = jnp.where(row >= 1, h[:, HY_W:], 0.0)
    inv_norm = 1.0 / jnp.sum(jnp.abs(h_fwd) + jnp.abs(h_bwd), axis=0, keepdims=True)
    even = h_fwd + h_bwd
    odd = h_fwd - h_bwd
    h_re = _dot3_const(fhi_ref[0:L, :], flo_ref[0:L, :], even) * inv_norm
    h_im = _dot3_const(fhi_ref[L:2 * L, :], flo_ref[L:2 * L, :], odd) * inv_norm
    alt = jnp.where(row % 2 == 0, 1.0, -1.0)
    nyq = jnp.sum(even * alt, axis=0, keepdims=True) * inv_norm
    a_ref[0] = h_re
    b_ref[0] = jnp.where(row >= 1, h_im, 0.0)
    d_ref[0] = jnp.where(row >= 1, h_re, nyq)


def _filter_spectrum(L, feats, f_hi, f_lo, w1p, b1, w2, b2, w3, freq, decay):
    full = lambda shape: pl.BlockSpec(shape, lambda l: (0,) * len(shape))
    per_layer = lambda shape: pl.BlockSpec((1,) + shape, lambda l: (l,) + (0,) * len(shape))
    out = jax.ShapeDtypeStruct((DEPTH, L, HY_W), F32)
    return pl.pallas_call(
        functools.partial(_filter_kernel, L),
        grid=(DEPTH,),
        in_specs=[
            full((L, FEAT_PAD)),
            per_layer((FEAT_PAD, HY_FH)), per_layer((1, HY_FH)),
            per_layer((HY_FH, HY_FH)), per_layer((1, HY_FH)),
            per_layer((HY_FH, 2 * HY_W)), per_layer((2, HY_FH)), per_layer((1, 2 * HY_W)),
            full((2 * L, L)), full((2 * L, L)),
        ],
        out_specs=[per_layer((L, HY_W))] * 3,
        out_shape=[out, out, out],
        compiler_params=_cparams(1),
        name=f"hyena_filter_L{L}",
    )(feats, w1p, b1, w2, b2, w3, freq, decay, f_hi, f_lo)


def _rope(x, cos, sin):
    lane = lax.broadcasted_iota(jnp.int32, x.shape, 1)
    partner = jnp.where(lane % 2 == 0, pltpu.roll(x, LANES - 1, 1), pltpu.roll(x, 1, 1))
    return x * cos + partner * sin


def _head_mean_square(x, e):
    sq = (x * x).astype(BF16)
    width = x.shape[1]
    if width <= MXU_DIM:
        return _dot(sq, e[:width, :width])
    parts = [_dot(sq[:, i:i + MXU_DIM], e) for i in range(0, width, MXU_DIM)]
    return jnp.concatenate(parts, axis=1)


def _store_kv_heads(k_dst, v_dst, k, v):
    ones = jnp.ones((v.shape[0], HEAD_DIM), BF16)
    for j in range(N_KV_HEADS):
        js = slice(j * HEAD_DIM, (j + 1) * HEAD_DIM)
        vj = v[:, js].astype(BF16)
        k_dst[j] = k[:, js].astype(BF16)
        v_dst[j, 0, :, 0:HEAD_DIM] = vj
        v_dst[j, 0, :, HEAD_DIM:2 * HEAD_DIM] = ones
        v_dst[j, 1, :, 0:HEAD_DIM] = ones
        v_dst[j, 1, :, HEAD_DIM:2 * HEAD_DIM] = vj


def _attn_kernel(latent, aliased, layer, L, nb, qc, past, *refs):
    x_ref, mod_ref, ng_ref, w_ref, gq_ref, gk_ref, e_ref = refs[:7]
    refs = refs[7:]
    if latent:
        (cos_ref, sin_ref, ck_ref, cv_ref, h_out, mix_out,
         q_s, k_s, v_s, g_s, o_s, s_s, mx_s, acc_s, ck_s, cv_s) = refs
    else:
        if aliased:
            refs = refs[2:]
        h_out, mix_out, k_out, v_out, q_s, k_s, v_s, g_s, o_s, s_s, mx_s, acc_s = refs
    rows = nb * L

    x = x_ref[...].reshape(rows, D_MODEL)
    m = mod_ref[0, 0]
    shift, scale = m[0:1], m[1:2]
    ms = jnp.mean(x * x, axis=-1, keepdims=True)
    h = (x * lax.rsqrt(ms + EPS) * ng_ref[0]) * (1.0 + scale) + shift
    hb = h.astype(BF16)
    h_out[...] = hb.reshape(nb, L, D_MODEL)

    u = _dot(hb, w_ref[0])
    q = u[:, :ATTN_W]
    k = u[:, ATTN_W:ATTN_W + KV_W]
    v = u[:, ATTN_W + KV_W:ATTN_W + 2 * KV_W]
    g_s[...] = u[:, ATTN_W + 2 * KV_W:]
    e = e_ref[...]
    qn = q * lax.rsqrt(_head_mean_square(q, e) + EPS) * gq_ref[0]
    kn = k * lax.rsqrt(_head_mean_square(k, e) + EPS) * gk_ref[0]
    if latent:
        cos, sin = cos_ref[...], sin_ref[...]
        qn = jnp.concatenate([_rope(qn[:, i:i + LANES], cos, sin) for i in range(0, ATTN_W, LANES)], axis=1)
        kn = _rope(kn, cos, sin)
        _store_kv_heads(ck_s, cv_s, ck_ref[0, 0], cv_ref[0, 0])
    else:
        for b in range(nb):
            kb, vb = kn[b * L:(b + 1) * L], v[b * L:(b + 1) * L]
            if aliased:
                k_out[b, 0] = kb
                v_out[b, 0] = vb
            else:
                for slot in range(DEPTH):
                    k_out[b, slot] = kb if slot == layer else jnp.zeros_like(kb)
                    v_out[b, slot] = vb if slot == layer else jnp.zeros_like(vb)
    qb = (qn * ATTN_SCALE).astype(BF16)
    for hd in range(N_Q_HEADS):
        j, g = divmod(hd, Q_PER_KV)
        q_s[j * Q_PER_KV + (g % 2) * HEAD_PAIRS + g // 2] = qb[:, hd * HEAD_DIM:(hd + 1) * HEAD_DIM]
    _store_kv_heads(k_s, v_s, kn, v)

    contract_last = (((1,), (1,)), ((), ()))
    chunks_per_seq = L // qc
    lane = lax.broadcasted_iota(jnp.int32, (qc, LANES), 1)
    groups = [(j, e) for j in range(N_KV_HEADS) for e in range(2)]

    def chunk(c, carry):
        r0 = pl.multiple_of(c * qc, qc)
        k0 = pl.multiple_of((c // chunks_per_seq) * L, L)
        for idx, (j, e) in enumerate(groups):
            slot = j * Q_PER_KV + e * HEAD_PAIRS
            qh = q_s[slot:slot + HEAD_PAIRS, pl.ds(r0, qc), :].reshape(HEAD_PAIRS * qc, HEAD_DIM)
            s = lax.dot_general(qh, k_s[j, pl.ds(k0, L), :], contract_last, preferred_element_type=F32)
            mx = jnp.max(s, axis=-1, keepdims=True)
            if latent:
                s0 = lax.dot_general(qh, ck_s[j], contract_last, preferred_element_type=F32)
                mx = jnp.maximum(mx, jnp.max(s0, axis=-1, keepdims=True))
                s_s[idx, :, 0:past] = s0
                s_s[idx, :, past:] = s
            else:
                s_s[idx] = s
            mx_s[idx] = jnp.broadcast_to(mx, (HEAD_PAIRS * qc, LANES))
        for idx, (j, e) in enumerate(groups):
            p = jnp.exp(s_s[idx] - jnp.tile(mx_s[idx], (1, (past + L) // LANES))).astype(BF16)
            if latent:
                acc = _dot(p[:, :past], cv_s[j, e]) + _dot(p[:, past:], v_s[j, e, pl.ds(k0, L), :])
            else:
                acc = _dot(p, v_s[j, e, pl.ds(k0, L), :])
            acc_s[idx] = acc
        for j in range(N_KV_HEADS):
            for t in range(HEAD_PAIRS):
                a_even = acc_s[2 * j, t * qc:(t + 1) * qc, :]
                a_odd = acc_s[2 * j + 1, t * qc:(t + 1) * qc, :]
                out = jnp.where(lane < HEAD_DIM, a_even, a_odd)
                den = pltpu.roll(jnp.where(lane < HEAD_DIM, a_odd, a_even), HEAD_DIM, 1)
                blk = j * HEAD_PAIRS + t
                o_s[pl.ds(r0, qc), blk * LANES:(blk + 1) * LANES] = out / den
        return carry

    lax.fori_loop(0, nb * chunks_per_seq, chunk, 0)
    mix_out[...] = (o_s[...] * _silu(g_s[...])).astype(BF16).reshape(nb, L, ATTN_W)


def _attention_branch(latent, layer, nb, x, mod, norm_g, w_in_b, gq, gk, e_avg, rope=None, cache=None, kv_prev=None):
    B, L, _ = x.shape
    rows = nb * L
    assert nb == 1 or not latent
    full = lambda shape: pl.BlockSpec(shape, lambda b: (0,) * len(shape))
    per_b = lambda shape: pl.BlockSpec((nb,) + shape, lambda b: (b,) + (0,) * len(shape))
    per_layer = lambda shape: pl.BlockSpec((1,) + shape, lambda b: (layer,) + (0,) * len(shape))
    mod_spec = pl.BlockSpec((1, 1, 3, D_MODEL), (lambda b: (layer, 1 + b, 0, 0)) if latent else (lambda b: (layer, 0, 0, 0)))
    in_specs = [per_b((L, D_MODEL)), mod_spec, per_layer((1, D_MODEL)), per_layer((D_MODEL, ATTN_COLS)),
                per_layer((1, ATTN_W)), per_layer((1, KV_W)), full((MXU_DIM, MXU_DIM))]
    args = [x, mod, norm_g, w_in_b, gq, gk, e_avg]
    out_specs = [per_b((L, D_MODEL)), per_b((L, ATTN_W))]
    out_shape = [jax.ShapeDtypeStruct((B, L, D_MODEL), BF16), jax.ShapeDtypeStruct((B, L, ATTN_W), BF16)]
    qc = Q_CHUNK_LATENT if latent else Q_CHUNK
    past = cache[0].shape[2] if latent else 0
    n_groups = 2 * N_KV_HEADS
    scratch = [pltpu.VMEM((N_Q_HEADS, rows, HEAD_DIM), BF16), pltpu.VMEM((N_KV_HEADS, rows, HEAD_DIM), BF16),
               pltpu.VMEM((N_KV_HEADS, 2, rows, 2 * HEAD_DIM), BF16),
               pltpu.VMEM((rows, ATTN_W), F32), pltpu.VMEM((rows, ATTN_W), F32),
               pltpu.VMEM((n_groups, HEAD_PAIRS * qc, past + L), F32),
               pltpu.VMEM((n_groups, HEAD_PAIRS * qc, LANES), F32),
               pltpu.VMEM((n_groups, HEAD_PAIRS * qc, 2 * HEAD_DIM), F32)]
    aliases = {}
    if latent:
        cache_k, cache_v = cache
        cache_spec = pl.BlockSpec((1, 1, past, KV_W), lambda b: (b, layer, 0, 0))
        in_specs += [full((L, LANES)), full((L, LANES)), cache_spec, cache_spec]
        args += [rope[0], rope[1], cache_k, cache_v]
        scratch += [pltpu.VMEM((N_KV_HEADS, past, HEAD_DIM), BF16),
                    pltpu.VMEM((N_KV_HEADS, 2, past, 2 * HEAD_DIM), BF16)]
    else:
        if kv_prev is not None:
            kv_spec = pl.BlockSpec((nb, 1, L, KV_W), lambda b: (b, layer, 0, 0))
        else:
            kv_spec = pl.BlockSpec((nb, DEPTH, L, KV_W), lambda b: (b, 0, 0, 0))
        kv_shape = jax.ShapeDtypeStruct((B, DEPTH, L, KV_W), F32)
        out_specs += [kv_spec, kv_spec]
        out_shape += [kv_shape, kv_shape]
        if kv_prev is not None:
            in_specs += [pl.BlockSpec(memory_space=pl.ANY)] * 2
            aliases = {len(args): 2, len(args) + 1: 3}
            args += list(kv_prev)
    return pl.pallas_call(
        functools.partial(_attn_kernel, latent, kv_prev is not None, layer, L, nb, qc, past),
        grid=(B // nb,),
        in_specs=in_specs,
        out_specs=out_specs,
        out_shape=out_shape,
        scratch_shapes=scratch,
        input_output_aliases=aliases,
        compiler_params=_cparams(1),
        name=f"attn_branch_L{L}",
    )(*args)


def _shifted(x, offset, period):
    rows = x.shape[0]
    t = lax.broadcasted_iota(jnp.int32, x.shape, 0)
    if rows != period:
        t = t % period
    rolled = pltpu.roll(x, (-offset) % rows, 0)
    valid = (t >= -offset) if offset < 0 else (t < period - offset)
    return jnp.where(valid, rolled, 0.0)


def _hyena_kernel(L, nb, h_ref, w0_ref, w1_ref, w2_ref, wg_ref, sw0_ref, sw1_ref, sw2_ref, sb0_ref, sb1_ref, sb2_ref,
                  fa_ref, fb_ref, fd_ref, bias_ref, fwd_ref, inv_ref, mix_out):
    hb = h_ref[...].reshape(nb * L, D_MODEL)

    def short_conv(w_ref, sw_ref, sb_ref):
        xs = _dot(hb, w_ref[0])
        w = sw_ref[0]
        return sb_ref[0] + _shifted(xs, -1, L) * w[0:1] + xs * w[1:2] + _shifted(xs, 1, L) * w[2:3]

    x0 = short_conv(w0_ref, sw0_ref, sb0_ref)
    x1 = short_conv(w1_ref, sw1_ref, sb1_ref)
    hv = short_conv(w2_ref, sw2_ref, sb2_ref)
    gated = x0 * _silu(_dot(hb, wg_ref[0]))
    z = x1 * hv
    zb = z.astype(BF16)
    fa, fb, fd = fa_ref[0], fb_ref[0], fd_ref[0]
    for b in range(nb):
        rs = slice(b * L, (b + 1) * L)
        zf = _dot(fwd_ref[...], zb[rs])
        re, im = zf[:L], zf[L:]
        y_re = re * fa - im * fb
        y_im = re * fb + im * fd
        y = _dot(inv_ref[:, :L], y_re.astype(BF16)) + _dot(inv_ref[:, L:], y_im.astype(BF16))
        y = y + z[rs] * bias_ref[0]
        mix_out[b] = (y * gated[rs]).astype(BF16)


def _hyena_branch(layer, nb, hmod, w_in_b, short_w, short_b, filt, bias, fwd, inv):
    B, L, _ = hmod.shape
    cb = COL_BLOCK
    nt = HY_W // cb
    fa, fb, fd = filt
    w_spec = lambda part: pl.BlockSpec((1, D_MODEL, cb), lambda i, b: (layer, 0, HY_BLOCK0 + part * nt + i))
    sw_spec = lambda part: pl.BlockSpec((1, 3, cb), lambda i, b: (layer, 0, part * nt + i))
    sb_spec = lambda part: pl.BlockSpec((1, 1, cb), lambda i, b: (layer, 0, part * nt + i))
    filt_spec = pl.BlockSpec((1, L, cb), lambda i, b: (layer, 0, i))
    return pl.pallas_call(
        functools.partial(_hyena_kernel, L, nb),
        grid=(nt, B // nb),
        in_specs=[
            pl.BlockSpec((nb, L, D_MODEL), lambda i, b: (b, 0, 0)),
            w_spec(0), w_spec(1), w_spec(2), w_spec(3),
            sw_spec(0), sw_spec(1), sw_spec(2),
            sb_spec(0), sb_spec(1), sb_spec(2),
            filt_spec, filt_spec, filt_spec,
            pl.BlockSpec((1, 1, cb), lambda i, b: (layer, 0, i)),
            pl.BlockSpec((2 * L, L), lambda i, b: (0, 0)),
            pl.BlockSpec((L, 2 * L), lambda i, b: (0, 0)),
        ],
        out_specs=pl.BlockSpec((nb, L, cb), lambda i, b: (b, 0, i)),
        out_shape=jax.ShapeDtypeStruct((B, L, HY_W), BF16),
        compiler_params=_cparams(2),
        name=f"hyena_branch_L{L}",
    )(hmod, w_in_b, w_in_b, w_in_b, w_in_b, short_w, short_w, short_w, short_b, short_b, short_b,
      fa, fb, fd, bias, fwd, inv)


def _lru_kernel(latent, aliased, layer, L, nb, *refs):
    (h_ref, wx0_ref, wx1_ref, wg0_ref, wg1_ref, cw_ref, cb_ref, wa_ref, ba_ref, wxg_ref, bx_ref, lam_ref) = refs[:12]
    refs = refs[12:]
    if latent:
        h0_ref, mix_out, gate_w_s, a_s, b_s, y_s, g_s = refs
    else:
        if aliased:
            refs = refs[1:]
        mix_out, hl_out, gate_w_s, a_s, b_s, y_s, g_s = refs
    half_w = LRU_W // 2
    blocks_per_half = half_w // LRU_BS
    n_tiles = L // SUBLANES

    @pl.when(pl.program_id(0) == 0)
    def _build_gate_weights():
        gate_w_s[...] = jnp.zeros(gate_w_s.shape, BF16)
        for half in range(2):
            for kind, (w_ref, d) in enumerate(((wa_ref, 0), (wxg_ref, 0), (wa_ref, 1), (wxg_ref, 1))):
                for j in range(blocks_per_half):
                    blk = w_ref[0, d, half * blocks_per_half + j].astype(BF16)
                    gate_w_s[half, j * LRU_BS:(j + 1) * LRU_BS,
                             kind * half_w + j * LRU_BS:kind * half_w + (j + 1) * LRU_BS] = blk

    hb = h_ref[...].reshape(nb * L, D_MODEL)
    xs = jnp.concatenate([_dot(hb, wx0_ref[0]), _dot(hb, wx1_ref[0])], axis=1)
    g_s[:, :half_w] = _dot(hb, wg0_ref[0])
    g_s[:, half_w:] = _dot(hb, wg1_ref[0])
    cw = cw_ref[0]
    xc = (cb_ref[0] + _shifted(xs, -1, L) * cw[0:1] + xs * cw[1:2] + _shifted(xs, 1, L) * cw[2:3]
          + _shifted(xs, 2, L) * cw[3:4])
    xcb = xc.astype(BF16)
    sp = jax.nn.softplus(-lam_ref[0])
    ba, bx = ba_ref[0], bx_ref[0]
    for half in range(2):
        cols = slice(half * half_w, (half + 1) * half_w)
        bias = jnp.concatenate([ba[0:1, cols], bx[0:1, cols], ba[1:2, cols], bx[1:2, cols]], axis=1)
        gates = _dot(xcb[:, cols], gate_w_s[half]) + bias
        xh = xc[:, cols]
        for d in range(2):
            r = jax.nn.sigmoid(gates[:, (2 * d) * half_w:(2 * d + 1) * half_w])
            i = jax.nn.sigmoid(gates[:, (2 * d + 1) * half_w:(2 * d + 2) * half_w])
            log_a = -LRU_C * r * sp[d:d + 1, cols]
            a = jnp.exp(log_a)
            bcoef = jnp.sqrt(-jnp.tanh(log_a) * (a * a + 1.0)) * (i * xh)
            a_s[d, :, :, cols] = a.reshape(nb * n_tiles, SUBLANES, half_w)
            b_s[d, :, :, cols] = bcoef.reshape(nb * n_tiles, SUBLANES, half_w)

    if latent:
        init = []
        for b in range(nb):
            h0 = h0_ref[b, 0]
            init += [h0[0:1], h0[1:2]]
    else:
        init = [jnp.zeros((1, LRU_W), F32)] * (2 * nb)

    def tile_step(i, carry):
        carry = list(carry)
        for j in range(SUBLANES):
            for b in range(nb):
                tf = b * n_tiles + i
                tb = b * n_tiles + (n_tiles - 1 - i)
                jb = SUBLANES - 1 - j
                hf = a_s[0, tf, j:j + 1, :] * carry[2 * b] + b_s[0, tf, j:j + 1, :]
                y_s[0, tf, j:j + 1, :] = hf
                hbk = a_s[1, tb, jb:jb + 1, :] * carry[2 * b + 1] + b_s[1, tb, jb:jb + 1, :]
                y_s[1, tb, jb:jb + 1, :] = hbk
                carry[2 * b], carry[2 * b + 1] = hf, hbk
        return tuple(carry)

    last = lax.fori_loop(0, n_tiles, tile_step, tuple(init))
    if not latent:
        if not aliased:
            hl_out[...] = jnp.zeros(hl_out.shape, F32)
        slot = 0 if aliased else layer
        for b in range(nb):
            hl_out[b, slot, 0:1, :] = last[2 * b]
            hl_out[b, slot, 1:2, :] = last[2 * b + 1]
    y = (y_s[0] + y_s[1]).reshape(nb * L, LRU_W)
    mix_out[...] = (y * _silu(g_s[...])).astype(BF16).reshape(nb, L, LRU_W)


def _lru_branch(latent, layer, nb, hmod, w_in_b, conv_w, conv_b, wa, ba, wx, bx, lam, state=None, hl_prev=None):
    B, L, _ = hmod.shape
    cb = COL_BLOCK
    half_w = LRU_W // 2
    rows = nb * L
    per_layer = lambda shape: pl.BlockSpec((1,) + shape, lambda b: (layer,) + (0,) * len(shape))
    w_spec = lambda j: pl.BlockSpec((1, D_MODEL, cb), lambda b: (layer, 0, LRU_BLOCK0 + j))
    gate_blocks = (2, LRU_BLOCKS, LRU_BS, LRU_BS)
    in_specs = [pl.BlockSpec((nb, L, D_MODEL), lambda b: (b, 0, 0)), w_spec(0), w_spec(1), w_spec(2), w_spec(3),
                per_layer((4, LRU_W)), per_layer((1, LRU_W)), per_layer(gate_blocks), per_layer((2, LRU_W)),
                per_layer(gate_blocks), per_layer((2, LRU_W)), per_layer((2, LRU_W))]
    args = [hmod, w_in_b, w_in_b, w_in_b, w_in_b, conv_w, conv_b, wa, ba, wx, bx, lam]
    out_specs = [pl.BlockSpec((nb, L, LRU_W), lambda b: (b, 0, 0))]
    out_shape = [jax.ShapeDtypeStruct((B, L, LRU_W), BF16)]
    aliases = {}
    if latent:
        in_specs.append(pl.BlockSpec((nb, 1, 2, LRU_W), lambda b: (b, layer, 0, 0)))
        args.append(state)
    else:
        if hl_prev is not None:
            out_specs.append(pl.BlockSpec((nb, 1, 2, LRU_W), lambda b: (b, layer, 0, 0)))
        else:
            out_specs.append(pl.BlockSpec((nb, DEPTH, 2, LRU_W), lambda b: (b, 0, 0, 0)))
        out_shape.append(jax.ShapeDtypeStruct((B, DEPTH, 2, LRU_W), F32))
        if hl_prev is not None:
            in_specs.append(pl.BlockSpec(memory_space=pl.ANY))
            aliases = {len(args): 1}
            args.append(hl_prev)
    scan_buf = pltpu.VMEM((2, rows // SUBLANES, SUBLANES, LRU_W), F32)
    return pl.pallas_call(
        functools.partial(_lru_kernel, latent, hl_prev is not None, layer, L, nb),
        grid=(B // nb,),
        in_specs=in_specs,
        out_specs=out_specs,
        out_shape=out_shape,
        scratch_shapes=[pltpu.VMEM((2, half_w, 4 * half_w), BF16), scan_buf, scan_buf, scan_buf,
                        pltpu.VMEM((rows, LRU_W), F32)],
        input_output_aliases=aliases,
        compiler_params=_cparams(1),
        name=f"lru_branch_L{L}",
    )(*args)


def _out_kernel(final, x_ref, ma_ref, mh_ref, ml_ref, w_ref, mod_ref, fg_ref, o_ref):
    out = (_dot(ma_ref[...], w_ref[0, 0:ATTN_W, :]) + _dot(mh_ref[...], w_ref[0, ATTN_W:ATTN_W + HY_W, :])
           + _dot(ml_ref[...], w_ref[0, ATTN_W + HY_W:, :]))
    gate = mod_ref[0, 0][2:3]
    xn = x_ref[...] + gate * out
    if final:
        ms = jnp.mean(xn * xn, axis=-1, keepdims=True)
        xn = xn * lax.rsqrt(ms + EPS) * fg_ref[...]
    o_ref[...] = xn


def _out_projection(final, latent, layer, x, mix_a, mix_h, mix_l, w_out_b, mod, final_g):
    B, L, _ = x.shape
    tm = OUT_ROWS
    rows = B * L
    tile = lambda w: pl.BlockSpec((tm, w), lambda i: (i, 0))
    mod_row = (lambda i: (layer, 1 + (i * tm) // L, 0, 0)) if latent else (lambda i: (layer, 0, 0, 0))
    y = pl.pallas_call(
        functools.partial(_out_kernel, final),
        grid=(rows // tm,),
        in_specs=[tile(D_MODEL), tile(ATTN_W), tile(HY_W), tile(LRU_W),
                  pl.BlockSpec((1, D_MIX, D_MODEL), lambda i: (layer, 0, 0)),
                  pl.BlockSpec((1, 1, 3, D_MODEL), mod_row), pl.BlockSpec((1, D_MODEL), lambda i: (0, 0))],
        out_specs=tile(D_MODEL),
        out_shape=jax.ShapeDtypeStruct((rows, D_MODEL), F32),
        compiler_params=_cparams(1),
        name=f"out_proj_L{L}",
    )(x.reshape(rows, D_MODEL), mix_a.reshape(rows, ATTN_W), mix_h.reshape(rows, HY_W), mix_l.reshape(rows, LRU_W),
      w_out_b, mod, final_g)
    return y.reshape(B, L, D_MODEL)


def _mixer_pass(latent, x, p, filt, dft, rope=None, cache=None, state=None):
    B, L, _ = x.shape
    fwd, inv = dft
    nb = 1 if latent else 4
    kv, hl = None, None
    for l in range(DEPTH):
        outs = _attention_branch(latent, l, nb, x, p['mod'], p['norm_g'], p['w_in'], p['gq'], p['gk'], p['e_avg'],
                                 rope, cache, kv)
        hmod, mix_a = outs[0], outs[1]
        if not latent:
            kv = (outs[2], outs[3])
        mix_h = _hyena_branch(l, nb, hmod, p['w_in'], p['short_w'], p['short_b'], filt, p['hy_bias'], fwd, inv)
        lru = _lru_branch(latent, l, nb, hmod, p['w_in'], p['conv_w'], p['conv_b'], p['wa'], p['ba'], p['wx'],
                          p['bx'], p['lam'], state, hl)
        if not latent:
            hl = lru[1]
        x = _out_projection(l == DEPTH - 1, latent, l, x, mix_a, mix_h, lru[0], p['w_out'], p['mod'], p['final_g'])
    return x, kv, hl


def kernel(x_prompt, x_sample, cache_k, cache_v, state_lru, c, c_ctx, norm_g, w_ada, b_ada, w_in, q_norm_g, k_norm_g,
           hy_short_w, hy_short_b, hy_filt_w1, hy_filt_b1, hy_filt_w2, hy_filt_b2, hy_filt_w3, hy_filt_freq,
           hy_filt_decay, hy_bias, lru_conv_w, lru_conv_b, lru_wa, lru_ba, lru_wx, lru_bx, lru_lambda, w_out, final_g):
    batch, seq, _ = x_prompt.shape
    dec_batch, dec_seq, _ = x_sample.shape
    past = cache_k.shape[2]

    cvecs = jnp.concatenate([c_ctx[None, :], c, jnp.zeros((MOD_ROWS - 1 - dec_batch, D_MODEL), F32)], axis=0)
    mod = _modulation(cvecs, w_ada, b_ada).reshape(DEPTH, MOD_ROWS, 3, D_MODEL)

    w1p = jnp.pad(hy_filt_w1, ((0, 0), (0, FEAT_PAD - HY_EMB), (0, 0)))
    dft, filt = {}, {}
    for L in (seq, dec_seq):
        fwd, inv = (jnp.asarray(m) for m in _dft_tables(L))
        f_hi, f_lo = _split_bf16(fwd)
        dft[L] = (f_hi, inv.astype(BF16))
        filt[L] = _filter_spectrum(L, jnp.asarray(_hyena_feats(L)), f_hi, f_lo, w1p, hy_filt_b1[:, None, :],
                                   hy_filt_w2, hy_filt_b2[:, None, :], hy_filt_w3, hy_filt_freq,
                                   hy_filt_decay[:, None, :])

    params = {
        'mod': mod,
        'norm_g': norm_g[:, None, :],
        'w_in': w_in.astype(BF16),
        'gq': jnp.tile(q_norm_g, (1, N_Q_HEADS))[:, None, :],
        'gk': jnp.tile(k_norm_g, (1, N_KV_HEADS))[:, None, :],
        'e_avg': jnp.asarray(_head_average_matrix()).astype(BF16),
        'short_w': hy_short_w, 'short_b': hy_short_b[:, None, :], 'hy_bias': hy_bias[:, None, :],
        'conv_w': lru_conv_w, 'conv_b': lru_conv_b[:, None, :],
        'wa': lru_wa, 'ba': lru_ba, 'wx': lru_wx, 'bx': lru_bx, 'lam': lru_lambda,
        'w_out': w_out.astype(BF16),
        'final_g': final_g[None, :],
    }

    y_prompt, kv, new_lru = _mixer_pass(False, x_prompt, params, filt[seq], dft[seq])
    new_k = kv[0].reshape(batch, DEPTH, seq, N_KV_HEADS, HEAD_DIM)
    new_v = kv[1].reshape(batch, DEPTH, seq, N_KV_HEADS, HEAD_DIM)

    rope = tuple(jnp.asarray(t) for t in _rope_tables(dec_seq))
    cache = (cache_k.reshape(dec_batch, DEPTH, past, KV_W), cache_v.reshape(dec_batch, DEPTH, past, KV_W))
    y_sample, _, _ = _mixer_pass(True, x_sample, params, filt[dec_seq], dft[dec_seq], rope, cache, state_lru)
    return (y_prompt, y_sample, new_k, new_v, new_lru)
```

```python
import functools
import math

import numpy as np
import jax
import jax.numpy as jnp
from jax import lax
from jax.experimental import pallas as pl
from jax.experimental.pallas import tpu as pltpu

F32 = jnp.float32
BF16 = jnp.bfloat16

D_MODEL = 1024
DEPTH = 2
GRID_W = 64
HEAD_DIM = 64
N_Q_HEADS = 8
N_KV_HEADS = 2
Q_PER_KV = N_Q_HEADS // N_KV_HEADS
ATTN_W = N_Q_HEADS * HEAD_DIM
KV_W = N_KV_HEADS * HEAD_DIM
ROPE_THETA = 10000.0
ATTN_SCALE = HEAD_DIM ** -0.5
HY_W = 512
HY_BANDS = 16
HY_EMB = 2 * HY_BANDS + 1
HY_FH = 64
LRU_W = 512
LRU_BLOCKS = 8
LRU_BS = LRU_W // LRU_BLOCKS
LRU_C = 8.0
EPS = 1e-6

ATTN_COLS = ATTN_W + 2 * KV_W + ATTN_W
D_MIX = ATTN_W + HY_W + LRU_W

LANES = 128
SUBLANES = 8
MXU_DIM = 256
COL_BLOCK = MXU_DIM
HY_BLOCK0 = ATTN_COLS // COL_BLOCK
LRU_BLOCK0 = (ATTN_COLS + 4 * HY_W) // COL_BLOCK
Q_CHUNK = 256
Q_CHUNK_LATENT = 128
HEAD_PAIRS = Q_PER_KV // 2
OUT_ROWS = 1024
FEAT_PAD = 128
MOD_ROWS = 8
VMEM_LIMIT = 56 * 1024 * 1024


def _cparams(n_axes):
    return pltpu.CompilerParams(dimension_semantics=("arbitrary",) * n_axes, vmem_limit_bytes=VMEM_LIMIT)


def _split_bf16(a):
    hi = a.astype(BF16)
    lo = (a - hi.astype(F32)).astype(BF16)
    return hi, lo


def _dot(a, b):
    return jnp.dot(a, b, preferred_element_type=F32)


def _dot3(a, b):
    a_hi, a_lo = _split_bf16(a)
    b_hi, b_lo = _split_bf16(b)
    return _dot(a_hi, b_hi) + _dot(a_hi, b_lo) + _dot(a_lo, b_hi)


def _dot3_const(m_hi, m_lo, b):
    b_hi, b_lo = _split_bf16(b)
    return _dot(m_hi, b_hi) + _dot(m_hi, b_lo) + _dot(m_lo, b_hi)


def _silu(x):
    half = 0.5 * x
    return half * jnp.tanh(half) + half


@functools.lru_cache(maxsize=None)
def _dft_tables(L):
    n = 2 * L
    t = np.arange(L)
    kt = (t[:, None] * t[None, :]) % n
    ang = 2.0 * np.pi * kt / n
    cosm, sinm = np.cos(ang), np.sin(ang)
    alt = (-1.0) ** t
    f_s = -sinm
    f_s[0, :] = alt
    fwd = np.concatenate([cosm, f_s], axis=0)
    g_c = 2.0 * cosm.T / n
    g_c[:, 0] = 1.0 / n
    g_s = -2.0 * sinm.T / n
    g_s[:, 0] = alt / n
    inv = np.concatenate([g_c, g_s], axis=1)
    return fwd.astype(np.float32), inv.astype(np.float32)


@functools.lru_cache(maxsize=None)
def _hyena_feats(L):
    t = np.linspace(0.0, 1.0, L)[:, None]
    w = 2.0 * math.pi * np.arange(L)[:, None] / L
    f = np.linspace(1e-4, HY_BANDS - 1, HY_BANDS)[None, :]
    out = np.zeros((L, FEAT_PAD), np.float32)
    out[:, :HY_EMB] = np.concatenate([t, np.cos(f * w), -np.sin(f * w)], axis=-1)
    return out


@functools.lru_cache(maxsize=None)
def _rope_tables(L):
    rows = L // GRID_W
    row = np.repeat(np.arange(rows), GRID_W).astype(np.float64)
    col = np.tile(np.arange(GRID_W), rows).astype(np.float64)
    n_freq = HEAD_DIM // 4
    inv = ROPE_THETA ** (-np.arange(n_freq) / n_freq)
    ang = np.concatenate([row[:, None] * inv, col[:, None] * inv], axis=-1)
    cos = np.repeat(np.cos(ang), 2, axis=-1)
    sin = np.repeat(np.sin(ang), 2, axis=-1) * np.tile(np.array([-1.0, 1.0]), HEAD_DIM // 2)
    reps = LANES // HEAD_DIM
    return np.tile(cos, (1, reps)).astype(np.float32), np.tile(sin, (1, reps)).astype(np.float32)


@functools.lru_cache(maxsize=None)
def _head_average_matrix():
    return np.kron(np.eye(MXU_DIM // HEAD_DIM), np.full((HEAD_DIM, HEAD_DIM), 1.0 / HEAD_DIM)).astype(np.float32)


def _mod_kernel(c_ref, w_ref, b_ref, o_ref):
    s = _silu(c_ref[...])
    o_ref[0] = _dot3(s, w_ref[0]) + b_ref[0]


def _modulation(cvecs, w_ada, b_ada):
    tn = D_MODEL
    return pl.pallas_call(
        _mod_kernel,
        grid=(DEPTH, 3 * D_MODEL // tn),
        in_specs=[
            pl.BlockSpec((MOD_ROWS, D_MODEL), lambda l, j: (0, 0)),
            pl.BlockSpec((1, D_MODEL, tn), lambda l, j: (l, 0, j)),
            pl.BlockSpec((1, 1, tn), lambda l, j: (l, 0, j)),
        ],
        out_specs=pl.BlockSpec((1, MOD_ROWS, tn), lambda l, j: (l, 0, j)),
        out_shape=jax.ShapeDtypeStruct((DEPTH, MOD_ROWS, 3 * D_MODEL), F32),
        compiler_params=_cparams(2),
        name="adaln_mod",
    )(cvecs, w_ada, b_ada.reshape(DEPTH, 1, 3 * D_MODEL))


def _filter_kernel(L, feats_ref, w1_ref, b1_ref, w2_ref, b2_ref, w3_ref, freq_ref, decay_ref,
                   fhi_ref, flo_ref, a_ref, b_ref, d_ref):
    feats = feats_ref[...]
    t = feats[:, 0:1]
    freq = freq_ref[0]
    hdn = jnp.sin(freq[0:1] * (_dot3(feats, w1_ref[0]) + b1_ref[0]))
    hdn = jnp.sin(freq[1:2] * (_dot3(hdn, w2_ref[0]) + b2_ref[0]))
    h = _dot3(hdn, w3_ref[0]) * jnp.exp(-t * decay_ref[0])
    row = lax.broadcasted_iota(jnp.int32, (L, HY_W), 0)
    h_fwd = h[:, :HY_W]
    h_bwd = jnp.where(row >= 1, h[:, HY_W:], 0.0)
    inv_norm = 1.0 / jnp.sum(jnp.abs(h_fwd) + jnp.abs(h_bwd), axis=0, keepdims=True)
    even = h_fwd + h_bwd
    odd = h_fwd - h_bwd
    h_re = _dot3_const(fhi_ref[0:L, :], flo_ref[0:L, :], even) * inv_norm
    h_im = _dot3_const(fhi_ref[L:2 * L, :], flo_ref[L:2 * L, :], odd) * inv_norm
    alt = jnp.where(row % 2 == 0, 1.0, -1.0)
    nyq = jnp.sum(even * alt, axis=0, keepdims=True) * inv_norm
    a_ref[0] = h_re
    b_ref[0] = jnp.where(row >= 1, h_im, 0.0)
    d_ref[0] = jnp.where(row >= 1, h_re, nyq)


def _filter_spectrum(L, feats, f_hi, f_lo, w1p, b1, w2, b2, w3, freq, decay):
    full = lambda shape: pl.BlockSpec(shape, lambda l: (0,) * len(shape))
    per_layer = lambda shape: pl.BlockSpec((1,) + shape, lambda l: (l,) + (0,) * len(shape))
    out = jax.ShapeDtypeStruct((DEPTH, L, HY_W), F32)
    return pl.pallas_call(
        functools.partial(_filter_kernel, L),
        grid=(DEPTH,),
        in_specs=[
            full((L, FEAT_PAD)),
            per_layer((FEAT_PAD, HY_FH)), per_layer((1, HY_FH)),
            per_layer((HY_FH, HY_FH)), per_layer((1, HY_FH)),
            per_layer((HY_FH, 2 * HY_W)), per_layer((2, HY_FH)), per_layer((1, 2 * HY_W)),
            full((2 * L, L)), full((2 * L, L)),
        ],
        out_specs=[per_layer((L, HY_W))] * 3,
        out_shape=[out, out, out],
        compiler_params=_cparams(1),
        name=f"hyena_filter_L{L}",
    )(feats, w1p, b1, w2, b2, w3, freq, decay, f_hi, f_lo)


def _rope(x, cos, sin):
    lane = lax.broadcasted_iota(jnp.int32, x.shape, 1)
    partner = jnp.where(lane % 2 == 0, pltpu.roll(x, LANES - 1, 1), pltpu.roll(x, 1, 1))
    return x * cos + partner * sin


def _head_mean_square(x, e):
    sq = (x * x).astype(BF16)
    width = x.shape[1]
    if width <= MXU_DIM:
        return _dot(sq, e[:width, :width])
    parts = [_dot(sq[:, i:i + MXU_DIM], e) for i in range(0, width, MXU_DIM)]
    return jnp.concatenate(parts, axis=1)


def _store_kv_heads(k_dst, v_dst, k, v):
    ones = jnp.ones((v.shape[0], HEAD_DIM), BF16)
    for j in range(N_KV_HEADS):
        js = slice(j * HEAD_DIM, (j + 1) * HEAD_DIM)
        vj = v[:, js].astype(BF16)
        k_dst[j] = k[:, js].astype(BF16)
        v_dst[j, 0, :, 0:HEAD_DIM] = vj
        v_dst[j, 0, :, HEAD_DIM:2 * HEAD_DIM] = ones
        v_dst[j, 1, :, 0:HEAD_DIM] = ones
        v_dst[j, 1, :, HEAD_DIM:2 * HEAD_DIM] = vj


def _attn_kernel(latent, aliased, layer, L, nb, qc, past, *refs):
    x_ref, mod_ref, ng_ref, w_ref, gq_ref, gk_ref, e_ref = refs[:7]
    refs = refs[7:]
    if latent:
        (cos_ref, sin_ref, ck_ref, cv_ref, h_out, mix_out,
         q_s, k_s, v_s, g_s, o_s, s_s, mx_s, acc_s, ck_s, cv_s) = refs
    else:
        if aliased:
            refs = refs[2:]
        h_out, mix_out, k_out, v_out, q_s, k_s, v_s, g_s, o_s, s_s, mx_s, acc_s = refs
    rows = nb * L

    x = x_ref[...].reshape(rows, D_MODEL)
    m = mod_ref[0, 0]
    shift, scale = m[0:1], m[1:2]
    ms = jnp.mean(x * x, axis=-1, keepdims=True)
    h = (x * lax.rsqrt(ms + EPS) * ng_ref[0]) * (1.0 + scale) + shift
    hb = h.astype(BF16)
    h_out[...] = hb.reshape(nb, L, D_MODEL)

    u = _dot(hb, w_ref[0])
    q = u[:, :ATTN_W]
    k = u[:, ATTN_W:ATTN_W + KV_W]
    v = u[:, ATTN_W + KV_W:ATTN_W + 2 * KV_W]
    g_s[...] = u[:, ATTN_W + 2 * KV_W:]
    e = e_ref[...]
    qn = q * lax.rsqrt(_head_mean_square(q, e) + EPS) * gq_ref[0]
    kn = k * lax.rsqrt(_head_mean_square(k, e) + EPS) * gk_ref[0]
    if latent:
        cos, sin = cos_ref[...], sin_ref[...]
        qn = jnp.concatenate([_rope(qn[:, i:i + LANES], cos, sin) for i in range(0, ATTN_W, LANES)], axis=1)
        kn = _rope(kn, cos, sin)
        _store_kv_heads(ck_s, cv_s, ck_ref[0, 0], cv_ref[0, 0])
    else:
        for b in range(nb):
            kb, vb = kn[b * L:(b + 1) * L], v[b * L:(b + 1) * L]
            if aliased:
                k_out[b, 0] = kb
                v_out[b, 0] = vb
            else:
                for slot in range(DEPTH):
                    k_out[b, slot] = kb if slot == layer else jnp.zeros_like(kb)
                    v_out[b, slot] = vb if slot == layer else jnp.zeros_like(vb)
    qb = (qn * ATTN_SCALE).astype(BF16)
    for hd in range(N_Q_HEADS):
        j, g = divmod(hd, Q_PER_KV)
        q_s[j * Q_PER_KV + (g % 2) * HEAD_PAIRS + g // 2] = qb[:, hd * HEAD_DIM:(hd + 1) * HEAD_DIM]
    _store_kv_heads(k_s, v_s, kn, v)

    contract_last = (((1,), (1,)), ((), ()))
    chunks_per_seq = L // qc
    lane = lax.broadcasted_iota(jnp.int32, (qc, LANES), 1)
    groups = [(j, e) for j in range(N_KV_HEADS) for e in range(2)]

    def chunk(c, carry):
        r0 = pl.multiple_of(c * qc, qc)
        k0 = pl.multiple_of((c // chunks_per_seq) * L, L)
        for idx, (j, e) in enumerate(groups):
            slot = j * Q_PER_KV + e * HEAD_PAIRS
            qh = q_s[slot:slot + HEAD_PAIRS, pl.ds(r0, qc), :].reshape(HEAD_PAIRS * qc, HEAD_DIM)
            s = lax.dot_general(qh, k_s[j, pl.ds(k0, L), :], contract_last, preferred_element_type=F32)
            mx = jnp.max(s, axis=-1, keepdims=True)
            if latent:
                s0 = lax.dot_general(qh, ck_s[j], contract_last, preferred_element_type=F32)
                mx = jnp.maximum(mx, jnp.max(s0, axis=-1, keepdims=True))
                s_s[idx, :, 0:past] = s0
                s_s[idx, :, past:] = s
            else:
                s_s[idx] = s
            mx_s[idx] = jnp.broadcast_to(mx, (HEAD_PAIRS * qc, LANES))
        for idx, (j, e) in enumerate(groups):
            p = jnp.exp(s_s[idx] - jnp.tile(mx_s[idx], (1, (past + L) // LANES))).astype(BF16)
            if latent:
                acc = _dot(p[:, :past], cv_s[j, e]) + _dot(p[:, past:], v_s[j, e, pl.ds(k0, L), :])
            else:
                acc = _dot(p, v_s[j, e, pl.ds(k0, L), :])
            acc_s[idx] = acc
        for j in range(N_KV_HEADS):
            for t in range(HEAD_PAIRS):
                a_even = acc_s[2 * j, t * qc:(t + 1) * qc, :]
                a_odd = acc_s[2 * j + 1, t * qc:(t + 1) * qc, :]
                out = jnp.where(lane < HEAD_DIM, a_even, a_odd)
                den = pltpu.roll(jnp.where(lane < HEAD_DIM, a_odd, a_even), HEAD_DIM, 1)
                blk = j * HEAD_PAIRS + t
                o_s[pl.ds(r0, qc), blk * LANES:(blk + 1) * LANES] = out / den
        return carry

    lax.fori_loop(0, nb * chunks_per_seq, chunk, 0)
    mix_out[...] = (o_s[...] * _silu(g_s[...])).astype(BF16).reshape(nb, L, ATTN_W)


def _attention_branch(latent, layer, nb, x, mod, norm_g, w_in_b, gq, gk, e_avg, rope=None, cache=None, kv_prev=None):
    B, L, _ = x.shape
    rows = nb * L
    assert nb == 1 or not latent
    full = lambda shape: pl.BlockSpec(shape, lambda b: (0,) * len(shape))
    per_b = lambda shape: pl.BlockSpec((nb,) + shape, lambda b: (b,) + (0,) * len(shape))
    per_layer = lambda shape: pl.BlockSpec((1,) + shape, lambda b: (layer,) + (0,) * len(shape))
    mod_spec = pl.BlockSpec((1, 1, 3, D_MODEL), (lambda b: (layer, 1 + b, 0, 0)) if latent else (lambda b: (layer, 0, 0, 0)))
    in_specs = [per_b((L, D_MODEL)), mod_spec, per_layer((1, D_MODEL)), per_layer((D_MODEL, ATTN_COLS)),
                per_layer((1, ATTN_W)), per_layer((1, KV_W)), full((MXU_DIM, MXU_DIM))]
    args = [x, mod, norm_g, w_in_b, gq, gk, e_avg]
    out_specs = [per_b((L, D_MODEL)), per_b((L, ATTN_W))]
    out_shape = [jax.ShapeDtypeStruct((B, L, D_MODEL), BF16), jax.ShapeDtypeStruct((B, L, ATTN_W), BF16)]
    qc = Q_CHUNK_LATENT if latent else Q_CHUNK
    past = cache[0].shape[2] if latent else 0
    n_groups = 2 * N_KV_HEADS
    scratch = [pltpu.VMEM((N_Q_HEADS, rows, HEAD_DIM), BF16), pltpu.VMEM((N_KV_HEADS, rows, HEAD_DIM), BF16),
               pltpu.VMEM((N_KV_HEADS, 2, rows, 2 * HEAD_DIM), BF16),
               pltpu.VMEM((rows, ATTN_W), F32), pltpu.VMEM((rows, ATTN_W), F32),
               pltpu.VMEM((n_groups, HEAD_PAIRS * qc, past + L), F32),
               pltpu.VMEM((n_groups, HEAD_PAIRS * qc, LANES), F32),
               pltpu.VMEM((n_groups, HEAD_PAIRS * qc, 2 * HEAD_DIM), F32)]
    aliases = {}
    if latent:
        cache_k, cache_v = cache
        cache_spec = pl.BlockSpec((1, 1, past, KV_W), lambda b: (b, layer, 0, 0))
        in_specs += [full((L, LANES)), full((L, LANES)), cache_spec, cache_spec]
        args += [rope[0], rope[1], cache_k, cache_v]
        scratch += [pltpu.VMEM((N_KV_HEADS, past, HEAD_DIM), BF16),
                    pltpu.VMEM((N_KV_HEADS, 2, past, 2 * HEAD_DIM), BF16)]
    else:
        if kv_prev is not None:
            kv_spec = pl.BlockSpec((nb, 1, L, KV_W), lambda b: (b, layer, 0, 0))
        else:
            kv_spec = pl.BlockSpec((nb, DEPTH, L, KV_W), lambda b: (b, 0, 0, 0))
        kv_shape = jax.ShapeDtypeStruct((B, DEPTH, L, KV_W), F32)
        out_specs += [kv_spec, kv_spec]
        out_shape += [kv_shape, kv_shape]
        if kv_prev is not None:
            in_specs += [pl.BlockSpec(memory_space=pl.ANY)] * 2
            aliases = {len(args): 2, len(args) + 1: 3}
            args += list(kv_prev)
    return pl.pallas_call(
        functools.partial(_attn_kernel, latent, kv_prev is not None, layer, L, nb, qc, past),
        grid=(B // nb,),
        in_specs=in_specs,
        out_specs=out_specs,
        out_shape=out_shape,
        scratch_shapes=scratch,
        input_output_aliases=aliases,
        compiler_params=_cparams(1),
        name=f"attn_branch_L{L}",
    )(*args)


def _shifted(x, offset, period):
    rows = x.shape[0]
    t = lax.broadcasted_iota(jnp.int32, x.shape, 0)
    if rows != period:
        t = t % period
    rolled = pltpu.roll(x, (-offset) % rows, 0)
    valid = (t >= -offset) if offset < 0 else (t < period - offset)
    return jnp.where(valid, rolled, 0.0)


def _hyena_kernel(L, nb, h_ref, w0_ref, w1_ref, w2_ref, wg_ref, sw0_ref, sw1_ref, sw2_ref, sb0_ref, sb1_ref, sb2_ref,
                  fa_ref, fb_ref, fd_ref, bias_ref, fwd_ref, inv_ref, mix_out):
    hb = h_ref[...].reshape(nb * L, D_MODEL)

    def short_conv(w_ref, sw_ref, sb_ref):
        xs = _dot(hb, w_ref[0])
        w = sw_ref[0]
        return sb_ref[0] + _shifted(xs, -1, L) * w[0:1] + xs * w[1:2] + _shifted(xs, 1, L) * w[2:3]

    x0 = short_conv(w0_ref, sw0_ref, sb0_ref)
    x1 = short_conv(w1_ref, sw1_ref, sb1_ref)
    hv = short_conv(w2_ref, sw2_ref, sb2_ref)
    gated = x0 * _silu(_dot(hb, wg_ref[0]))
    z = x1 * hv
    zb = z.astype(BF16)
    fa, fb, fd = fa_ref[0], fb_ref[0], fd_ref[0]
    for b in range(nb):
        rs = slice(b * L, (b + 1) * L)
        zf = _dot(fwd_ref[...], zb[rs])
        re, im = zf[:L], zf[L:]
        y_re = re * fa - im * fb
        y_im = re * fb + im * fd
        y = _dot(inv_ref[:, :L], y_re.astype(BF16)) + _dot(inv_ref[:, L:], y_im.astype(BF16))
        y = y + z[rs] * bias_ref[0]
        mix_out[b] = (y * gated[rs]).astype(BF16)


def _hyena_branch(layer, nb, hmod, w_in_b, short_w, short_b, filt, bias, fwd, inv):
    B, L, _ = hmod.shape
    cb = COL_BLOCK
    nt = HY_W // cb
    fa, fb, fd = filt
    w_spec = lambda part: pl.BlockSpec((1, D_MODEL, cb), lambda i, b: (layer, 0, HY_BLOCK0 + part * nt + i))
    sw_spec = lambda part: pl.BlockSpec((1, 3, cb), lambda i, b: (layer, 0, part * nt + i))
    sb_spec = lambda part: pl.BlockSpec((1, 1, cb), lambda i, b: (layer, 0, part * nt + i))
    filt_spec = pl.BlockSpec((1, L, cb), lambda i, b: (layer, 0, i))
    return pl.pallas_call(
        functools.partial(_hyena_kernel, L, nb),
        grid=(nt, B // nb),
        in_specs=[
            pl.BlockSpec((nb, L, D_MODEL), lambda i, b: (b, 0, 0)),
            w_spec(0), w_spec(1), w_spec(2), w_spec(3),
            sw_spec(0), sw_spec(1), sw_spec(2),
            sb_spec(0), sb_spec(1), sb_spec(2),
            filt_spec, filt_spec, filt_spec,
            pl.BlockSpec((1, 1, cb), lambda i, b: (layer, 0, i)),
            pl.BlockSpec((2 * L, L), lambda i, b: (0, 0)),
            pl.BlockSpec((L, 2 * L), lambda i, b: (0, 0)),
        ],
        out_specs=pl.BlockSpec((nb, L, cb), lambda i, b: (b, 0, i)),
        out_shape=jax.ShapeDtypeStruct((B, L, HY_W), BF16),
        compiler_params=_cparams(2),
        name=f"hyena_branch_L{L}",
    )(hmod, w_in_b, w_in_b, w_in_b, w_in_b, short_w, short_w, short_w, short_b, short_b, short_b,
      fa, fb, fd, bias, fwd, inv)


def _lru_kernel(latent, aliased, layer, L, nb, *refs):
    (h_ref, wx0_ref, wx1_ref, wg0_ref, wg1_ref, cw_ref, cb_ref, wa_ref, ba_ref, wxg_ref, bx_ref, lam_ref) = refs[:12]
    refs = refs[12:]
    if latent:
        h0_ref, mix_out, gate_w_s, a_s, b_s, y_s, g_s = refs
    else:
        if aliased:
            refs = refs[1:]
        mix_out, hl_out, gate_w_s, a_s, b_s, y_s, g_s = refs
    half_w = LRU_W // 2
    blocks_per_half = half_w // LRU_BS
    n_tiles = L // SUBLANES

    @pl.when(pl.program_id(0) == 0)
    def _build_gate_weights():
        gate_w_s[...] = jnp.zeros(gate_w_s.shape, BF16)
        for half in range(2):
            for kind, (w_ref, d) in enumerate(((wa_ref, 0), (wxg_ref, 0), (wa_ref, 1), (wxg_ref, 1))):
                for j in range(blocks_per_half):
                    blk = (0.5 * w_ref[0, d, half * blocks_per_half + j]).astype(BF16)
                    gate_w_s[half, j * LRU_BS:(j + 1) * LRU_BS,
                             kind * half_w + j * LRU_BS:kind * half_w + (j + 1) * LRU_BS] = blk

    hb = h_ref[...].reshape(nb * L, D_MODEL)
    xs = jnp.concatenate([_dot(hb, wx0_ref[0]), _dot(hb, wx1_ref[0])], axis=1)
    g_s[:, :half_w] = _dot(hb, wg0_ref[0])
    g_s[:, half_w:] = _dot(hb, wg1_ref[0])
    cw = cw_ref[0]
    xc = (cb_ref[0] + _shifted(xs, -1, L) * cw[0:1] + xs * cw[1:2] + _shifted(xs, 1, L) * cw[2:3]
          + _shifted(xs, 2, L) * cw[3:4])
    xcb = xc.astype(BF16)
    c_half = (-0.5 * LRU_C) * jax.nn.softplus(-lam_ref[0])
    ba, bx = 0.5 * ba_ref[0], 0.5 * bx_ref[0]
    for half in range(2):
        cols = slice(half * half_w, (half + 1) * half_w)
        bias = jnp.concatenate([ba[0:1, cols], bx[0:1, cols], ba[1:2, cols], bx[1:2, cols]], axis=1)
        gates = _dot(xcb[:, cols], gate_w_s[half]) + bias
        xh_half = 0.5 * xc[:, cols]
        for d in range(2):
            t_r = jnp.tanh(gates[:, (2 * d) * half_w:(2 * d + 1) * half_w])
            t_i = jnp.tanh(gates[:, (2 * d + 1) * half_w:(2 * d + 2) * half_w])
            ch = c_half[d:d + 1, cols]
            log_a = ch * t_r + ch
            a = jnp.exp(log_a)
            y = jnp.tanh(log_a) * (-1.0 - a * a)
            root = jnp.where(y > 0.0, y * lax.rsqrt(y), 0.0)
            bcoef = root * (xh_half * t_i + xh_half)
            a_s[d, :, :, cols] = a.reshape(nb * n_tiles, SUBLANES, half_w)
            b_s[d, :, :, cols] = bcoef.reshape(nb * n_tiles, SUBLANES, half_w)

    if latent:
        init = []
        for b in range(nb):
            h0 = h0_ref[b, 0]
            init += [h0[0:1], h0[1:2]]
    else:
        init = [jnp.zeros((1, LRU_W), F32)] * (2 * nb)

    def tile_step(i, carry):
        carry = list(carry)
        for j in range(SUBLANES):
            for b in range(nb):
                tf = b * n_tiles + i
                tb = b * n_tiles + (n_tiles - 1 - i)
                jb = SUBLANES - 1 - j
                hf = a_s[0, tf, j:j + 1, :] * carry[2 * b] + b_s[0, tf, j:j + 1, :]
                y_s[0, tf, j:j + 1, :] = hf
                hbk = a_s[1, tb, jb:jb + 1, :] * carry[2 * b + 1] + b_s[1, tb, jb:jb + 1, :]
                y_s[1, tb, jb:jb + 1, :] = hbk
                carry[2 * b], carry[2 * b + 1] = hf, hbk
        return tuple(carry)

    last = lax.fori_loop(0, n_tiles, tile_step, tuple(init))
    if not latent:
        if not aliased:
            hl_out[...] = jnp.zeros(hl_out.shape, F32)
        slot = 0 if aliased else layer
        for b in range(nb):
            hl_out[b, slot, 0:1, :] = last[2 * b]
            hl_out[b, slot, 1:2, :] = last[2 * b + 1]
    y = (y_s[0] + y_s[1]).reshape(nb * L, LRU_W)
    mix_out[...] = (y * _silu(g_s[...])).astype(BF16).reshape(nb, L, LRU_W)


def _lru_branch(latent, layer, nb, hmod, w_in_b, conv_w, conv_b, wa, ba, wx, bx, lam, state=None, hl_prev=None):
    B, L, _ = hmod.shape
    cb = COL_BLOCK
    half_w = LRU_W // 2
    rows = nb * L
    per_layer = lambda shape: pl.BlockSpec((1,) + shape, lambda b: (layer,) + (0,) * len(shape))
    w_spec = lambda j: pl.BlockSpec((1, D_MODEL, cb), lambda b: (layer, 0, LRU_BLOCK0 + j))
    gate_blocks = (2, LRU_BLOCKS, LRU_BS, LRU_BS)
    in_specs = [pl.BlockSpec((nb, L, D_MODEL), lambda b: (b, 0, 0)), w_spec(0), w_spec(1), w_spec(2), w_spec(3),
                per_layer((4, LRU_W)), per_layer((1, LRU_W)), per_layer(gate_blocks), per_layer((2, LRU_W)),
                per_layer(gate_blocks), per_layer((2, LRU_W)), per_layer((2, LRU_W))]
    args = [hmod, w_in_b, w_in_b, w_in_b, w_in_b, conv_w, conv_b, wa, ba, wx, bx, lam]
    out_specs = [pl.BlockSpec((nb, L, LRU_W), lambda b: (b, 0, 0))]
    out_shape = [jax.ShapeDtypeStruct((B, L, LRU_W), BF16)]
    aliases = {}
    if latent:
        in_specs.append(pl.BlockSpec((nb, 1, 2, LRU_W), lambda b: (b, layer, 0, 0)))
        args.append(state)
    else:
        if hl_prev is not None:
            out_specs.append(pl.BlockSpec((nb, 1, 2, LRU_W), lambda b: (b, layer, 0, 0)))
        else:
            out_specs.append(pl.BlockSpec((nb, DEPTH, 2, LRU_W), lambda b: (b, 0, 0, 0)))
        out_shape.append(jax.ShapeDtypeStruct((B, DEPTH, 2, LRU_W), F32))
        if hl_prev is not None:
            in_specs.append(pl.BlockSpec(memory_space=pl.ANY))
            aliases = {len(args): 1}
            args.append(hl_prev)
    scan_buf = pltpu.VMEM((2, rows // SUBLANES, SUBLANES, LRU_W), F32)
    return pl.pallas_call(
        functools.partial(_lru_kernel, latent, hl_prev is not None, layer, L, nb),
        grid=(B // nb,),
        in_specs=in_specs,
        out_specs=out_specs,
        out_shape=out_shape,
        scratch_shapes=[pltpu.VMEM((2, half_w, 4 * half_w), BF16), scan_buf, scan_buf, scan_buf,
                        pltpu.VMEM((rows, LRU_W), F32)],
        input_output_aliases=aliases,
        compiler_params=_cparams(1),
        name=f"lru_branch_L{L}",
    )(*args)


def _out_kernel(final, x_ref, ma_ref, mh_ref, ml_ref, w_ref, mod_ref, fg_ref, o_ref):
    out = (_dot(ma_ref[...], w_ref[0, 0:ATTN_W, :]) + _dot(mh_ref[...], w_ref[0, ATTN_W:ATTN_W + HY_W, :])
           + _dot(ml_ref[...], w_ref[0, ATTN_W + HY_W:, :]))
    gate = mod_ref[0, 0][2:3]
    xn = x_ref[...] + gate * out
    if final:
        ms = jnp.mean(xn * xn, axis=-1, keepdims=True)
        xn = xn * lax.rsqrt(ms + EPS) * fg_ref[...]
    o_ref[...] = xn


def _out_projection(final, latent, layer, x, mix_a, mix_h, mix_l, w_out_b, mod, final_g):
    B, L, _ = x.shape
    tm = OUT_ROWS
    rows = B * L
    tile = lambda w: pl.BlockSpec((tm, w), lambda i: (i, 0))
    mod_row = (lambda i: (layer, 1 + (i * tm) // L, 0, 0)) if latent else (lambda i: (layer, 0, 0, 0))
    y = pl.pallas_call(
        functools.partial(_out_kernel, final),
        grid=(rows // tm,),
        in_specs=[tile(D_MODEL), tile(ATTN_W), tile(HY_W), tile(LRU_W),
                  pl.BlockSpec((1, D_MIX, D_MODEL), lambda i: (layer, 0, 0)),
                  pl.BlockSpec((1, 1, 3, D_MODEL), mod_row), pl.BlockSpec((1, D_MODEL), lambda i: (0, 0))],
        out_specs=tile(D_MODEL),
        out_shape=jax.ShapeDtypeStruct((rows, D_MODEL), F32),
        compiler_params=_cparams(1),
        name=f"out_proj_L{L}",
    )(x.reshape(rows, D_MODEL), mix_a.reshape(rows, ATTN_W), mix_h.reshape(rows, HY_W), mix_l.reshape(rows, LRU_W),
      w_out_b, mod, final_g)
    return y.reshape(B, L, D_MODEL)


def _mixer_pass(latent, x, p, filt, dft, rope=None, cache=None, state=None):
    B, L, _ = x.shape
    fwd, inv = dft
    nb = 1 if latent else 4
    kv, hl = None, None
    for l in range(DEPTH):
        outs = _attention_branch(latent, l, nb, x, p['mod'], p['norm_g'], p['w_in'], p['gq'], p['gk'], p['e_avg'],
                                 rope, cache, kv)
        hmod, mix_a = outs[0], outs[1]
        if not latent:
            kv = (outs[2], outs[3])
        mix_h = _hyena_branch(l, nb, hmod, p['w_in'], p['short_w'], p['short_b'], filt, p['hy_bias'], fwd, inv)
        lru = _lru_branch(latent, l, nb, hmod, p['w_in'], p['conv_w'], p['conv_b'], p['wa'], p['ba'], p['wx'],
                          p['bx'], p['lam'], state, hl)
        if not latent:
            hl = lru[1]
        x = _out_projection(l == DEPTH - 1, latent, l, x, mix_a, mix_h, lru[0], p['w_out'], p['mod'], p['final_g'])
    return x, kv, hl


def kernel(x_prompt, x_sample, cache_k, cache_v, state_lru, c, c_ctx, norm_g, w_ada, b_ada, w_in, q_norm_g, k_norm_g,
           hy_short_w, hy_short_b, hy_filt_w1, hy_filt_b1, hy_filt_w2, hy_filt_b2, hy_filt_w3, hy_filt_freq,
           hy_filt_decay, hy_bias, lru_conv_w, lru_conv_b, lru_wa, lru_ba, lru_wx, lru_bx, lru_lambda, w_out, final_g):
    batch, seq, _ = x_prompt.shape
    dec_batch, dec_seq, _ = x_sample.shape
    past = cache_k.shape[2]

    cvecs = jnp.concatenate([c_ctx[None, :], c, jnp.zeros((MOD_ROWS - 1 - dec_batch, D_MODEL), F32)], axis=0)
    mod = _modulation(cvecs, w_ada, b_ada).reshape(DEPTH, MOD_ROWS, 3, D_MODEL)

    w1p = jnp.pad(hy_filt_w1, ((0, 0), (0, FEAT_PAD - HY_EMB), (0, 0)))
    dft, filt = {}, {}
    for L in (seq, dec_seq):
        fwd, inv = (jnp.asarray(m) for m in _dft_tables(L))
        f_hi, f_lo = _split_bf16(fwd)
        dft[L] = (f_hi, inv.astype(BF16))
        filt[L] = _filter_spectrum(L, jnp.asarray(_hyena_feats(L)), f_hi, f_lo, w1p, hy_filt_b1[:, None, :],
                                   hy_filt_w2, hy_filt_b2[:, None, :], hy_filt_w3, hy_filt_freq,
                                   hy_filt_decay[:, None, :])

    params = {
        'mod': mod,
        'norm_g': norm_g[:, None, :],
        'w_in': w_in.astype(BF16),
        'gq': jnp.tile(q_norm_g, (1, N_Q_HEADS))[:, None, :],
        'gk': jnp.tile(k_norm_g, (1, N_KV_HEADS))[:, None, :],
        'e_avg': jnp.asarray(_head_average_matrix()).astype(BF16),
        'short_w': hy_short_w, 'short_b': hy_short_b[:, None, :], 'hy_bias': hy_bias[:, None, :],
        'conv_w': lru_conv_w, 'conv_b': lru_conv_b[:, None, :],
        'wa': lru_wa, 'ba': lru_ba, 'wx': lru_wx, 'bx': lru_bx, 'lam': lru_lambda,
        'w_out': w_out.astype(BF16),
        'final_g': final_g[None, :],
    }

    y_prompt, kv, new_lru = _mixer_pass(False, x_prompt, params, filt[seq], dft[seq])
    new_k = kv[0].reshape(batch, DEPTH, seq, N_KV_HEADS, HEAD_DIM)
    new_v = kv[1].reshape(batch, DEPTH, seq, N_KV_HEADS, HEAD_DIM)

    rope = tuple(jnp.asarray(t) for t in _rope_tables(dec_seq))
    cache = (cache_k.reshape(dec_batch, DEPTH, past, KV_W), cache_v.reshape(dec_batch, DEPTH, past, KV_W))
    y_sample, _, _ = _mixer_pass(True, x_sample, params, filt[dec_seq], dft[dec_seq], rope, cache, state_lru)
    return (y_prompt, y_sample, new_k, new_v, new_lru)
```

```python
import functools
import math

import numpy as np
import jax
import jax.numpy as jnp
from jax import lax
from jax.experimental import pallas as pl
from jax.experimental.pallas import tpu as pltpu

F32 = jnp.float32
BF16 = jnp.bfloat16

D_MODEL = 1024
DEPTH = 2
GRID_W = 64
HEAD_DIM = 64
N_Q_HEADS = 8
N_KV_HEADS = 2
Q_PER_KV = N_Q_HEADS // N_KV_HEADS
ATTN_W = N_Q_HEADS * HEAD_DIM
KV_W = N_KV_HEADS * HEAD_DIM
ROPE_THETA = 10000.0
ATTN_SCALE = HEAD_DIM ** -0.5
HY_W = 512
HY_BANDS = 16
HY_EMB = 2 * HY_BANDS + 1
HY_FH = 64
LRU_W = 512
LRU_BLOCKS = 8
LRU_BS = LRU_W // LRU_BLOCKS
LRU_C = 8.0
EPS = 1e-6

ATTN_COLS = ATTN_W + 2 * KV_W + ATTN_W
D_MIX = ATTN_W + HY_W + LRU_W

LANES = 128
SUBLANES = 8
MXU_DIM = 256
COL_BLOCK = MXU_DIM
HY_BLOCK0 = ATTN_COLS // COL_BLOCK
LRU_BLOCK0 = (ATTN_COLS + 4 * HY_W) // COL_BLOCK
Q_CHUNK = 256
Q_CHUNK_LATENT = 128
HEAD_PAIRS = Q_PER_KV // 2
FEAT_PAD = 128
MOD_ROWS = 8
VMEM_LIMIT = 56 * 1024 * 1024


def _cparams(n_axes):
    return pltpu.CompilerParams(dimension_semantics=("arbitrary",) * n_axes, vmem_limit_bytes=VMEM_LIMIT)


def _split_bf16(a):
    hi = a.astype(BF16)
    lo = (a - hi.astype(F32)).astype(BF16)
    return hi, lo


def _dot(a, b):
    return jnp.dot(a, b, preferred_element_type=F32)


def _dot3(a, b):
    a_hi, a_lo = _split_bf16(a)
    b_hi, b_lo = _split_bf16(b)
    return _dot(a_hi, b_hi) + _dot(a_hi, b_lo) + _dot(a_lo, b_hi)


def _dot3_const(m_hi, m_lo, b):
    b_hi, b_lo = _split_bf16(b)
    return _dot(m_hi, b_hi) + _dot(m_hi, b_lo) + _dot(m_lo, b_hi)


def _silu(x):
    half = 0.5 * x
    return half * jnp.tanh(half) + half


@functools.lru_cache(maxsize=None)
def _dft_tables(L):
    n = 2 * L
    t = np.arange(L)
    kt = (t[:, None] * t[None, :]) % n
    ang = 2.0 * np.pi * kt / n
    cosm, sinm = np.cos(ang), np.sin(ang)
    alt = (-1.0) ** t
    f_s = -sinm
    f_s[0, :] = alt
    fwd = np.concatenate([cosm, f_s], axis=0)
    g_c = 2.0 * cosm.T / n
    g_c[:, 0] = 1.0 / n
    g_s = -2.0 * sinm.T / n
    g_s[:, 0] = alt / n
    inv = np.concatenate([g_c, g_s], axis=1)
    return fwd.astype(np.float32), inv.astype(np.float32)


@functools.lru_cache(maxsize=None)
def _hyena_feats(L):
    t = np.linspace(0.0, 1.0, L)[:, None]
    w = 2.0 * math.pi * np.arange(L)[:, None] / L
    f = np.linspace(1e-4, HY_BANDS - 1, HY_BANDS)[None, :]
    out = np.zeros((L, FEAT_PAD), np.float32)
    out[:, :HY_EMB] = np.concatenate([t, np.cos(f * w), -np.sin(f * w)], axis=-1)
    return out


@functools.lru_cache(maxsize=None)
def _rope_tables(L):
    rows = L // GRID_W
    row = np.repeat(np.arange(rows), GRID_W).astype(np.float64)
    col = np.tile(np.arange(GRID_W), rows).astype(np.float64)
    n_freq = HEAD_DIM // 4
    inv = ROPE_THETA ** (-np.arange(n_freq) / n_freq)
    ang = np.concatenate([row[:, None] * inv, col[:, None] * inv], axis=-1)
    cos = np.repeat(np.cos(ang), 2, axis=-1)
    sin = np.repeat(np.sin(ang), 2, axis=-1) * np.tile(np.array([-1.0, 1.0]), HEAD_DIM // 2)
    reps = LANES // HEAD_DIM
    return np.tile(cos, (1, reps)).astype(np.float32), np.tile(sin, (1, reps)).astype(np.float32)


@functools.lru_cache(maxsize=None)
def _head_average_matrix():
    return np.kron(np.eye(MXU_DIM // HEAD_DIM), np.full((HEAD_DIM, HEAD_DIM), 1.0 / HEAD_DIM)).astype(np.float32)


def _mod_kernel(c_ref, w_ref, b_ref, o_ref):
    s = _silu(c_ref[...])
    o_ref[0] = _dot3(s, w_ref[0]) + b_ref[0]


def _modulation(cvecs, w_ada, b_ada):
    tn = D_MODEL
    return pl.pallas_call(
        _mod_kernel,
        grid=(DEPTH, 3 * D_MODEL // tn),
        in_specs=[
            pl.BlockSpec((MOD_ROWS, D_MODEL), lambda l, j: (0, 0)),
            pl.BlockSpec((1, D_MODEL, tn), lambda l, j: (l, 0, j)),
            pl.BlockSpec((1, 1, tn), lambda l, j: (l, 0, j)),
        ],
        out_specs=pl.BlockSpec((1, MOD_ROWS, tn), lambda l, j: (l, 0, j)),
        out_shape=jax.ShapeDtypeStruct((DEPTH, MOD_ROWS, 3 * D_MODEL), F32),
        compiler_params=_cparams(2),
        name="adaln_mod",
    )(cvecs, w_ada, b_ada.reshape(DEPTH, 1, 3 * D_MODEL))


def _filter_kernel(L, feats_ref, w1_ref, b1_ref, w2_ref, b2_ref, w3_ref, freq_ref, decay_ref,
                   fhi_ref, flo_ref, a_ref, b_ref, d_ref):
    feats = feats_ref[...]
    t = feats[:, 0:1]
    freq = freq_ref[0]
    hdn = jnp.sin(freq[0:1] * (_dot3(feats, w1_ref[0]) + b1_ref[0]))
    hdn = jnp.sin(freq[1:2] * (_dot3(hdn, w2_ref[0]) + b2_ref[0]))
    h = _dot3(hdn, w3_ref[0]) * jnp.exp(-t * decay_ref[0])
    row = lax.broadcasted_iota(jnp.int32, (L, HY_W), 0)
    h_fwd = h[:, :HY_W]
    h_bwd = jnp.where(row >= 1, h[:, HY_W:], 0.0)
    inv_norm = 1.0 / jnp.sum(jnp.abs(h_fwd) + jnp.abs(h_bwd), axis=0, keepdims=True)
    even = h_fwd + h_bwd
    odd = h_fwd - h_bwd
    h_re = _dot3_const(fhi_ref[0:L, :], flo_ref[0:L, :], even) * inv_norm
    h_im = _dot3_const(fhi_ref[L:2 * L, :], flo_ref[L:2 * L, :], odd) * inv_norm
    alt = jnp.where(row % 2 == 0, 1.0, -1.0)
    nyq = jnp.sum(even * alt, axis=0, keepdims=True) * inv_norm
    a_ref[0] = h_re
    b_ref[0] = jnp.where(row >= 1, h_im, 0.0)
    d_ref[0] = jnp.where(row >= 1, h_re, nyq)


def _filter_spectrum(L, feats, f_hi, f_lo, w1p, b1, w2, b2, w3, freq, decay):
    full = lambda shape: pl.BlockSpec(shape, lambda l: (0,) * len(shape))
    per_layer = lambda shape: pl.BlockSpec((1,) + shape, lambda l: (l,) + (0,) * len(shape))
    out = jax.ShapeDtypeStruct((DEPTH, L, HY_W), F32)
    return pl.pallas_call(
        functools.partial(_filter_kernel, L),
        grid=(DEPTH,),
        in_specs=[
            full((L, FEAT_PAD)),
            per_layer((FEAT_PAD, HY_FH)), per_layer((1, HY_FH)),
            per_layer((HY_FH, HY_FH)), per_layer((1, HY_FH)),
            per_layer((HY_FH, 2 * HY_W)), per_layer((2, HY_FH)), per_layer((1, 2 * HY_W)),
            full((2 * L, L)), full((2 * L, L)),
        ],
        out_specs=[per_layer((L, HY_W))] * 3,
        out_shape=[out, out, out],
        compiler_params=_cparams(1),
        name=f"hyena_filter_L{L}",
    )(feats, w1p, b1, w2, b2, w3, freq, decay, f_hi, f_lo)


def _rope(x, cos, sin):
    lane = lax.broadcasted_iota(jnp.int32, x.shape, 1)
    partner = jnp.where(lane % 2 == 0, pltpu.roll(x, LANES - 1, 1), pltpu.roll(x, 1, 1))
    return x * cos + partner * sin


def _head_mean_square(x, e):
    sq = (x * x).astype(BF16)
    width = x.shape[1]
    if width <= MXU_DIM:
        return _dot(sq, e[:width, :width])
    parts = [_dot(sq[:, i:i + MXU_DIM], e) for i in range(0, width, MXU_DIM)]
    return jnp.concatenate(parts, axis=1)


def _store_kv_heads(k_dst, v_dst, k, v):
    ones = jnp.ones((v.shape[0], HEAD_DIM), BF16)
    for j in range(N_KV_HEADS):
        js = slice(j * HEAD_DIM, (j + 1) * HEAD_DIM)
        vj = v[:, js].astype(BF16)
        k_dst[j] = k[:, js].astype(BF16)
        v_dst[j, 0, :, 0:HEAD_DIM] = vj
        v_dst[j, 0, :, HEAD_DIM:2 * HEAD_DIM] = ones
        v_dst[j, 1, :, 0:HEAD_DIM] = ones
        v_dst[j, 1, :, HEAD_DIM:2 * HEAD_DIM] = vj


def _attn_kernel(latent, aliased, layer, L, nb, qc, past, *refs):
    x_ref, mod_ref, ng_ref, w_ref, gq_ref, gk_ref, e_ref = refs[:7]
    refs = refs[7:]
    if latent:
        (cos_ref, sin_ref, ck_ref, cv_ref, h_out, mix_out,
         q_s, k_s, v_s, g_s, o_s, s_s, mx_s, acc_s, ck_s, cv_s) = refs
    else:
        if aliased:
            refs = refs[2:]
        h_out, mix_out, k_out, v_out, q_s, k_s, v_s, g_s, o_s, s_s, mx_s, acc_s = refs
    rows = nb * L

    x = x_ref[...].reshape(rows, D_MODEL)
    m = mod_ref[0, 0]
    shift, scale = m[0:1], m[1:2]
    ms = jnp.mean(x * x, axis=-1, keepdims=True)
    h = (x * lax.rsqrt(ms + EPS) * ng_ref[0]) * (1.0 + scale) + shift
    hb = h.astype(BF16)
    h_out[...] = hb.reshape(nb, L, D_MODEL)

    u = _dot(hb, w_ref[0])
    q = u[:, :ATTN_W]
    k = u[:, ATTN_W:ATTN_W + KV_W]
    v = u[:, ATTN_W + KV_W:ATTN_W + 2 * KV_W]
    g_s[...] = u[:, ATTN_W + 2 * KV_W:]
    e = e_ref[...]
    qn = q * lax.rsqrt(_head_mean_square(q, e) + EPS) * gq_ref[0]
    kn = k * lax.rsqrt(_head_mean_square(k, e) + EPS) * gk_ref[0]
    if latent:
        cos, sin = cos_ref[...], sin_ref[...]
        qn = jnp.concatenate([_rope(qn[:, i:i + LANES], cos, sin) for i in range(0, ATTN_W, LANES)], axis=1)
        kn = _rope(kn, cos, sin)
        _store_kv_heads(ck_s, cv_s, ck_ref[0, 0], cv_ref[0, 0])
    else:
        for b in range(nb):
            kb, vb = kn[b * L:(b + 1) * L], v[b * L:(b + 1) * L]
            if aliased:
                k_out[b, 0] = kb
                v_out[b, 0] = vb
            else:
                for slot in range(DEPTH):
                    k_out[b, slot] = kb if slot == layer else jnp.zeros_like(kb)
                    v_out[b, slot] = vb if slot == layer else jnp.zeros_like(vb)
    qb = (qn * ATTN_SCALE).astype(BF16)
    for hd in range(N_Q_HEADS):
        j, g = divmod(hd, Q_PER_KV)
        q_s[j * Q_PER_KV + (g % 2) * HEAD_PAIRS + g // 2] = qb[:, hd * HEAD_DIM:(hd + 1) * HEAD_DIM]
    _store_kv_heads(k_s, v_s, kn, v)

    contract_last = (((1,), (1,)), ((), ()))
    chunks_per_seq = L // qc
    lane = lax.broadcasted_iota(jnp.int32, (qc, LANES), 1)
    groups = [(j, e) for j in range(N_KV_HEADS) for e in range(2)]

    def chunk(c, carry):
        r0 = pl.multiple_of(c * qc, qc)
        k0 = pl.multiple_of((c // chunks_per_seq) * L, L)
        for idx, (j, e) in enumerate(groups):
            slot = j * Q_PER_KV + e * HEAD_PAIRS
            qh = q_s[slot:slot + HEAD_PAIRS, pl.ds(r0, qc), :].reshape(HEAD_PAIRS * qc, HEAD_DIM)
            s = lax.dot_general(qh, k_s[j, pl.ds(k0, L), :], contract_last, preferred_element_type=F32)
            mx = jnp.max(s, axis=-1, keepdims=True)
            if latent:
                s0 = lax.dot_general(qh, ck_s[j], contract_last, preferred_element_type=F32)
                mx = jnp.maximum(mx, jnp.max(s0, axis=-1, keepdims=True))
                s_s[idx, :, 0:past] = s0
                s_s[idx, :, past:] = s
            else:
                s_s[idx] = s
            mx_s[idx] = jnp.broadcast_to(mx, (HEAD_PAIRS * qc, LANES))
        for idx, (j, e) in enumerate(groups):
            p = jnp.exp(s_s[idx] - jnp.tile(mx_s[idx], (1, (past + L) // LANES))).astype(BF16)
            if latent:
                acc = _dot(p[:, :past], cv_s[j, e]) + _dot(p[:, past:], v_s[j, e, pl.ds(k0, L), :])
            else:
                acc = _dot(p, v_s[j, e, pl.ds(k0, L), :])
            acc_s[idx] = acc
        for j in range(N_KV_HEADS):
            for t in range(HEAD_PAIRS):
                a_even = acc_s[2 * j, t * qc:(t + 1) * qc, :]
                a_odd = acc_s[2 * j + 1, t * qc:(t + 1) * qc, :]
                out = jnp.where(lane < HEAD_DIM, a_even, a_odd)
                den = pltpu.roll(jnp.where(lane < HEAD_DIM, a_odd, a_even), HEAD_DIM, 1)
                blk = j * HEAD_PAIRS + t
                o_s[pl.ds(r0, qc), blk * LANES:(blk + 1) * LANES] = out / den
        return carry

    lax.fori_loop(0, nb * chunks_per_seq, chunk, 0)
    mix_out[...] = (o_s[...] * _silu(g_s[...])).astype(BF16).reshape(nb, L, ATTN_W)


def _attention_branch(latent, layer, nb, x, mod, norm_g, w_in_b, gq, gk, e_avg, rope=None, cache=None, kv_prev=None):
    B, L, _ = x.shape
    rows = nb * L
    assert nb == 1 or not latent
    full = lambda shape: pl.BlockSpec(shape, lambda b: (0,) * len(shape))
    per_b = lambda shape: pl.BlockSpec((nb,) + shape, lambda b: (b,) + (0,) * len(shape))
    per_layer = lambda shape: pl.BlockSpec((1,) + shape, lambda b: (layer,) + (0,) * len(shape))
    mod_spec = pl.BlockSpec((1, 1, 3, D_MODEL), (lambda b: (layer, 1 + b, 0, 0)) if latent else (lambda b: (layer, 0, 0, 0)))
    in_specs = [per_b((L, D_MODEL)), mod_spec, per_layer((1, D_MODEL)), per_layer((D_MODEL, ATTN_COLS)),
                per_layer((1, ATTN_W)), per_layer((1, KV_W)), full((MXU_DIM, MXU_DIM))]
    args = [x, mod, norm_g, w_in_b, gq, gk, e_avg]
    out_specs = [per_b((L, D_MODEL)), per_b((L, ATTN_W))]
    out_shape = [jax.ShapeDtypeStruct((B, L, D_MODEL), BF16), jax.ShapeDtypeStruct((B, L, ATTN_W), BF16)]
    qc = Q_CHUNK_LATENT if latent else Q_CHUNK
    past = cache[0].shape[2] if latent else 0
    n_groups = 2 * N_KV_HEADS
    scratch = [pltpu.VMEM((N_Q_HEADS, rows, HEAD_DIM), BF16), pltpu.VMEM((N_KV_HEADS, rows, HEAD_DIM), BF16),
               pltpu.VMEM((N_KV_HEADS, 2, rows, 2 * HEAD_DIM), BF16),
               pltpu.VMEM((rows, ATTN_W), F32), pltpu.VMEM((rows, ATTN_W), F32),
               pltpu.VMEM((n_groups, HEAD_PAIRS * qc, past + L), F32),
               pltpu.VMEM((n_groups, HEAD_PAIRS * qc, LANES), F32),
               pltpu.VMEM((n_groups, HEAD_PAIRS * qc, 2 * HEAD_DIM), F32)]
    aliases = {}
    if latent:
        cache_k, cache_v = cache
        cache_spec = pl.BlockSpec((1, 1, past, KV_W), lambda b: (b, layer, 0, 0))
        in_specs += [full((L, LANES)), full((L, LANES)), cache_spec, cache_spec]
        args += [rope[0], rope[1], cache_k, cache_v]
        scratch += [pltpu.VMEM((N_KV_HEADS, past, HEAD_DIM), BF16),
                    pltpu.VMEM((N_KV_HEADS, 2, past, 2 * HEAD_DIM), BF16)]
    else:
        if kv_prev is not None:
            kv_spec = pl.BlockSpec((nb, 1, L, KV_W), lambda b: (b, layer, 0, 0))
        else:
            kv_spec = pl.BlockSpec((nb, DEPTH, L, KV_W), lambda b: (b, 0, 0, 0))
        kv_shape = jax.ShapeDtypeStruct((B, DEPTH, L, KV_W), F32)
        out_specs += [kv_spec, kv_spec]
        out_shape += [kv_shape, kv_shape]
        if kv_prev is not None:
            in_specs += [pl.BlockSpec(memory_space=pl.ANY)] * 2
            aliases = {len(args): 2, len(args) + 1: 3}
            args += list(kv_prev)
    return pl.pallas_call(
        functools.partial(_attn_kernel, latent, kv_prev is not None, layer, L, nb, qc, past),
        grid=(B // nb,),
        in_specs=in_specs,
        out_specs=out_specs,
        out_shape=out_shape,
        scratch_shapes=scratch,
        input_output_aliases=aliases,
        compiler_params=_cparams(1),
        name=f"attn_branch_L{L}",
    )(*args)


def _shifted(x, offset, period):
    rows = x.shape[0]
    t = lax.broadcasted_iota(jnp.int32, x.shape, 0)
    if rows != period:
        t = t % period
    rolled = pltpu.roll(x, (-offset) % rows, 0)
    valid = (t >= -offset) if offset < 0 else (t < period - offset)
    return jnp.where(valid, rolled, 0.0)


def _hyena_kernel(L, nb, h_ref, w0_ref, w1_ref, w2_ref, wg_ref, sw0_ref, sw1_ref, sw2_ref, sb0_ref, sb1_ref, sb2_ref,
                  fa_ref, fb_ref, fd_ref, bias_ref, fwd_ref, inv_ref, mix_out):
    hb = h_ref[...].reshape(nb * L, D_MODEL)

    def short_conv(w_ref, sw_ref, sb_ref):
        xs = _dot(hb, w_ref[0])
        w = sw_ref[0]
        return sb_ref[0] + _shifted(xs, -1, L) * w[0:1] + xs * w[1:2] + _shifted(xs, 1, L) * w[2:3]

    x0 = short_conv(w0_ref, sw0_ref, sb0_ref)
    x1 = short_conv(w1_ref, sw1_ref, sb1_ref)
    hv = short_conv(w2_ref, sw2_ref, sb2_ref)
    gated = x0 * _silu(_dot(hb, wg_ref[0]))
    z = x1 * hv
    zb = z.astype(BF16)
    fa, fb, fd = fa_ref[0], fb_ref[0], fd_ref[0]
    for b in range(nb):
        rs = slice(b * L, (b + 1) * L)
        zf = _dot(fwd_ref[...], zb[rs])
        re, im = zf[:L], zf[L:]
        y_re = re * fa - im * fb
        y_im = re * fb + im * fd
        y = _dot(inv_ref[:, :L], y_re.astype(BF16)) + _dot(inv_ref[:, L:], y_im.astype(BF16))
        y = y + z[rs] * bias_ref[0]
        mix_out[b] = (y * gated[rs]).astype(BF16)


def _hyena_branch(layer, nb, hmod, w_in_b, short_w, short_b, filt, bias, fwd, inv):
    B, L, _ = hmod.shape
    cb = COL_BLOCK
    nt = HY_W // cb
    fa, fb, fd = filt
    w_spec = lambda part: pl.BlockSpec((1, D_MODEL, cb), lambda i, b: (layer, 0, HY_BLOCK0 + part * nt + i))
    sw_spec = lambda part: pl.BlockSpec((1, 3, cb), lambda i, b: (layer, 0, part * nt + i))
    sb_spec = lambda part: pl.BlockSpec((1, 1, cb), lambda i, b: (layer, 0, part * nt + i))
    filt_spec = pl.BlockSpec((1, L, cb), lambda i, b: (layer, 0, i))
    return pl.pallas_call(
        functools.partial(_hyena_kernel, L, nb),
        grid=(nt, B // nb),
        in_specs=[
            pl.BlockSpec((nb, L, D_MODEL), lambda i, b: (b, 0, 0)),
            w_spec(0), w_spec(1), w_spec(2), w_spec(3),
            sw_spec(0), sw_spec(1), sw_spec(2),
            sb_spec(0), sb_spec(1), sb_spec(2),
            filt_spec, filt_spec, filt_spec,
            pl.BlockSpec((1, 1, cb), lambda i, b: (layer, 0, i)),
            pl.BlockSpec((2 * L, L), lambda i, b: (0, 0)),
            pl.BlockSpec((L, 2 * L), lambda i, b: (0, 0)),
        ],
        out_specs=pl.BlockSpec((nb, L, cb), lambda i, b: (b, 0, i)),
        out_shape=jax.ShapeDtypeStruct((B, L, HY_W), BF16),
        compiler_params=_cparams(2),
        name=f"hyena_branch_L{L}",
    )(hmod, w_in_b, w_in_b, w_in_b, w_in_b, short_w, short_w, short_w, short_b, short_b, short_b,
      fa, fb, fd, bias, fwd, inv)


def _lru_kernel(latent, aliased, final, layer, L, nb, *refs):
    (h_ref, wx0_ref, wx1_ref, wg0_ref, wg1_ref, cw_ref, cb_ref, wa_ref, ba_ref, wxg_ref, bx_ref, lam_ref,
     x_ref, ma_ref, mh_ref, wo_ref, mod_ref, fg_ref) = refs[:18]
    refs = refs[18:]
    if latent:
        h0_ref, x_out, gate_w_s, a_s, b_s, y_s, g_s, part_s = refs
    else:
        if aliased:
            refs = refs[1:]
        x_out, hl_out, gate_w_s, a_s, b_s, y_s, g_s, part_s = refs
    half_w = LRU_W // 2
    blocks_per_half = half_w // LRU_BS
    n_tiles = L // SUBLANES

    @pl.when(pl.program_id(0) == 0)
    def _build_gate_weights():
        gate_w_s[...] = jnp.zeros(gate_w_s.shape, BF16)
        for half in range(2):
            for kind, (w_ref, d) in enumerate(((wa_ref, 0), (wxg_ref, 0), (wa_ref, 1), (wxg_ref, 1))):
                for j in range(blocks_per_half):
                    blk = (0.5 * w_ref[0, d, half * blocks_per_half + j]).astype(BF16)
                    gate_w_s[half, j * LRU_BS:(j + 1) * LRU_BS,
                             kind * half_w + j * LRU_BS:kind * half_w + (j + 1) * LRU_BS] = blk

    part_s[...] = (_dot(ma_ref[...].reshape(nb * L, ATTN_W), wo_ref[0, 0:ATTN_W, :])
                   + _dot(mh_ref[...].reshape(nb * L, HY_W), wo_ref[0, ATTN_W:ATTN_W + HY_W, :]))

    hb = h_ref[...].reshape(nb * L, D_MODEL)
    xs = jnp.concatenate([_dot(hb, wx0_ref[0]), _dot(hb, wx1_ref[0])], axis=1)
    g_s[:, :half_w] = _dot(hb, wg0_ref[0])
    g_s[:, half_w:] = _dot(hb, wg1_ref[0])
    cw = cw_ref[0]
    xc = (cb_ref[0] + _shifted(xs, -1, L) * cw[0:1] + xs * cw[1:2] + _shifted(xs, 1, L) * cw[2:3]
          + _shifted(xs, 2, L) * cw[3:4])
    xcb = xc.astype(BF16)
    c_half = (-0.5 * LRU_C) * jax.nn.softplus(-lam_ref[0])
    ba, bx = 0.5 * ba_ref[0], 0.5 * bx_ref[0]
    for half in range(2):
        cols = slice(half * half_w, (half + 1) * half_w)
        bias = jnp.concatenate([ba[0:1, cols], bx[0:1, cols], ba[1:2, cols], bx[1:2, cols]], axis=1)
        gates = _dot(xcb[:, cols], gate_w_s[half]) + bias
        xh_half = 0.5 * xc[:, cols]
        for d in range(2):
            t_r = jnp.tanh(gates[:, (2 * d) * half_w:(2 * d + 1) * half_w])
            t_i = jnp.tanh(gates[:, (2 * d + 1) * half_w:(2 * d + 2) * half_w])
            ch = c_half[d:d + 1, cols]
            log_a = ch * t_r + ch
            a = jnp.exp(log_a)
            y = jnp.tanh(log_a) * (-1.0 - a * a)
            root = jnp.where(y > 0.0, y * lax.rsqrt(y), 0.0)
            bcoef = root * (xh_half * t_i + xh_half)
            a_s[d, :, :, cols] = a.reshape(nb * n_tiles, SUBLANES, half_w)
            b_s[d, :, :, cols] = bcoef.reshape(nb * n_tiles, SUBLANES, half_w)

    if latent:
        init = []
        for b in range(nb):
            h0 = h0_ref[b, 0]
            init += [h0[0:1], h0[1:2]]
    else:
        init = [jnp.zeros((1, LRU_W), F32)] * (2 * nb)

    def tile_step(i, carry):
        carry = list(carry)
        for j in range(SUBLANES):
            for b in range(nb):
                tf = b * n_tiles + i
                tb = b * n_tiles + (n_tiles - 1 - i)
                jb = SUBLANES - 1 - j
                hf = a_s[0, tf, j:j + 1, :] * carry[2 * b] + b_s[0, tf, j:j + 1, :]
                y_s[0, tf, j:j + 1, :] = hf
                hbk = a_s[1, tb, jb:jb + 1, :] * carry[2 * b + 1] + b_s[1, tb, jb:jb + 1, :]
                y_s[1, tb, jb:jb + 1, :] = hbk
                carry[2 * b], carry[2 * b + 1] = hf, hbk
        return tuple(carry)

    last = lax.fori_loop(0, n_tiles, tile_step, tuple(init))
    if not latent:
        if not aliased:
            hl_out[...] = jnp.zeros(hl_out.shape, F32)
        slot = 0 if aliased else layer
        for b in range(nb):
            hl_out[b, slot, 0:1, :] = last[2 * b]
            hl_out[b, slot, 1:2, :] = last[2 * b + 1]
    y = (y_s[0] + y_s[1]).reshape(nb * L, LRU_W)
    mix_l = (y * _silu(g_s[...])).astype(BF16)
    out = part_s[...] + _dot(mix_l, wo_ref[0, ATTN_W + HY_W:, :])
    xn = x_ref[...].reshape(nb * L, D_MODEL) + mod_ref[0, 0][2:3] * out
    if final:
        ms = jnp.mean(xn * xn, axis=-1, keepdims=True)
        xn = xn * lax.rsqrt(ms + EPS) * fg_ref[...]
    x_out[...] = xn.reshape(nb, L, D_MODEL)


def _lru_out_branch(latent, layer, nb, hmod, w_in_b, conv_w, conv_b, wa, ba, wx, bx, lam, x, mix_a, mix_h, w_out_b,
                    mod, final_g, state=None, hl_prev=None):
    B, L, _ = hmod.shape
    final = layer == DEPTH - 1
    cb = COL_BLOCK
    half_w = LRU_W // 2
    rows = nb * L
    per_layer = lambda shape: pl.BlockSpec((1,) + shape, lambda b: (layer,) + (0,) * len(shape))
    w_spec = lambda j: pl.BlockSpec((1, D_MODEL, cb), lambda b: (layer, 0, LRU_BLOCK0 + j))
    gate_blocks = (2, LRU_BLOCKS, LRU_BS, LRU_BS)
    in_specs = [pl.BlockSpec((nb, L, D_MODEL), lambda b: (b, 0, 0)), w_spec(0), w_spec(1), w_spec(2), w_spec(3),
                per_layer((4, LRU_W)), per_layer((1, LRU_W)), per_layer(gate_blocks), per_layer((2, LRU_W)),
                per_layer(gate_blocks), per_layer((2, LRU_W)), per_layer((2, LRU_W))]
    per_b = lambda w: pl.BlockSpec((nb, L, w), lambda b: (b, 0, 0))
    mod_row = (lambda b: (layer, 1 + b, 0, 0)) if latent else (lambda b: (layer, 0, 0, 0))
    in_specs += [per_b(D_MODEL), per_b(ATTN_W), per_b(HY_W), per_layer((D_MIX, D_MODEL)),
                 pl.BlockSpec((1, 1, 3, D_MODEL), mod_row), pl.BlockSpec((1, D_MODEL), lambda b: (0, 0))]
    args = [hmod, w_in_b, w_in_b, w_in_b, w_in_b, conv_w, conv_b, wa, ba, wx, bx, lam,
            x, mix_a, mix_h, w_out_b, mod, final_g]
    out_specs = [per_b(D_MODEL)]
    out_shape = [jax.ShapeDtypeStruct((B, L, D_MODEL), F32)]
    aliases = {}
    if latent:
        in_specs.append(pl.BlockSpec((nb, 1, 2, LRU_W), lambda b: (b, layer, 0, 0)))
        args.append(state)
    else:
        if hl_prev is not None:
            out_specs.append(pl.BlockSpec((nb, 1, 2, LRU_W), lambda b: (b, layer, 0, 0)))
        else:
            out_specs.append(pl.BlockSpec((nb, DEPTH, 2, LRU_W), lambda b: (b, 0, 0, 0)))
        out_shape.append(jax.ShapeDtypeStruct((B, DEPTH, 2, LRU_W), F32))
        if hl_prev is not None:
            in_specs.append(pl.BlockSpec(memory_space=pl.ANY))
            aliases = {len(args): 1}
            args.append(hl_prev)
    scan_buf = pltpu.VMEM((2, rows // SUBLANES, SUBLANES, LRU_W), F32)
    return pl.pallas_call(
        functools.partial(_lru_kernel, latent, hl_prev is not None, final, layer, L, nb),
        grid=(B // nb,),
        in_specs=in_specs,
        out_specs=out_specs,
        out_shape=out_shape,
        scratch_shapes=[pltpu.VMEM((2, half_w, 4 * half_w), BF16), scan_buf, scan_buf, scan_buf,
                        pltpu.VMEM((rows, LRU_W), F32), pltpu.VMEM((rows, D_MODEL), F32)],
        input_output_aliases=aliases,
        compiler_params=_cparams(1),
        name=f"lru_out_L{L}",
    )(*args)


def _mixer_pass(latent, x, p, filt, dft, rope=None, cache=None, state=None):
    B, L, _ = x.shape
    fwd, inv = dft
    nb = 1 if latent else 4
    nb_lru = 1 if latent else 2
    kv, hl = None, None
    for l in range(DEPTH):
        outs = _attention_branch(latent, l, nb, x, p['mod'], p['norm_g'], p['w_in'], p['gq'], p['gk'], p['e_avg'],
                                 rope, cache, kv)
        hmod, mix_a = outs[0], outs[1]
        if not latent:
            kv = (outs[2], outs[3])
        mix_h = _hyena_branch(l, nb, hmod, p['w_in'], p['short_w'], p['short_b'], filt, p['hy_bias'], fwd, inv)
        outs = _lru_out_branch(latent, l, nb_lru, hmod, p['w_in'], p['conv_w'], p['conv_b'], p['wa'], p['ba'], p['wx'],
                               p['bx'], p['lam'], x, mix_a, mix_h, p['w_out'], p['mod'], p['final_g'], state, hl)
        x = outs[0]
        if not latent:
            hl = outs[1]
    return x, kv, hl


def kernel(x_prompt, x_sample, cache_k, cache_v, state_lru, c, c_ctx, norm_g, w_ada, b_ada, w_in, q_norm_g, k_norm_g,
           hy_short_w, hy_short_b, hy_filt_w1, hy_filt_b1, hy_filt_w2, hy_filt_b2, hy_filt_w3, hy_filt_freq,
           hy_filt_decay, hy_bias, lru_conv_w, lru_conv_b, lru_wa, lru_ba, lru_wx, lru_bx, lru_lambda, w_out, final_g):
    batch, seq, _ = x_prompt.shape
    dec_batch, dec_seq, _ = x_sample.shape
    past = cache_k.shape[2]

    cvecs = jnp.concatenate([c_ctx[None, :], c, jnp.zeros((MOD_ROWS - 1 - dec_batch, D_MODEL), F32)], axis=0)
    mod = _modulation(cvecs, w_ada, b_ada).reshape(DEPTH, MOD_ROWS, 3, D_MODEL)

    w1p = jnp.pad(hy_filt_w1, ((0, 0), (0, FEAT_PAD - HY_EMB), (0, 0)))
    dft, filt = {}, {}
    for L in (seq, dec_seq):
        fwd, inv = (jnp.asarray(m) for m in _dft_tables(L))
        f_hi, f_lo = _split_bf16(fwd)
        dft[L] = (f_hi, inv.astype(BF16))
        filt[L] = _filter_spectrum(L, jnp.asarray(_hyena_feats(L)), f_hi, f_lo, w1p, hy_filt_b1[:, None, :],
                                   hy_filt_w2, hy_filt_b2[:, None, :], hy_filt_w3, hy_filt_freq,
                                   hy_filt_decay[:, None, :])

    params = {
        'mod': mod,
        'norm_g': norm_g[:, None, :],
        'w_in': w_in.astype(BF16),
        'gq': jnp.tile(q_norm_g, (1, N_Q_HEADS))[:, None, :],
        'gk': jnp.tile(k_norm_g, (1, N_KV_HEADS))[:, None, :],
        'e_avg': jnp.asarray(_head_average_matrix()).astype(BF16),
        'short_w': hy_short_w, 'short_b': hy_short_b[:, None, :], 'hy_bias': hy_bias[:, None, :],
        'conv_w': lru_conv_w, 'conv_b': lru_conv_b[:, None, :],
        'wa': lru_wa, 'ba': lru_ba, 'wx': lru_wx, 'bx': lru_bx, 'lam': lru_lambda,
        'w_out': w_out.astype(BF16),
        'final_g': final_g[None, :],
    }

    y_prompt, kv, new_lru = _mixer_pass(False, x_prompt, params, filt[seq], dft[seq])
    new_k = kv[0].reshape(batch, DEPTH, seq, N_KV_HEADS, HEAD_DIM)
    new_v = kv[1].reshape(batch, DEPTH, seq, N_KV_HEADS, HEAD_DIM)

    rope = tuple(jnp.asarray(t) for t in _rope_tables(dec_seq))
    cache = (cache_k.reshape(dec_batch, DEPTH, past, KV_W), cache_v.reshape(dec_batch, DEPTH, past, KV_W))
    y_sample, _, _ = _mixer_pass(True, x_sample, params, filt[dec_seq], dft[dec_seq], rope, cache, state_lru)
    return (y_prompt, y_sample, new_k, new_v, new_lru)
```

```python
import functools
import math

import numpy as np
import jax
import jax.numpy as jnp
from jax import lax
from jax.experimental import pallas as pl
from jax.experimental.pallas import tpu as pltpu

F32 = jnp.float32
BF16 = jnp.bfloat16

D_MODEL = 1024
DEPTH = 2
GRID_W = 64
HEAD_DIM = 64
N_Q_HEADS = 8
N_KV_HEADS = 2
Q_PER_KV = N_Q_HEADS // N_KV_HEADS
ATTN_W = N_Q_HEADS * HEAD_DIM
KV_W = N_KV_HEADS * HEAD_DIM
ROPE_THETA = 10000.0
ATTN_SCALE = HEAD_DIM ** -0.5
HY_W = 512
HY_BANDS = 16
HY_EMB = 2 * HY_BANDS + 1
HY_FH = 64
LRU_W = 512
LRU_BLOCKS = 8
LRU_BS = LRU_W // LRU_BLOCKS
LRU_C = 8.0
EPS = 1e-6

ATTN_COLS = ATTN_W + 2 * KV_W + ATTN_W
D_MIX = ATTN_W + HY_W + LRU_W

LANES = 128
SUBLANES = 8
MXU_DIM = 256
COL_BLOCK = MXU_DIM
HY_BLOCK0 = ATTN_COLS // COL_BLOCK
LRU_BLOCK0 = (ATTN_COLS + 4 * HY_W) // COL_BLOCK
Q_CHUNK = 256
Q_CHUNK_LATENT = 128
HEAD_PAIRS = Q_PER_KV // 2
FEAT_PAD = 128
MOD_ROWS = 8
VMEM_LIMIT = 56 * 1024 * 1024


def _cparams(n_axes):
    return pltpu.CompilerParams(dimension_semantics=("arbitrary",) * n_axes, vmem_limit_bytes=VMEM_LIMIT)


def _split_bf16(a):
    hi = a.astype(BF16)
    lo = (a - hi.astype(F32)).astype(BF16)
    return hi, lo


def _dot(a, b):
    return jnp.dot(a, b, preferred_element_type=F32)


def _dot3(a, b):
    a_hi, a_lo = _split_bf16(a)
    b_hi, b_lo = _split_bf16(b)
    return _dot(a_hi, b_hi) + _dot(a_hi, b_lo) + _dot(a_lo, b_hi)


def _dot3_const(m_hi, m_lo, b):
    b_hi, b_lo = _split_bf16(b)
    return _dot(m_hi, b_hi) + _dot(m_hi, b_lo) + _dot(m_lo, b_hi)


def _silu(x):
    half = 0.5 * x
    return half * jnp.tanh(half) + half


@functools.lru_cache(maxsize=None)
def _dft_tables(L):
    n = 2 * L
    t = np.arange(L)
    kt = (t[:, None] * t[None, :]) % n
    ang = 2.0 * np.pi * kt / n
    cosm, sinm = np.cos(ang), np.sin(ang)
    alt = (-1.0) ** t
    f_s = -sinm
    f_s[0, :] = alt
    fwd = np.concatenate([cosm, f_s], axis=0)
    g_c = 2.0 * cosm.T / n
    g_c[:, 0] = 1.0 / n
    g_s = -2.0 * sinm.T / n
    g_s[:, 0] = alt / n
    inv = np.concatenate([g_c, g_s], axis=1)
    return fwd.astype(np.float32), inv.astype(np.float32)


@functools.lru_cache(maxsize=None)
def _hyena_feats(L):
    t = np.linspace(0.0, 1.0, L)[:, None]
    w = 2.0 * math.pi * np.arange(L)[:, None] / L
    f = np.linspace(1e-4, HY_BANDS - 1, HY_BANDS)[None, :]
    out = np.zeros((L, FEAT_PAD), np.float32)
    out[:, :HY_EMB] = np.concatenate([t, np.cos(f * w), -np.sin(f * w)], axis=-1)
    return out


@functools.lru_cache(maxsize=None)
def _rope_tables(L):
    rows = L // GRID_W
    row = np.repeat(np.arange(rows), GRID_W).astype(np.float64)
    col = np.tile(np.arange(GRID_W), rows).astype(np.float64)
    n_freq = HEAD_DIM // 4
    inv = ROPE_THETA ** (-np.arange(n_freq) / n_freq)
    ang = np.concatenate([row[:, None] * inv, col[:, None] * inv], axis=-1)
    cos = np.repeat(np.cos(ang), 2, axis=-1)
    sin = np.repeat(np.sin(ang), 2, axis=-1) * np.tile(np.array([-1.0, 1.0]), HEAD_DIM // 2)
    reps = LANES // HEAD_DIM
    return np.tile(cos, (1, reps)).astype(np.float32), np.tile(sin, (1, reps)).astype(np.float32)


@functools.lru_cache(maxsize=None)
def _head_average_matrix():
    return np.kron(np.eye(MXU_DIM // HEAD_DIM), np.full((HEAD_DIM, HEAD_DIM), 1.0 / HEAD_DIM)).astype(np.float32)


def _mod_kernel(c_ref, w_ref, b_ref, o_ref):
    s = _silu(c_ref[...])
    o_ref[0] = _dot3(s, w_ref[0]) + b_ref[0]


def _modulation(cvecs, w_ada, b_ada):
    tn = D_MODEL
    return pl.pallas_call(
        _mod_kernel,
        grid=(DEPTH, 3 * D_MODEL // tn),
        in_specs=[
            pl.BlockSpec((MOD_ROWS, D_MODEL), lambda l, j: (0, 0)),
            pl.BlockSpec((1, D_MODEL, tn), lambda l, j: (l, 0, j)),
            pl.BlockSpec((1, 1, tn), lambda l, j: (l, 0, j)),
        ],
        out_specs=pl.BlockSpec((1, MOD_ROWS, tn), lambda l, j: (l, 0, j)),
        out_shape=jax.ShapeDtypeStruct((DEPTH, MOD_ROWS, 3 * D_MODEL), F32),
        compiler_params=_cparams(2),
        name="adaln_mod",
    )(cvecs, w_ada, b_ada.reshape(DEPTH, 1, 3 * D_MODEL))


def _filter_kernel(L, feats_ref, w1_ref, b1_ref, w2_ref, b2_ref, w3_ref, freq_ref, decay_ref,
                   fhi_ref, flo_ref, a_ref, b_ref, d_ref):
    feats = feats_ref[...]
    t = feats[:, 0:1]
    freq = freq_ref[0]
    hdn = jnp.sin(freq[0:1] * (_dot3(feats, w1_ref[0]) + b1_ref[0]))
    hdn = jnp.sin(freq[1:2] * (_dot3(hdn, w2_ref[0]) + b2_ref[0]))
    h = _dot3(hdn, w3_ref[0]) * jnp.exp(-t * decay_ref[0])
    row = lax.broadcasted_iota(jnp.int32, (L, HY_W), 0)
    h_fwd = h[:, :HY_W]
    h_bwd = jnp.where(row >= 1, h[:, HY_W:], 0.0)
    inv_norm = 1.0 / jnp.sum(jnp.abs(h_fwd) + jnp.abs(h_bwd), axis=0, keepdims=True)
    even = h_fwd + h_bwd
    odd = h_fwd - h_bwd
    h_re = _dot3_const(fhi_ref[0:L, :], flo_ref[0:L, :], even) * inv_norm
    h_im = _dot3_const(fhi_ref[L:2 * L, :], flo_ref[L:2 * L, :], odd) * inv_norm
    alt = jnp.where(row % 2 == 0, 1.0, -1.0)
    nyq = jnp.sum(even * alt, axis=0, keepdims=True) * inv_norm
    a_ref[0] = h_re
    b_ref[0] = jnp.where(row >= 1, h_im, 0.0)
    d_ref[0] = jnp.where(row >= 1, h_re, nyq)


def _filter_spectrum(L, feats, f_hi, f_lo, w1p, b1, w2, b2, w3, freq, decay):
    full = lambda shape: pl.BlockSpec(shape, lambda l: (0,) * len(shape))
    per_layer = lambda shape: pl.BlockSpec((1,) + shape, lambda l: (l,) + (0,) * len(shape))
    out = jax.ShapeDtypeStruct((DEPTH, L, HY_W), F32)
    return pl.pallas_call(
        functools.partial(_filter_kernel, L),
        grid=(DEPTH,),
        in_specs=[
            full((L, FEAT_PAD)),
            per_layer((FEAT_PAD, HY_FH)), per_layer((1, HY_FH)),
            per_layer((HY_FH, HY_FH)), per_layer((1, HY_FH)),
            per_layer((HY_FH, 2 * HY_W)), per_layer((2, HY_FH)), per_layer((1, 2 * HY_W)),
            full((2 * L, L)), full((2 * L, L)),
        ],
        out_specs=[per_layer((L, HY_W))] * 3,
        out_shape=[out, out, out],
        compiler_params=_cparams(1),
        name=f"hyena_filter_L{L}",
    )(feats, w1p, b1, w2, b2, w3, freq, decay, f_hi, f_lo)


def _rope(x, cos, sin):
    lane = lax.broadcasted_iota(jnp.int32, x.shape, 1)
    partner = jnp.where(lane % 2 == 0, pltpu.roll(x, LANES - 1, 1), pltpu.roll(x, 1, 1))
    return x * cos + partner * sin


def _head_mean_square(x, e):
    sq = (x * x).astype(BF16)
    width = x.shape[1]
    if width <= MXU_DIM:
        return _dot(sq, e[:width, :width])
    parts = [_dot(sq[:, i:i + MXU_DIM], e) for i in range(0, width, MXU_DIM)]
    return jnp.concatenate(parts, axis=1)


def _store_kv_heads(k_dst, v_dst, k, v):
    ones = jnp.ones((v.shape[0], HEAD_DIM), BF16)
    for j in range(N_KV_HEADS):
        js = slice(j * HEAD_DIM, (j + 1) * HEAD_DIM)
        vj = v[:, js].astype(BF16)
        k_dst[j] = k[:, js].astype(BF16)
        v_dst[j, 0, :, 0:HEAD_DIM] = vj
        v_dst[j, 0, :, HEAD_DIM:2 * HEAD_DIM] = ones
        v_dst[j, 1, :, 0:HEAD_DIM] = ones
        v_dst[j, 1, :, HEAD_DIM:2 * HEAD_DIM] = vj


def _attn_kernel(latent, aliased, layer, L, nb, qc, past, *refs):
    x_ref, mod_ref, ng_ref, w_ref, gq_ref, gk_ref, e_ref = refs[:7]
    refs = refs[7:]
    if latent:
        (cos_ref, sin_ref, ck_ref, cv_ref, h_out, mix_out,
         q_s, k_s, v_s, g_s, o_s, s_s, mx_s, acc_s, ck_s, cv_s) = refs
    else:
        if aliased:
            refs = refs[2:]
        h_out, mix_out, k_out, v_out, q_s, k_s, v_s, g_s, o_s, s_s, mx_s, acc_s = refs
    rows = nb * L

    x = x_ref[...].reshape(rows, D_MODEL)
    m = mod_ref[0, 0]
    shift, scale = m[0:1], m[1:2]
    ms = jnp.mean(x * x, axis=-1, keepdims=True)
    h = (x * lax.rsqrt(ms + EPS) * ng_ref[0]) * (1.0 + scale) + shift
    hb = h.astype(BF16)
    h_out[...] = hb.reshape(nb, L, D_MODEL)

    u = _dot(hb, w_ref[0])
    q = u[:, :ATTN_W]
    k = u[:, ATTN_W:ATTN_W + KV_W]
    v = u[:, ATTN_W + KV_W:ATTN_W + 2 * KV_W]
    g_s[...] = u[:, ATTN_W + 2 * KV_W:]
    e = e_ref[...]
    qn = q * lax.rsqrt(_head_mean_square(q, e) + EPS) * gq_ref[0]
    kn = k * lax.rsqrt(_head_mean_square(k, e) + EPS) * gk_ref[0]
    if latent:
        cos, sin = cos_ref[...], sin_ref[...]
        qn = jnp.concatenate([_rope(qn[:, i:i + LANES], cos, sin) for i in range(0, ATTN_W, LANES)], axis=1)
        kn = _rope(kn, cos, sin)
        _store_kv_heads(ck_s, cv_s, ck_ref[0, 0], cv_ref[0, 0])
    else:
        for b in range(nb):
            kb, vb = kn[b * L:(b + 1) * L], v[b * L:(b + 1) * L]
            if aliased:
                k_out[b, 0] = kb
                v_out[b, 0] = vb
            else:
                for slot in range(DEPTH):
                    k_out[b, slot] = kb if slot == layer else jnp.zeros_like(kb)
                    v_out[b, slot] = vb if slot == layer else jnp.zeros_like(vb)
    qb = (qn * ATTN_SCALE).astype(BF16)
    for hd in range(N_Q_HEADS):
        j, g = divmod(hd, Q_PER_KV)
        q_s[j * Q_PER_KV + (g % 2) * HEAD_PAIRS + g // 2] = qb[:, hd * HEAD_DIM:(hd + 1) * HEAD_DIM]
    _store_kv_heads(k_s, v_s, kn, v)

    contract_last = (((1,), (1,)), ((), ()))
    chunks_per_seq = L // qc
    lane = lax.broadcasted_iota(jnp.int32, (qc, LANES), 1)
    groups = [(j, e) for j in range(N_KV_HEADS) for e in range(2)]

    def chunk(c, carry):
        r0 = pl.multiple_of(c * qc, qc)
        k0 = pl.multiple_of((c // chunks_per_seq) * L, L)
        for idx, (j, e) in enumerate(groups):
            slot = j * Q_PER_KV + e * HEAD_PAIRS
            qh = q_s[slot:slot + HEAD_PAIRS, pl.ds(r0, qc), :].reshape(HEAD_PAIRS * qc, HEAD_DIM)
            s = lax.dot_general(qh, k_s[j, pl.ds(k0, L), :], contract_last, preferred_element_type=F32)
            mx = jnp.max(s, axis=-1, keepdims=True)
            if latent:
                s0 = lax.dot_general(qh, ck_s[j], contract_last, preferred_element_type=F32)
                mx = jnp.maximum(mx, jnp.max(s0, axis=-1, keepdims=True))
                s_s[idx, :, 0:past] = s0
                s_s[idx, :, past:] = s
            else:
                s_s[idx] = s
            mx_s[idx] = jnp.broadcast_to(mx, (HEAD_PAIRS * qc, LANES))
        for idx, (j, e) in enumerate(groups):
            p = jnp.exp(s_s[idx] - jnp.tile(mx_s[idx], (1, (past + L) // LANES))).astype(BF16)
            if latent:
                acc = _dot(p[:, :past], cv_s[j, e]) + _dot(p[:, past:], v_s[j, e, pl.ds(k0, L), :])
            else:
                acc = _dot(p, v_s[j, e, pl.ds(k0, L), :])
            acc_s[idx] = acc
        for j in range(N_KV_HEADS):
            for t in range(HEAD_PAIRS):
                a_even = acc_s[2 * j, t * qc:(t + 1) * qc, :]
                a_odd = acc_s[2 * j + 1, t * qc:(t + 1) * qc, :]
                out = jnp.where(lane < HEAD_DIM, a_even, a_odd)
                den = pltpu.roll(jnp.where(lane < HEAD_DIM, a_odd, a_even), HEAD_DIM, 1)
                blk = j * HEAD_PAIRS + t
                o_s[pl.ds(r0, qc), blk * LANES:(blk + 1) * LANES] = out / den
        return carry

    lax.fori_loop(0, nb * chunks_per_seq, chunk, 0)
    mix_out[...] = (o_s[...] * _silu(g_s[...])).astype(BF16).reshape(nb, L, ATTN_W)


def _attention_branch(latent, layer, nb, x, mod, norm_g, w_in_b, gq, gk, e_avg, rope=None, cache=None, kv_prev=None):
    B, L, _ = x.shape
    rows = nb * L
    assert nb == 1 or not latent
    full = lambda shape: pl.BlockSpec(shape, lambda b: (0,) * len(shape))
    per_b = lambda shape: pl.BlockSpec((nb,) + shape, lambda b: (b,) + (0,) * len(shape))
    per_layer = lambda shape: pl.BlockSpec((1,) + shape, lambda b: (layer,) + (0,) * len(shape))
    mod_spec = pl.BlockSpec((1, 1, 3, D_MODEL), (lambda b: (layer, 1 + b, 0, 0)) if latent else (lambda b: (layer, 0, 0, 0)))
    in_specs = [per_b((L, D_MODEL)), mod_spec, per_layer((1, D_MODEL)), per_layer((D_MODEL, ATTN_COLS)),
                per_layer((1, ATTN_W)), per_layer((1, KV_W)), full((MXU_DIM, MXU_DIM))]
    args = [x, mod, norm_g, w_in_b, gq, gk, e_avg]
    out_specs = [per_b((L, D_MODEL)), per_b((L, ATTN_W))]
    out_shape = [jax.ShapeDtypeStruct((B, L, D_MODEL), BF16), jax.ShapeDtypeStruct((B, L, ATTN_W), BF16)]
    qc = Q_CHUNK_LATENT if latent else Q_CHUNK
    past = cache[0].shape[2] if latent else 0
    n_groups = 2 * N_KV_HEADS
    scratch = [pltpu.VMEM((N_Q_HEADS, rows, HEAD_DIM), BF16), pltpu.VMEM((N_KV_HEADS, rows, HEAD_DIM), BF16),
               pltpu.VMEM((N_KV_HEADS, 2, rows, 2 * HEAD_DIM), BF16),
               pltpu.VMEM((rows, ATTN_W), F32), pltpu.VMEM((rows, ATTN_W), F32),
               pltpu.VMEM((n_groups, HEAD_PAIRS * qc, past + L), F32),
               pltpu.VMEM((n_groups, HEAD_PAIRS * qc, LANES), F32),
               pltpu.VMEM((n_groups, HEAD_PAIRS * qc, 2 * HEAD_DIM), F32)]
    aliases = {}
    if latent:
        cache_k, cache_v = cache
        cache_spec = pl.BlockSpec((1, 1, past, KV_W), lambda b: (b, layer, 0, 0))
        in_specs += [full((L, LANES)), full((L, LANES)), cache_spec, cache_spec]
        args += [rope[0], rope[1], cache_k, cache_v]
        scratch += [pltpu.VMEM((N_KV_HEADS, past, HEAD_DIM), BF16),
                    pltpu.VMEM((N_KV_HEADS, 2, past, 2 * HEAD_DIM), BF16)]
    else:
        if kv_prev is not None:
            kv_spec = pl.BlockSpec((nb, 1, L, KV_W), lambda b: (b, layer, 0, 0))
        else:
            kv_spec = pl.BlockSpec((nb, DEPTH, L, KV_W), lambda b: (b, 0, 0, 0))
        kv_shape = jax.ShapeDtypeStruct((B, DEPTH, L, KV_W), F32)
        out_specs += [kv_spec, kv_spec]
        out_shape += [kv_shape, kv_shape]
        if kv_prev is not None:
            in_specs += [pl.BlockSpec(memory_space=pl.ANY)] * 2
            aliases = {len(args): 2, len(args) + 1: 3}
            args += list(kv_prev)
    return pl.pallas_call(
        functools.partial(_attn_kernel, latent, kv_prev is not None, layer, L, nb, qc, past),
        grid=(B // nb,),
        in_specs=in_specs,
        out_specs=out_specs,
        out_shape=out_shape,
        scratch_shapes=scratch,
        input_output_aliases=aliases,
        compiler_params=_cparams(1),
        name=f"attn_branch_L{L}",
    )(*args)


def _shifted(x, offset, period):
    rows = x.shape[0]
    t = lax.broadcasted_iota(jnp.int32, x.shape, 0)
    if rows != period:
        t = t % period
    rolled = pltpu.roll(x, (-offset) % rows, 0)
    valid = (t >= -offset) if offset < 0 else (t < period - offset)
    return jnp.where(valid, rolled, 0.0)


def _hyena_kernel(L, nb, h_ref, w0_ref, w1_ref, w2_ref, wg_ref, sw0_ref, sw1_ref, sw2_ref, sb0_ref, sb1_ref, sb2_ref,
                  fa_ref, fb_ref, fd_ref, bias_ref, fwd_ref, inv_ref, mix_out):
    hb = h_ref[...].reshape(nb * L, D_MODEL)

    def short_conv(w_ref, sw_ref, sb_ref):
        xs = _dot(hb, w_ref[0])
        w = sw_ref[0]
        return sb_ref[0] + _shifted(xs, -1, L) * w[0:1] + xs * w[1:2] + _shifted(xs, 1, L) * w[2:3]

    x0 = short_conv(w0_ref, sw0_ref, sb0_ref)
    x1 = short_conv(w1_ref, sw1_ref, sb1_ref)
    hv = short_conv(w2_ref, sw2_ref, sb2_ref)
    gated = x0 * _silu(_dot(hb, wg_ref[0]))
    z = x1 * hv
    zb = z.astype(BF16)
    fa, fb, fd = fa_ref[0], fb_ref[0], fd_ref[0]
    for b in range(nb):
        rs = slice(b * L, (b + 1) * L)
        zf = _dot(fwd_ref[...], zb[rs])
        re, im = zf[:L], zf[L:]
        y_re = re * fa - im * fb
        y_im = re * fb + im * fd
        y = _dot(inv_ref[:, :L], y_re.astype(BF16)) + _dot(inv_ref[:, L:], y_im.astype(BF16))
        y = y + z[rs] * bias_ref[0]
        mix_out[b] = (y * gated[rs]).astype(BF16)


def _hyena_branch(layer, nb, hmod, w_in_b, short_w, short_b, filt, bias, fwd, inv):
    B, L, _ = hmod.shape
    cb = COL_BLOCK
    nt = HY_W // cb
    fa, fb, fd = filt
    w_spec = lambda part: pl.BlockSpec((1, D_MODEL, cb), lambda i, b: (layer, 0, HY_BLOCK0 + part * nt + i))
    sw_spec = lambda part: pl.BlockSpec((1, 3, cb), lambda i, b: (layer, 0, part * nt + i))
    sb_spec = lambda part: pl.BlockSpec((1, 1, cb), lambda i, b: (layer, 0, part * nt + i))
    filt_spec = pl.BlockSpec((1, L, cb), lambda i, b: (layer, 0, i))
    return pl.pallas_call(
        functools.partial(_hyena_kernel, L, nb),
        grid=(nt, B // nb),
        in_specs=[
            pl.BlockSpec((nb, L, D_MODEL), lambda i, b: (b, 0, 0)),
            w_spec(0), w_spec(1), w_spec(2), w_spec(3),
            sw_spec(0), sw_spec(1), sw_spec(2),
            sb_spec(0), sb_spec(1), sb_spec(2),
            filt_spec, filt_spec, filt_spec,
            pl.BlockSpec((1, 1, cb), lambda i, b: (layer, 0, i)),
            pl.BlockSpec((2 * L, L), lambda i, b: (0, 0)),
            pl.BlockSpec((L, 2 * L), lambda i, b: (0, 0)),
        ],
        out_specs=pl.BlockSpec((nb, L, cb), lambda i, b: (b, 0, i)),
        out_shape=jax.ShapeDtypeStruct((B, L, HY_W), BF16),
        compiler_params=_cparams(2),
        name=f"hyena_branch_L{L}",
    )(hmod, w_in_b, w_in_b, w_in_b, w_in_b, short_w, short_w, short_w, short_b, short_b, short_b,
      fa, fb, fd, bias, fwd, inv)


def _lru_kernel(latent, aliased, final, layer, L, nb, *refs):
    (h_ref, wx0_ref, wx1_ref, wg0_ref, wg1_ref, cw_ref, cb_ref, wa_ref, ba_ref, wxg_ref, bx_ref, lam_ref,
     x_ref, ma_ref, mh_ref, wo_ref, mod_ref, fg_ref) = refs[:18]
    refs = refs[18:]
    if latent:
        h0_ref, x_out, gate_w_s, a_s, b_s, y_s, g_s, part_s = refs
    else:
        if aliased:
            refs = refs[1:]
        x_out, hl_out, gate_w_s, a_s, b_s, y_s, g_s, part_s = refs
    half_w = LRU_W // 2
    blocks_per_half = half_w // LRU_BS
    n_tiles = L // SUBLANES

    @pl.when(pl.program_id(0) == 0)
    def _build_gate_weights():
        gate_w_s[...] = jnp.zeros(gate_w_s.shape, BF16)
        for half in range(2):
            for kind, (w_ref, d) in enumerate(((wa_ref, 0), (wxg_ref, 0), (wa_ref, 1), (wxg_ref, 1))):
                for j in range(blocks_per_half):
                    blk = (0.5 * w_ref[0, d, half * blocks_per_half + j]).astype(BF16)
                    gate_w_s[half, j * LRU_BS:(j + 1) * LRU_BS,
                             kind * half_w + j * LRU_BS:kind * half_w + (j + 1) * LRU_BS] = blk

    part_s[...] = (_dot(ma_ref[...].reshape(nb * L, ATTN_W), wo_ref[0, 0:ATTN_W, :])
                   + _dot(mh_ref[...].reshape(nb * L, HY_W), wo_ref[0, ATTN_W:ATTN_W + HY_W, :]))

    hb = h_ref[...].reshape(nb * L, D_MODEL)
    xs = jnp.concatenate([_dot(hb, wx0_ref[0]), _dot(hb, wx1_ref[0])], axis=1)
    g_s[:, :half_w] = _dot(hb, wg0_ref[0])
    g_s[:, half_w:] = _dot(hb, wg1_ref[0])
    cw = cw_ref[0]
    xc = (cb_ref[0] + _shifted(xs, -1, L) * cw[0:1] + xs * cw[1:2] + _shifted(xs, 1, L) * cw[2:3]
          + _shifted(xs, 2, L) * cw[3:4])
    xcb = xc.astype(BF16)
    c_half = (-0.5 * LRU_C) * jax.nn.softplus(-lam_ref[0])
    ba, bx = 0.5 * ba_ref[0], 0.5 * bx_ref[0]
    for half in range(2):
        cols = slice(half * half_w, (half + 1) * half_w)
        bias = jnp.concatenate([ba[0:1, cols], bx[0:1, cols], ba[1:2, cols], bx[1:2, cols]], axis=1)
        gates = _dot(xcb[:, cols], gate_w_s[half]) + bias
        xh_half = 0.5 * xc[:, cols]
        for d in range(2):
            t_r = jnp.tanh(gates[:, (2 * d) * half_w:(2 * d + 1) * half_w])
            t_i = jnp.tanh(gates[:, (2 * d + 1) * half_w:(2 * d + 2) * half_w])
            ch = c_half[d:d + 1, cols]
            log_a = ch * t_r + ch
            a = jnp.exp(log_a)
            y = jnp.tanh(log_a) * (-1.0 - a * a)
            root = jnp.where(y > 0.0, y * lax.rsqrt(y), 0.0)
            bcoef = root * (xh_half * t_i + xh_half)
            a_s[d, :, :, cols] = a.reshape(nb * n_tiles, SUBLANES, half_w)
            b_s[d, :, :, cols] = bcoef.reshape(nb * n_tiles, SUBLANES, half_w)

    if latent:
        init = []
        for b in range(nb):
            h0 = h0_ref[b, 0]
            init += [h0[0:1], h0[1:2]]
    else:
        init = [jnp.zeros((1, LRU_W), F32)] * (2 * nb)

    def advance(d, tile, rows_g, h):
        a = [a_s[d, tile, r:r + 1, :] for r in rows_g]
        c = [b_s[d, tile, r:r + 1, :] for r in rows_g]
        a01, c01 = a[1] * a[0], a[1] * c[0] + c[1]
        hs = [a[0] * h + c[0], a01 * h + c01]
        if len(rows_g) == 4:
            a23, c23 = a[3] * a[2], a[3] * c[2] + c[3]
            hs += [a[2] * hs[1] + c[2], (a23 * a01) * h + (a23 * c01 + c23)]
        for r, v in zip(rows_g, hs):
            y_s[d, tile, r:r + 1, :] = v
        return hs[-1]

    group = 4 if nb == 1 else 2

    def tile_step(i, carry):
        carry = list(carry)
        for g0 in range(0, SUBLANES, group):
            fwd_rows = list(range(g0, g0 + group))
            bwd_rows = [SUBLANES - 1 - r for r in fwd_rows]
            for b in range(nb):
                carry[2 * b] = advance(0, b * n_tiles + i, fwd_rows, carry[2 * b])
                carry[2 * b + 1] = advance(1, b * n_tiles + (n_tiles - 1 - i), bwd_rows, carry[2 * b + 1])
        return tuple(carry)

    def two_tiles(i, carry):
        return tile_step(2 * i + 1, tile_step(2 * i, carry))

    last = lax.fori_loop(0, n_tiles // 2, two_tiles, tuple(init))
    if not latent:
        if not aliased:
            hl_out[...] = jnp.zeros(hl_out.shape, F32)
        slot = 0 if aliased else layer
        for b in range(nb):
            hl_out[b, slot, 0:1, :] = last[2 * b]
            hl_out[b, slot, 1:2, :] = last[2 * b + 1]
    y = (y_s[0] + y_s[1]).reshape(nb * L, LRU_W)
    mix_l = (y * _silu(g_s[...])).astype(BF16)
    out = part_s[...] + _dot(mix_l, wo_ref[0, ATTN_W + HY_W:, :])
    xn = x_ref[...].reshape(nb * L, D_MODEL) + mod_ref[0, 0][2:3] * out
    if final:
        ms = jnp.mean(xn * xn, axis=-1, keepdims=True)
        xn = xn * lax.rsqrt(ms + EPS) * fg_ref[...]
    x_out[...] = xn.reshape(nb, L, D_MODEL)


def _lru_out_branch(latent, layer, nb, hmod, w_in_b, conv_w, conv_b, wa, ba, wx, bx, lam, x, mix_a, mix_h, w_out_b,
                    mod, final_g, state=None, hl_prev=None):
    B, L, _ = hmod.shape
    final = layer == DEPTH - 1
    cb = COL_BLOCK
    half_w = LRU_W // 2
    rows = nb * L
    per_layer = lambda shape: pl.BlockSpec((1,) + shape, lambda b: (layer,) + (0,) * len(shape))
    w_spec = lambda j: pl.BlockSpec((1, D_MODEL, cb), lambda b: (layer, 0, LRU_BLOCK0 + j))
    gate_blocks = (2, LRU_BLOCKS, LRU_BS, LRU_BS)
    in_specs = [pl.BlockSpec((nb, L, D_MODEL), lambda b: (b, 0, 0)), w_spec(0), w_spec(1), w_spec(2), w_spec(3),
                per_layer((4, LRU_W)), per_layer((1, LRU_W)), per_layer(gate_blocks), per_layer((2, LRU_W)),
                per_layer(gate_blocks), per_layer((2, LRU_W)), per_layer((2, LRU_W))]
    per_b = lambda w: pl.BlockSpec((nb, L, w), lambda b: (b, 0, 0))
    mod_row = (lambda b: (layer, 1 + b, 0, 0)) if latent else (lambda b: (layer, 0, 0, 0))
    in_specs += [per_b(D_MODEL), per_b(ATTN_W), per_b(HY_W), per_layer((D_MIX, D_MODEL)),
                 pl.BlockSpec((1, 1, 3, D_MODEL), mod_row), pl.BlockSpec((1, D_MODEL), lambda b: (0, 0))]
    args = [hmod, w_in_b, w_in_b, w_in_b, w_in_b, conv_w, conv_b, wa, ba, wx, bx, lam,
            x, mix_a, mix_h, w_out_b, mod, final_g]
    out_specs = [per_b(D_MODEL)]
    out_shape = [jax.ShapeDtypeStruct((B, L, D_MODEL), F32)]
    aliases = {}
    if latent:
        in_specs.append(pl.BlockSpec((nb, 1, 2, LRU_W), lambda b: (b, layer, 0, 0)))
        args.append(state)
    else:
        if hl_prev is not None:
            out_specs.append(pl.BlockSpec((nb, 1, 2, LRU_W), lambda b: (b, layer, 0, 0)))
        else:
            out_specs.append(pl.BlockSpec((nb, DEPTH, 2, LRU_W), lambda b: (b, 0, 0, 0)))
        out_shape.append(jax.ShapeDtypeStruct((B, DEPTH, 2, LRU_W), F32))
        if hl_prev is not None:
            in_specs.append(pl.BlockSpec(memory_space=pl.ANY))
            aliases = {len(args): 1}
            args.append(hl_prev)
    scan_buf = pltpu.VMEM((2, rows // SUBLANES, SUBLANES, LRU_W), F32)
    return pl.pallas_call(
        functools.partial(_lru_kernel, latent, hl_prev is not None, final, layer, L, nb),
        grid=(B // nb,),
        in_specs=in_specs,
        out_specs=out_specs,
        out_shape=out_shape,
        scratch_shapes=[pltpu.VMEM((2, half_w, 4 * half_w), BF16), scan_buf, scan_buf, scan_buf,
                        pltpu.VMEM((rows, LRU_W), F32), pltpu.VMEM((rows, D_MODEL), F32)],
        input_output_aliases=aliases,
        compiler_params=_cparams(1),
        name=f"lru_out_L{L}",
    )(*args)


def _mixer_pass(latent, x, p, filt, dft, rope=None, cache=None, state=None):
    B, L, _ = x.shape
    fwd, inv = dft
    nb = 1 if latent else 4
    nb_lru = 1 if latent else 2
    kv, hl = None, None
    for l in range(DEPTH):
        outs = _attention_branch(latent, l, nb, x, p['mod'], p['norm_g'], p['w_in'], p['gq'], p['gk'], p['e_avg'],
                                 rope, cache, kv)
        hmod, mix_a = outs[0], outs[1]
        if not latent:
            kv = (outs[2], outs[3])
        mix_h = _hyena_branch(l, nb, hmod, p['w_in'], p['short_w'], p['short_b'], filt, p['hy_bias'], fwd, inv)
        outs = _lru_out_branch(latent, l, nb_lru, hmod, p['w_in'], p['conv_w'], p['conv_b'], p['wa'], p['ba'], p['wx'],
                               p['bx'], p['lam'], x, mix_a, mix_h, p['w_out'], p['mod'], p['final_g'], state, hl)
        x = outs[0]
        if not latent:
            hl = outs[1]
    return x, kv, hl


def kernel(x_prompt, x_sample, cache_k, cache_v, state_lru, c, c_ctx, norm_g, w_ada, b_ada, w_in, q_norm_g, k_norm_g,
           hy_short_w, hy_short_b, hy_filt_w1, hy_filt_b1, hy_filt_w2, hy_filt_b2, hy_filt_w3, hy_filt_freq,
           hy_filt_decay, hy_bias, lru_conv_w, lru_conv_b, lru_wa, lru_ba, lru_wx, lru_bx, lru_lambda, w_out, final_g):
    batch, seq, _ = x_prompt.shape
    dec_batch, dec_seq, _ = x_sample.shape
    past = cache_k.shape[2]

    cvecs = jnp.concatenate([c_ctx[None, :], c, jnp.zeros((MOD_ROWS - 1 - dec_batch, D_MODEL), F32)], axis=0)
    mod = _modulation(cvecs, w_ada, b_ada).reshape(DEPTH, MOD_ROWS, 3, D_MODEL)

    w1p = jnp.pad(hy_filt_w1, ((0, 0), (0, FEAT_PAD - HY_EMB), (0, 0)))
    dft, filt = {}, {}
    for L in (seq, dec_seq):
        fwd, inv = (jnp.asarray(m) for m in _dft_tables(L))
        f_hi, f_lo = _split_bf16(fwd)
        dft[L] = (f_hi, inv.astype(BF16))
        filt[L] = _filter_spectrum(L, jnp.asarray(_hyena_feats(L)), f_hi, f_lo, w1p, hy_filt_b1[:, None, :],
                                   hy_filt_w2, hy_filt_b2[:, None, :], hy_filt_w3, hy_filt_freq,
                                   hy_filt_decay[:, None, :])

    params = {
        'mod': mod,
        'norm_g': norm_g[:, None, :],
        'w_in': w_in.astype(BF16),
        'gq': jnp.tile(q_norm_g, (1, N_Q_HEADS))[:, None, :],
        'gk': jnp.tile(k_norm_g, (1, N_KV_HEADS))[:, None, :],
        'e_avg': jnp.asarray(_head_average_matrix()).astype(BF16),
        'short_w': hy_short_w, 'short_b': hy_short_b[:, None, :], 'hy_bias': hy_bias[:, None, :],
        'conv_w': lru_conv_w, 'conv_b': lru_conv_b[:, None, :],
        'wa': lru_wa, 'ba': lru_ba, 'wx': lru_wx, 'bx': lru_bx, 'lam': lru_lambda,
        'w_out': w_out.astype(BF16),
        'final_g': final_g[None, :],
    }

    y_prompt, kv, new_lru = _mixer_pass(False, x_prompt, params, filt[seq], dft[seq])
    new_k = kv[0].reshape(batch, DEPTH, seq, N_KV_HEADS, HEAD_DIM)
    new_v = kv[1].reshape(batch, DEPTH, seq, N_KV_HEADS, HEAD_DIM)

    rope = tuple(jnp.asarray(t) for t in _rope_tables(dec_seq))
    cache = (cache_k.reshape(dec_batch, DEPTH, past, KV_W), cache_v.reshape(dec_batch, DEPTH, past, KV_W))
    y_sample, _, _ = _mixer_pass(True, x_sample, params, filt[dec_seq], dft[dec_seq], rope, cache, state_lru)
    return (y_prompt, y_sample, new_k, new_v, new_lru)
```

```python
import functools
import math

import numpy as np
import jax
import jax.numpy as jnp
from jax import lax
from jax.experimental import pallas as pl
from jax.experimental.pallas import tpu as pltpu

F32 = jnp.float32
BF16 = jnp.bfloat16

D_MODEL = 1024
DEPTH = 2
GRID_W = 64
HEAD_DIM = 64
N_Q_HEADS = 8
N_KV_HEADS = 2
Q_PER_KV = N_Q_HEADS // N_KV_HEADS
ATTN_W = N_Q_HEADS * HEAD_DIM
KV_W = N_KV_HEADS * HEAD_DIM
ROPE_THETA = 10000.0
ATTN_SCALE = HEAD_DIM ** -0.5
HY_W = 512
HY_BANDS = 16
HY_EMB = 2 * HY_BANDS + 1
HY_FH = 64
LRU_W = 512
LRU_BLOCKS = 8
LRU_BS = LRU_W // LRU_BLOCKS
LRU_C = 8.0
EPS = 1e-6

ATTN_COLS = ATTN_W + 2 * KV_W + ATTN_W
D_MIX = ATTN_W + HY_W + LRU_W

LANES = 128
SUBLANES = 8
MXU_DIM = 256
COL_BLOCK = MXU_DIM
HY_BLOCK0 = ATTN_COLS // COL_BLOCK
LRU_BLOCK0 = (ATTN_COLS + 4 * HY_W) // COL_BLOCK
Q_CHUNK = 256
Q_CHUNK_LATENT = 128
HEAD_PAIRS = Q_PER_KV // 2
FEAT_PAD = 128
MOD_ROWS = 8
VMEM_LIMIT = 56 * 1024 * 1024


def _cparams(n_axes):
    return pltpu.CompilerParams(dimension_semantics=("arbitrary",) * n_axes, vmem_limit_bytes=VMEM_LIMIT)


def _split_bf16(a):
    hi = a.astype(BF16)
    lo = (a - hi.astype(F32)).astype(BF16)
    return hi, lo


def _dot(a, b):
    return jnp.dot(a, b, preferred_element_type=F32)


def _dot3(a, b):
    a_hi, a_lo = _split_bf16(a)
    b_hi, b_lo = _split_bf16(b)
    return _dot(a_hi, b_hi) + _dot(a_hi, b_lo) + _dot(a_lo, b_hi)


def _silu(x):
    half = 0.5 * x
    return half * jnp.tanh(half) + half


@functools.lru_cache(maxsize=None)
def _dft_tables(L):
    n = 2 * L
    t = np.arange(L)
    kt = (t[:, None] * t[None, :]) % n
    ang = 2.0 * np.pi * kt / n
    cosm, sinm = np.cos(ang), np.sin(ang)
    alt = (-1.0) ** t
    f_s = -sinm
    f_s[0, :] = alt
    fwd = np.concatenate([cosm, f_s], axis=0)
    g_c = 2.0 * cosm.T / n
    g_c[:, 0] = 1.0 / n
    g_s = -2.0 * sinm.T / n
    g_s[:, 0] = alt / n
    inv = np.concatenate([g_c, g_s], axis=1)
    return fwd.astype(np.float32), inv.astype(np.float32)


@functools.lru_cache(maxsize=None)
def _hyena_feats(L):
    t = np.linspace(0.0, 1.0, L)[:, None]
    w = 2.0 * math.pi * np.arange(L)[:, None] / L
    f = np.linspace(1e-4, HY_BANDS - 1, HY_BANDS)[None, :]
    out = np.zeros((L, FEAT_PAD), np.float32)
    out[:, :HY_EMB] = np.concatenate([t, np.cos(f * w), -np.sin(f * w)], axis=-1)
    return out


@functools.lru_cache(maxsize=None)
def _rope_tables(L):
    rows = L // GRID_W
    row = np.repeat(np.arange(rows), GRID_W).astype(np.float64)
    col = np.tile(np.arange(GRID_W), rows).astype(np.float64)
    n_freq = HEAD_DIM // 4
    inv = ROPE_THETA ** (-np.arange(n_freq) / n_freq)
    ang = np.concatenate([row[:, None] * inv, col[:, None] * inv], axis=-1)
    cos = np.repeat(np.cos(ang), 2, axis=-1)
    sin = np.repeat(np.sin(ang), 2, axis=-1) * np.tile(np.array([-1.0, 1.0]), HEAD_DIM // 2)
    reps = LANES // HEAD_DIM
    return np.tile(cos, (1, reps)).astype(np.float32), np.tile(sin, (1, reps)).astype(np.float32)


@functools.lru_cache(maxsize=None)
def _head_average_matrix():
    return np.kron(np.eye(MXU_DIM // HEAD_DIM), np.full((HEAD_DIM, HEAD_DIM), 1.0 / HEAD_DIM)).astype(np.float32)


def _mod_kernel(c_ref, w_ref, b_ref, o_ref):
    s = _silu(c_ref[...])
    o_ref[0] = _dot3(s, w_ref[0]) + b_ref[0]


def _modulation(cvecs, w_ada, b_ada):
    tn = D_MODEL
    return pl.pallas_call(
        _mod_kernel,
        grid=(DEPTH, 3 * D_MODEL // tn),
        in_specs=[
            pl.BlockSpec((MOD_ROWS, D_MODEL), lambda l, j: (0, 0)),
            pl.BlockSpec((1, D_MODEL, tn), lambda l, j: (l, 0, j)),
            pl.BlockSpec((1, 1, tn), lambda l, j: (l, 0, j)),
        ],
        out_specs=pl.BlockSpec((1, MOD_ROWS, tn), lambda l, j: (l, 0, j)),
        out_shape=jax.ShapeDtypeStruct((DEPTH, MOD_ROWS, 3 * D_MODEL), F32),
        compiler_params=_cparams(2),
        name="adaln_mod",
    )(cvecs, w_ada, b_ada.reshape(DEPTH, 1, 3 * D_MODEL))


def _filter_kernel(L, feats_ref, w1_ref, b1_ref, w2_ref, b2_ref, w3_ref, freq_ref, decay_ref,
                   fwd_ref, a_ref, b_ref, d_ref):
    feats = feats_ref[...]
    t = feats[:, 0:1]
    freq = freq_ref[0]
    hdn = jnp.sin(freq[0:1] * (_dot3(feats, w1_ref[0]) + b1_ref[0]))
    hdn = jnp.sin(freq[1:2] * (_dot3(hdn, w2_ref[0]) + b2_ref[0]))
    h = _dot3(hdn, w3_ref[0]) * jnp.exp(-t * decay_ref[0])
    row = lax.broadcasted_iota(jnp.int32, (L, HY_W), 0)
    h_fwd = h[:, :HY_W]
    h_bwd = jnp.where(row >= 1, h[:, HY_W:], 0.0)
    inv_norm = 1.0 / jnp.sum(jnp.abs(h_fwd) + jnp.abs(h_bwd), axis=0, keepdims=True)
    even = h_fwd + h_bwd
    odd = h_fwd - h_bwd
    h_re = _dot(fwd_ref[0:L, :], even.astype(BF16)) * inv_norm
    h_im = _dot(fwd_ref[L:2 * L, :], odd.astype(BF16)) * inv_norm
    alt = jnp.where(row % 2 == 0, 1.0, -1.0)
    nyq = jnp.sum(even * alt, axis=0, keepdims=True) * inv_norm
    a_ref[0] = h_re
    b_ref[0] = jnp.where(row >= 1, h_im, 0.0)
    d_ref[0] = jnp.where(row >= 1, h_re, nyq)


def _filter_spectrum(L, feats, fwd, w1p, b1, w2, b2, w3, freq, decay):
    full = lambda shape: pl.BlockSpec(shape, lambda l: (0,) * len(shape))
    per_layer = lambda shape: pl.BlockSpec((1,) + shape, lambda l: (l,) + (0,) * len(shape))
    out = jax.ShapeDtypeStruct((DEPTH, L, HY_W), F32)
    return pl.pallas_call(
        functools.partial(_filter_kernel, L),
        grid=(DEPTH,),
        in_specs=[
            full((L, FEAT_PAD)),
            per_layer((FEAT_PAD, HY_FH)), per_layer((1, HY_FH)),
            per_layer((HY_FH, HY_FH)), per_layer((1, HY_FH)),
            per_layer((HY_FH, 2 * HY_W)), per_layer((2, HY_FH)), per_layer((1, 2 * HY_W)),
            full((2 * L, L)),
        ],
        out_specs=[per_layer((L, HY_W))] * 3,
        out_shape=[out, out, out],
        compiler_params=_cparams(1),
        name=f"hyena_filter_L{L}",
    )(feats, w1p, b1, w2, b2, w3, freq, decay, fwd)


def _rope(x, cos, sin):
    lane = lax.broadcasted_iota(jnp.int32, x.shape, 1)
    partner = jnp.where(lane % 2 == 0, pltpu.roll(x, LANES - 1, 1), pltpu.roll(x, 1, 1))
    return x * cos + partner * sin


def _head_mean_square(x, e):
    sq = (x * x).astype(BF16)
    width = x.shape[1]
    if width <= MXU_DIM:
        return _dot(sq, e[:width, :width])
    parts = [_dot(sq[:, i:i + MXU_DIM], e) for i in range(0, width, MXU_DIM)]
    return jnp.concatenate(parts, axis=1)


def _store_kv_heads(k_dst, v_dst, k, v):
    ones = jnp.ones((v.shape[0], HEAD_DIM), BF16)
    for j in range(N_KV_HEADS):
        js = slice(j * HEAD_DIM, (j + 1) * HEAD_DIM)
        vj = v[:, js].astype(BF16)
        k_dst[j] = k[:, js].astype(BF16)
        v_dst[j, 0, :, 0:HEAD_DIM] = vj
        v_dst[j, 0, :, HEAD_DIM:2 * HEAD_DIM] = ones
        v_dst[j, 1, :, 0:HEAD_DIM] = ones
        v_dst[j, 1, :, HEAD_DIM:2 * HEAD_DIM] = vj


def _attn_kernel(latent, aliased, layer, L, nb, qc, past, *refs):
    x_ref, mod_ref, ng_ref, w_ref, gq_ref, gk_ref, e_ref = refs[:7]
    refs = refs[7:]
    if latent:
        (cos_ref, sin_ref, ck_ref, cv_ref, h_out, mix_out,
         q_s, k_s, v_s, g_s, o_s, s_s, mx_s, acc_s, ck_s, cv_s) = refs
    else:
        if aliased:
            refs = refs[2:]
        h_out, mix_out, k_out, v_out, q_s, k_s, v_s, g_s, o_s, s_s, mx_s, acc_s = refs
    rows = nb * L

    x = x_ref[...].reshape(rows, D_MODEL)
    m = mod_ref[0, 0]
    shift, scale = m[0:1], m[1:2]
    ms = jnp.mean(x * x, axis=-1, keepdims=True)
    h = (x * lax.rsqrt(ms + EPS)) * (ng_ref[0] * (1.0 + scale)) + shift
    hb = h.astype(BF16)
    h_out[...] = hb.reshape(nb, L, D_MODEL)

    u = _dot(hb, w_ref[0])
    q = u[:, :ATTN_W]
    k = u[:, ATTN_W:ATTN_W + KV_W]
    v = u[:, ATTN_W + KV_W:ATTN_W + 2 * KV_W]
    g_s[...] = u[:, ATTN_W + 2 * KV_W:]
    e = e_ref[...]
    qn = q * lax.rsqrt(_head_mean_square(q, e) + EPS) * gq_ref[0]
    kn = k * lax.rsqrt(_head_mean_square(k, e) + EPS) * gk_ref[0]
    if latent:
        cos, sin = cos_ref[...], sin_ref[...]
        qn = jnp.concatenate([_rope(qn[:, i:i + LANES], cos, sin) for i in range(0, ATTN_W, LANES)], axis=1)
        kn = _rope(kn, cos, sin)
        _store_kv_heads(ck_s, cv_s, ck_ref[0, 0], cv_ref[0, 0])
    else:
        for b in range(nb):
            kb, vb = kn[b * L:(b + 1) * L], v[b * L:(b + 1) * L]
            if aliased:
                k_out[b, 0] = kb
                v_out[b, 0] = vb
            else:
                for slot in range(DEPTH):
                    k_out[b, slot] = kb if slot == layer else jnp.zeros_like(kb)
                    v_out[b, slot] = vb if slot == layer else jnp.zeros_like(vb)
    qb = (qn * ATTN_SCALE).astype(BF16)
    for hd in range(N_Q_HEADS):
        j, g = divmod(hd, Q_PER_KV)
        q_s[j * Q_PER_KV + (g % 2) * HEAD_PAIRS + g // 2] = qb[:, hd * HEAD_DIM:(hd + 1) * HEAD_DIM]
    _store_kv_heads(k_s, v_s, kn, v)

    contract_last = (((1,), (1,)), ((), ()))
    chunks_per_seq = L // qc
    lane = lax.broadcasted_iota(jnp.int32, (qc, LANES), 1)
    groups = [(j, e) for j in range(N_KV_HEADS) for e in range(2)]

    def chunk(c, carry):
        r0 = pl.multiple_of(c * qc, qc)
        k0 = pl.multiple_of((c // chunks_per_seq) * L, L)
        for idx, (j, e) in enumerate(groups):
            slot = j * Q_PER_KV + e * HEAD_PAIRS
            qh = q_s[slot:slot + HEAD_PAIRS, pl.ds(r0, qc), :].reshape(HEAD_PAIRS * qc, HEAD_DIM)
            s = lax.dot_general(qh, k_s[j, pl.ds(k0, L), :], contract_last, preferred_element_type=F32)
            mx = jnp.max(s, axis=-1, keepdims=True)
            if latent:
                s0 = lax.dot_general(qh, ck_s[j], contract_last, preferred_element_type=F32)
                mx = jnp.maximum(mx, jnp.max(s0, axis=-1, keepdims=True))
                s_s[idx, :, 0:past] = s0
                s_s[idx, :, past:] = s
            else:
                s_s[idx] = s
            mx_s[idx] = jnp.broadcast_to(mx, (HEAD_PAIRS * qc, LANES))
        for idx, (j, e) in enumerate(groups):
            p = jnp.exp(s_s[idx] - jnp.tile(mx_s[idx], (1, (past + L) // LANES))).astype(BF16)
            if latent:
                acc = _dot(p[:, :past], cv_s[j, e]) + _dot(p[:, past:], v_s[j, e, pl.ds(k0, L), :])
            else:
                acc = _dot(p, v_s[j, e, pl.ds(k0, L), :])
            acc_s[idx] = acc
        for j in range(N_KV_HEADS):
            for t in range(HEAD_PAIRS):
                a_even = acc_s[2 * j, t * qc:(t + 1) * qc, :]
                a_odd = acc_s[2 * j + 1, t * qc:(t + 1) * qc, :]
                out = jnp.where(lane < HEAD_DIM, a_even, a_odd)
                den = pltpu.roll(jnp.where(lane < HEAD_DIM, a_odd, a_even), HEAD_DIM, 1)
                blk = j * HEAD_PAIRS + t
                o_s[pl.ds(r0, qc), blk * LANES:(blk + 1) * LANES] = out / den
        return carry

    lax.fori_loop(0, nb * chunks_per_seq, chunk, 0)
    mix_out[...] = (o_s[...] * _silu(g_s[...])).astype(BF16).reshape(nb, L, ATTN_W)


def _attention_branch(latent, layer, nb, x, mod, norm_g, w_in_b, gq, gk, e_avg, rope=None, cache=None, kv_prev=None):
    B, L, _ = x.shape
    rows = nb * L
    assert nb == 1 or not latent
    full = lambda shape: pl.BlockSpec(shape, lambda b: (0,) * len(shape))
    per_b = lambda shape: pl.BlockSpec((nb,) + shape, lambda b: (b,) + (0,) * len(shape))
    per_layer = lambda shape: pl.BlockSpec((1,) + shape, lambda b: (layer,) + (0,) * len(shape))
    mod_spec = pl.BlockSpec((1, 1, 3, D_MODEL), (lambda b: (layer, 1 + b, 0, 0)) if latent else (lambda b: (layer, 0, 0, 0)))
    in_specs = [per_b((L, D_MODEL)), mod_spec, per_layer((1, D_MODEL)), per_layer((D_MODEL, ATTN_COLS)),
                per_layer((1, ATTN_W)), per_layer((1, KV_W)), full((MXU_DIM, MXU_DIM))]
    args = [x, mod, norm_g, w_in_b, gq, gk, e_avg]
    out_specs = [per_b((L, D_MODEL)), per_b((L, ATTN_W))]
    out_shape = [jax.ShapeDtypeStruct((B, L, D_MODEL), BF16), jax.ShapeDtypeStruct((B, L, ATTN_W), BF16)]
    qc = Q_CHUNK_LATENT if latent else Q_CHUNK
    past = cache[0].shape[2] if latent else 0
    n_groups = 2 * N_KV_HEADS
    scratch = [pltpu.VMEM((N_Q_HEADS, rows, HEAD_DIM), BF16), pltpu.VMEM((N_KV_HEADS, rows, HEAD_DIM), BF16),
               pltpu.VMEM((N_KV_HEADS, 2, rows, 2 * HEAD_DIM), BF16),
               pltpu.VMEM((rows, ATTN_W), F32), pltpu.VMEM((rows, ATTN_W), F32),
               pltpu.VMEM((n_groups, HEAD_PAIRS * qc, past + L), F32),
               pltpu.VMEM((n_groups, HEAD_PAIRS * qc, LANES), F32),
               pltpu.VMEM((n_groups, HEAD_PAIRS * qc, 2 * HEAD_DIM), F32)]
    aliases = {}
    if latent:
        cache_k, cache_v = cache
        cache_spec = pl.BlockSpec((1, 1, past, KV_W), lambda b: (b, layer, 0, 0))
        in_specs += [full((L, LANES)), full((L, LANES)), cache_spec, cache_spec]
        args += [rope[0], rope[1], cache_k, cache_v]
        scratch += [pltpu.VMEM((N_KV_HEADS, past, HEAD_DIM), BF16),
                    pltpu.VMEM((N_KV_HEADS, 2, past, 2 * HEAD_DIM), BF16)]
    else:
        if kv_prev is not None:
            kv_spec = pl.BlockSpec((nb, 1, L, KV_W), lambda b: (b, layer, 0, 0))
        else:
            kv_spec = pl.BlockSpec((nb, DEPTH, L, KV_W), lambda b: (b, 0, 0, 0))
        kv_shape = jax.ShapeDtypeStruct((B, DEPTH, L, KV_W), F32)
        out_specs += [kv_spec, kv_spec]
        out_shape += [kv_shape, kv_shape]
        if kv_prev is not None:
            in_specs += [pl.BlockSpec(memory_space=pl.ANY)] * 2
            aliases = {len(args): 2, len(args) + 1: 3}
            args += list(kv_prev)
    return pl.pallas_call(
        functools.partial(_attn_kernel, latent, kv_prev is not None, layer, L, nb, qc, past),
        grid=(B // nb,),
        in_specs=in_specs,
        out_specs=out_specs,
        out_shape=out_shape,
        scratch_shapes=scratch,
        input_output_aliases=aliases,
        compiler_params=_cparams(1),
        name=f"attn_branch_L{L}",
    )(*args)


def _shifted(x, offset, period):
    rows = x.shape[0]
    t = lax.broadcasted_iota(jnp.int32, x.shape, 0)
    if rows != period:
        t = t % period
    rolled = pltpu.roll(x, (-offset) % rows, 0)
    valid = (t >= -offset) if offset < 0 else (t < period - offset)
    return jnp.where(valid, rolled, 0.0)


def _hyena_kernel(L, nb, h_ref, w0_ref, w1_ref, w2_ref, wg_ref, sw0_ref, sw1_ref, sw2_ref, sb0_ref, sb1_ref, sb2_ref,
                  fa_ref, fb_ref, fd_ref, bias_ref, fwd_ref, inv_ref, mix_out):
    hb = h_ref[...].reshape(nb * L, D_MODEL)

    def short_conv(w_ref, sw_ref, sb_ref):
        xs = _dot(hb, w_ref[0])
        w = sw_ref[0]
        return sb_ref[0] + _shifted(xs, -1, L) * w[0:1] + xs * w[1:2] + _shifted(xs, 1, L) * w[2:3]

    x0 = short_conv(w0_ref, sw0_ref, sb0_ref)
    x1 = short_conv(w1_ref, sw1_ref, sb1_ref)
    hv = short_conv(w2_ref, sw2_ref, sb2_ref)
    gated = x0 * _silu(_dot(hb, wg_ref[0]))
    z = x1 * hv
    zb = z.astype(BF16)
    fa, fb, fd = fa_ref[0], fb_ref[0], fd_ref[0]
    for b in range(nb):
        rs = slice(b * L, (b + 1) * L)
        zf = _dot(fwd_ref[...], zb[rs])
        re, im = zf[:L], zf[L:]
        y_re = re * fa - im * fb
        y_im = re * fb + im * fd
        y = _dot(inv_ref[:, :L], y_re.astype(BF16)) + _dot(inv_ref[:, L:], y_im.astype(BF16))
        y = y + z[rs] * bias_ref[0]
        mix_out[b] = (y * gated[rs]).astype(BF16)


def _hyena_branch(layer, nb, hmod, w_in_b, short_w, short_b, filt, bias, fwd, inv):
    B, L, _ = hmod.shape
    cb = COL_BLOCK
    nt = HY_W // cb
    fa, fb, fd = filt
    w_spec = lambda part: pl.BlockSpec((1, D_MODEL, cb), lambda i, b: (layer, 0, HY_BLOCK0 + part * nt + i))
    sw_spec = lambda part: pl.BlockSpec((1, 3, cb), lambda i, b: (layer, 0, part * nt + i))
    sb_spec = lambda part: pl.BlockSpec((1, 1, cb), lambda i, b: (layer, 0, part * nt + i))
    filt_spec = pl.BlockSpec((1, L, cb), lambda i, b: (layer, 0, i))
    return pl.pallas_call(
        functools.partial(_hyena_kernel, L, nb),
        grid=(nt, B // nb),
        in_specs=[
            pl.BlockSpec((nb, L, D_MODEL), lambda i, b: (b, 0, 0)),
            w_spec(0), w_spec(1), w_spec(2), w_spec(3),
            sw_spec(0), sw_spec(1), sw_spec(2),
            sb_spec(0), sb_spec(1), sb_spec(2),
            filt_spec, filt_spec, filt_spec,
            pl.BlockSpec((1, 1, cb), lambda i, b: (layer, 0, i)),
            pl.BlockSpec((2 * L, L), lambda i, b: (0, 0)),
            pl.BlockSpec((L, 2 * L), lambda i, b: (0, 0)),
        ],
        out_specs=pl.BlockSpec((nb, L, cb), lambda i, b: (b, 0, i)),
        out_shape=jax.ShapeDtypeStruct((B, L, HY_W), BF16),
        compiler_params=_cparams(2),
        name=f"hyena_branch_L{L}",
    )(hmod, w_in_b, w_in_b, w_in_b, w_in_b, short_w, short_w, short_w, short_b, short_b, short_b,
      fa, fb, fd, bias, fwd, inv)


def _lru_kernel(latent, aliased, final, layer, L, nb, *refs):
    (h_ref, wx0_ref, wx1_ref, wg0_ref, wg1_ref, cw_ref, cb_ref, wa_ref, ba_ref, wxg_ref, bx_ref, lam_ref,
     x_ref, ma_ref, mh_ref, wo_ref, mod_ref, fg_ref) = refs[:18]
    refs = refs[18:]
    if latent:
        h0_ref, x_out, gate_w_s, a_s, b_s, y_s, g_s, part_s = refs
    else:
        if aliased:
            refs = refs[1:]
        x_out, hl_out, gate_w_s, a_s, b_s, y_s, g_s, part_s = refs
    half_w = LRU_W // 2
    blocks_per_half = half_w // LRU_BS
    n_tiles = L // SUBLANES

    @pl.when(pl.program_id(0) == 0)
    def _build_gate_weights():
        gate_w_s[...] = jnp.zeros(gate_w_s.shape, BF16)
        for half in range(2):
            for kind, (w_ref, d) in enumerate(((wa_ref, 0), (wxg_ref, 0), (wa_ref, 1), (wxg_ref, 1))):
                for j in range(blocks_per_half):
                    blk = (0.5 * w_ref[0, d, half * blocks_per_half + j]).astype(BF16)
                    gate_w_s[half, j * LRU_BS:(j + 1) * LRU_BS,
                             kind * half_w + j * LRU_BS:kind * half_w + (j + 1) * LRU_BS] = blk

    part_s[...] = (_dot(ma_ref[...].reshape(nb * L, ATTN_W), wo_ref[0, 0:ATTN_W, :])
                   + _dot(mh_ref[...].reshape(nb * L, HY_W), wo_ref[0, ATTN_W:ATTN_W + HY_W, :]))

    hb = h_ref[...].reshape(nb * L, D_MODEL)
    xs = jnp.concatenate([_dot(hb, wx0_ref[0]), _dot(hb, wx1_ref[0])], axis=1)
    g_s[:, :half_w] = _dot(hb, wg0_ref[0])
    g_s[:, half_w:] = _dot(hb, wg1_ref[0])
    cw = cw_ref[0]
    xc = (cb_ref[0] + _shifted(xs, -1, L) * cw[0:1] + xs * cw[1:2] + _shifted(xs, 1, L) * cw[2:3]
          + _shifted(xs, 2, L) * cw[3:4])
    xcb = xc.astype(BF16)
    c_half = (-0.5 * LRU_C) * jax.nn.softplus(-lam_ref[0])
    ba, bx = 0.5 * ba_ref[0], 0.5 * bx_ref[0]
    for half in range(2):
        cols = slice(half * half_w, (half + 1) * half_w)
        bias = jnp.concatenate([ba[0:1, cols], bx[0:1, cols], ba[1:2, cols], bx[1:2, cols]], axis=1)
        gates = _dot(xcb[:, cols], gate_w_s[half]) + bias
        xh_half = 0.5 * xc[:, cols]
        for d in range(2):
            t_r = jnp.tanh(gates[:, (2 * d) * half_w:(2 * d + 1) * half_w])
            t_i = jnp.tanh(gates[:, (2 * d + 1) * half_w:(2 * d + 2) * half_w])
            ch = c_half[d:d + 1, cols]
            log_a = ch * t_r + ch
            a = jnp.exp(log_a)
            y = jnp.tanh(log_a) * (-1.0 - a * a)
            root = jnp.where(y > 0.0, y * lax.rsqrt(y), 0.0)
            bcoef = root * (xh_half * t_i + xh_half)
            a_s[d, :, :, cols] = a.reshape(nb * n_tiles, SUBLANES, half_w)
            b_s[d, :, :, cols] = bcoef.reshape(nb * n_tiles, SUBLANES, half_w)

    if latent:
        init = []
        for b in range(nb):
            h0 = h0_ref[b, 0]
            init += [h0[0:1], h0[1:2]]
    else:
        init = [jnp.zeros((1, LRU_W), F32)] * (2 * nb)

    def advance(d, tile, rows_g, h):
        a = [a_s[d, tile, r:r + 1, :] for r in rows_g]
        c = [b_s[d, tile, r:r + 1, :] for r in rows_g]
        a01, c01 = a[1] * a[0], a[1] * c[0] + c[1]
        hs = [a[0] * h + c[0], a01 * h + c01]
        if len(rows_g) == 4:
            a23, c23 = a[3] * a[2], a[3] * c[2] + c[3]
            hs += [a[2] * hs[1] + c[2], (a23 * a01) * h + (a23 * c01 + c23)]
        for r, v in zip(rows_g, hs):
            y_s[d, tile, r:r + 1, :] = v
        return hs[-1]

    group = 4 if nb == 1 else 2

    def tile_step(i, carry):
        carry = list(carry)
        for g0 in range(0, SUBLANES, group):
            fwd_rows = list(range(g0, g0 + group))
            bwd_rows = [SUBLANES - 1 - r for r in fwd_rows]
            for b in range(nb):
                carry[2 * b] = advance(0, b * n_tiles + i, fwd_rows, carry[2 * b])
                carry[2 * b + 1] = advance(1, b * n_tiles + (n_tiles - 1 - i), bwd_rows, carry[2 * b + 1])
        return tuple(carry)

    def two_tiles(i, carry):
        return tile_step(2 * i + 1, tile_step(2 * i, carry))

    last = lax.fori_loop(0, n_tiles // 2, two_tiles, tuple(init))
    if not latent:
        if not aliased:
            hl_out[...] = jnp.zeros(hl_out.shape, F32)
        slot = 0 if aliased else layer
        for b in range(nb):
            hl_out[b, slot, 0:1, :] = last[2 * b]
            hl_out[b, slot, 1:2, :] = last[2 * b + 1]
    y = (y_s[0] + y_s[1]).reshape(nb * L, LRU_W)
    mix_l = (y * _silu(g_s[...])).astype(BF16)
    out = part_s[...] + _dot(mix_l, wo_ref[0, ATTN_W + HY_W:, :])
    xn = x_ref[...].reshape(nb * L, D_MODEL) + mod_ref[0, 0][2:3] * out
    if final:
        ms = jnp.mean(xn * xn, axis=-1, keepdims=True)
        xn = xn * lax.rsqrt(ms + EPS) * fg_ref[...]
    x_out[...] = xn.reshape(nb, L, D_MODEL)


def _lru_out_branch(latent, layer, nb, hmod, w_in_b, conv_w, conv_b, wa, ba, wx, bx, lam, x, mix_a, mix_h, w_out_b,
                    mod, final_g, state=None, hl_prev=None):
    B, L, _ = hmod.shape
    final = layer == DEPTH - 1
    cb = COL_BLOCK
    half_w = LRU_W // 2
    rows = nb * L
    per_layer = lambda shape: pl.BlockSpec((1,) + shape, lambda b: (layer,) + (0,) * len(shape))
    w_spec = lambda j: pl.BlockSpec((1, D_MODEL, cb), lambda b: (layer, 0, LRU_BLOCK0 + j))
    gate_blocks = (2, LRU_BLOCKS, LRU_BS, LRU_BS)
    in_specs = [pl.BlockSpec((nb, L, D_MODEL), lambda b: (b, 0, 0)), w_spec(0), w_spec(1), w_spec(2), w_spec(3),
                per_layer((4, LRU_W)), per_layer((1, LRU_W)), per_layer(gate_blocks), per_layer((2, LRU_W)),
                per_layer(gate_blocks), per_layer((2, LRU_W)), per_layer((2, LRU_W))]
    per_b = lambda w: pl.BlockSpec((nb, L, w), lambda b: (b, 0, 0))
    mod_row = (lambda b: (layer, 1 + b, 0, 0)) if latent else (lambda b: (layer, 0, 0, 0))
    in_specs += [per_b(D_MODEL), per_b(ATTN_W), per_b(HY_W), per_layer((D_MIX, D_MODEL)),
                 pl.BlockSpec((1, 1, 3, D_MODEL), mod_row), pl.BlockSpec((1, D_MODEL), lambda b: (0, 0))]
    args = [hmod, w_in_b, w_in_b, w_in_b, w_in_b, conv_w, conv_b, wa, ba, wx, bx, lam,
            x, mix_a, mix_h, w_out_b, mod, final_g]
    out_specs = [per_b(D_MODEL)]
    out_shape = [jax.ShapeDtypeStruct((B, L, D_MODEL), F32)]
    aliases = {}
    if latent:
        in_specs.append(pl.BlockSpec((nb, 1, 2, LRU_W), lambda b: (b, layer, 0, 0)))
        args.append(state)
    else:
        if hl_prev is not None:
            out_specs.append(pl.BlockSpec((nb, 1, 2, LRU_W), lambda b: (b, layer, 0, 0)))
        else:
            out_specs.append(pl.BlockSpec((nb, DEPTH, 2, LRU_W), lambda b: (b, 0, 0, 0)))
        out_shape.append(jax.ShapeDtypeStruct((B, DEPTH, 2, LRU_W), F32))
        if hl_prev is not None:
            in_specs.append(pl.BlockSpec(memory_space=pl.ANY))
            aliases = {len(args): 1}
            args.append(hl_prev)
    scan_buf = pltpu.VMEM((2, rows // SUBLANES, SUBLANES, LRU_W), F32)
    return pl.pallas_call(
        functools.partial(_lru_kernel, latent, hl_prev is not None, final, layer, L, nb),
        grid=(B // nb,),
        in_specs=in_specs,
        out_specs=out_specs,
        out_shape=out_shape,
        scratch_shapes=[pltpu.VMEM((2, half_w, 4 * half_w), BF16), scan_buf, scan_buf, scan_buf,
                        pltpu.VMEM((rows, LRU_W), F32), pltpu.VMEM((rows, D_MODEL), F32)],
        input_output_aliases=aliases,
        compiler_params=_cparams(1),
        name=f"lru_out_L{L}",
    )(*args)


def _mixer_pass(latent, x, p, filt, dft, rope=None, cache=None, state=None):
    B, L, _ = x.shape
    fwd, inv = dft
    nb = 1 if latent else 4
    nb_lru = 1 if latent else 2
    kv, hl = None, None
    for l in range(DEPTH):
        outs = _attention_branch(latent, l, nb, x, p['mod'], p['norm_g'], p['w_in'], p['gq'], p['gk'], p['e_avg'],
                                 rope, cache, kv)
        hmod, mix_a = outs[0], outs[1]
        if not latent:
            kv = (outs[2], outs[3])
        mix_h = _hyena_branch(l, nb, hmod, p['w_in'], p['short_w'], p['short_b'], filt, p['hy_bias'], fwd, inv)
        outs = _lru_out_branch(latent, l, nb_lru, hmod, p['w_in'], p['conv_w'], p['conv_b'], p['wa'], p['ba'], p['wx'],
                               p['bx'], p['lam'], x, mix_a, mix_h, p['w_out'], p['mod'], p['final_g'], state, hl)
        x = outs[0]
        if not latent:
            hl = outs[1]
    return x, kv, hl


def kernel(x_prompt, x_sample, cache_k, cache_v, state_lru, c, c_ctx, norm_g, w_ada, b_ada, w_in, q_norm_g, k_norm_g,
           hy_short_w, hy_short_b, hy_filt_w1, hy_filt_b1, hy_filt_w2, hy_filt_b2, hy_filt_w3, hy_filt_freq,
           hy_filt_decay, hy_bias, lru_conv_w, lru_conv_b, lru_wa, lru_ba, lru_wx, lru_bx, lru_lambda, w_out, final_g):
    batch, seq, _ = x_prompt.shape
    dec_batch, dec_seq, _ = x_sample.shape
    past = cache_k.shape[2]

    cvecs = jnp.concatenate([c_ctx[None, :], c, jnp.zeros((MOD_ROWS - 1 - dec_batch, D_MODEL), F32)], axis=0)
    mod = _modulation(cvecs, w_ada, b_ada).reshape(DEPTH, MOD_ROWS, 3, D_MODEL)

    w1p = jnp.pad(hy_filt_w1, ((0, 0), (0, FEAT_PAD - HY_EMB), (0, 0)))
    dft, filt = {}, {}
    for L in (seq, dec_seq):
        dft[L] = tuple(jnp.asarray(m).astype(BF16) for m in _dft_tables(L))
        filt[L] = _filter_spectrum(L, jnp.asarray(_hyena_feats(L)), dft[L][0], w1p, hy_filt_b1[:, None, :],
                                   hy_filt_w2, hy_filt_b2[:, None, :], hy_filt_w3, hy_filt_freq,
                                   hy_filt_decay[:, None, :])

    params = {
        'mod': mod,
        'norm_g': norm_g[:, None, :],
        'w_in': w_in.astype(BF16),
        'gq': jnp.tile(q_norm_g, (1, N_Q_HEADS))[:, None, :],
        'gk': jnp.tile(k_norm_g, (1, N_KV_HEADS))[:, None, :],
        'e_avg': jnp.asarray(_head_average_matrix()).astype(BF16),
        'short_w': hy_short_w, 'short_b': hy_short_b[:, None, :], 'hy_bias': hy_bias[:, None, :],
        'conv_w': lru_conv_w, 'conv_b': lru_conv_b[:, None, :],
        'wa': lru_wa, 'ba': lru_ba, 'wx': lru_wx, 'bx': lru_bx, 'lam': lru_lambda,
        'w_out': w_out.astype(BF16),
        'final_g': final_g[None, :],
    }

    y_prompt, kv, new_lru = _mixer_pass(False, x_prompt, params, filt[seq], dft[seq])
    new_k = kv[0].reshape(batch, DEPTH, seq, N_KV_HEADS, HEAD_DIM)
    new_v = kv[1].reshape(batch, DEPTH, seq, N_KV_HEADS, HEAD_DIM)

    rope = tuple(jnp.asarray(t) for t in _rope_tables(dec_seq))
    cache = (cache_k.reshape(dec_batch, DEPTH, past, KV_W), cache_v.reshape(dec_batch, DEPTH, past, KV_W))
    y_sample, _, _ = _mixer_pass(True, x_sample, params, filt[dec_seq], dft[dec_seq], rope, cache, state_lru)
    return (y_prompt, y_sample, new_k, new_v, new_lru)
```

```python
import functools
import math

import numpy as np
import jax
import jax.numpy as jnp
from jax import lax
from jax.experimental import pallas as pl
from jax.experimental.pallas import tpu as pltpu

F32 = jnp.float32
BF16 = jnp.bfloat16

D_MODEL = 1024
DEPTH = 2
GRID_W = 64
HEAD_DIM = 64
N_Q_HEADS = 8
N_KV_HEADS = 2
Q_PER_KV = N_Q_HEADS // N_KV_HEADS
ATTN_W = N_Q_HEADS * HEAD_DIM
KV_W = N_KV_HEADS * HEAD_DIM
ROPE_THETA = 10000.0
ATTN_SCALE = HEAD_DIM ** -0.5
HY_W = 512
HY_BANDS = 16
HY_EMB = 2 * HY_BANDS + 1
HY_FH = 64
LRU_W = 512
LRU_BLOCKS = 8
LRU_BS = LRU_W // LRU_BLOCKS
LRU_C = 8.0
EPS = 1e-6

ATTN_COLS = ATTN_W + 2 * KV_W + ATTN_W
D_MIX = ATTN_W + HY_W + LRU_W

LANES = 128
SUBLANES = 8
MXU_DIM = 256
COL_BLOCK = MXU_DIM
HY_BLOCK0 = ATTN_COLS // COL_BLOCK
LRU_BLOCK0 = (ATTN_COLS + 4 * HY_W) // COL_BLOCK
Q_CHUNK = 256
Q_CHUNK_LATENT = 128
HEAD_PAIRS = Q_PER_KV // 2
FEAT_PAD = 128
MOD_ROWS = 8
VMEM_LIMIT = 56 * 1024 * 1024


def _cparams(n_axes):
    return pltpu.CompilerParams(dimension_semantics=("arbitrary",) * n_axes, vmem_limit_bytes=VMEM_LIMIT)


def _split_bf16(a):
    hi = a.astype(BF16)
    lo = (a - hi.astype(F32)).astype(BF16)
    return hi, lo


def _dot(a, b):
    return jnp.dot(a, b, preferred_element_type=F32)


def _dot3(a, b):
    a_hi, a_lo = _split_bf16(a)
    b_hi, b_lo = _split_bf16(b)
    return _dot(a_hi, b_hi) + _dot(a_hi, b_lo) + _dot(a_lo, b_hi)


def _silu(x):
    half = 0.5 * x
    return half * jnp.tanh(half) + half


@functools.lru_cache(maxsize=None)
def _dft_tables(L):
    n = 2 * L
    t = np.arange(L)
    kt = (t[:, None] * t[None, :]) % n
    ang = 2.0 * np.pi * kt / n
    cosm, sinm = np.cos(ang), np.sin(ang)
    alt = (-1.0) ** t
    f_s = -sinm
    f_s[0, :] = alt
    fwd = np.concatenate([cosm, f_s], axis=0)
    g_c = 2.0 * cosm.T / n
    g_c[:, 0] = 1.0 / n
    g_s = -2.0 * sinm.T / n
    g_s[:, 0] = alt / n
    inv = np.concatenate([g_c, g_s], axis=1)
    return fwd.astype(np.float32), inv.astype(np.float32)


@functools.lru_cache(maxsize=None)
def _hyena_feats(L):
    t = np.linspace(0.0, 1.0, L)[:, None]
    w = 2.0 * math.pi * np.arange(L)[:, None] / L
    f = np.linspace(1e-4, HY_BANDS - 1, HY_BANDS)[None, :]
    out = np.zeros((L, FEAT_PAD), np.float32)
    out[:, :HY_EMB] = np.concatenate([t, np.cos(f * w), -np.sin(f * w)], axis=-1)
    return out


@functools.lru_cache(maxsize=None)
def _rope_tables(L):
    rows = L // GRID_W
    row = np.repeat(np.arange(rows), GRID_W).astype(np.float64)
    col = np.tile(np.arange(GRID_W), rows).astype(np.float64)
    n_freq = HEAD_DIM // 4
    inv = ROPE_THETA ** (-np.arange(n_freq) / n_freq)
    ang = np.concatenate([row[:, None] * inv, col[:, None] * inv], axis=-1)
    cos = np.repeat(np.cos(ang), 2, axis=-1)
    sin = np.repeat(np.sin(ang), 2, axis=-1) * np.tile(np.array([-1.0, 1.0]), HEAD_DIM // 2)
    reps = LANES // HEAD_DIM
    return np.tile(cos, (1, reps)).astype(np.float32), np.tile(sin, (1, reps)).astype(np.float32)


@functools.lru_cache(maxsize=None)
def _head_average_matrix():
    return np.kron(np.eye(MXU_DIM // HEAD_DIM), np.full((HEAD_DIM, HEAD_DIM), 1.0 / HEAD_DIM)).astype(np.float32)


def _mod_kernel(c_ref, w_ref, b_ref, o_ref):
    s = _silu(c_ref[...])
    o_ref[0] = _dot3(s, w_ref[0]) + b_ref[0]


def _modulation(cvecs, w_ada, b_ada):
    tn = D_MODEL
    return pl.pallas_call(
        _mod_kernel,
        grid=(DEPTH, 3 * D_MODEL // tn),
        in_specs=[
            pl.BlockSpec((MOD_ROWS, D_MODEL), lambda l, j: (0, 0)),
            pl.BlockSpec((1, D_MODEL, tn), lambda l, j: (l, 0, j)),
            pl.BlockSpec((1, 1, tn), lambda l, j: (l, 0, j)),
        ],
        out_specs=pl.BlockSpec((1, MOD_ROWS, tn), lambda l, j: (l, 0, j)),
        out_shape=jax.ShapeDtypeStruct((DEPTH, MOD_ROWS, 3 * D_MODEL), F32),
        compiler_params=_cparams(2),
        name="adaln_mod",
    )(cvecs, w_ada, b_ada.reshape(DEPTH, 1, 3 * D_MODEL))


def _filter_kernel(L, feats_ref, w1_ref, b1_ref, w2_ref, b2_ref, w3_ref, freq_ref, decay_ref,
                   fwd_ref, a_ref, b_ref, d_ref):
    feats = feats_ref[...]
    t = feats[:, 0:1]
    freq = freq_ref[0]
    hdn = jnp.sin(freq[0:1] * (_dot3(feats, w1_ref[0]) + b1_ref[0]))
    hdn = jnp.sin(freq[1:2] * (_dot3(hdn, w2_ref[0]) + b2_ref[0]))
    h = _dot3(hdn, w3_ref[0]) * jnp.exp(-t * decay_ref[0])
    row = lax.broadcasted_iota(jnp.int32, (L, HY_W), 0)
    h_fwd = h[:, :HY_W]
    h_bwd = jnp.where(row >= 1, h[:, HY_W:], 0.0)
    inv_norm = 1.0 / jnp.sum(jnp.abs(h_fwd) + jnp.abs(h_bwd), axis=0, keepdims=True)
    even = h_fwd + h_bwd
    odd = h_fwd - h_bwd
    h_re = _dot(fwd_ref[0:L, :], even.astype(BF16)) * inv_norm
    h_im = _dot(fwd_ref[L:2 * L, :], odd.astype(BF16)) * inv_norm
    alt = jnp.where(row % 2 == 0, 1.0, -1.0)
    nyq = jnp.sum(even * alt, axis=0, keepdims=True) * inv_norm
    a_ref[0] = h_re
    b_ref[0] = jnp.where(row >= 1, h_im, 0.0)
    d_ref[0] = jnp.where(row >= 1, h_re, nyq)


def _filter_spectrum(L, feats, fwd, w1p, b1, w2, b2, w3, freq, decay):
    full = lambda shape: pl.BlockSpec(shape, lambda l: (0,) * len(shape))
    per_layer = lambda shape: pl.BlockSpec((1,) + shape, lambda l: (l,) + (0,) * len(shape))
    out = jax.ShapeDtypeStruct((DEPTH, L, HY_W), F32)
    return pl.pallas_call(
        functools.partial(_filter_kernel, L),
        grid=(DEPTH,),
        in_specs=[
            full((L, FEAT_PAD)),
            per_layer((FEAT_PAD, HY_FH)), per_layer((1, HY_FH)),
            per_layer((HY_FH, HY_FH)), per_layer((1, HY_FH)),
            per_layer((HY_FH, 2 * HY_W)), per_layer((2, HY_FH)), per_layer((1, 2 * HY_W)),
            full((2 * L, L)),
        ],
        out_specs=[per_layer((L, HY_W))] * 3,
        out_shape=[out, out, out],
        compiler_params=_cparams(1),
        name=f"hyena_filter_L{L}",
    )(feats, w1p, b1, w2, b2, w3, freq, decay, fwd)


def _rope(x, cos, sin):
    lane = lax.broadcasted_iota(jnp.int32, x.shape, 1)
    partner = jnp.where(lane % 2 == 0, pltpu.roll(x, LANES - 1, 1), pltpu.roll(x, 1, 1))
    return x * cos + partner * sin


def _head_mean_square(x, e):
    sq = (x * x).astype(BF16)
    width = x.shape[1]
    if width <= MXU_DIM:
        return _dot(sq, e[:width, :width])
    parts = [_dot(sq[:, i:i + MXU_DIM], e) for i in range(0, width, MXU_DIM)]
    return jnp.concatenate(parts, axis=1)


def _store_kv_heads(k_dst, v_dst, k, v):
    ones = jnp.ones((v.shape[0], HEAD_DIM), BF16)
    for j in range(N_KV_HEADS):
        js = slice(j * HEAD_DIM, (j + 1) * HEAD_DIM)
        vj = v[:, js].astype(BF16)
        k_dst[j] = k[:, js].astype(BF16)
        v_dst[j, 0, :, 0:HEAD_DIM] = vj
        v_dst[j, 0, :, HEAD_DIM:2 * HEAD_DIM] = ones
        v_dst[j, 1, :, 0:HEAD_DIM] = ones
        v_dst[j, 1, :, HEAD_DIM:2 * HEAD_DIM] = vj


def _attn_kernel(latent, aliased, layer, L, nb, qc, past, *refs):
    x_ref, mod_ref, ng_ref, w_ref, gq_ref, gk_ref, e_ref = refs[:7]
    refs = refs[7:]
    if latent:
        (cos_ref, sin_ref, ck_ref, cv_ref, h_out, mix_out,
         q_s, k_s, v_s, g_s, o_s, s_s, mx_s, acc_s, ck_s, cv_s) = refs
    else:
        if aliased:
            refs = refs[2:]
        h_out, mix_out, k_out, v_out, q_s, k_s, v_s, g_s, o_s, s_s, mx_s, acc_s = refs
    rows = nb * L

    x = x_ref[...].reshape(rows, D_MODEL)
    m = mod_ref[0, 0]
    shift, scale = m[0:1], m[1:2]
    ms = jnp.mean(x * x, axis=-1, keepdims=True)
    h = (x * lax.rsqrt(ms + EPS)) * (ng_ref[0] * (1.0 + scale)) + shift
    hb = h.astype(BF16)
    h_out[...] = hb.reshape(nb, L, D_MODEL)

    u = _dot(hb, w_ref[0])
    q = u[:, :ATTN_W]
    k = u[:, ATTN_W:ATTN_W + KV_W]
    v = u[:, ATTN_W + KV_W:ATTN_W + 2 * KV_W]
    g_s[...] = u[:, ATTN_W + 2 * KV_W:]
    e = e_ref[...]
    qn = q * lax.rsqrt(_head_mean_square(q, e) + EPS) * gq_ref[0]
    kn = k * lax.rsqrt(_head_mean_square(k, e) + EPS) * gk_ref[0]
    if latent:
        cos, sin = cos_ref[...], sin_ref[...]
        qn = jnp.concatenate([_rope(qn[:, i:i + LANES], cos, sin) for i in range(0, ATTN_W, LANES)], axis=1)
        kn = _rope(kn, cos, sin)
        _store_kv_heads(ck_s, cv_s, ck_ref[0, 0], cv_ref[0, 0])
    else:
        for b in range(nb):
            kb, vb = kn[b * L:(b + 1) * L], v[b * L:(b + 1) * L]
            if aliased:
                k_out[b, 0] = kb
                v_out[b, 0] = vb
            else:
                for slot in range(DEPTH):
                    k_out[b, slot] = kb if slot == layer else jnp.zeros_like(kb)
                    v_out[b, slot] = vb if slot == layer else jnp.zeros_like(vb)
    qb = (qn * ATTN_SCALE).astype(BF16)
    for hd in range(N_Q_HEADS):
        j, g = divmod(hd, Q_PER_KV)
        q_s[j * Q_PER_KV + (g % 2) * HEAD_PAIRS + g // 2] = qb[:, hd * HEAD_DIM:(hd + 1) * HEAD_DIM]
    _store_kv_heads(k_s, v_s, kn, v)

    contract_last = (((1,), (1,)), ((), ()))
    chunks_per_seq = L // qc
    lane = lax.broadcasted_iota(jnp.int32, (qc, LANES), 1)
    groups = [(j, e) for j in range(N_KV_HEADS) for e in range(2)]

    def chunk(c, carry):
        r0 = pl.multiple_of(c * qc, qc)
        k0 = pl.multiple_of((c // chunks_per_seq) * L, L)
        for idx, (j, e) in enumerate(groups):
            slot = j * Q_PER_KV + e * HEAD_PAIRS
            qh = q_s[slot:slot + HEAD_PAIRS, pl.ds(r0, qc), :].reshape(HEAD_PAIRS * qc, HEAD_DIM)
            s = lax.dot_general(qh, k_s[j, pl.ds(k0, L), :], contract_last, preferred_element_type=F32)
            mx = jnp.max(s, axis=-1, keepdims=True)
            if latent:
                s0 = lax.dot_general(qh, ck_s[j], contract_last, preferred_element_type=F32)
                mx = jnp.maximum(mx, jnp.max(s0, axis=-1, keepdims=True))
                s_s[idx, :, 0:past] = s0
                s_s[idx, :, past:] = s
            else:
                s_s[idx] = s
            mx_s[idx] = jnp.broadcast_to(mx, (HEAD_PAIRS * qc, LANES))
        for idx, (j, e) in enumerate(groups):
            p = jnp.exp(s_s[idx] - jnp.tile(mx_s[idx], (1, (past + L) // LANES))).astype(BF16)
            if latent:
                acc = _dot(p[:, :past], cv_s[j, e]) + _dot(p[:, past:], v_s[j, e, pl.ds(k0, L), :])
            else:
                acc = _dot(p, v_s[j, e, pl.ds(k0, L), :])
            acc_s[idx] = acc
        for j in range(N_KV_HEADS):
            for t in range(HEAD_PAIRS):
                a_even = acc_s[2 * j, t * qc:(t + 1) * qc, :]
                a_odd = acc_s[2 * j + 1, t * qc:(t + 1) * qc, :]
                out = jnp.where(lane < HEAD_DIM, a_even, a_odd)
                den = pltpu.roll(jnp.where(lane < HEAD_DIM, a_odd, a_even), HEAD_DIM, 1)
                blk = j * HEAD_PAIRS + t
                o_s[pl.ds(r0, qc), blk * LANES:(blk + 1) * LANES] = out / den
        return carry

    lax.fori_loop(0, nb * chunks_per_seq, chunk, 0)
    mix_out[...] = (o_s[...] * _silu(g_s[...])).astype(BF16).reshape(nb, L, ATTN_W)


def _attention_branch(latent, layer, nb, x, mod, norm_g, w_in_b, gq, gk, e_avg, rope=None, cache=None, kv_prev=None):
    B, L, _ = x.shape
    rows = nb * L
    assert nb == 1 or not latent
    full = lambda shape: pl.BlockSpec(shape, lambda b: (0,) * len(shape))
    per_b = lambda shape: pl.BlockSpec((nb,) + shape, lambda b: (b,) + (0,) * len(shape))
    per_layer = lambda shape: pl.BlockSpec((1,) + shape, lambda b: (layer,) + (0,) * len(shape))
    mod_spec = pl.BlockSpec((1, 1, 3, D_MODEL), (lambda b: (layer, 1 + b, 0, 0)) if latent else (lambda b: (layer, 0, 0, 0)))
    in_specs = [per_b((L, D_MODEL)), mod_spec, per_layer((1, D_MODEL)), per_layer((D_MODEL, ATTN_COLS)),
                per_layer((1, ATTN_W)), per_layer((1, KV_W)), full((MXU_DIM, MXU_DIM))]
    args = [x, mod, norm_g, w_in_b, gq, gk, e_avg]
    out_specs = [per_b((L, D_MODEL)), per_b((L, ATTN_W))]
    out_shape = [jax.ShapeDtypeStruct((B, L, D_MODEL), BF16), jax.ShapeDtypeStruct((B, L, ATTN_W), BF16)]
    qc = Q_CHUNK_LATENT if latent else Q_CHUNK
    past = cache[0].shape[2] if latent else 0
    n_groups = 2 * N_KV_HEADS
    scratch = [pltpu.VMEM((N_Q_HEADS, rows, HEAD_DIM), BF16), pltpu.VMEM((N_KV_HEADS, rows, HEAD_DIM), BF16),
               pltpu.VMEM((N_KV_HEADS, 2, rows, 2 * HEAD_DIM), BF16),
               pltpu.VMEM((rows, ATTN_W), F32), pltpu.VMEM((rows, ATTN_W), F32),
               pltpu.VMEM((n_groups, HEAD_PAIRS * qc, past + L), F32),
               pltpu.VMEM((n_groups, HEAD_PAIRS * qc, LANES), F32),
               pltpu.VMEM((n_groups, HEAD_PAIRS * qc, 2 * HEAD_DIM), F32)]
    aliases = {}
    if latent:
        cache_k, cache_v = cache
        cache_spec = pl.BlockSpec((1, 1, past, KV_W), lambda b: (b, layer, 0, 0))
        in_specs += [full((L, LANES)), full((L, LANES)), cache_spec, cache_spec]
        args += [rope[0], rope[1], cache_k, cache_v]
        scratch += [pltpu.VMEM((N_KV_HEADS, past, HEAD_DIM), BF16),
                    pltpu.VMEM((N_KV_HEADS, 2, past, 2 * HEAD_DIM), BF16)]
    else:
        if kv_prev is not None:
            kv_spec = pl.BlockSpec((nb, 1, L, KV_W), lambda b: (b, layer, 0, 0))
        else:
            kv_spec = pl.BlockSpec((nb, DEPTH, L, KV_W), lambda b: (b, 0, 0, 0))
        kv_shape = jax.ShapeDtypeStruct((B, DEPTH, L, KV_W), F32)
        out_specs += [kv_spec, kv_spec]
        out_shape += [kv_shape, kv_shape]
        if kv_prev is not None:
            in_specs += [pl.BlockSpec(memory_space=pl.ANY)] * 2
            aliases = {len(args): 2, len(args) + 1: 3}
            args += list(kv_prev)
    return pl.pallas_call(
        functools.partial(_attn_kernel, latent, kv_prev is not None, layer, L, nb, qc, past),
        grid=(B // nb,),
        in_specs=in_specs,
        out_specs=out_specs,
        out_shape=out_shape,
        scratch_shapes=scratch,
        input_output_aliases=aliases,
        compiler_params=_cparams(1),
        name=f"attn_branch_L{L}",
    )(*args)


def _shifted(x, offset, period):
    rows = x.shape[0]
    t = lax.broadcasted_iota(jnp.int32, x.shape, 0)
    if rows != period:
        t = t % period
    rolled = pltpu.roll(x, (-offset) % rows, 0)
    valid = (t >= -offset) if offset < 0 else (t < period - offset)
    return jnp.where(valid, rolled, 0.0)


def _hyena_kernel(L, nb, h_ref, w0_ref, w1_ref, w2_ref, wg_ref, sw0_ref, sw1_ref, sw2_ref, sb0_ref, sb1_ref, sb2_ref,
                  fa_ref, fb_ref, fd_ref, bias_ref, fwd_ref, inv_ref, mix_out):
    hb = h_ref[...].reshape(nb * L, D_MODEL)

    def short_conv(w_ref, sw_ref, sb_ref):
        xs = _dot(hb, w_ref[0])
        w = sw_ref[0]
        return sb_ref[0] + _shifted(xs, -1, L) * w[0:1] + xs * w[1:2] + _shifted(xs, 1, L) * w[2:3]

    x0 = short_conv(w0_ref, sw0_ref, sb0_ref)
    x1 = short_conv(w1_ref, sw1_ref, sb1_ref)
    hv = short_conv(w2_ref, sw2_ref, sb2_ref)
    gated = x0 * _silu(_dot(hb, wg_ref[0]))
    z = x1 * hv
    zb = z.astype(BF16)
    fa, fb, fd = fa_ref[0], fb_ref[0], fd_ref[0]
    for b in range(nb):
        rs = slice(b * L, (b + 1) * L)
        zf = _dot(fwd_ref[...], zb[rs])
        re, im = zf[:L], zf[L:]
        y_re = re * fa - im * fb
        y_im = re * fb + im * fd
        y = _dot(inv_ref[:, :L], y_re.astype(BF16)) + _dot(inv_ref[:, L:], y_im.astype(BF16))
        y = y + z[rs] * bias_ref[0]
        mix_out[b] = (y * gated[rs]).astype(BF16)


def _hyena_branch(layer, nb, hmod, w_in_b, short_w, short_b, filt, bias, fwd, inv):
    B, L, _ = hmod.shape
    cb = COL_BLOCK
    nt = HY_W // cb
    fa, fb, fd = filt
    w_spec = lambda part: pl.BlockSpec((1, D_MODEL, cb), lambda i, b: (layer, 0, HY_BLOCK0 + part * nt + i))
    sw_spec = lambda part: pl.BlockSpec((1, 3, cb), lambda i, b: (layer, 0, part * nt + i))
    sb_spec = lambda part: pl.BlockSpec((1, 1, cb), lambda i, b: (layer, 0, part * nt + i))
    filt_spec = pl.BlockSpec((1, L, cb), lambda i, b: (layer, 0, i))
    return pl.pallas_call(
        functools.partial(_hyena_kernel, L, nb),
        grid=(nt, B // nb),
        in_specs=[
            pl.BlockSpec((nb, L, D_MODEL), lambda i, b: (b, 0, 0)),
            w_spec(0), w_spec(1), w_spec(2), w_spec(3),
            sw_spec(0), sw_spec(1), sw_spec(2),
            sb_spec(0), sb_spec(1), sb_spec(2),
            filt_spec, filt_spec, filt_spec,
            pl.BlockSpec((1, 1, cb), lambda i, b: (layer, 0, i)),
            pl.BlockSpec((2 * L, L), lambda i, b: (0, 0)),
            pl.BlockSpec((L, 2 * L), lambda i, b: (0, 0)),
        ],
        out_specs=pl.BlockSpec((nb, L, cb), lambda i, b: (b, 0, i)),
        out_shape=jax.ShapeDtypeStruct((B, L, HY_W), BF16),
        compiler_params=_cparams(2),
        name=f"hyena_branch_L{L}",
    )(hmod, w_in_b, w_in_b, w_in_b, w_in_b, short_w, short_w, short_w, short_b, short_b, short_b,
      fa, fb, fd, bias, fwd, inv)


def _lru_kernel(latent, aliased, final, layer, L, nb, *refs):
    (h_ref, wx0_ref, wx1_ref, wg0_ref, wg1_ref, cw_ref, cb_ref, wa_ref, ba_ref, wxg_ref, bx_ref, lam_ref,
     x_ref, ma_ref, mh_ref, wo_ref, mod_ref, fg_ref) = refs[:18]
    refs = refs[18:]
    if latent:
        h0_ref, x_out, gate_w_s, a_s, b_s, y_s, g_s, part_s = refs
    else:
        if aliased:
            refs = refs[1:]
        x_out, hl_out, gate_w_s, a_s, b_s, y_s, g_s, part_s = refs
    half_w = LRU_W // 2
    blocks_per_half = half_w // LRU_BS
    n_tiles = L // SUBLANES

    @pl.when(pl.program_id(0) == 0)
    def _build_gate_weights():
        gate_w_s[...] = jnp.zeros(gate_w_s.shape, BF16)
        for half in range(2):
            for kind, (w_ref, d) in enumerate(((wa_ref, 0), (wxg_ref, 0), (wa_ref, 1), (wxg_ref, 1))):
                for j in range(blocks_per_half):
                    blk = (0.5 * w_ref[0, d, half * blocks_per_half + j]).astype(BF16)
                    gate_w_s[half, j * LRU_BS:(j + 1) * LRU_BS,
                             kind * half_w + j * LRU_BS:kind * half_w + (j + 1) * LRU_BS] = blk

    part_s[...] = (_dot(ma_ref[...].reshape(nb * L, ATTN_W), wo_ref[0, 0:ATTN_W, :])
                   + _dot(mh_ref[...].reshape(nb * L, HY_W), wo_ref[0, ATTN_W:ATTN_W + HY_W, :]))

    hb = h_ref[...].reshape(nb * L, D_MODEL)
    xs = jnp.concatenate([_dot(hb, wx0_ref[0]), _dot(hb, wx1_ref[0])], axis=1)
    g_s[:, :half_w] = _dot(hb, wg0_ref[0])
    g_s[:, half_w:] = _dot(hb, wg1_ref[0])
    cw = cw_ref[0]
    xc = (cb_ref[0] + _shifted(xs, -1, L) * cw[0:1] + xs * cw[1:2] + _shifted(xs, 1, L) * cw[2:3]
          + _shifted(xs, 2, L) * cw[3:4])
    xcb = xc.astype(BF16)
    c_half = (-0.5 * LRU_C) * jax.nn.softplus(-lam_ref[0])
    ba, bx = 0.5 * ba_ref[0], 0.5 * bx_ref[0]
    for half in range(2):
        cols = slice(half * half_w, (half + 1) * half_w)
        bias = jnp.concatenate([ba[0:1, cols], bx[0:1, cols], ba[1:2, cols], bx[1:2, cols]], axis=1)
        gates = _dot(xcb[:, cols], gate_w_s[half]) + bias
        xh_half = 0.5 * xc[:, cols]
        for d in range(2):
            t_r = jnp.tanh(gates[:, (2 * d) * half_w:(2 * d + 1) * half_w])
            t_i = jnp.tanh(gates[:, (2 * d + 1) * half_w:(2 * d + 2) * half_w])
            ch = c_half[d:d + 1, cols]
            log_a = ch * t_r + ch
            a = jnp.exp(log_a)
            y = jnp.tanh(log_a) * (-1.0 - a * a)
            root = jnp.where(y > 0.0, y * lax.rsqrt(y), 0.0)
            bcoef = root * (xh_half * t_i + xh_half)
            a_s[d, :, :, cols] = a.reshape(nb * n_tiles, SUBLANES, half_w)
            b_s[d, :, :, cols] = bcoef.reshape(nb * n_tiles, SUBLANES, half_w)

    if latent:
        init = []
        for b in range(nb):
            h0 = h0_ref[b, 0]
            init += [h0[0:1], h0[1:2]]
    else:
        init = [jnp.zeros((1, LRU_W), F32)] * (2 * nb)

    def advance(d, tile, rows_g, h):
        a = [a_s[d, tile, r:r + 1, :] for r in rows_g]
        c = [b_s[d, tile, r:r + 1, :] for r in rows_g]
        a01, c01 = a[1] * a[0], a[1] * c[0] + c[1]
        hs = [a[0] * h + c[0], a01 * h + c01]
        if len(rows_g) == 4:
            a23, c23 = a[3] * a[2], a[3] * c[2] + c[3]
            hs += [a[2] * hs[1] + c[2], (a23 * a01) * h + (a23 * c01 + c23)]
        for r, v in zip(rows_g, hs):
            y_s[d, tile, r:r + 1, :] = v
        return hs[-1]

    group = 4 if nb == 1 else 2

    def tile_step(i, carry):
        carry = list(carry)
        for g0 in range(0, SUBLANES, group):
            fwd_rows = list(range(g0, g0 + group))
            bwd_rows = [SUBLANES - 1 - r for r in fwd_rows]
            for b in range(nb):
                carry[2 * b] = advance(0, b * n_tiles + i, fwd_rows, carry[2 * b])
                carry[2 * b + 1] = advance(1, b * n_tiles + (n_tiles - 1 - i), bwd_rows, carry[2 * b + 1])
        return tuple(carry)

    def two_tiles(i, carry):
        return tile_step(2 * i + 1, tile_step(2 * i, carry))

    last = lax.fori_loop(0, n_tiles // 2, two_tiles, tuple(init))
    if not latent:
        if not aliased:
            hl_out[...] = jnp.zeros(hl_out.shape, F32)
        slot = 0 if aliased else layer
        for b in range(nb):
            hl_out[b, slot, 0:1, :] = last[2 * b]
            hl_out[b, slot, 1:2, :] = last[2 * b + 1]
    y = (y_s[0] + y_s[1]).reshape(nb * L, LRU_W)
    mix_l = (y * _silu(g_s[...])).astype(BF16)
    out = part_s[...] + _dot(mix_l, wo_ref[0, ATTN_W + HY_W:, :])
    xn = x_ref[...].reshape(nb * L, D_MODEL) + mod_ref[0, 0][2:3] * out
    if final:
        ms = jnp.mean(xn * xn, axis=-1, keepdims=True)
        xn = xn * lax.rsqrt(ms + EPS) * fg_ref[...]
    x_out[...] = xn.reshape(nb, L, D_MODEL)


def _lru_out_branch(latent, layer, nb, hmod, w_in_b, conv_w, conv_b, wa, ba, wx, bx, lam, x, mix_a, mix_h, w_out_b,
                    mod, final_g, state=None, hl_prev=None):
    B, L, _ = hmod.shape
    final = layer == DEPTH - 1
    cb = COL_BLOCK
    half_w = LRU_W // 2
    rows = nb * L
    per_layer = lambda shape: pl.BlockSpec((1,) + shape, lambda b: (layer,) + (0,) * len(shape))
    w_spec = lambda j: pl.BlockSpec((1, D_MODEL, cb), lambda b: (layer, 0, LRU_BLOCK0 + j))
    gate_blocks = (2, LRU_BLOCKS, LRU_BS, LRU_BS)
    in_specs = [pl.BlockSpec((nb, L, D_MODEL), lambda b: (b, 0, 0)), w_spec(0), w_spec(1), w_spec(2), w_spec(3),
                per_layer((4, LRU_W)), per_layer((1, LRU_W)), per_layer(gate_blocks), per_layer((2, LRU_W)),
                per_layer(gate_blocks), per_layer((2, LRU_W)), per_layer((2, LRU_W))]
    per_b = lambda w: pl.BlockSpec((nb, L, w), lambda b: (b, 0, 0))
    mod_row = (lambda b: (layer, 1 + b, 0, 0)) if latent else (lambda b: (layer, 0, 0, 0))
    in_specs += [per_b(D_MODEL), per_b(ATTN_W), per_b(HY_W), per_layer((D_MIX, D_MODEL)),
                 pl.BlockSpec((1, 1, 3, D_MODEL), mod_row), pl.BlockSpec((1, D_MODEL), lambda b: (0, 0))]
    args = [hmod, w_in_b, w_in_b, w_in_b, w_in_b, conv_w, conv_b, wa, ba, wx, bx, lam,
            x, mix_a, mix_h, w_out_b, mod, final_g]
    out_specs = [per_b(D_MODEL)]
    out_shape = [jax.ShapeDtypeStruct((B, L, D_MODEL), F32)]
    aliases = {}
    if latent:
        in_specs.append(pl.BlockSpec((nb, 1, 2, LRU_W), lambda b: (b, layer, 0, 0)))
        args.append(state)
    else:
        if hl_prev is not None:
            out_specs.append(pl.BlockSpec((nb, 1, 2, LRU_W), lambda b: (b, layer, 0, 0)))
        else:
            out_specs.append(pl.BlockSpec((nb, DEPTH, 2, LRU_W), lambda b: (b, 0, 0, 0)))
        out_shape.append(jax.ShapeDtypeStruct((B, DEPTH, 2, LRU_W), F32))
        if hl_prev is not None:
            in_specs.append(pl.BlockSpec(memory_space=pl.ANY))
            aliases = {len(args): 1}
            args.append(hl_prev)
    scan_buf = pltpu.VMEM((2, rows // SUBLANES, SUBLANES, LRU_W), F32)
    return pl.pallas_call(
        functools.partial(_lru_kernel, latent, hl_prev is not None, final, layer, L, nb),
        grid=(B // nb,),
        in_specs=in_specs,
        out_specs=out_specs,
        out_shape=out_shape,
        scratch_shapes=[pltpu.VMEM((2, half_w, 4 * half_w), BF16), scan_buf, scan_buf, scan_buf,
                        pltpu.VMEM((rows, LRU_W), F32), pltpu.VMEM((rows, D_MODEL), F32)],
        input_output_aliases=aliases,
        compiler_params=_cparams(1),
        name=f"lru_out_L{L}",
    )(*args)


def _mixer_pass(latent, x, p, filt, dft, rope=None, cache=None, state=None):
    B, L, _ = x.shape
    fwd, inv = dft
    nb = 1 if latent else 4
    nb_hyena = min(B, 2) if latent else 4
    nb_lru = 1 if latent else 2
    kv, hl = None, None
    for l in range(DEPTH):
        outs = _attention_branch(latent, l, nb, x, p['mod'], p['norm_g'], p['w_in'], p['gq'], p['gk'], p['e_avg'],
                                 rope, cache, kv)
        hmod, mix_a = outs[0], outs[1]
        if not latent:
            kv = (outs[2], outs[3])
        mix_h = _hyena_branch(l, nb_hyena, hmod, p['w_in'], p['short_w'], p['short_b'], filt, p['hy_bias'], fwd, inv)
        outs = _lru_out_branch(latent, l, nb_lru, hmod, p['w_in'], p['conv_w'], p['conv_b'], p['wa'], p['ba'], p['wx'],
                               p['bx'], p['lam'], x, mix_a, mix_h, p['w_out'], p['mod'], p['final_g'], state, hl)
        x = outs[0]
        if not latent:
            hl = outs[1]
    return x, kv, hl


def kernel(x_prompt, x_sample, cache_k, cache_v, state_lru, c, c_ctx, norm_g, w_ada, b_ada, w_in, q_norm_g, k_norm_g,
           hy_short_w, hy_short_b, hy_filt_w1, hy_filt_b1, hy_filt_w2, hy_filt_b2, hy_filt_w3, hy_filt_freq,
           hy_filt_decay, hy_bias, lru_conv_w, lru_conv_b, lru_wa, lru_ba, lru_wx, lru_bx, lru_lambda, w_out, final_g):
    batch, seq, _ = x_prompt.shape
    dec_batch, dec_seq, _ = x_sample.shape
    past = cache_k.shape[2]

    cvecs = jnp.concatenate([c_ctx[None, :], c, jnp.zeros((MOD_ROWS - 1 - dec_batch, D_MODEL), F32)], axis=0)
    mod = _modulation(cvecs, w_ada, b_ada).reshape(DEPTH, MOD_ROWS, 3, D_MODEL)

    w1p = jnp.pad(hy_filt_w1, ((0, 0), (0, FEAT_PAD - HY_EMB), (0, 0)))
    dft, filt = {}, {}
    for L in (seq, dec_seq):
        dft[L] = tuple(jnp.asarray(m).astype(BF16) for m in _dft_tables(L))
        filt[L] = _filter_spectrum(L, jnp.asarray(_hyena_feats(L)), dft[L][0], w1p, hy_filt_b1[:, None, :],
                                   hy_filt_w2, hy_filt_b2[:, None, :], hy_filt_w3, hy_filt_freq,
                                   hy_filt_decay[:, None, :])

    params = {
        'mod': mod,
        'norm_g': norm_g[:, None, :],
        'w_in': w_in.astype(BF16),
        'gq': jnp.tile(q_norm_g, (1, N_Q_HEADS))[:, None, :],
        'gk': jnp.tile(k_norm_g, (1, N_KV_HEADS))[:, None, :],
        'e_avg': jnp.asarray(_head_average_matrix()).astype(BF16),
        'short_w': hy_short_w, 'short_b': hy_short_b[:, None, :], 'hy_bias': hy_bias[:, None, :],
        'conv_w': lru_conv_w, 'conv_b': lru_conv_b[:, None, :],
        'wa': lru_wa, 'ba': lru_ba, 'wx': lru_wx, 'bx': lru_bx, 'lam': lru_lambda,
        'w_out': w_out.astype(BF16),
        'final_g': final_g[None, :],
    }

    y_prompt, kv, new_lru = _mixer_pass(False, x_prompt, params, filt[seq], dft[seq])
    new_k = kv[0].reshape(batch, DEPTH, seq, N_KV_HEADS, HEAD_DIM)
    new_v = kv[1].reshape(batch, DEPTH, seq, N_KV_HEADS, HEAD_DIM)

    rope = tuple(jnp.asarray(t) for t in _rope_tables(dec_seq))
    cache = (cache_k.reshape(dec_batch, DEPTH, past, KV_W), cache_v.reshape(dec_batch, DEPTH, past, KV_W))
    y_sample, _, _ = _mixer_pass(True, x_sample, params, filt[dec_seq], dft[dec_seq], rope, cache, state_lru)
    return (y_prompt, y_sample, new_k, new_v, new_lru)
```

```python
import functools
import math

import numpy as np
import jax
import jax.numpy as jnp
from jax import lax
from jax.experimental import pallas as pl
from jax.experimental.pallas import tpu as pltpu

F32 = jnp.float32
BF16 = jnp.bfloat16

D_MODEL = 1024
DEPTH = 2
GRID_W = 64
HEAD_DIM = 64
N_Q_HEADS = 8
N_KV_HEADS = 2
Q_PER_KV = N_Q_HEADS // N_KV_HEADS
ATTN_W = N_Q_HEADS * HEAD_DIM
KV_W = N_KV_HEADS * HEAD_DIM
ROPE_THETA = 10000.0
ATTN_SCALE = HEAD_DIM ** -0.5
HY_W = 512
HY_BANDS = 16
HY_EMB = 2 * HY_BANDS + 1
HY_FH = 64
LRU_W = 512
LRU_BLOCKS = 8
LRU_BS = LRU_W // LRU_BLOCKS
LRU_C = 8.0
EPS = 1e-6

ATTN_COLS = ATTN_W + 2 * KV_W + ATTN_W
D_MIX = ATTN_W + HY_W + LRU_W

LANES = 128
SUBLANES = 8
MXU_DIM = 256
COL_BLOCK = MXU_DIM
HY_BLOCK0 = ATTN_COLS // COL_BLOCK
LRU_BLOCK0 = (ATTN_COLS + 4 * HY_W) // COL_BLOCK
Q_CHUNK = 256
Q_CHUNK_LATENT = 128
HEAD_PAIRS = Q_PER_KV // 2
FEAT_PAD = 128
MOD_ROWS = 8
VMEM_LIMIT = 56 * 1024 * 1024


def _cparams(n_axes):
    return pltpu.CompilerParams(dimension_semantics=("arbitrary",) * n_axes, vmem_limit_bytes=VMEM_LIMIT)


def _split_bf16(a):
    hi = a.astype(BF16)
    lo = (a - hi.astype(F32)).astype(BF16)
    return hi, lo


def _dot(a, b):
    return jnp.dot(a, b, preferred_element_type=F32)


def _dot3(a, b):
    a_hi, a_lo = _split_bf16(a)
    b_hi, b_lo = _split_bf16(b)
    return _dot(a_hi, b_hi) + _dot(a_hi, b_lo) + _dot(a_lo, b_hi)


def _silu(x):
    half = 0.5 * x
    return half * jnp.tanh(half) + half


@functools.lru_cache(maxsize=None)
def _dft_tables(L):
    n = 2 * L
    t = np.arange(L)
    kt = (t[:, None] * t[None, :]) % n
    ang = 2.0 * np.pi * kt / n
    cosm, sinm = np.cos(ang), np.sin(ang)
    alt = (-1.0) ** t
    f_s = -sinm
    f_s[0, :] = alt
    fwd = np.concatenate([cosm, f_s], axis=0)
    g_c = 2.0 * cosm.T / n
    g_c[:, 0] = 1.0 / n
    g_s = -2.0 * sinm.T / n
    g_s[:, 0] = alt / n
    inv = np.concatenate([g_c, g_s], axis=1)
    return fwd.astype(np.float32), inv.astype(np.float32)


@functools.lru_cache(maxsize=None)
def _hyena_feats(L):
    t = np.linspace(0.0, 1.0, L)[:, None]
    w = 2.0 * math.pi * np.arange(L)[:, None] / L
    f = np.linspace(1e-4, HY_BANDS - 1, HY_BANDS)[None, :]
    out = np.zeros((L, FEAT_PAD), np.float32)
    out[:, :HY_EMB] = np.concatenate([t, np.cos(f * w), -np.sin(f * w)], axis=-1)
    return out


@functools.lru_cache(maxsize=None)
def _rope_tables(L):
    rows = L // GRID_W
    row = np.repeat(np.arange(rows), GRID_W).astype(np.float64)
    col = np.tile(np.arange(GRID_W), rows).astype(np.float64)
    n_freq = HEAD_DIM // 4
    inv = ROPE_THETA ** (-np.arange(n_freq) / n_freq)
    ang = np.concatenate([row[:, None] * inv, col[:, None] * inv], axis=-1)
    cos = np.repeat(np.cos(ang), 2, axis=-1)
    sin = np.repeat(np.sin(ang), 2, axis=-1) * np.tile(np.array([-1.0, 1.0]), HEAD_DIM // 2)
    reps = LANES // HEAD_DIM
    return np.tile(cos, (1, reps)).astype(np.float32), np.tile(sin, (1, reps)).astype(np.float32)


@functools.lru_cache(maxsize=None)
def _head_average_matrix():
    return np.kron(np.eye(MXU_DIM // HEAD_DIM), np.full((HEAD_DIM, HEAD_DIM), 1.0 / HEAD_DIM)).astype(np.float32)


def _mod_kernel(c_ref, w_ref, b_ref, o_ref):
    s = _silu(c_ref[...])
    o_ref[0] = _dot3(s, w_ref[0]) + b_ref[0]


def _modulation(cvecs, w_ada, b_ada):
    tn = D_MODEL
    return pl.pallas_call(
        _mod_kernel,
        grid=(DEPTH, 3 * D_MODEL // tn),
        in_specs=[
            pl.BlockSpec((MOD_ROWS, D_MODEL), lambda l, j: (0, 0)),
            pl.BlockSpec((1, D_MODEL, tn), lambda l, j: (l, 0, j)),
            pl.BlockSpec((1, 1, tn), lambda l, j: (l, 0, j)),
        ],
        out_specs=pl.BlockSpec((1, MOD_ROWS, tn), lambda l, j: (l, 0, j)),
        out_shape=jax.ShapeDtypeStruct((DEPTH, MOD_ROWS, 3 * D_MODEL), F32),
        compiler_params=_cparams(2),
        name="adaln_mod",
    )(cvecs, w_ada, b_ada.reshape(DEPTH, 1, 3 * D_MODEL))


def _filter_kernel(L, feats_ref, w1_ref, b1_ref, w2_ref, b2_ref, w3_ref, freq_ref, decay_ref,
                   fwd_ref, a_ref, b_ref, d_ref):
    feats = feats_ref[...]
    t = feats[:, 0:1]
    freq = freq_ref[0]
    hdn = jnp.sin(freq[0:1] * (_dot3(feats, w1_ref[0]) + b1_ref[0]))
    hdn = jnp.sin(freq[1:2] * (_dot3(hdn, w2_ref[0]) + b2_ref[0]))
    h = _dot3(hdn, w3_ref[0]) * jnp.exp(-t * decay_ref[0])
    row = lax.broadcasted_iota(jnp.int32, (L, HY_W), 0)
    h_fwd = h[:, :HY_W]
    h_bwd = jnp.where(row >= 1, h[:, HY_W:], 0.0)
    inv_norm = 1.0 / jnp.sum(jnp.abs(h_fwd) + jnp.abs(h_bwd), axis=0, keepdims=True)
    even = h_fwd + h_bwd
    odd = h_fwd - h_bwd
    h_re = _dot(fwd_ref[0:L, :], even.astype(BF16)) * inv_norm
    h_im = _dot(fwd_ref[L:2 * L, :], odd.astype(BF16)) * inv_norm
    alt = jnp.where(row % 2 == 0, 1.0, -1.0)
    nyq = jnp.sum(even * alt, axis=0, keepdims=True) * inv_norm
    a_ref[0] = h_re
    b_ref[0] = jnp.where(row >= 1, h_im, 0.0)
    d_ref[0] = jnp.where(row >= 1, h_re, nyq)


def _filter_spectrum(L, feats, fwd, w1p, b1, w2, b2, w3, freq, decay):
    full = lambda shape: pl.BlockSpec(shape, lambda l: (0,) * len(shape))
    per_layer = lambda shape: pl.BlockSpec((1,) + shape, lambda l: (l,) + (0,) * len(shape))
    out = jax.ShapeDtypeStruct((DEPTH, L, HY_W), F32)
    return pl.pallas_call(
        functools.partial(_filter_kernel, L),
        grid=(DEPTH,),
        in_specs=[
            full((L, FEAT_PAD)),
            per_layer((FEAT_PAD, HY_FH)), per_layer((1, HY_FH)),
            per_layer((HY_FH, HY_FH)), per_layer((1, HY_FH)),
            per_layer((HY_FH, 2 * HY_W)), per_layer((2, HY_FH)), per_layer((1, 2 * HY_W)),
            full((2 * L, L)),
        ],
        out_specs=[per_layer((L, HY_W))] * 3,
        out_shape=[out, out, out],
        compiler_params=_cparams(1),
        name=f"hyena_filter_L{L}",
    )(feats, w1p, b1, w2, b2, w3, freq, decay, fwd)


def _rope(x, cos, sin):
    lane = lax.broadcasted_iota(jnp.int32, x.shape, 1)
    partner = jnp.where(lane % 2 == 0, pltpu.roll(x, LANES - 1, 1), pltpu.roll(x, 1, 1))
    return x * cos + partner * sin


def _head_mean_square(x, e):
    sq = (x * x).astype(BF16)
    width = x.shape[1]
    if width <= MXU_DIM:
        return _dot(sq, e[:width, :width])
    parts = [_dot(sq[:, i:i + MXU_DIM], e) for i in range(0, width, MXU_DIM)]
    return jnp.concatenate(parts, axis=1)


def _store_kv_heads(k_dst, v_dst, k, v):
    ones = jnp.ones((v.shape[0], HEAD_DIM), BF16)
    for j in range(N_KV_HEADS):
        js = slice(j * HEAD_DIM, (j + 1) * HEAD_DIM)
        vj = v[:, js].astype(BF16)
        k_dst[j] = k[:, js].astype(BF16)
        v_dst[j, 0, :, 0:HEAD_DIM] = vj
        v_dst[j, 0, :, HEAD_DIM:2 * HEAD_DIM] = ones
        v_dst[j, 1, :, 0:HEAD_DIM] = ones
        v_dst[j, 1, :, HEAD_DIM:2 * HEAD_DIM] = vj


def _attn_kernel(latent, aliased, layer, L, nb, qc, past, *refs):
    x_ref, mod_ref, ng_ref, w_ref, gq_ref, gk_ref, e_ref = refs[:7]
    refs = refs[7:]
    if latent:
        (cos_ref, sin_ref, ck_ref, cv_ref, h_out, mix_out,
         q_s, k_s, v_s, g_s, s_s, mx_s, acc_s, ck_s, cv_s) = refs
    else:
        if aliased:
            refs = refs[2:]
        h_out, mix_out, k_out, v_out, q_s, k_s, v_s, g_s, s_s, mx_s, acc_s = refs
    rows = nb * L

    x = x_ref[...].reshape(rows, D_MODEL)
    m = mod_ref[0, 0]
    shift, scale = m[0:1], m[1:2]
    ms = jnp.mean(x * x, axis=-1, keepdims=True)
    h = (x * lax.rsqrt(ms + EPS)) * (ng_ref[0] * (1.0 + scale)) + shift
    hb = h.astype(BF16)
    h_out[...] = hb.reshape(nb, L, D_MODEL)

    u = _dot(hb, w_ref[0])
    q = u[:, :ATTN_W]
    k = u[:, ATTN_W:ATTN_W + KV_W]
    v = u[:, ATTN_W + KV_W:ATTN_W + 2 * KV_W]
    g_s[...] = u[:, ATTN_W + 2 * KV_W:]
    e = e_ref[...]
    qn = q * lax.rsqrt(_head_mean_square(q, e) + EPS) * gq_ref[0]
    kn = k * lax.rsqrt(_head_mean_square(k, e) + EPS) * gk_ref[0]
    if latent:
        cos, sin = cos_ref[...], sin_ref[...]
        qn = jnp.concatenate([_rope(qn[:, i:i + LANES], cos, sin) for i in range(0, ATTN_W, LANES)], axis=1)
        kn = _rope(kn, cos, sin)
        _store_kv_heads(ck_s, cv_s, ck_ref[0, 0], cv_ref[0, 0])
    else:
        for b in range(nb):
            kb, vb = kn[b * L:(b + 1) * L], v[b * L:(b + 1) * L]
            if aliased:
                k_out[b, 0] = kb
                v_out[b, 0] = vb
            else:
                for slot in range(DEPTH):
                    k_out[b, slot] = kb if slot == layer else jnp.zeros_like(kb)
                    v_out[b, slot] = vb if slot == layer else jnp.zeros_like(vb)
    qb = (qn * ATTN_SCALE).astype(BF16)
    for hd in range(N_Q_HEADS):
        j, g = divmod(hd, Q_PER_KV)
        q_s[j * Q_PER_KV + (g % 2) * HEAD_PAIRS + g // 2] = qb[:, hd * HEAD_DIM:(hd + 1) * HEAD_DIM]
    _store_kv_heads(k_s, v_s, kn, v)

    contract_last = (((1,), (1,)), ((), ()))
    chunks_per_seq = L // qc
    lane = lax.broadcasted_iota(jnp.int32, (qc, LANES), 1)
    groups = [(j, e) for j in range(N_KV_HEADS) for e in range(2)]

    def chunk(c, carry):
        r0 = pl.multiple_of(c * qc, qc)
        k0 = pl.multiple_of((c // chunks_per_seq) * L, L)
        for idx, (j, e) in enumerate(groups):
            slot = j * Q_PER_KV + e * HEAD_PAIRS
            qh = q_s[slot:slot + HEAD_PAIRS, pl.ds(r0, qc), :].reshape(HEAD_PAIRS * qc, HEAD_DIM)
            s = lax.dot_general(qh, k_s[j, pl.ds(k0, L), :], contract_last, preferred_element_type=F32)
            mx = jnp.max(s, axis=-1, keepdims=True)
            if latent:
                s0 = lax.dot_general(qh, ck_s[j], contract_last, preferred_element_type=F32)
                mx = jnp.maximum(mx, jnp.max(s0, axis=-1, keepdims=True))
                s_s[idx, :, 0:past] = s0
                s_s[idx, :, past:] = s
            else:
                s_s[idx] = s
            mx_s[idx] = jnp.broadcast_to(mx, (HEAD_PAIRS * qc, LANES))
        for idx, (j, e) in enumerate(groups):
            p = jnp.exp(s_s[idx] - jnp.tile(mx_s[idx], (1, (past + L) // LANES))).astype(BF16)
            if latent:
                acc = _dot(p[:, :past], cv_s[j, e]) + _dot(p[:, past:], v_s[j, e, pl.ds(k0, L), :])
            else:
                acc = _dot(p, v_s[j, e, pl.ds(k0, L), :])
            acc_s[idx] = acc
        b_idx = c // chunks_per_seq
        row0 = pl.multiple_of((c % chunks_per_seq) * qc, qc)
        for j in range(N_KV_HEADS):
            for t in range(HEAD_PAIRS):
                a_even = acc_s[2 * j, t * qc:(t + 1) * qc, :]
                a_odd = acc_s[2 * j + 1, t * qc:(t + 1) * qc, :]
                out = jnp.where(lane < HEAD_DIM, a_even, a_odd)
                den = pltpu.roll(jnp.where(lane < HEAD_DIM, a_odd, a_even), HEAD_DIM, 1)
                blk = slice((j * HEAD_PAIRS + t) * LANES, (j * HEAD_PAIRS + t + 1) * LANES)
                gate = _silu(g_s[pl.ds(r0, qc), blk])
                mix_out[b_idx, pl.ds(row0, qc), blk] = (out / den * gate).astype(BF16)
        return carry

    lax.fori_loop(0, nb * chunks_per_seq, chunk, 0)


def _attention_branch(latent, layer, nb, x, mod, norm_g, w_in_b, gq, gk, e_avg, rope=None, cache=None, kv_prev=None):
    B, L, _ = x.shape
    rows = nb * L
    assert nb == 1 or not latent
    full = lambda shape: pl.BlockSpec(shape, lambda b: (0,) * len(shape))
    per_b = lambda shape: pl.BlockSpec((nb,) + shape, lambda b: (b,) + (0,) * len(shape))
    per_layer = lambda shape: pl.BlockSpec((1,) + shape, lambda b: (layer,) + (0,) * len(shape))
    mod_spec = pl.BlockSpec((1, 1, 3, D_MODEL), (lambda b: (layer, 1 + b, 0, 0)) if latent else (lambda b: (layer, 0, 0, 0)))
    in_specs = [per_b((L, D_MODEL)), mod_spec, per_layer((1, D_MODEL)), per_layer((D_MODEL, ATTN_COLS)),
                per_layer((1, ATTN_W)), per_layer((1, KV_W)), full((MXU_DIM, MXU_DIM))]
    args = [x, mod, norm_g, w_in_b, gq, gk, e_avg]
    out_specs = [per_b((L, D_MODEL)), per_b((L, ATTN_W))]
    out_shape = [jax.ShapeDtypeStruct((B, L, D_MODEL), BF16), jax.ShapeDtypeStruct((B, L, ATTN_W), BF16)]
    qc = Q_CHUNK_LATENT if latent else Q_CHUNK
    past = cache[0].shape[2] if latent else 0
    n_groups = 2 * N_KV_HEADS
    scratch = [pltpu.VMEM((N_Q_HEADS, rows, HEAD_DIM), BF16), pltpu.VMEM((N_KV_HEADS, rows, HEAD_DIM), BF16),
               pltpu.VMEM((N_KV_HEADS, 2, rows, 2 * HEAD_DIM), BF16),
               pltpu.VMEM((rows, ATTN_W), F32),
               pltpu.VMEM((n_groups, HEAD_PAIRS * qc, past + L), F32),
               pltpu.VMEM((n_groups, HEAD_PAIRS * qc, LANES), F32),
               pltpu.VMEM((n_groups, HEAD_PAIRS * qc, 2 * HEAD_DIM), F32)]
    aliases = {}
    if latent:
        cache_k, cache_v = cache
        cache_spec = pl.BlockSpec((1, 1, past, KV_W), lambda b: (b, layer, 0, 0))
        in_specs += [full((L, LANES)), full((L, LANES)), cache_spec, cache_spec]
        args += [rope[0], rope[1], cache_k, cache_v]
        scratch += [pltpu.VMEM((N_KV_HEADS, past, HEAD_DIM), BF16),
                    pltpu.VMEM((N_KV_HEADS, 2, past, 2 * HEAD_DIM), BF16)]
    else:
        if kv_prev is not None:
            kv_spec = pl.BlockSpec((nb, 1, L, KV_W), lambda b: (b, layer, 0, 0))
        else:
            kv_spec = pl.BlockSpec((nb, DEPTH, L, KV_W), lambda b: (b, 0, 0, 0))
        kv_shape = jax.ShapeDtypeStruct((B, DEPTH, L, KV_W), F32)
        out_specs += [kv_spec, kv_spec]
        out_shape += [kv_shape, kv_shape]
        if kv_prev is not None:
            in_specs += [pl.BlockSpec(memory_space=pl.ANY)] * 2
            aliases = {len(args): 2, len(args) + 1: 3}
            args += list(kv_prev)
    return pl.pallas_call(
        functools.partial(_attn_kernel, latent, kv_prev is not None, layer, L, nb, qc, past),
        grid=(B // nb,),
        in_specs=in_specs,
        out_specs=out_specs,
        out_shape=out_shape,
        scratch_shapes=scratch,
        input_output_aliases=aliases,
        compiler_params=_cparams(1),
        name=f"attn_branch_L{L}",
    )(*args)


def _shifted(x, offset, period):
    rows = x.shape[0]
    t = lax.broadcasted_iota(jnp.int32, x.shape, 0)
    if rows != period:
        t = t % period
    rolled = pltpu.roll(x, (-offset) % rows, 0)
    valid = (t >= -offset) if offset < 0 else (t < period - offset)
    return jnp.where(valid, rolled, 0.0)


def _hyena_kernel(L, nb, h_ref, w0_ref, w1_ref, w2_ref, wg_ref, sw0_ref, sw1_ref, sw2_ref, sb0_ref, sb1_ref, sb2_ref,
                  fa_ref, fb_ref, fd_ref, bias_ref, fwd_ref, inv_ref, mix_out):
    hb = h_ref[...].reshape(nb * L, D_MODEL)

    def short_conv(w_ref, sw_ref, sb_ref):
        xs = _dot(hb, w_ref[0])
        w = sw_ref[0]
        return sb_ref[0] + _shifted(xs, -1, L) * w[0:1] + xs * w[1:2] + _shifted(xs, 1, L) * w[2:3]

    x0 = short_conv(w0_ref, sw0_ref, sb0_ref)
    x1 = short_conv(w1_ref, sw1_ref, sb1_ref)
    hv = short_conv(w2_ref, sw2_ref, sb2_ref)
    gated = x0 * _silu(_dot(hb, wg_ref[0]))
    z = x1 * hv
    zb = z.astype(BF16)
    fa, fb, fd = fa_ref[0], fb_ref[0], fd_ref[0]
    for b in range(nb):
        rs = slice(b * L, (b + 1) * L)
        zf = _dot(fwd_ref[...], zb[rs])
        re, im = zf[:L], zf[L:]
        y_re = re * fa - im * fb
        y_im = re * fb + im * fd
        y = _dot(inv_ref[:, :L], y_re.astype(BF16)) + _dot(inv_ref[:, L:], y_im.astype(BF16))
        y = y + z[rs] * bias_ref[0]
        mix_out[b] = (y * gated[rs]).astype(BF16)


def _hyena_branch(layer, nb, hmod, w_in_b, short_w, short_b, filt, bias, fwd, inv):
    B, L, _ = hmod.shape
    cb = COL_BLOCK
    nt = HY_W // cb
    fa, fb, fd = filt
    w_spec = lambda part: pl.BlockSpec((1, D_MODEL, cb), lambda i, b: (layer, 0, HY_BLOCK0 + part * nt + i))
    sw_spec = lambda part: pl.BlockSpec((1, 3, cb), lambda i, b: (layer, 0, part * nt + i))
    sb_spec = lambda part: pl.BlockSpec((1, 1, cb), lambda i, b: (layer, 0, part * nt + i))
    filt_spec = pl.BlockSpec((1, L, cb), lambda i, b: (layer, 0, i))
    return pl.pallas_call(
        functools.partial(_hyena_kernel, L, nb),
        grid=(nt, B // nb),
        in_specs=[
            pl.BlockSpec((nb, L, D_MODEL), lambda i, b: (b, 0, 0)),
            w_spec(0), w_spec(1), w_spec(2), w_spec(3),
            sw_spec(0), sw_spec(1), sw_spec(2),
            sb_spec(0), sb_spec(1), sb_spec(2),
            filt_spec, filt_spec, filt_spec,
            pl.BlockSpec((1, 1, cb), lambda i, b: (layer, 0, i)),
            pl.BlockSpec((2 * L, L), lambda i, b: (0, 0)),
            pl.BlockSpec((L, 2 * L), lambda i, b: (0, 0)),
        ],
        out_specs=pl.BlockSpec((nb, L, cb), lambda i, b: (b, 0, i)),
        out_shape=jax.ShapeDtypeStruct((B, L, HY_W), BF16),
        compiler_params=_cparams(2),
        name=f"hyena_branch_L{L}",
    )(hmod, w_in_b, w_in_b, w_in_b, w_in_b, short_w, short_w, short_w, short_b, short_b, short_b,
      fa, fb, fd, bias, fwd, inv)


def _lru_kernel(latent, aliased, final, layer, L, nb, *refs):
    (h_ref, wx0_ref, wx1_ref, wg0_ref, wg1_ref, cw_ref, cb_ref, wa_ref, ba_ref, wxg_ref, bx_ref, lam_ref,
     x_ref, ma_ref, mh_ref, wo_ref, mod_ref, fg_ref) = refs[:18]
    refs = refs[18:]
    if latent:
        h0_ref, x_out, gate_w_s, a_s, b_s, y_s, g_s, part_s = refs
    else:
        if aliased:
            refs = refs[1:]
        x_out, hl_out, gate_w_s, a_s, b_s, y_s, g_s, part_s = refs
    half_w = LRU_W // 2
    blocks_per_half = half_w // LRU_BS
    n_tiles = L // SUBLANES

    @pl.when(pl.program_id(0) == 0)
    def _build_gate_weights():
        gate_w_s[...] = jnp.zeros(gate_w_s.shape, BF16)
        for half in range(2):
            for kind, (w_ref, d) in enumerate(((wa_ref, 0), (wxg_ref, 0), (wa_ref, 1), (wxg_ref, 1))):
                for j in range(blocks_per_half):
                    blk = (0.5 * w_ref[0, d, half * blocks_per_half + j]).astype(BF16)
                    gate_w_s[half, j * LRU_BS:(j + 1) * LRU_BS,
                             kind * half_w + j * LRU_BS:kind * half_w + (j + 1) * LRU_BS] = blk

    part_s[...] = (_dot(ma_ref[...].reshape(nb * L, ATTN_W), wo_ref[0, 0:ATTN_W, :])
                   + _dot(mh_ref[...].reshape(nb * L, HY_W), wo_ref[0, ATTN_W:ATTN_W + HY_W, :]))

    hb = h_ref[...].reshape(nb * L, D_MODEL)
    xs = jnp.concatenate([_dot(hb, wx0_ref[0]), _dot(hb, wx1_ref[0])], axis=1)
    g_s[:, :half_w] = _dot(hb, wg0_ref[0])
    g_s[:, half_w:] = _dot(hb, wg1_ref[0])
    cw = cw_ref[0]
    xc = (cb_ref[0] + _shifted(xs, -1, L) * cw[0:1] + xs * cw[1:2] + _shifted(xs, 1, L) * cw[2:3]
          + _shifted(xs, 2, L) * cw[3:4])
    xcb = xc.astype(BF16)
    c_half = (-0.5 * LRU_C) * jax.nn.softplus(-lam_ref[0])
    ba, bx = 0.5 * ba_ref[0], 0.5 * bx_ref[0]
    for half in range(2):
        cols = slice(half * half_w, (half + 1) * half_w)
        bias = jnp.concatenate([ba[0:1, cols], bx[0:1, cols], ba[1:2, cols], bx[1:2, cols]], axis=1)
        gates = _dot(xcb[:, cols], gate_w_s[half]) + bias
        xh_half = 0.5 * xc[:, cols]
        for d in range(2):
            t_r = jnp.tanh(gates[:, (2 * d) * half_w:(2 * d + 1) * half_w])
            t_i = jnp.tanh(gates[:, (2 * d + 1) * half_w:(2 * d + 2) * half_w])
            ch = c_half[d:d + 1, cols]
            log_a = ch * t_r + ch
            a = jnp.exp(log_a)
            y = jnp.tanh(log_a) * (-1.0 - a * a)
            root = jnp.where(y > 0.0, y * lax.rsqrt(y), 0.0)
            bcoef = root * (xh_half * t_i + xh_half)
            a_s[d, :, :, cols] = a.reshape(nb * n_tiles, SUBLANES, half_w)
            b_s[d, :, :, cols] = bcoef.reshape(nb * n_tiles, SUBLANES, half_w)

    if latent:
        init = []
        for b in range(nb):
            h0 = h0_ref[b, 0]
            init += [h0[0:1], h0[1:2]]
    else:
        init = [jnp.zeros((1, LRU_W), F32)] * (2 * nb)

    def advance(d, tile, rows_g, h):
        a = [a_s[d, tile, r:r + 1, :] for r in rows_g]
        c = [b_s[d, tile, r:r + 1, :] for r in rows_g]
        a01, c01 = a[1] * a[0], a[1] * c[0] + c[1]
        hs = [a[0] * h + c[0], a01 * h + c01]
        if len(rows_g) == 4:
            a23, c23 = a[3] * a[2], a[3] * c[2] + c[3]
            hs += [a[2] * hs[1] + c[2], (a23 * a01) * h + (a23 * c01 + c23)]
        for r, v in zip(rows_g, hs):
            y_s[d, tile, r:r + 1, :] = v
        return hs[-1]

    group = 4 if nb == 1 else 2

    def tile_step(i, carry):
        carry = list(carry)
        for g0 in range(0, SUBLANES, group):
            fwd_rows = list(range(g0, g0 + group))
            bwd_rows = [SUBLANES - 1 - r for r in fwd_rows]
            for b in range(nb):
                carry[2 * b] = advance(0, b * n_tiles + i, fwd_rows, carry[2 * b])
                carry[2 * b + 1] = advance(1, b * n_tiles + (n_tiles - 1 - i), bwd_rows, carry[2 * b + 1])
        return tuple(carry)

    def two_tiles(i, carry):
        return tile_step(2 * i + 1, tile_step(2 * i, carry))

    last = lax.fori_loop(0, n_tiles // 2, two_tiles, tuple(init))
    if not latent:
        if not aliased:
            hl_out[...] = jnp.zeros(hl_out.shape, F32)
        slot = 0 if aliased else layer
        for b in range(nb):
            hl_out[b, slot, 0:1, :] = last[2 * b]
            hl_out[b, slot, 1:2, :] = last[2 * b + 1]
    y = (y_s[0] + y_s[1]).reshape(nb * L, LRU_W)
    mix_l = (y * _silu(g_s[...])).astype(BF16)
    out = part_s[...] + _dot(mix_l, wo_ref[0, ATTN_W + HY_W:, :])
    xn = x_ref[...].reshape(nb * L, D_MODEL) + mod_ref[0, 0][2:3] * out
    if final:
        ms = jnp.mean(xn * xn, axis=-1, keepdims=True)
        xn = xn * lax.rsqrt(ms + EPS) * fg_ref[...]
    x_out[...] = xn.reshape(nb, L, D_MODEL)


def _lru_out_branch(latent, layer, nb, hmod, w_in_b, conv_w, conv_b, wa, ba, wx, bx, lam, x, mix_a, mix_h, w_out_b,
                    mod, final_g, state=None, hl_prev=None):
    B, L, _ = hmod.shape
    final = layer == DEPTH - 1
    cb = COL_BLOCK
    half_w = LRU_W // 2
    rows = nb * L
    per_layer = lambda shape: pl.BlockSpec((1,) + shape, lambda b: (layer,) + (0,) * len(shape))
    w_spec = lambda j: pl.BlockSpec((1, D_MODEL, cb), lambda b: (layer, 0, LRU_BLOCK0 + j))
    gate_blocks = (2, LRU_BLOCKS, LRU_BS, LRU_BS)
    in_specs = [pl.BlockSpec((nb, L, D_MODEL), lambda b: (b, 0, 0)), w_spec(0), w_spec(1), w_spec(2), w_spec(3),
                per_layer((4, LRU_W)), per_layer((1, LRU_W)), per_layer(gate_blocks), per_layer((2, LRU_W)),
                per_layer(gate_blocks), per_layer((2, LRU_W)), per_layer((2, LRU_W))]
    per_b = lambda w: pl.BlockSpec((nb, L, w), lambda b: (b, 0, 0))
    mod_row = (lambda b: (layer, 1 + b, 0, 0)) if latent else (lambda b: (layer, 0, 0, 0))
    in_specs += [per_b(D_MODEL), per_b(ATTN_W), per_b(HY_W), per_layer((D_MIX, D_MODEL)),
                 pl.BlockSpec((1, 1, 3, D_MODEL), mod_row), pl.BlockSpec((1, D_MODEL), lambda b: (0, 0))]
    args = [hmod, w_in_b, w_in_b, w_in_b, w_in_b, conv_w, conv_b, wa, ba, wx, bx, lam,
            x, mix_a, mix_h, w_out_b, mod, final_g]
    out_specs = [per_b(D_MODEL)]
    out_shape = [jax.ShapeDtypeStruct((B, L, D_MODEL), F32)]
    aliases = {}
    if latent:
        in_specs.append(pl.BlockSpec((nb, 1, 2, LRU_W), lambda b: (b, layer, 0, 0)))
        args.append(state)
    else:
        if hl_prev is not None:
            out_specs.append(pl.BlockSpec((nb, 1, 2, LRU_W), lambda b: (b, layer, 0, 0)))
        else:
            out_specs.append(pl.BlockSpec((nb, DEPTH, 2, LRU_W), lambda b: (b, 0, 0, 0)))
        out_shape.append(jax.ShapeDtypeStruct((B, DEPTH, 2, LRU_W), F32))
        if hl_prev is not None:
            in_specs.append(pl.BlockSpec(memory_space=pl.ANY))
            aliases = {len(args): 1}
            args.append(hl_prev)
    scan_buf = pltpu.VMEM((2, rows // SUBLANES, SUBLANES, LRU_W), F32)
    return pl.pallas_call(
        functools.partial(_lru_kernel, latent, hl_prev is not None, final, layer, L, nb),
        grid=(B // nb,),
        in_specs=in_specs,
        out_specs=out_specs,
        out_shape=out_shape,
        scratch_shapes=[pltpu.VMEM((2, half_w, 4 * half_w), BF16), scan_buf, scan_buf, scan_buf,
                        pltpu.VMEM((rows, LRU_W), F32), pltpu.VMEM((rows, D_MODEL), F32)],
        input_output_aliases=aliases,
        compiler_params=_cparams(1),
        name=f"lru_out_L{L}",
    )(*args)


def _mixer_pass(latent, x, p, filt, dft, rope=None, cache=None, state=None):
    B, L, _ = x.shape
    fwd, inv = dft
    nb = 1 if latent else 4
    nb_hyena = min(B, 2) if latent else 4
    nb_lru = 1 if latent else 2
    kv, hl = None, None
    for l in range(DEPTH):
        outs = _attention_branch(latent, l, nb, x, p['mod'], p['norm_g'], p['w_in'], p['gq'], p['gk'], p['e_avg'],
                                 rope, cache, kv)
        hmod, mix_a = outs[0], outs[1]
        if not latent:
            kv = (outs[2], outs[3])
        mix_h = _hyena_branch(l, nb_hyena, hmod, p['w_in'], p['short_w'], p['short_b'], filt, p['hy_bias'], fwd, inv)
        outs = _lru_out_branch(latent, l, nb_lru, hmod, p['w_in'], p['conv_w'], p['conv_b'], p['wa'], p['ba'], p['wx'],
                               p['bx'], p['lam'], x, mix_a, mix_h, p['w_out'], p['mod'], p['final_g'], state, hl)
        x = outs[0]
        if not latent:
            hl = outs[1]
    return x, kv, hl


def kernel(x_prompt, x_sample, cache_k, cache_v, state_lru, c, c_ctx, norm_g, w_ada, b_ada, w_in, q_norm_g, k_norm_g,
           hy_short_w, hy_short_b, hy_filt_w1, hy_filt_b1, hy_filt_w2, hy_filt_b2, hy_filt_w3, hy_filt_freq,
           hy_filt_decay, hy_bias, lru_conv_w, lru_conv_b, lru_wa, lru_ba, lru_wx, lru_bx, lru_lambda, w_out, final_g):
    batch, seq, _ = x_prompt.shape
    dec_batch, dec_seq, _ = x_sample.shape
    past = cache_k.shape[2]

    cvecs = jnp.concatenate([c_ctx[None, :], c, jnp.zeros((MOD_ROWS - 1 - dec_batch, D_MODEL), F32)], axis=0)
    mod = _modulation(cvecs, w_ada, b_ada).reshape(DEPTH, MOD_ROWS, 3, D_MODEL)

    w1p = jnp.pad(hy_filt_w1, ((0, 0), (0, FEAT_PAD - HY_EMB), (0, 0)))
    dft, filt = {}, {}
    for L in (seq, dec_seq):
        dft[L] = tuple(jnp.asarray(m).astype(BF16) for m in _dft_tables(L))
        filt[L] = _filter_spectrum(L, jnp.asarray(_hyena_feats(L)), dft[L][0], w1p, hy_filt_b1[:, None, :],
                                   hy_filt_w2, hy_filt_b2[:, None, :], hy_filt_w3, hy_filt_freq,
                                   hy_filt_decay[:, None, :])

    params = {
        'mod': mod,
        'norm_g': norm_g[:, None, :],
        'w_in': w_in.astype(BF16),
        'gq': jnp.tile(q_norm_g, (1, N_Q_HEADS))[:, None, :],
        'gk': jnp.tile(k_norm_g, (1, N_KV_HEADS))[:, None, :],
        'e_avg': jnp.asarray(_head_average_matrix()).astype(BF16),
        'short_w': hy_short_w, 'short_b': hy_short_b[:, None, :], 'hy_bias': hy_bias[:, None, :],
        'conv_w': lru_conv_w, 'conv_b': lru_conv_b[:, None, :],
        'wa': lru_wa, 'ba': lru_ba, 'wx': lru_wx, 'bx': lru_bx, 'lam': lru_lambda,
        'w_out': w_out.astype(BF16),
        'final_g': final_g[None, :],
    }

    y_prompt, kv, new_lru = _mixer_pass(False, x_prompt, params, filt[seq], dft[seq])
    new_k = kv[0].reshape(batch, DEPTH, seq, N_KV_HEADS, HEAD_DIM)
    new_v = kv[1].reshape(batch, DEPTH, seq, N_KV_HEADS, HEAD_DIM)

    rope = tuple(jnp.asarray(t) for t in _rope_tables(dec_seq))
    cache = (cache_k.reshape(dec_batch, DEPTH, past, KV_W), cache_v.reshape(dec_batch, DEPTH, past, KV_W))
    y_sample, _, _ = _mixer_pass(True, x_sample, params, filt[dec_seq], dft[dec_seq], rope, cache, state_lru)
    return (y_prompt, y_sample, new_k, new_v, new_lru)
```

```python
import functools
import math

import numpy as np
import jax
import jax.numpy as jnp
from jax import lax
from jax.experimental import pallas as pl
from jax.experimental.pallas import tpu as pltpu

F32 = jnp.float32
BF16 = jnp.bfloat16

D_MODEL = 1024
DEPTH = 2
GRID_W = 64
HEAD_DIM = 64
N_Q_HEADS = 8
N_KV_HEADS = 2
Q_PER_KV = N_Q_HEADS // N_KV_HEADS
ATTN_W = N_Q_HEADS * HEAD_DIM
KV_W = N_KV_HEADS * HEAD_DIM
ROPE_THETA = 10000.0
ATTN_SCALE = HEAD_DIM ** -0.5
HY_W = 512
HY_BANDS = 16
HY_EMB = 2 * HY_BANDS + 1
HY_FH = 64
LRU_W = 512
LRU_BLOCKS = 8
LRU_BS = LRU_W // LRU_BLOCKS
LRU_C = 8.0
EPS = 1e-6

ATTN_COLS = ATTN_W + 2 * KV_W + ATTN_W
D_MIX = ATTN_W + HY_W + LRU_W

LANES = 128
SUBLANES = 8
MXU_DIM = 256
COL_BLOCK = MXU_DIM
HY_BLOCK0 = ATTN_COLS // COL_BLOCK
LRU_BLOCK0 = (ATTN_COLS + 4 * HY_W) // COL_BLOCK
Q_CHUNK = 256
Q_CHUNK_LATENT = 128
HEAD_PAIRS = Q_PER_KV // 2
FEAT_PAD = 128
MOD_ROWS = 8
VMEM_LIMIT = 56 * 1024 * 1024


def _cparams(n_axes):
    return pltpu.CompilerParams(dimension_semantics=("arbitrary",) * n_axes, vmem_limit_bytes=VMEM_LIMIT)


def _split_bf16(a):
    hi = a.astype(BF16)
    lo = (a - hi.astype(F32)).astype(BF16)
    return hi, lo


def _dot(a, b):
    return jnp.dot(a, b, preferred_element_type=F32)


def _dot3(a, b):
    a_hi, a_lo = _split_bf16(a)
    b_hi, b_lo = _split_bf16(b)
    return _dot(a_hi, b_hi) + _dot(a_hi, b_lo) + _dot(a_lo, b_hi)


def _silu(x):
    half = 0.5 * x
    return half * jnp.tanh(half) + half


@functools.lru_cache(maxsize=None)
def _dft_tables(L):
    n = 2 * L
    t = np.arange(L)
    kt = (t[:, None] * t[None, :]) % n
    ang = 2.0 * np.pi * kt / n
    cosm, sinm = np.cos(ang), np.sin(ang)
    alt = (-1.0) ** t
    f_s = -sinm
    f_s[0, :] = alt
    fwd = np.concatenate([cosm, f_s], axis=0)
    g_c = 2.0 * cosm.T / n
    g_c[:, 0] = 1.0 / n
    g_s = -2.0 * sinm.T / n
    g_s[:, 0] = alt / n
    inv = np.concatenate([g_c, g_s], axis=1)
    return fwd.astype(np.float32), inv.astype(np.float32)


@functools.lru_cache(maxsize=None)
def _hyena_feats(L):
    t = np.linspace(0.0, 1.0, L)[:, None]
    w = 2.0 * math.pi * np.arange(L)[:, None] / L
    f = np.linspace(1e-4, HY_BANDS - 1, HY_BANDS)[None, :]
    out = np.zeros((L, FEAT_PAD), np.float32)
    out[:, :HY_EMB] = np.concatenate([t, np.cos(f * w), -np.sin(f * w)], axis=-1)
    return out


@functools.lru_cache(maxsize=None)
def _rope_tables(L):
    rows = L // GRID_W
    row = np.repeat(np.arange(rows), GRID_W).astype(np.float64)
    col = np.tile(np.arange(GRID_W), rows).astype(np.float64)
    n_freq = HEAD_DIM // 4
    inv = ROPE_THETA ** (-np.arange(n_freq) / n_freq)
    ang = np.concatenate([row[:, None] * inv, col[:, None] * inv], axis=-1)
    cos = np.repeat(np.cos(ang), 2, axis=-1)
    sin = np.repeat(np.sin(ang), 2, axis=-1) * np.tile(np.array([-1.0, 1.0]), HEAD_DIM // 2)
    reps = LANES // HEAD_DIM
    return np.tile(cos, (1, reps)).astype(np.float32), np.tile(sin, (1, reps)).astype(np.float32)


@functools.lru_cache(maxsize=None)
def _head_average_matrix():
    return np.kron(np.eye(MXU_DIM // HEAD_DIM), np.full((HEAD_DIM, HEAD_DIM), 1.0 / HEAD_DIM)).astype(np.float32)


def _mod_kernel(c_ref, w_ref, b_ref, o_ref):
    s = _silu(c_ref[...])
    o_ref[0] = _dot3(s, w_ref[0]) + b_ref[0]


def _modulation(cvecs, w_ada, b_ada):
    tn = D_MODEL
    return pl.pallas_call(
        _mod_kernel,
        grid=(DEPTH, 3 * D_MODEL // tn),
        in_specs=[
            pl.BlockSpec((MOD_ROWS, D_MODEL), lambda l, j: (0, 0)),
            pl.BlockSpec((1, D_MODEL, tn), lambda l, j: (l, 0, j)),
            pl.BlockSpec((1, 1, tn), lambda l, j: (l, 0, j)),
        ],
        out_specs=pl.BlockSpec((1, MOD_ROWS, tn), lambda l, j: (l, 0, j)),
        out_shape=jax.ShapeDtypeStruct((DEPTH, MOD_ROWS, 3 * D_MODEL), F32),
        compiler_params=_cparams(2),
        name="adaln_mod",
    )(cvecs, w_ada, b_ada.reshape(DEPTH, 1, 3 * D_MODEL))


def _filter_kernel(L, feats_ref, w1_ref, b1_ref, w2_ref, b2_ref, w3_ref, freq_ref, decay_ref,
                   fwd_ref, a_ref, b_ref, d_ref):
    feats = feats_ref[...]
    t = feats[:, 0:1]
    freq = freq_ref[0]
    hdn = jnp.sin(freq[0:1] * (_dot3(feats, w1_ref[0]) + b1_ref[0]))
    hdn = jnp.sin(freq[1:2] * (_dot3(hdn, w2_ref[0]) + b2_ref[0]))
    h = _dot3(hdn, w3_ref[0]) * jnp.exp(-t * decay_ref[0])
    row = lax.broadcasted_iota(jnp.int32, (L, HY_W), 0)
    h_fwd = h[:, :HY_W]
    h_bwd = jnp.where(row >= 1, h[:, HY_W:], 0.0)
    inv_norm = 1.0 / jnp.sum(jnp.abs(h_fwd) + jnp.abs(h_bwd), axis=0, keepdims=True)
    even = h_fwd + h_bwd
    odd = h_fwd - h_bwd
    h_re = _dot(fwd_ref[0:L, :], even.astype(BF16)) * inv_norm
    h_im = _dot(fwd_ref[L:2 * L, :], odd.astype(BF16)) * inv_norm
    alt = jnp.where(row % 2 == 0, 1.0, -1.0)
    nyq = jnp.sum(even * alt, axis=0, keepdims=True) * inv_norm
    a_ref[0] = h_re
    b_ref[0] = jnp.where(row >= 1, h_im, 0.0)
    d_ref[0] = jnp.where(row >= 1, h_re, nyq)


def _filter_spectrum(L, feats, fwd, w1p, b1, w2, b2, w3, freq, decay):
    full = lambda shape: pl.BlockSpec(shape, lambda l: (0,) * len(shape))
    per_layer = lambda shape: pl.BlockSpec((1,) + shape, lambda l: (l,) + (0,) * len(shape))
    out = jax.ShapeDtypeStruct((DEPTH, L, HY_W), F32)
    return pl.pallas_call(
        functools.partial(_filter_kernel, L),
        grid=(DEPTH,),
        in_specs=[
            full((L, FEAT_PAD)),
            per_layer((FEAT_PAD, HY_FH)), per_layer((1, HY_FH)),
            per_layer((HY_FH, HY_FH)), per_layer((1, HY_FH)),
            per_layer((HY_FH, 2 * HY_W)), per_layer((2, HY_FH)), per_layer((1, 2 * HY_W)),
            full((2 * L, L)),
        ],
        out_specs=[per_layer((L, HY_W))] * 3,
        out_shape=[out, out, out],
        compiler_params=_cparams(1),
        name=f"hyena_filter_L{L}",
    )(feats, w1p, b1, w2, b2, w3, freq, decay, fwd)


def _rope(x, cos, sin):
    lane = lax.broadcasted_iota(jnp.int32, x.shape, 1)
    partner = jnp.where(lane % 2 == 0, pltpu.roll(x, LANES - 1, 1), pltpu.roll(x, 1, 1))
    return x * cos + partner * sin


def _head_mean_square(x, e):
    sq = (x * x).astype(BF16)
    width = x.shape[1]
    if width <= MXU_DIM:
        return _dot(sq, e[:width, :width])
    parts = [_dot(sq[:, i:i + MXU_DIM], e) for i in range(0, width, MXU_DIM)]
    return jnp.concatenate(parts, axis=1)


def _store_kv_heads(k_dst, v_dst, k, v):
    ones = jnp.ones((v.shape[0], HEAD_DIM), BF16)
    for j in range(N_KV_HEADS):
        js = slice(j * HEAD_DIM, (j + 1) * HEAD_DIM)
        vj = v[:, js].astype(BF16)
        k_dst[j] = k[:, js].astype(BF16)
        v_dst[j, 0, :, 0:HEAD_DIM] = vj
        v_dst[j, 0, :, HEAD_DIM:2 * HEAD_DIM] = ones
        v_dst[j, 1, :, 0:HEAD_DIM] = ones
        v_dst[j, 1, :, HEAD_DIM:2 * HEAD_DIM] = vj


def _attn_kernel(latent, aliased, layer, L, nb, qc, past, *refs):
    x_ref, mod_ref, ng_ref, w_ref, gq_ref, gk_ref, e_ref = refs[:7]
    refs = refs[7:]
    if latent:
        (cos_ref, sin_ref, ck_ref, cv_ref, h_out, mix_out,
         q_s, k_s, v_s, g_s, s_s, mx_s, acc_s, ck_s, cv_s) = refs
    else:
        if aliased:
            refs = refs[2:]
        h_out, mix_out, k_out, v_out, q_s, k_s, v_s, g_s, s_s, mx_s, acc_s = refs
    rows = nb * L

    x = x_ref[...].reshape(rows, D_MODEL)
    m = mod_ref[0, 0]
    shift, scale = m[0:1], m[1:2]
    ms = jnp.mean(x * x, axis=-1, keepdims=True)
    h = (x * lax.rsqrt(ms + EPS)) * (ng_ref[0] * (1.0 + scale)) + shift
    hb = h.astype(BF16)
    h_out[...] = hb.reshape(nb, L, D_MODEL)

    u = _dot(hb, w_ref[0])
    q = u[:, :ATTN_W]
    k = u[:, ATTN_W:ATTN_W + KV_W]
    v = u[:, ATTN_W + KV_W:ATTN_W + 2 * KV_W]
    g_s[...] = u[:, ATTN_W + 2 * KV_W:]
    e = e_ref[...]
    qn = q * lax.rsqrt(_head_mean_square(q, e) + EPS) * gq_ref[0]
    kn = k * lax.rsqrt(_head_mean_square(k, e) + EPS) * gk_ref[0]
    if latent:
        cos, sin = cos_ref[...], sin_ref[...]
        qn = jnp.concatenate([_rope(qn[:, i:i + LANES], cos, sin) for i in range(0, ATTN_W, LANES)], axis=1)
        kn = _rope(kn, cos, sin)
        _store_kv_heads(ck_s, cv_s, ck_ref[0, 0], cv_ref[0, 0])
    else:
        for b in range(nb):
            kb, vb = kn[b * L:(b + 1) * L], v[b * L:(b + 1) * L]
            if aliased:
                k_out[b, 0] = kb
                v_out[b, 0] = vb
            else:
                for slot in range(DEPTH):
                    k_out[b, slot] = kb if slot == layer else jnp.zeros_like(kb)
                    v_out[b, slot] = vb if slot == layer else jnp.zeros_like(vb)
    qb = (qn * ATTN_SCALE).astype(BF16)
    for hd in range(N_Q_HEADS):
        j, g = divmod(hd, Q_PER_KV)
        q_s[j * Q_PER_KV + (g % 2) * HEAD_PAIRS + g // 2] = qb[:, hd * HEAD_DIM:(hd + 1) * HEAD_DIM]
    _store_kv_heads(k_s, v_s, kn, v)

    contract_last = (((1,), (1,)), ((), ()))
    chunks_per_seq = L // qc
    lane = lax.broadcasted_iota(jnp.int32, (qc, LANES), 1)
    groups = [(j, e) for j in range(N_KV_HEADS) for e in range(2)]

    def chunk(c, carry):
        r0 = pl.multiple_of(c * qc, qc)
        k0 = pl.multiple_of((c // chunks_per_seq) * L, L)
        for idx, (j, e) in enumerate(groups):
            slot = j * Q_PER_KV + e * HEAD_PAIRS
            qh = q_s[slot:slot + HEAD_PAIRS, pl.ds(r0, qc), :].reshape(HEAD_PAIRS * qc, HEAD_DIM)
            s = lax.dot_general(qh, k_s[j, pl.ds(k0, L), :], contract_last, preferred_element_type=F32)
            mx = jnp.max(s, axis=-1, keepdims=True)
            if latent:
                s0 = lax.dot_general(qh, ck_s[j], contract_last, preferred_element_type=F32)
                mx = jnp.maximum(mx, jnp.max(s0, axis=-1, keepdims=True))
                s_s[idx, :, 0:past] = s0
                s_s[idx, :, past:] = s
            else:
                s_s[idx] = s
            mx_s[idx] = jnp.broadcast_to(mx, (HEAD_PAIRS * qc, LANES))
        for idx, (j, e) in enumerate(groups):
            p = jnp.exp(s_s[idx] - jnp.tile(mx_s[idx], (1, (past + L) // LANES))).astype(BF16)
            if latent:
                acc = _dot(p[:, :past], cv_s[j, e]) + _dot(p[:, past:], v_s[j, e, pl.ds(k0, L), :])
            else:
                acc = _dot(p, v_s[j, e, pl.ds(k0, L), :])
            acc_s[idx] = acc
        b_idx = c // chunks_per_seq
        row0 = pl.multiple_of((c % chunks_per_seq) * qc, qc)
        for j in range(N_KV_HEADS):
            for t in range(HEAD_PAIRS):
                a_even = acc_s[2 * j, t * qc:(t + 1) * qc, :]
                a_odd = acc_s[2 * j + 1, t * qc:(t + 1) * qc, :]
                out = jnp.where(lane < HEAD_DIM, a_even, a_odd)
                den = pltpu.roll(jnp.where(lane < HEAD_DIM, a_odd, a_even), HEAD_DIM, 1)
                blk = slice((j * HEAD_PAIRS + t) * LANES, (j * HEAD_PAIRS + t + 1) * LANES)
                gate = _silu(g_s[pl.ds(r0, qc), blk])
                mix_out[b_idx, pl.ds(row0, qc), blk] = (out / den * gate).astype(BF16)
        return carry

    lax.fori_loop(0, nb * chunks_per_seq, chunk, 0)


def _attention_branch(latent, layer, nb, x, mod, norm_g, w_in_b, gq, gk, e_avg, rope=None, cache=None, kv_prev=None):
    B, L, _ = x.shape
    rows = nb * L
    assert nb == 1 or not latent
    full = lambda shape: pl.BlockSpec(shape, lambda b: (0,) * len(shape))
    per_b = lambda shape: pl.BlockSpec((nb,) + shape, lambda b: (b,) + (0,) * len(shape))
    per_layer = lambda shape: pl.BlockSpec((1,) + shape, lambda b: (layer,) + (0,) * len(shape))
    mod_spec = pl.BlockSpec((1, 1, 3, D_MODEL), (lambda b: (layer, 1 + b, 0, 0)) if latent else (lambda b: (layer, 0, 0, 0)))
    in_specs = [per_b((L, D_MODEL)), mod_spec, per_layer((1, D_MODEL)), per_layer((D_MODEL, ATTN_COLS)),
                per_layer((1, ATTN_W)), per_layer((1, KV_W)), full((MXU_DIM, MXU_DIM))]
    args = [x, mod, norm_g, w_in_b, gq, gk, e_avg]
    out_specs = [per_b((L, D_MODEL)), per_b((L, ATTN_W))]
    out_shape = [jax.ShapeDtypeStruct((B, L, D_MODEL), BF16), jax.ShapeDtypeStruct((B, L, ATTN_W), BF16)]
    qc = Q_CHUNK_LATENT if latent else Q_CHUNK
    past = cache[0].shape[2] if latent else 0
    assert B % nb == 0 and L % qc == 0 and L % LANES == 0 and past % LANES == 0
    n_groups = 2 * N_KV_HEADS
    scratch = [pltpu.VMEM((N_Q_HEADS, rows, HEAD_DIM), BF16), pltpu.VMEM((N_KV_HEADS, rows, HEAD_DIM), BF16),
               pltpu.VMEM((N_KV_HEADS, 2, rows, 2 * HEAD_DIM), BF16),
               pltpu.VMEM((rows, ATTN_W), F32),
               pltpu.VMEM((n_groups, HEAD_PAIRS * qc, past + L), F32),
               pltpu.VMEM((n_groups, HEAD_PAIRS * qc, LANES), F32),
               pltpu.VMEM((n_groups, HEAD_PAIRS * qc, 2 * HEAD_DIM), F32)]
    aliases = {}
    if latent:
        cache_k, cache_v = cache
        cache_spec = pl.BlockSpec((1, 1, past, KV_W), lambda b: (b, layer, 0, 0))
        in_specs += [full((L, LANES)), full((L, LANES)), cache_spec, cache_spec]
        args += [rope[0], rope[1], cache_k, cache_v]
        scratch += [pltpu.VMEM((N_KV_HEADS, past, HEAD_DIM), BF16),
                    pltpu.VMEM((N_KV_HEADS, 2, past, 2 * HEAD_DIM), BF16)]
    else:
        if kv_prev is not None:
            kv_spec = pl.BlockSpec((nb, 1, L, KV_W), lambda b: (b, layer, 0, 0))
        else:
            kv_spec = pl.BlockSpec((nb, DEPTH, L, KV_W), lambda b: (b, 0, 0, 0))
        kv_shape = jax.ShapeDtypeStruct((B, DEPTH, L, KV_W), F32)
        out_specs += [kv_spec, kv_spec]
        out_shape += [kv_shape, kv_shape]
        if kv_prev is not None:
            in_specs += [pl.BlockSpec(memory_space=pl.ANY)] * 2
            aliases = {len(args): 2, len(args) + 1: 3}
            args += list(kv_prev)
    return pl.pallas_call(
        functools.partial(_attn_kernel, latent, kv_prev is not None, layer, L, nb, qc, past),
        grid=(B // nb,),
        in_specs=in_specs,
        out_specs=out_specs,
        out_shape=out_shape,
        scratch_shapes=scratch,
        input_output_aliases=aliases,
        compiler_params=_cparams(1),
        name=f"attn_branch_L{L}",
    )(*args)


def _shifted(x, offset, period):
    rows = x.shape[0]
    t = lax.broadcasted_iota(jnp.int32, x.shape, 0)
    if rows != period:
        t = t % period
    rolled = pltpu.roll(x, (-offset) % rows, 0)
    valid = (t >= -offset) if offset < 0 else (t < period - offset)
    return jnp.where(valid, rolled, 0.0)


def _hyena_kernel(L, nb, h_ref, w0_ref, w1_ref, w2_ref, wg_ref, sw0_ref, sw1_ref, sw2_ref, sb0_ref, sb1_ref, sb2_ref,
                  fa_ref, fb_ref, fd_ref, bias_ref, fwd_ref, inv_ref, mix_out):
    hb = h_ref[...].reshape(nb * L, D_MODEL)

    def short_conv(w_ref, sw_ref, sb_ref):
        xs = _dot(hb, w_ref[0])
        w = sw_ref[0]
        return sb_ref[0] + _shifted(xs, -1, L) * w[0:1] + xs * w[1:2] + _shifted(xs, 1, L) * w[2:3]

    x0 = short_conv(w0_ref, sw0_ref, sb0_ref)
    x1 = short_conv(w1_ref, sw1_ref, sb1_ref)
    hv = short_conv(w2_ref, sw2_ref, sb2_ref)
    gated = x0 * _silu(_dot(hb, wg_ref[0]))
    z = x1 * hv
    zb = z.astype(BF16)
    fa, fb, fd = fa_ref[0], fb_ref[0], fd_ref[0]
    cb = z.shape[1]
    z_cat = jnp.concatenate([zb[b * L:(b + 1) * L] for b in range(nb)], axis=1)
    zf = _dot(fwd_ref[...], z_cat)
    y_re, y_im = [], []
    for b in range(nb):
        re, im = zf[:L, b * cb:(b + 1) * cb], zf[L:, b * cb:(b + 1) * cb]
        y_re.append((re * fa - im * fb).astype(BF16))
        y_im.append((re * fb + im * fd).astype(BF16))
    y_all = (_dot(inv_ref[:, :L], jnp.concatenate(y_re, axis=1))
             + _dot(inv_ref[:, L:], jnp.concatenate(y_im, axis=1)))
    for b in range(nb):
        rs = slice(b * L, (b + 1) * L)
        y = y_all[:, b * cb:(b + 1) * cb] + z[rs] * bias_ref[0]
        mix_out[b] = (y * gated[rs]).astype(BF16)


def _hyena_branch(layer, nb, hmod, w_in_b, short_w, short_b, filt, bias, fwd, inv):
    B, L, _ = hmod.shape
    cb = COL_BLOCK
    nt = HY_W // cb
    fa, fb, fd = filt
    assert B % nb == 0 and L % LANES == 0
    w_spec = lambda part: pl.BlockSpec((1, D_MODEL, cb), lambda i, b: (layer, 0, HY_BLOCK0 + part * nt + i))
    sw_spec = lambda part: pl.BlockSpec((1, 3, cb), lambda i, b: (layer, 0, part * nt + i))
    sb_spec = lambda part: pl.BlockSpec((1, 1, cb), lambda i, b: (layer, 0, part * nt + i))
    filt_spec = pl.BlockSpec((1, L, cb), lambda i, b: (layer, 0, i))
    return pl.pallas_call(
        functools.partial(_hyena_kernel, L, nb),
        grid=(nt, B // nb),
        in_specs=[
            pl.BlockSpec((nb, L, D_MODEL), lambda i, b: (b, 0, 0)),
            w_spec(0), w_spec(1), w_spec(2), w_spec(3),
            sw_spec(0), sw_spec(1), sw_spec(2),
            sb_spec(0), sb_spec(1), sb_spec(2),
            filt_spec, filt_spec, filt_spec,
            pl.BlockSpec((1, 1, cb), lambda i, b: (layer, 0, i)),
            pl.BlockSpec((2 * L, L), lambda i, b: (0, 0)),
            pl.BlockSpec((L, 2 * L), lambda i, b: (0, 0)),
        ],
        out_specs=pl.BlockSpec((nb, L, cb), lambda i, b: (b, 0, i)),
        out_shape=jax.ShapeDtypeStruct((B, L, HY_W), BF16),
        compiler_params=_cparams(2),
        name=f"hyena_branch_L{L}",
    )(hmod, w_in_b, w_in_b, w_in_b, w_in_b, short_w, short_w, short_w, short_b, short_b, short_b,
      fa, fb, fd, bias, fwd, inv)


def _lru_kernel(latent, aliased, final, layer, L, nb, *refs):
    (h_ref, wx0_ref, wx1_ref, wg0_ref, wg1_ref, cw_ref, cb_ref, wa_ref, ba_ref, wxg_ref, bx_ref, lam_ref,
     x_ref, ma_ref, mh_ref, wo_ref, mod_ref, fg_ref) = refs[:18]
    refs = refs[18:]
    if latent:
        h0_ref, x_out, gate_w_s, a_s, b_s, y_s, g_s, part_s = refs
    else:
        if aliased:
            refs = refs[1:]
        x_out, hl_out, gate_w_s, a_s, b_s, y_s, g_s, part_s = refs
    half_w = LRU_W // 2
    blocks_per_half = half_w // LRU_BS
    n_tiles = L // SUBLANES

    @pl.when(pl.program_id(0) == 0)
    def _build_gate_weights():
        gate_w_s[...] = jnp.zeros(gate_w_s.shape, BF16)
        for half in range(2):
            for kind, (w_ref, d) in enumerate(((wa_ref, 0), (wxg_ref, 0), (wa_ref, 1), (wxg_ref, 1))):
                for j in range(blocks_per_half):
                    blk = (0.5 * w_ref[0, d, half * blocks_per_half + j]).astype(BF16)
                    gate_w_s[half, j * LRU_BS:(j + 1) * LRU_BS,
                             kind * half_w + j * LRU_BS:kind * half_w + (j + 1) * LRU_BS] = blk

    part_s[...] = (_dot(ma_ref[...].reshape(nb * L, ATTN_W), wo_ref[0, 0:ATTN_W, :])
                   + _dot(mh_ref[...].reshape(nb * L, HY_W), wo_ref[0, ATTN_W:ATTN_W + HY_W, :]))

    hb = h_ref[...].reshape(nb * L, D_MODEL)
    xs = jnp.concatenate([_dot(hb, wx0_ref[0]), _dot(hb, wx1_ref[0])], axis=1)
    g_s[:, :half_w] = _dot(hb, wg0_ref[0])
    g_s[:, half_w:] = _dot(hb, wg1_ref[0])
    cw = cw_ref[0]
    xc = (cb_ref[0] + _shifted(xs, -1, L) * cw[0:1] + xs * cw[1:2] + _shifted(xs, 1, L) * cw[2:3]
          + _shifted(xs, 2, L) * cw[3:4])
    xcb = xc.astype(BF16)
    c_half = (-0.5 * LRU_C) * jax.nn.softplus(-lam_ref[0])
    ba, bx = 0.5 * ba_ref[0], 0.5 * bx_ref[0]
    for half in range(2):
        cols = slice(half * half_w, (half + 1) * half_w)
        bias = jnp.concatenate([ba[0:1, cols], bx[0:1, cols], ba[1:2, cols], bx[1:2, cols]], axis=1)
        gates = _dot(xcb[:, cols], gate_w_s[half]) + bias
        xh_half = 0.5 * xc[:, cols]
        for d in range(2):
            t_r = jnp.tanh(gates[:, (2 * d) * half_w:(2 * d + 1) * half_w])
            t_i = jnp.tanh(gates[:, (2 * d + 1) * half_w:(2 * d + 2) * half_w])
            ch = c_half[d:d + 1, cols]
            log_a = ch * t_r + ch
            a = jnp.exp(log_a)
            y = jnp.tanh(log_a) * (-1.0 - a * a)
            root = jnp.where(y > 0.0, y * lax.rsqrt(y), 0.0)
            bcoef = root * (xh_half * t_i + xh_half)
            a_s[d, :, :, cols] = a.reshape(nb * n_tiles, SUBLANES, half_w)
            b_s[d, :, :, cols] = bcoef.reshape(nb * n_tiles, SUBLANES, half_w)

    if latent:
        init = []
        for b in range(nb):
            h0 = h0_ref[b, 0]
            init += [h0[0:1], h0[1:2]]
    else:
        init = [jnp.zeros((1, LRU_W), F32)] * (2 * nb)

    def advance(d, tile, rows_g, h):
        a = [a_s[d, tile, r:r + 1, :] for r in rows_g]
        c = [b_s[d, tile, r:r + 1, :] for r in rows_g]
        a01, c01 = a[1] * a[0], a[1] * c[0] + c[1]
        hs = [a[0] * h + c[0], a01 * h + c01]
        if len(rows_g) == 4:
            a23, c23 = a[3] * a[2], a[3] * c[2] + c[3]
            hs += [a[2] * hs[1] + c[2], (a23 * a01) * h + (a23 * c01 + c23)]
        for r, v in zip(rows_g, hs):
            y_s[d, tile, r:r + 1, :] = v
        return hs[-1]

    group = 4 if nb == 1 else 2

    def tile_step(i, carry):
        carry = list(carry)
        for g0 in range(0, SUBLANES, group):
            fwd_rows = list(range(g0, g0 + group))
            bwd_rows = [SUBLANES - 1 - r for r in fwd_rows]
            for b in range(nb):
                carry[2 * b] = advance(0, b * n_tiles + i, fwd_rows, carry[2 * b])
                carry[2 * b + 1] = advance(1, b * n_tiles + (n_tiles - 1 - i), bwd_rows, carry[2 * b + 1])
        return tuple(carry)

    def two_tiles(i, carry):
        return tile_step(2 * i + 1, tile_step(2 * i, carry))

    last = lax.fori_loop(0, n_tiles // 2, two_tiles, tuple(init))
    if not latent:
        if not aliased:
            hl_out[...] = jnp.zeros(hl_out.shape, F32)
        slot = 0 if aliased else layer
        for b in range(nb):
            hl_out[b, slot, 0:1, :] = last[2 * b]
            hl_out[b, slot, 1:2, :] = last[2 * b + 1]
    y = (y_s[0] + y_s[1]).reshape(nb * L, LRU_W)
    mix_l = (y * _silu(g_s[...])).astype(BF16)
    out = part_s[...] + _dot(mix_l, wo_ref[0, ATTN_W + HY_W:, :])
    xn = x_ref[...].reshape(nb * L, D_MODEL) + mod_ref[0, 0][2:3] * out
    if final:
        ms = jnp.mean(xn * xn, axis=-1, keepdims=True)
        xn = xn * lax.rsqrt(ms + EPS) * fg_ref[...]
    x_out[...] = xn.reshape(nb, L, D_MODEL)


def _lru_out_branch(latent, layer, nb, hmod, w_in_b, conv_w, conv_b, wa, ba, wx, bx, lam, x, mix_a, mix_h, w_out_b,
                    mod, final_g, state=None, hl_prev=None):
    B, L, _ = hmod.shape
    assert B % nb == 0 and L % (2 * SUBLANES) == 0
    final = layer == DEPTH - 1
    cb = COL_BLOCK
    half_w = LRU_W // 2
    rows = nb * L
    per_layer = lambda shape: pl.BlockSpec((1,) + shape, lambda b: (layer,) + (0,) * len(shape))
    w_spec = lambda j: pl.BlockSpec((1, D_MODEL, cb), lambda b: (layer, 0, LRU_BLOCK0 + j))
    gate_blocks = (2, LRU_BLOCKS, LRU_BS, LRU_BS)
    in_specs = [pl.BlockSpec((nb, L, D_MODEL), lambda b: (b, 0, 0)), w_spec(0), w_spec(1), w_spec(2), w_spec(3),
                per_layer((4, LRU_W)), per_layer((1, LRU_W)), per_layer(gate_blocks), per_layer((2, LRU_W)),
                per_layer(gate_blocks), per_layer((2, LRU_W)), per_layer((2, LRU_W))]
    per_b = lambda w: pl.BlockSpec((nb, L, w), lambda b: (b, 0, 0))
    mod_row = (lambda b: (layer, 1 + b, 0, 0)) if latent else (lambda b: (layer, 0, 0, 0))
    in_specs += [per_b(D_MODEL), per_b(ATTN_W), per_b(HY_W), per_layer((D_MIX, D_MODEL)),
                 pl.BlockSpec((1, 1, 3, D_MODEL), mod_row), pl.BlockSpec((1, D_MODEL), lambda b: (0, 0))]
    args = [hmod, w_in_b, w_in_b, w_in_b, w_in_b, conv_w, conv_b, wa, ba, wx, bx, lam,
            x, mix_a, mix_h, w_out_b, mod, final_g]
    out_specs = [per_b(D_MODEL)]
    out_shape = [jax.ShapeDtypeStruct((B, L, D_MODEL), F32)]
    aliases = {}
    if latent:
        in_specs.append(pl.BlockSpec((nb, 1, 2, LRU_W), lambda b: (b, layer, 0, 0)))
        args.append(state)
    else:
        if hl_prev is not None:
            out_specs.append(pl.BlockSpec((nb, 1, 2, LRU_W), lambda b: (b, layer, 0, 0)))
        else:
            out_specs.append(pl.BlockSpec((nb, DEPTH, 2, LRU_W), lambda b: (b, 0, 0, 0)))
        out_shape.append(jax.ShapeDtypeStruct((B, DEPTH, 2, LRU_W), F32))
        if hl_prev is not None:
            in_specs.append(pl.BlockSpec(memory_space=pl.ANY))
            aliases = {len(args): 1}
            args.append(hl_prev)
    scan_buf = pltpu.VMEM((2, rows // SUBLANES, SUBLANES, LRU_W), F32)
    return pl.pallas_call(
        functools.partial(_lru_kernel, latent, hl_prev is not None, final, layer, L, nb),
        grid=(B // nb,),
        in_specs=in_specs,
        out_specs=out_specs,
        out_shape=out_shape,
        scratch_shapes=[pltpu.VMEM((2, half_w, 4 * half_w), BF16), scan_buf, scan_buf, scan_buf,
                        pltpu.VMEM((rows, LRU_W), F32), pltpu.VMEM((rows, D_MODEL), F32)],
        input_output_aliases=aliases,
        compiler_params=_cparams(1),
        name=f"lru_out_L{L}",
    )(*args)


def _mixer_pass(latent, x, p, filt, dft, rope=None, cache=None, state=None):
    B, L, _ = x.shape
    fwd, inv = dft
    nb = 1 if latent else 4
    nb_hyena = min(B, 2) if latent else 4
    nb_lru = 1 if latent else 2
    kv, hl = None, None
    for l in range(DEPTH):
        outs = _attention_branch(latent, l, nb, x, p['mod'], p['norm_g'], p['w_in'], p['gq'], p['gk'], p['e_avg'],
                                 rope, cache, kv)
        hmod, mix_a = outs[0], outs[1]
        if not latent:
            kv = (outs[2], outs[3])
        mix_h = _hyena_branch(l, nb_hyena, hmod, p['w_in'], p['short_w'], p['short_b'], filt, p['hy_bias'], fwd, inv)
        outs = _lru_out_branch(latent, l, nb_lru, hmod, p['w_in'], p['conv_w'], p['conv_b'], p['wa'], p['ba'], p['wx'],
                               p['bx'], p['lam'], x, mix_a, mix_h, p['w_out'], p['mod'], p['final_g'], state, hl)
        x = outs[0]
        if not latent:
            hl = outs[1]
    return x, kv, hl


def kernel(x_prompt, x_sample, cache_k, cache_v, state_lru, c, c_ctx, norm_g, w_ada, b_ada, w_in, q_norm_g, k_norm_g,
           hy_short_w, hy_short_b, hy_filt_w1, hy_filt_b1, hy_filt_w2, hy_filt_b2, hy_filt_w3, hy_filt_freq,
           hy_filt_decay, hy_bias, lru_conv_w, lru_conv_b, lru_wa, lru_ba, lru_wx, lru_bx, lru_lambda, w_out, final_g):
    batch, seq, _ = x_prompt.shape
    dec_batch, dec_seq, _ = x_sample.shape
    past = cache_k.shape[2]

    cvecs = jnp.concatenate([c_ctx[None, :], c, jnp.zeros((MOD_ROWS - 1 - dec_batch, D_MODEL), F32)], axis=0)
    mod = _modulation(cvecs, w_ada, b_ada).reshape(DEPTH, MOD_ROWS, 3, D_MODEL)

    w1p = jnp.pad(hy_filt_w1, ((0, 0), (0, FEAT_PAD - HY_EMB), (0, 0)))
    dft, filt = {}, {}
    for L in (seq, dec_seq):
        dft[L] = tuple(jnp.asarray(m).astype(BF16) for m in _dft_tables(L))
        filt[L] = _filter_spectrum(L, jnp.asarray(_hyena_feats(L)), dft[L][0], w1p, hy_filt_b1[:, None, :],
                                   hy_filt_w2, hy_filt_b2[:, None, :], hy_filt_w3, hy_filt_freq,
                                   hy_filt_decay[:, None, :])

    params = {
        'mod': mod,
        'norm_g': norm_g[:, None, :],
        'w_in': w_in.astype(BF16),
        'gq': jnp.tile(q_norm_g, (1, N_Q_HEADS))[:, None, :],
        'gk': jnp.tile(k_norm_g, (1, N_KV_HEADS))[:, None, :],
        'e_avg': jnp.asarray(_head_average_matrix()).astype(BF16),
        'short_w': hy_short_w, 'short_b': hy_short_b[:, None, :], 'hy_bias': hy_bias[:, None, :],
        'conv_w': lru_conv_w, 'conv_b': lru_conv_b[:, None, :],
        'wa': lru_wa, 'ba': lru_ba, 'wx': lru_wx, 'bx': lru_bx, 'lam': lru_lambda,
        'w_out': w_out.astype(BF16),
        'final_g': final_g[None, :],
    }

    y_prompt, kv, new_lru = _mixer_pass(False, x_prompt, params, filt[seq], dft[seq])
    new_k = kv[0].reshape(batch, DEPTH, seq, N_KV_HEADS, HEAD_DIM)
    new_v = kv[1].reshape(batch, DEPTH, seq, N_KV_HEADS, HEAD_DIM)

    rope = tuple(jnp.asarray(t) for t in _rope_tables(dec_seq))
    cache = (cache_k.reshape(dec_batch, DEPTH, past, KV_W), cache_v.reshape(dec_batch, DEPTH, past, KV_W))
    y_sample, _, _ = _mixer_pass(True, x_sample, params, filt[dec_seq], dft[dec_seq], rope, cache, state_lru)
    return (y_prompt, y_sample, new_k, new_v, new_lru)
```

```python
import functools
import math

import numpy as np
import jax
import jax.numpy as jnp
from jax import lax
from jax.experimental import pallas as pl
from jax.experimental.pallas import tpu as pltpu

F32 = jnp.float32
BF16 = jnp.bfloat16

D_MODEL = 1024
DEPTH = 2
GRID_W = 64
HEAD_DIM = 64
N_Q_HEADS = 8
N_KV_HEADS = 2
Q_PER_KV = N_Q_HEADS // N_KV_HEADS
ATTN_W = N_Q_HEADS * HEAD_DIM
KV_W = N_KV_HEADS * HEAD_DIM
ROPE_THETA = 10000.0
ATTN_SCALE = HEAD_DIM ** -0.5
HY_W = 512
HY_BANDS = 16
HY_EMB = 2 * HY_BANDS + 1
HY_FH = 64
LRU_W = 512
LRU_BLOCKS = 8
LRU_BS = LRU_W // LRU_BLOCKS
LRU_C = 8.0
EPS = 1e-6

ATTN_COLS = ATTN_W + 2 * KV_W + ATTN_W
D_MIX = ATTN_W + HY_W + LRU_W

LANES = 128
SUBLANES = 8
MXU_DIM = 256
COL_BLOCK = MXU_DIM
HY_BLOCK0 = ATTN_COLS // COL_BLOCK
LRU_BLOCK0 = (ATTN_COLS + 4 * HY_W) // COL_BLOCK
Q_CHUNK = 256
Q_CHUNK_LATENT = 128
HEAD_PAIRS = Q_PER_KV // 2
FEAT_PAD = 128
MOD_ROWS = 8
VMEM_LIMIT = 56 * 1024 * 1024


def _cparams(n_axes):
    return pltpu.CompilerParams(dimension_semantics=("arbitrary",) * n_axes, vmem_limit_bytes=VMEM_LIMIT)


def _split_bf16(a):
    hi = a.astype(BF16)
    lo = (a - hi.astype(F32)).astype(BF16)
    return hi, lo


def _dot(a, b):
    return jnp.dot(a, b, preferred_element_type=F32)


def _dot3(a, b):
    a_hi, a_lo = _split_bf16(a)
    b_hi, b_lo = _split_bf16(b)
    return _dot(a_hi, b_hi) + _dot(a_hi, b_lo) + _dot(a_lo, b_hi)


def _silu(x):
    half = 0.5 * x
    return half * jnp.tanh(half) + half


@functools.lru_cache(maxsize=None)
def _dft_tables(L):
    n = 2 * L
    t = np.arange(L)
    kt = (t[:, None] * t[None, :]) % n
    ang = 2.0 * np.pi * kt / n
    cosm, sinm = np.cos(ang), np.sin(ang)
    alt = (-1.0) ** t
    f_s = -sinm
    f_s[0, :] = alt
    fwd = np.concatenate([cosm, f_s], axis=0)
    g_c = 2.0 * cosm.T / n
    g_c[:, 0] = 1.0 / n
    g_s = -2.0 * sinm.T / n
    g_s[:, 0] = alt / n
    inv = np.concatenate([g_c, g_s], axis=1)
    return fwd.astype(np.float32), inv.astype(np.float32)


@functools.lru_cache(maxsize=None)
def _hyena_feats(L):
    t = np.linspace(0.0, 1.0, L)[:, None]
    w = 2.0 * math.pi * np.arange(L)[:, None] / L
    f = np.linspace(1e-4, HY_BANDS - 1, HY_BANDS)[None, :]
    out = np.zeros((L, FEAT_PAD), np.float32)
    out[:, :HY_EMB] = np.concatenate([t, np.cos(f * w), -np.sin(f * w)], axis=-1)
    return out


@functools.lru_cache(maxsize=None)
def _rope_tables(L):
    rows = L // GRID_W
    row = np.repeat(np.arange(rows), GRID_W).astype(np.float64)
    col = np.tile(np.arange(GRID_W), rows).astype(np.float64)
    n_freq = HEAD_DIM // 4
    inv = ROPE_THETA ** (-np.arange(n_freq) / n_freq)
    ang = np.concatenate([row[:, None] * inv, col[:, None] * inv], axis=-1)
    cos = np.repeat(np.cos(ang), 2, axis=-1)
    sin = np.repeat(np.sin(ang), 2, axis=-1) * np.tile(np.array([-1.0, 1.0]), HEAD_DIM // 2)
    reps = LANES // HEAD_DIM
    return np.tile(cos, (1, reps)).astype(np.float32), np.tile(sin, (1, reps)).astype(np.float32)


@functools.lru_cache(maxsize=None)
def _head_average_matrix():
    return np.kron(np.eye(MXU_DIM // HEAD_DIM), np.full((HEAD_DIM, HEAD_DIM), 1.0 / HEAD_DIM)).astype(np.float32)


def _mod_kernel(n_rows, ct_ref, w_ref, b_ref, o_ref):
    s = _silu(ct_ref[...])
    w = w_ref[0]
    o_ref[...] = jnp.zeros(o_ref.shape, F32)
    for r in range(n_rows):
        o_ref[0, r:r + 1, :] = jnp.sum(s[:, r:r + 1] * w, axis=0, keepdims=True) + b_ref[0]


def _modulation(n_rows, cvecs_t, w_ada, b_ada):
    tn = D_MODEL
    return pl.pallas_call(
        functools.partial(_mod_kernel, n_rows),
        grid=(DEPTH, 3 * D_MODEL // tn),
        in_specs=[
            pl.BlockSpec((D_MODEL, MOD_ROWS), lambda l, j: (0, 0)),
            pl.BlockSpec((1, D_MODEL, tn), lambda l, j: (l, 0, j)),
            pl.BlockSpec((1, 1, tn), lambda l, j: (l, 0, j)),
        ],
        out_specs=pl.BlockSpec((1, MOD_ROWS, tn), lambda l, j: (l, 0, j)),
        out_shape=jax.ShapeDtypeStruct((DEPTH, MOD_ROWS, 3 * D_MODEL), F32),
        compiler_params=_cparams(2),
        name="adaln_mod",
    )(cvecs_t, w_ada, b_ada.reshape(DEPTH, 1, 3 * D_MODEL))


def _filter_kernel(L, feats_ref, w1_ref, b1_ref, w2_ref, b2_ref, w3_ref, freq_ref, decay_ref,
                   fwd_ref, a_ref, b_ref, d_ref):
    feats = feats_ref[...]
    t = feats[:, 0:1]
    freq = freq_ref[0]
    hdn = jnp.sin(freq[0:1] * (_dot3(feats, w1_ref[0]) + b1_ref[0]))
    hdn = jnp.sin(freq[1:2] * (_dot3(hdn, w2_ref[0]) + b2_ref[0]))
    h = _dot3(hdn, w3_ref[0]) * jnp.exp(-t * decay_ref[0])
    row = lax.broadcasted_iota(jnp.int32, (L, HY_W), 0)
    h_fwd = h[:, :HY_W]
    h_bwd = jnp.where(row >= 1, h[:, HY_W:], 0.0)
    inv_norm = 1.0 / jnp.sum(jnp.abs(h_fwd) + jnp.abs(h_bwd), axis=0, keepdims=True)
    even = h_fwd + h_bwd
    odd = h_fwd - h_bwd
    h_re = _dot(fwd_ref[0:L, :], even.astype(BF16)) * inv_norm
    h_im = _dot(fwd_ref[L:2 * L, :], odd.astype(BF16)) * inv_norm
    alt = jnp.where(row % 2 == 0, 1.0, -1.0)
    nyq = jnp.sum(even * alt, axis=0, keepdims=True) * inv_norm
    a_ref[0] = h_re
    b_ref[0] = jnp.where(row >= 1, h_im, 0.0)
    d_ref[0] = jnp.where(row >= 1, h_re, nyq)


def _filter_spectrum(L, feats, fwd, w1p, b1, w2, b2, w3, freq, decay):
    full = lambda shape: pl.BlockSpec(shape, lambda l: (0,) * len(shape))
    per_layer = lambda shape: pl.BlockSpec((1,) + shape, lambda l: (l,) + (0,) * len(shape))
    out = jax.ShapeDtypeStruct((DEPTH, L, HY_W), F32)
    return pl.pallas_call(
        functools.partial(_filter_kernel, L),
        grid=(DEPTH,),
        in_specs=[
            full((L, FEAT_PAD)),
            per_layer((FEAT_PAD, HY_FH)), per_layer((1, HY_FH)),
            per_layer((HY_FH, HY_FH)), per_layer((1, HY_FH)),
            per_layer((HY_FH, 2 * HY_W)), per_layer((2, HY_FH)), per_layer((1, 2 * HY_W)),
            full((2 * L, L)),
        ],
        out_specs=[per_layer((L, HY_W))] * 3,
        out_shape=[out, out, out],
        compiler_params=_cparams(1),
        name=f"hyena_filter_L{L}",
    )(feats, w1p, b1, w2, b2, w3, freq, decay, fwd)


def _rope(x, cos, sin):
    lane = lax.broadcasted_iota(jnp.int32, x.shape, 1)
    partner = jnp.where(lane % 2 == 0, pltpu.roll(x, LANES - 1, 1), pltpu.roll(x, 1, 1))
    return x * cos + partner * sin


def _head_mean_square(x, e):
    sq = (x * x).astype(BF16)
    width = x.shape[1]
    if width <= MXU_DIM:
        return _dot(sq, e[:width, :width])
    parts = [_dot(sq[:, i:i + MXU_DIM], e) for i in range(0, width, MXU_DIM)]
    return jnp.concatenate(parts, axis=1)


def _store_kv_heads(k_dst, v_dst, k, v):
    ones = jnp.ones((v.shape[0], HEAD_DIM), BF16)
    for j in range(N_KV_HEADS):
        js = slice(j * HEAD_DIM, (j + 1) * HEAD_DIM)
        vj = v[:, js].astype(BF16)
        k_dst[j] = k[:, js].astype(BF16)
        v_dst[j, 0, :, 0:HEAD_DIM] = vj
        v_dst[j, 0, :, HEAD_DIM:2 * HEAD_DIM] = ones
        v_dst[j, 1, :, 0:HEAD_DIM] = ones
        v_dst[j, 1, :, HEAD_DIM:2 * HEAD_DIM] = vj


def _attn_kernel(latent, aliased, layer, L, nb, qc, past, *refs):
    x_ref, mod_ref, ng_ref, w_ref, gq_ref, gk_ref, e_ref = refs[:7]
    refs = refs[7:]
    if latent:
        (cos_ref, sin_ref, ck_ref, cv_ref, h_out, mix_out,
         q_s, k_s, v_s, g_s, s_s, mx_s, acc_s, ck_s, cv_s) = refs
    else:
        if aliased:
            refs = refs[2:]
        h_out, mix_out, k_out, v_out, q_s, k_s, v_s, g_s, s_s, mx_s, acc_s = refs
    rows = nb * L

    x = x_ref[...].reshape(rows, D_MODEL)
    m = mod_ref[0, 0]
    shift, scale = m[0:1], m[1:2]
    ms = jnp.mean(x * x, axis=-1, keepdims=True)
    h = (x * lax.rsqrt(ms + EPS)) * (ng_ref[0] * (1.0 + scale)) + shift
    hb = h.astype(BF16)
    h_out[...] = hb.reshape(nb, L, D_MODEL)

    u = _dot(hb, w_ref[0])
    q = u[:, :ATTN_W]
    k = u[:, ATTN_W:ATTN_W + KV_W]
    v = u[:, ATTN_W + KV_W:ATTN_W + 2 * KV_W]
    g_s[...] = u[:, ATTN_W + 2 * KV_W:]
    e = e_ref[...]
    qn = q * lax.rsqrt(_head_mean_square(q, e) + EPS) * gq_ref[0]
    kn = k * lax.rsqrt(_head_mean_square(k, e) + EPS) * gk_ref[0]
    if latent:
        cos, sin = cos_ref[...], sin_ref[...]
        qn = jnp.concatenate([_rope(qn[:, i:i + LANES], cos, sin) for i in range(0, ATTN_W, LANES)], axis=1)
        kn = _rope(kn, cos, sin)
        _store_kv_heads(ck_s, cv_s, ck_ref[0, 0], cv_ref[0, 0])
    else:
        for b in range(nb):
            kb, vb = kn[b * L:(b + 1) * L], v[b * L:(b + 1) * L]
            if aliased:
                k_out[b, 0] = kb
                v_out[b, 0] = vb
            else:
                for slot in range(DEPTH):
                    k_out[b, slot] = kb if slot == layer else jnp.zeros_like(kb)
                    v_out[b, slot] = vb if slot == layer else jnp.zeros_like(vb)
    qb = (qn * ATTN_SCALE).astype(BF16)
    for hd in range(N_Q_HEADS):
        j, g = divmod(hd, Q_PER_KV)
        q_s[j * Q_PER_KV + (g % 2) * HEAD_PAIRS + g // 2] = qb[:, hd * HEAD_DIM:(hd + 1) * HEAD_DIM]
    _store_kv_heads(k_s, v_s, kn, v)

    contract_last = (((1,), (1,)), ((), ()))
    chunks_per_seq = L // qc
    lane = lax.broadcasted_iota(jnp.int32, (qc, LANES), 1)
    groups = [(j, e) for j in range(N_KV_HEADS) for e in range(2)]

    def chunk(c, carry):
        r0 = pl.multiple_of(c * qc, qc)
        k0 = pl.multiple_of((c // chunks_per_seq) * L, L)
        for idx, (j, e) in enumerate(groups):
            slot = j * Q_PER_KV + e * HEAD_PAIRS
            qh = q_s[slot:slot + HEAD_PAIRS, pl.ds(r0, qc), :].reshape(HEAD_PAIRS * qc, HEAD_DIM)
            s = lax.dot_general(qh, k_s[j, pl.ds(k0, L), :], contract_last, preferred_element_type=F32)
            mx = jnp.max(s, axis=-1, keepdims=True)
            if latent:
                s0 = lax.dot_general(qh, ck_s[j], contract_last, preferred_element_type=F32)
                mx = jnp.maximum(mx, jnp.max(s0, axis=-1, keepdims=True))
                s_s[idx, :, 0:past] = s0
                s_s[idx, :, past:] = s
            else:
                s_s[idx] = s
            mx_s[idx] = jnp.broadcast_to(mx, (HEAD_PAIRS * qc, LANES))
        for idx, (j, e) in enumerate(groups):
            p = jnp.exp(s_s[idx] - jnp.tile(mx_s[idx], (1, (past + L) // LANES))).astype(BF16)
            if latent:
                acc = _dot(p[:, :past], cv_s[j, e]) + _dot(p[:, past:], v_s[j, e, pl.ds(k0, L), :])
            else:
                acc = _dot(p, v_s[j, e, pl.ds(k0, L), :])
            acc_s[idx] = acc
        b_idx = c // chunks_per_seq
        row0 = pl.multiple_of((c % chunks_per_seq) * qc, qc)
        for j in range(N_KV_HEADS):
            for t in range(HEAD_PAIRS):
                a_even = acc_s[2 * j, t * qc:(t + 1) * qc, :]
                a_odd = acc_s[2 * j + 1, t * qc:(t + 1) * qc, :]
                out = jnp.where(lane < HEAD_DIM, a_even, a_odd)
                den = pltpu.roll(jnp.where(lane < HEAD_DIM, a_odd, a_even), HEAD_DIM, 1)
                blk = slice((j * HEAD_PAIRS + t) * LANES, (j * HEAD_PAIRS + t + 1) * LANES)
                gate = _silu(g_s[pl.ds(r0, qc), blk])
                mix_out[b_idx, pl.ds(row0, qc), blk] = (out / den * gate).astype(BF16)
        return carry

    lax.fori_loop(0, nb * chunks_per_seq, chunk, 0)


def _attention_branch(latent, layer, nb, x, mod, norm_g, w_in_b, gq, gk, e_avg, rope=None, cache=None, kv_prev=None):
    B, L, _ = x.shape
    rows = nb * L
    assert nb == 1 or not latent
    full = lambda shape: pl.BlockSpec(shape, lambda b: (0,) * len(shape))
    per_b = lambda shape: pl.BlockSpec((nb,) + shape, lambda b: (b,) + (0,) * len(shape))
    per_layer = lambda shape: pl.BlockSpec((1,) + shape, lambda b: (layer,) + (0,) * len(shape))
    mod_spec = pl.BlockSpec((1, 1, 3, D_MODEL), (lambda b: (layer, 1 + b, 0, 0)) if latent else (lambda b: (layer, 0, 0, 0)))
    in_specs = [per_b((L, D_MODEL)), mod_spec, per_layer((1, D_MODEL)), per_layer((D_MODEL, ATTN_COLS)),
                per_layer((1, ATTN_W)), per_layer((1, KV_W)), full((MXU_DIM, MXU_DIM))]
    args = [x, mod, norm_g, w_in_b, gq, gk, e_avg]
    out_specs = [per_b((L, D_MODEL)), per_b((L, ATTN_W))]
    out_shape = [jax.ShapeDtypeStruct((B, L, D_MODEL), BF16), jax.ShapeDtypeStruct((B, L, ATTN_W), BF16)]
    qc = Q_CHUNK_LATENT if latent else Q_CHUNK
    past = cache[0].shape[2] if latent else 0
    assert B % nb == 0 and L % qc == 0 and L % LANES == 0 and past % LANES == 0
    n_groups = 2 * N_KV_HEADS
    scratch = [pltpu.VMEM((N_Q_HEADS, rows, HEAD_DIM), BF16), pltpu.VMEM((N_KV_HEADS, rows, HEAD_DIM), BF16),
               pltpu.VMEM((N_KV_HEADS, 2, rows, 2 * HEAD_DIM), BF16),
               pltpu.VMEM((rows, ATTN_W), F32),
               pltpu.VMEM((n_groups, HEAD_PAIRS * qc, past + L), F32),
               pltpu.VMEM((n_groups, HEAD_PAIRS * qc, LANES), F32),
               pltpu.VMEM((n_groups, HEAD_PAIRS * qc, 2 * HEAD_DIM), F32)]
    aliases = {}
    if latent:
        cache_k, cache_v = cache
        cache_spec = pl.BlockSpec((1, 1, past, KV_W), lambda b: (b, layer, 0, 0))
        in_specs += [full((L, LANES)), full((L, LANES)), cache_spec, cache_spec]
        args += [rope[0], rope[1], cache_k, cache_v]
        scratch += [pltpu.VMEM((N_KV_HEADS, past, HEAD_DIM), BF16),
                    pltpu.VMEM((N_KV_HEADS, 2, past, 2 * HEAD_DIM), BF16)]
    else:
        if kv_prev is not None:
            kv_spec = pl.BlockSpec((nb, 1, L, KV_W), lambda b: (b, layer, 0, 0))
        else:
            kv_spec = pl.BlockSpec((nb, DEPTH, L, KV_W), lambda b: (b, 0, 0, 0))
        kv_shape = jax.ShapeDtypeStruct((B, DEPTH, L, KV_W), F32)
        out_specs += [kv_spec, kv_spec]
        out_shape += [kv_shape, kv_shape]
        if kv_prev is not None:
            in_specs += [pl.BlockSpec(memory_space=pl.ANY)] * 2
            aliases = {len(args): 2, len(args) + 1: 3}
            args += list(kv_prev)
    return pl.pallas_call(
        functools.partial(_attn_kernel, latent, kv_prev is not None, layer, L, nb, qc, past),
        grid=(B // nb,),
        in_specs=in_specs,
        out_specs=out_specs,
        out_shape=out_shape,
        scratch_shapes=scratch,
        input_output_aliases=aliases,
        compiler_params=_cparams(1),
        name=f"attn_branch_L{L}",
    )(*args)


def _shifted(x, offset, period):
    rows = x.shape[0]
    t = lax.broadcasted_iota(jnp.int32, x.shape, 0)
    if rows != period:
        t = t % period
    rolled = pltpu.roll(x, (-offset) % rows, 0)
    valid = (t >= -offset) if offset < 0 else (t < period - offset)
    return jnp.where(valid, rolled, 0.0)


def _hyena_kernel(L, nb, h_ref, w0_ref, w1_ref, w2_ref, wg_ref, sw0_ref, sw1_ref, sw2_ref, sb0_ref, sb1_ref, sb2_ref,
                  fa_ref, fb_ref, fd_ref, bias_ref, fwd_ref, inv_ref, mix_out):
    hb = h_ref[...].reshape(nb * L, D_MODEL)

    def short_conv(w_ref, sw_ref, sb_ref):
        xs = _dot(hb, w_ref[0])
        w = sw_ref[0]
        return sb_ref[0] + _shifted(xs, -1, L) * w[0:1] + xs * w[1:2] + _shifted(xs, 1, L) * w[2:3]

    x0 = short_conv(w0_ref, sw0_ref, sb0_ref)
    x1 = short_conv(w1_ref, sw1_ref, sb1_ref)
    hv = short_conv(w2_ref, sw2_ref, sb2_ref)
    gated = x0 * _silu(_dot(hb, wg_ref[0]))
    z = x1 * hv
    zb = z.astype(BF16)
    fa, fb, fd = fa_ref[0], fb_ref[0], fd_ref[0]
    cb = z.shape[1]
    z_cat = jnp.concatenate([zb[b * L:(b + 1) * L] for b in range(nb)], axis=1)
    zf = _dot(fwd_ref[...], z_cat)
    y_re, y_im = [], []
    for b in range(nb):
        re, im = zf[:L, b * cb:(b + 1) * cb], zf[L:, b * cb:(b + 1) * cb]
        y_re.append((re * fa - im * fb).astype(BF16))
        y_im.append((re * fb + im * fd).astype(BF16))
    y_all = (_dot(inv_ref[:, :L], jnp.concatenate(y_re, axis=1))
             + _dot(inv_ref[:, L:], jnp.concatenate(y_im, axis=1)))
    for b in range(nb):
        rs = slice(b * L, (b + 1) * L)
        y = y_all[:, b * cb:(b + 1) * cb] + z[rs] * bias_ref[0]
        mix_out[b] = (y * gated[rs]).astype(BF16)


def _hyena_branch(layer, nb, hmod, w_in_b, short_w, short_b, filt, bias, fwd, inv):
    B, L, _ = hmod.shape
    cb = COL_BLOCK
    nt = HY_W // cb
    fa, fb, fd = filt
    assert B % nb == 0 and L % LANES == 0
    w_spec = lambda part: pl.BlockSpec((1, D_MODEL, cb), lambda i, b: (layer, 0, HY_BLOCK0 + part * nt + i))
    sw_spec = lambda part: pl.BlockSpec((1, 3, cb), lambda i, b: (layer, 0, part * nt + i))
    sb_spec = lambda part: pl.BlockSpec((1, 1, cb), lambda i, b: (layer, 0, part * nt + i))
    filt_spec = pl.BlockSpec((1, L, cb), lambda i, b: (layer, 0, i))
    return pl.pallas_call(
        functools.partial(_hyena_kernel, L, nb),
        grid=(nt, B // nb),
        in_specs=[
            pl.BlockSpec((nb, L, D_MODEL), lambda i, b: (b, 0, 0)),
            w_spec(0), w_spec(1), w_spec(2), w_spec(3),
            sw_spec(0), sw_spec(1), sw_spec(2),
            sb_spec(0), sb_spec(1), sb_spec(2),
            filt_spec, filt_spec, filt_spec,
            pl.BlockSpec((1, 1, cb), lambda i, b: (layer, 0, i)),
            pl.BlockSpec((2 * L, L), lambda i, b: (0, 0)),
            pl.BlockSpec((L, 2 * L), lambda i, b: (0, 0)),
        ],
        out_specs=pl.BlockSpec((nb, L, cb), lambda i, b: (b, 0, i)),
        out_shape=jax.ShapeDtypeStruct((B, L, HY_W), BF16),
        compiler_params=_cparams(2),
        name=f"hyena_branch_L{L}",
    )(hmod, w_in_b, w_in_b, w_in_b, w_in_b, short_w, short_w, short_w, short_b, short_b, short_b,
      fa, fb, fd, bias, fwd, inv)


def _lru_kernel(latent, aliased, final, layer, L, nb, *refs):
    (h_ref, wx0_ref, wx1_ref, wg0_ref, wg1_ref, cw_ref, cb_ref, wa_ref, ba_ref, wxg_ref, bx_ref, lam_ref,
     x_ref, ma_ref, mh_ref, wo_ref, mod_ref, fg_ref) = refs[:18]
    refs = refs[18:]
    if latent:
        h0_ref, x_out, gate_w_s, a_s, b_s, y_s, g_s, part_s = refs
    else:
        if aliased:
            refs = refs[1:]
        x_out, hl_out, gate_w_s, a_s, b_s, y_s, g_s, part_s = refs
    half_w = LRU_W // 2
    blocks_per_half = half_w // LRU_BS
    n_tiles = L // SUBLANES

    @pl.when(pl.program_id(0) == 0)
    def _build_gate_weights():
        gate_w_s[...] = jnp.zeros(gate_w_s.shape, BF16)
        for half in range(2):
            for kind, (w_ref, d) in enumerate(((wa_ref, 0), (wxg_ref, 0), (wa_ref, 1), (wxg_ref, 1))):
                for j in range(blocks_per_half):
                    blk = (0.5 * w_ref[0, d, half * blocks_per_half + j]).astype(BF16)
                    gate_w_s[half, j * LRU_BS:(j + 1) * LRU_BS,
                             kind * half_w + j * LRU_BS:kind * half_w + (j + 1) * LRU_BS] = blk

    part_s[...] = (_dot(ma_ref[...].reshape(nb * L, ATTN_W), wo_ref[0, 0:ATTN_W, :])
                   + _dot(mh_ref[...].reshape(nb * L, HY_W), wo_ref[0, ATTN_W:ATTN_W + HY_W, :]))

    hb = h_ref[...].reshape(nb * L, D_MODEL)
    xs = jnp.concatenate([_dot(hb, wx0_ref[0]), _dot(hb, wx1_ref[0])], axis=1)
    g_s[:, :half_w] = _dot(hb, wg0_ref[0])
    g_s[:, half_w:] = _dot(hb, wg1_ref[0])
    cw = cw_ref[0]
    xc = (cb_ref[0] + _shifted(xs, -1, L) * cw[0:1] + xs * cw[1:2] + _shifted(xs, 1, L) * cw[2:3]
          + _shifted(xs, 2, L) * cw[3:4])
    xcb = xc.astype(BF16)
    c_half = (-0.5 * LRU_C) * jax.nn.softplus(-lam_ref[0])
    ba, bx = 0.5 * ba_ref[0], 0.5 * bx_ref[0]
    for half in range(2):
        cols = slice(half * half_w, (half + 1) * half_w)
        bias = jnp.concatenate([ba[0:1, cols], bx[0:1, cols], ba[1:2, cols], bx[1:2, cols]], axis=1)
        gates = _dot(xcb[:, cols], gate_w_s[half]) + bias
        xh_half = 0.5 * xc[:, cols]
        for d in range(2):
            t_r = jnp.tanh(gates[:, (2 * d) * half_w:(2 * d + 1) * half_w])
            t_i = jnp.tanh(gates[:, (2 * d + 1) * half_w:(2 * d + 2) * half_w])
            ch = c_half[d:d + 1, cols]
            log_a = ch * t_r + ch
            a = jnp.exp(log_a)
            y = jnp.tanh(log_a) * (-1.0 - a * a)
            root = jnp.where(y > 0.0, y * lax.rsqrt(y), 0.0)
            bcoef = root * (xh_half * t_i + xh_half)
            a_s[d, :, :, cols] = a.reshape(nb * n_tiles, SUBLANES, half_w)
            b_s[d, :, :, cols] = bcoef.reshape(nb * n_tiles, SUBLANES, half_w)

    if latent:
        init = []
        for b in range(nb):
            h0 = h0_ref[b, 0]
            init += [h0[0:1], h0[1:2]]
    else:
        init = [jnp.zeros((1, LRU_W), F32)] * (2 * nb)

    def advance(d, tile, rows_g, h):
        a = [a_s[d, tile, r:r + 1, :] for r in rows_g]
        c = [b_s[d, tile, r:r + 1, :] for r in rows_g]
        a01, c01 = a[1] * a[0], a[1] * c[0] + c[1]
        hs = [a[0] * h + c[0], a01 * h + c01]
        if len(rows_g) == 4:
            a23, c23 = a[3] * a[2], a[3] * c[2] + c[3]
            hs += [a[2] * hs[1] + c[2], (a23 * a01) * h + (a23 * c01 + c23)]
        for r, v in zip(rows_g, hs):
            y_s[d, tile, r:r + 1, :] = v
        return hs[-1]

    group = 4 if nb == 1 else 2

    def tile_step(i, carry):
        carry = list(carry)
        for g0 in range(0, SUBLANES, group):
            fwd_rows = list(range(g0, g0 + group))
            bwd_rows = [SUBLANES - 1 - r for r in fwd_rows]
            for b in range(nb):
                carry[2 * b] = advance(0, b * n_tiles + i, fwd_rows, carry[2 * b])
                carry[2 * b + 1] = advance(1, b * n_tiles + (n_tiles - 1 - i), bwd_rows, carry[2 * b + 1])
        return tuple(carry)

    def two_tiles(i, carry):
        return tile_step(2 * i + 1, tile_step(2 * i, carry))

    last = lax.fori_loop(0, n_tiles // 2, two_tiles, tuple(init))
    if not latent:
        if not aliased:
            hl_out[...] = jnp.zeros(hl_out.shape, F32)
        slot = 0 if aliased else layer
        for b in range(nb):
            hl_out[b, slot, 0:1, :] = last[2 * b]
            hl_out[b, slot, 1:2, :] = last[2 * b + 1]
    y = (y_s[0] + y_s[1]).reshape(nb * L, LRU_W)
    mix_l = (y * _silu(g_s[...])).astype(BF16)
    out = part_s[...] + _dot(mix_l, wo_ref[0, ATTN_W + HY_W:, :])
    xn = x_ref[...].reshape(nb * L, D_MODEL) + mod_ref[0, 0][2:3] * out
    if final:
        ms = jnp.mean(xn * xn, axis=-1, keepdims=True)
        xn = xn * lax.rsqrt(ms + EPS) * fg_ref[...]
    x_out[...] = xn.reshape(nb, L, D_MODEL)


def _lru_out_branch(latent, layer, nb, hmod, w_in_b, conv_w, conv_b, wa, ba, wx, bx, lam, x, mix_a, mix_h, w_out_b,
                    mod, final_g, state=None, hl_prev=None):
    B, L, _ = hmod.shape
    assert B % nb == 0 and L % (2 * SUBLANES) == 0
    final = layer == DEPTH - 1
    cb = COL_BLOCK
    half_w = LRU_W // 2
    rows = nb * L
    per_layer = lambda shape: pl.BlockSpec((1,) + shape, lambda b: (layer,) + (0,) * len(shape))
    w_spec = lambda j: pl.BlockSpec((1, D_MODEL, cb), lambda b: (layer, 0, LRU_BLOCK0 + j))
    gate_blocks = (2, LRU_BLOCKS, LRU_BS, LRU_BS)
    in_specs = [pl.BlockSpec((nb, L, D_MODEL), lambda b: (b, 0, 0)), w_spec(0), w_spec(1), w_spec(2), w_spec(3),
                per_layer((4, LRU_W)), per_layer((1, LRU_W)), per_layer(gate_blocks), per_layer((2, LRU_W)),
                per_layer(gate_blocks), per_layer((2, LRU_W)), per_layer((2, LRU_W))]
    per_b = lambda w: pl.BlockSpec((nb, L, w), lambda b: (b, 0, 0))
    mod_row = (lambda b: (layer, 1 + b, 0, 0)) if latent else (lambda b: (layer, 0, 0, 0))
    in_specs += [per_b(D_MODEL), per_b(ATTN_W), per_b(HY_W), per_layer((D_MIX, D_MODEL)),
                 pl.BlockSpec((1, 1, 3, D_MODEL), mod_row), pl.BlockSpec((1, D_MODEL), lambda b: (0, 0))]
    args = [hmod, w_in_b, w_in_b, w_in_b, w_in_b, conv_w, conv_b, wa, ba, wx, bx, lam,
            x, mix_a, mix_h, w_out_b, mod, final_g]
    out_specs = [per_b(D_MODEL)]
    out_shape = [jax.ShapeDtypeStruct((B, L, D_MODEL), F32)]
    aliases = {}
    if latent:
        in_specs.append(pl.BlockSpec((nb, 1, 2, LRU_W), lambda b: (b, layer, 0, 0)))
        args.append(state)
    else:
        if hl_prev is not None:
            out_specs.append(pl.BlockSpec((nb, 1, 2, LRU_W), lambda b: (b, layer, 0, 0)))
        else:
            out_specs.append(pl.BlockSpec((nb, DEPTH, 2, LRU_W), lambda b: (b, 0, 0, 0)))
        out_shape.append(jax.ShapeDtypeStruct((B, DEPTH, 2, LRU_W), F32))
        if hl_prev is not None:
            in_specs.append(pl.BlockSpec(memory_space=pl.ANY))
            aliases = {len(args): 1}
            args.append(hl_prev)
    scan_buf = pltpu.VMEM((2, rows // SUBLANES, SUBLANES, LRU_W), F32)
    return pl.pallas_call(
        functools.partial(_lru_kernel, latent, hl_prev is not None, final, layer, L, nb),
        grid=(B // nb,),
        in_specs=in_specs,
        out_specs=out_specs,
        out_shape=out_shape,
        scratch_shapes=[pltpu.VMEM((2, half_w, 4 * half_w), BF16), scan_buf, scan_buf, scan_buf,
                        pltpu.VMEM((rows, LRU_W), F32), pltpu.VMEM((rows, D_MODEL), F32)],
        input_output_aliases=aliases,
        compiler_params=_cparams(1),
        name=f"lru_out_L{L}",
    )(*args)


def _mixer_pass(latent, x, p, filt, dft, rope=None, cache=None, state=None):
    B, L, _ = x.shape
    fwd, inv = dft
    nb = 1 if latent else 4
    nb_hyena = min(B, 2) if latent else 4
    nb_lru = 1 if latent else 2
    kv, hl = None, None
    for l in range(DEPTH):
        outs = _attention_branch(latent, l, nb, x, p['mod'], p['norm_g'], p['w_in'], p['gq'], p['gk'], p['e_avg'],
                                 rope, cache, kv)
        hmod, mix_a = outs[0], outs[1]
        if not latent:
            kv = (outs[2], outs[3])
        mix_h = _hyena_branch(l, nb_hyena, hmod, p['w_in'], p['short_w'], p['short_b'], filt, p['hy_bias'], fwd, inv)
        outs = _lru_out_branch(latent, l, nb_lru, hmod, p['w_in'], p['conv_w'], p['conv_b'], p['wa'], p['ba'], p['wx'],
                               p['bx'], p['lam'], x, mix_a, mix_h, p['w_out'], p['mod'], p['final_g'], state, hl)
        x = outs[0]
        if not latent:
            hl = outs[1]
    return x, kv, hl


def kernel(x_prompt, x_sample, cache_k, cache_v, state_lru, c, c_ctx, norm_g, w_ada, b_ada, w_in, q_norm_g, k_norm_g,
           hy_short_w, hy_short_b, hy_filt_w1, hy_filt_b1, hy_filt_w2, hy_filt_b2, hy_filt_w3, hy_filt_freq,
           hy_filt_decay, hy_bias, lru_conv_w, lru_conv_b, lru_wa, lru_ba, lru_wx, lru_bx, lru_lambda, w_out, final_g):
    batch, seq, _ = x_prompt.shape
    dec_batch, dec_seq, _ = x_sample.shape
    past = cache_k.shape[2]

    assert 1 + dec_batch <= MOD_ROWS
    cvecs_t = jnp.concatenate([c_ctx[:, None], c.T, jnp.zeros((D_MODEL, MOD_ROWS - 1 - dec_batch), F32)], axis=1)
    mod = _modulation(1 + dec_batch, cvecs_t, w_ada, b_ada).reshape(DEPTH, MOD_ROWS, 3, D_MODEL)

    w1p = jnp.pad(hy_filt_w1, ((0, 0), (0, FEAT_PAD - HY_EMB), (0, 0)))
    dft, filt = {}, {}
    for L in (seq, dec_seq):
        dft[L] = tuple(jnp.asarray(m).astype(BF16) for m in _dft_tables(L))
        filt[L] = _filter_spectrum(L, jnp.asarray(_hyena_feats(L)), dft[L][0], w1p, hy_filt_b1[:, None, :],
                                   hy_filt_w2, hy_filt_b2[:, None, :], hy_filt_w3, hy_filt_freq,
                                   hy_filt_decay[:, None, :])

    params = {
        'mod': mod,
        'norm_g': norm_g[:, None, :],
        'w_in': w_in.astype(BF16),
        'gq': jnp.tile(q_norm_g, (1, N_Q_HEADS))[:, None, :],
        'gk': jnp.tile(k_norm_g, (1, N_KV_HEADS))[:, None, :],
        'e_avg': jnp.asarray(_head_average_matrix()).astype(BF16),
        'short_w': hy_short_w, 'short_b': hy_short_b[:, None, :], 'hy_bias': hy_bias[:, None, :],
        'conv_w': lru_conv_w, 'conv_b': lru_conv_b[:, None, :],
        'wa': lru_wa, 'ba': lru_ba, 'wx': lru_wx, 'bx': lru_bx, 'lam': lru_lambda,
        'w_out': w_out.astype(BF16),
        'final_g': final_g[None, :],
    }

    y_prompt, kv, new_lru = _mixer_pass(False, x_prompt, params, filt[seq], dft[seq])
    new_k = kv[0].reshape(batch, DEPTH, seq, N_KV_HEADS, HEAD_DIM)
    new_v = kv[1].reshape(batch, DEPTH, seq, N_KV_HEADS, HEAD_DIM)

    rope = tuple(jnp.asarray(t) for t in _rope_tables(dec_seq))
    cache = (cache_k.reshape(dec_batch, DEPTH, past, KV_W), cache_v.reshape(dec_batch, DEPTH, past, KV_W))
    y_sample, _, _ = _mixer_pass(True, x_sample, params, filt[dec_seq], dft[dec_seq], rope, cache, state_lru)
    return (y_prompt, y_sample, new_k, new_v, new_lru)
```

```python
import functools
import math

import numpy as np
import jax
import jax.numpy as jnp
from jax import lax
from jax.experimental import pallas as pl
from jax.experimental.pallas import tpu as pltpu

F32 = jnp.float32
BF16 = jnp.bfloat16

D_MODEL = 1024
DEPTH = 2
GRID_W = 64
HEAD_DIM = 64
N_Q_HEADS = 8
N_KV_HEADS = 2
Q_PER_KV = N_Q_HEADS // N_KV_HEADS
ATTN_W = N_Q_HEADS * HEAD_DIM
KV_W = N_KV_HEADS * HEAD_DIM
ROPE_THETA = 10000.0
ATTN_SCALE = HEAD_DIM ** -0.5
HY_W = 512
HY_BANDS = 16
HY_EMB = 2 * HY_BANDS + 1
HY_FH = 64
LRU_W = 512
LRU_BLOCKS = 8
LRU_BS = LRU_W // LRU_BLOCKS
LRU_C = 8.0
EPS = 1e-6

ATTN_COLS = ATTN_W + 2 * KV_W + ATTN_W
D_MIX = ATTN_W + HY_W + LRU_W

LANES = 128
SUBLANES = 8
MXU_DIM = 256
COL_BLOCK = MXU_DIM
HY_BLOCK0 = ATTN_COLS // COL_BLOCK
LRU_BLOCK0 = (ATTN_COLS + 4 * HY_W) // COL_BLOCK
Q_CHUNK = 256
Q_CHUNK_LATENT = 128
HEAD_PAIRS = Q_PER_KV // 2
FEAT_PAD = 128
MOD_ROWS = 8
MOD_SLAB_ROWS = 256
MOD_COL_CHUNK = 1024
VMEM_LIMIT = 56 * 1024 * 1024


def _cparams(n_axes):
    return pltpu.CompilerParams(dimension_semantics=("arbitrary",) * n_axes, vmem_limit_bytes=VMEM_LIMIT)


def _split_bf16(a):
    hi = a.astype(BF16)
    lo = (a - hi.astype(F32)).astype(BF16)
    return hi, lo


def _dot(a, b):
    return jnp.dot(a, b, preferred_element_type=F32)


def _dot3(a, b):
    a_hi, a_lo = _split_bf16(a)
    b_hi, b_lo = _split_bf16(b)
    return _dot(a_hi, b_hi) + _dot(a_hi, b_lo) + _dot(a_lo, b_hi)


def _silu(x):
    half = 0.5 * x
    return half * jnp.tanh(half) + half


@functools.lru_cache(maxsize=None)
def _dft_tables(L):
    n = 2 * L
    t = np.arange(L)
    kt = (t[:, None] * t[None, :]) % n
    ang = 2.0 * np.pi * kt / n
    cosm, sinm = np.cos(ang), np.sin(ang)
    alt = (-1.0) ** t
    f_s = -sinm
    f_s[0, :] = alt
    fwd = np.concatenate([cosm, f_s], axis=0)
    g_c = 2.0 * cosm.T / n
    g_c[:, 0] = 1.0 / n
    g_s = -2.0 * sinm.T / n
    g_s[:, 0] = alt / n
    inv = np.concatenate([g_c, g_s], axis=1)
    return fwd.astype(np.float32), inv.astype(np.float32)


@functools.lru_cache(maxsize=None)
def _hyena_feats(L):
    t = np.linspace(0.0, 1.0, L)[:, None]
    w = 2.0 * math.pi * np.arange(L)[:, None] / L
    f = np.linspace(1e-4, HY_BANDS - 1, HY_BANDS)[None, :]
    out = np.zeros((L, FEAT_PAD), np.float32)
    out[:, :HY_EMB] = np.concatenate([t, np.cos(f * w), -np.sin(f * w)], axis=-1)
    return out


@functools.lru_cache(maxsize=None)
def _rope_tables(L):
    rows = L // GRID_W
    row = np.repeat(np.arange(rows), GRID_W).astype(np.float64)
    col = np.tile(np.arange(GRID_W), rows).astype(np.float64)
    n_freq = HEAD_DIM // 4
    inv = ROPE_THETA ** (-np.arange(n_freq) / n_freq)
    ang = np.concatenate([row[:, None] * inv, col[:, None] * inv], axis=-1)
    cos = np.repeat(np.cos(ang), 2, axis=-1)
    sin = np.repeat(np.sin(ang), 2, axis=-1) * np.tile(np.array([-1.0, 1.0]), HEAD_DIM // 2)
    reps = LANES // HEAD_DIM
    return np.tile(cos, (1, reps)).astype(np.float32), np.tile(sin, (1, reps)).astype(np.float32)


@functools.lru_cache(maxsize=None)
def _head_average_matrix():
    return np.kron(np.eye(MXU_DIM // HEAD_DIM), np.full((HEAD_DIM, HEAD_DIM), 1.0 / HEAD_DIM)).astype(np.float32)


def _mod_kernel(n_rows, ct_ref, w_ref, b_ref, o_ref):
    s = _silu(ct_ref[...])
    tk, width = w_ref.shape[1], w_ref.shape[2]

    @pl.when(pl.program_id(1) == 0)
    def _start_from_bias():
        o_ref[0] = jnp.broadcast_to(b_ref[0], o_ref.shape[1:])

    for c0 in range(0, width, MOD_COL_CHUNK):
        cols = slice(c0, c0 + MOD_COL_CHUNK)
        acc = [jnp.zeros((SUBLANES, MOD_COL_CHUNK), F32)] * n_rows
        for g0 in range(0, tk, SUBLANES):
            wg = w_ref[0, g0:g0 + SUBLANES, cols]
            sg = s[g0:g0 + SUBLANES]
            acc = [acc[r] + sg[:, r:r + 1] * wg for r in range(n_rows)]
        for r in range(n_rows):
            o_ref[0, r:r + 1, cols] += jnp.sum(acc[r], axis=0, keepdims=True)


def _modulation(n_rows, cvecs_t, w_ada, b_ada):
    tk = MOD_SLAB_ROWS
    return pl.pallas_call(
        functools.partial(_mod_kernel, n_rows),
        grid=(DEPTH, D_MODEL // tk),
        in_specs=[
            pl.BlockSpec((tk, MOD_ROWS), lambda l, k: (k, 0)),
            pl.BlockSpec((1, tk, 3 * D_MODEL), lambda l, k: (l, k, 0)),
            pl.BlockSpec((1, 1, 3 * D_MODEL), lambda l, k: (l, 0, 0)),
        ],
        out_specs=pl.BlockSpec((1, MOD_ROWS, 3 * D_MODEL), lambda l, k: (l, 0, 0)),
        out_shape=jax.ShapeDtypeStruct((DEPTH, MOD_ROWS, 3 * D_MODEL), F32),
        compiler_params=_cparams(2),
        name="adaln_mod",
    )(cvecs_t, w_ada, b_ada.reshape(DEPTH, 1, 3 * D_MODEL))


def _filter_kernel(L, feats_ref, w1_ref, b1_ref, w2_ref, b2_ref, w3_ref, freq_ref, decay_ref,
                   fwd_ref, a_ref, b_ref, d_ref):
    feats = feats_ref[...]
    t = feats[:, 0:1]
    freq = freq_ref[0]
    hdn = jnp.sin(freq[0:1] * (_dot3(feats, w1_ref[0]) + b1_ref[0]))
    hdn = jnp.sin(freq[1:2] * (_dot3(hdn, w2_ref[0]) + b2_ref[0]))
    h = _dot3(hdn, w3_ref[0]) * jnp.exp(-t * decay_ref[0])
    row = lax.broadcasted_iota(jnp.int32, (L, HY_W), 0)
    h_fwd = h[:, :HY_W]
    h_bwd = jnp.where(row >= 1, h[:, HY_W:], 0.0)
    inv_norm = 1.0 / jnp.sum(jnp.abs(h_fwd) + jnp.abs(h_bwd), axis=0, keepdims=True)
    even = h_fwd + h_bwd
    odd = h_fwd - h_bwd
    h_re = _dot(fwd_ref[0:L, :], even.astype(BF16)) * inv_norm
    h_im = _dot(fwd_ref[L:2 * L, :], odd.astype(BF16)) * inv_norm
    alt = jnp.where(row % 2 == 0, 1.0, -1.0)
    nyq = jnp.sum(even * alt, axis=0, keepdims=True) * inv_norm
    a_ref[0] = h_re
    b_ref[0] = jnp.where(row >= 1, h_im, 0.0)
    d_ref[0] = jnp.where(row >= 1, h_re, nyq)


def _filter_spectrum(L, feats, fwd, w1p, b1, w2, b2, w3, freq, decay):
    full = lambda shape: pl.BlockSpec(shape, lambda l: (0,) * len(shape))
    per_layer = lambda shape: pl.BlockSpec((1,) + shape, lambda l: (l,) + (0,) * len(shape))
    out = jax.ShapeDtypeStruct((DEPTH, L, HY_W), F32)
    return pl.pallas_call(
        functools.partial(_filter_kernel, L),
        grid=(DEPTH,),
        in_specs=[
            full((L, FEAT_PAD)),
            per_layer((FEAT_PAD, HY_FH)), per_layer((1, HY_FH)),
            per_layer((HY_FH, HY_FH)), per_layer((1, HY_FH)),
            per_layer((HY_FH, 2 * HY_W)), per_layer((2, HY_FH)), per_layer((1, 2 * HY_W)),
            full((2 * L, L)),
        ],
        out_specs=[per_layer((L, HY_W))] * 3,
        out_shape=[out, out, out],
        compiler_params=_cparams(1),
        name=f"hyena_filter_L{L}",
    )(feats, w1p, b1, w2, b2, w3, freq, decay, fwd)


def _rope(x, cos, sin):
    lane = lax.broadcasted_iota(jnp.int32, x.shape, 1)
    partner = jnp.where(lane % 2 == 0, pltpu.roll(x, LANES - 1, 1), pltpu.roll(x, 1, 1))
    return x * cos + partner * sin


def _head_mean_square(x, e):
    sq = (x * x).astype(BF16)
    width = x.shape[1]
    if width <= MXU_DIM:
        return _dot(sq, e[:width, :width])
    parts = [_dot(sq[:, i:i + MXU_DIM], e) for i in range(0, width, MXU_DIM)]
    return jnp.concatenate(parts, axis=1)


def _store_kv_heads(k_dst, v_dst, k, v):
    ones = jnp.ones((v.shape[0], HEAD_DIM), BF16)
    for j in range(N_KV_HEADS):
        js = slice(j * HEAD_DIM, (j + 1) * HEAD_DIM)
        vj = v[:, js].astype(BF16)
        k_dst[j] = k[:, js].astype(BF16)
        v_dst[j, 0, :, 0:HEAD_DIM] = vj
        v_dst[j, 0, :, HEAD_DIM:2 * HEAD_DIM] = ones
        v_dst[j, 1, :, 0:HEAD_DIM] = ones
        v_dst[j, 1, :, HEAD_DIM:2 * HEAD_DIM] = vj


def _attn_kernel(latent, aliased, layer, L, nb, qc, past, *refs):
    x_ref, mod_ref, ng_ref, w_ref, gq_ref, gk_ref, e_ref = refs[:7]
    refs = refs[7:]
    if latent:
        (cos_ref, sin_ref, ck_ref, cv_ref, h_out, mix_out,
         q_s, k_s, v_s, g_s, s_s, mx_s, acc_s, ck_s, cv_s) = refs
    else:
        if aliased:
            refs = refs[2:]
        h_out, mix_out, k_out, v_out, q_s, k_s, v_s, g_s, s_s, mx_s, acc_s = refs
    rows = nb * L

    x = x_ref[...].reshape(rows, D_MODEL)
    m = mod_ref[0, 0]
    shift, scale = m[0:1], m[1:2]
    ms = jnp.mean(x * x, axis=-1, keepdims=True)
    h = (x * lax.rsqrt(ms + EPS)) * (ng_ref[0] * (1.0 + scale)) + shift
    hb = h.astype(BF16)
    h_out[...] = hb.reshape(nb, L, D_MODEL)

    u = _dot(hb, w_ref[0])
    q = u[:, :ATTN_W]
    k = u[:, ATTN_W:ATTN_W + KV_W]
    v = u[:, ATTN_W + KV_W:ATTN_W + 2 * KV_W]
    g_s[...] = u[:, ATTN_W + 2 * KV_W:]
    e = e_ref[...]
    qn = q * lax.rsqrt(_head_mean_square(q, e) + EPS) * gq_ref[0]
    kn = k * lax.rsqrt(_head_mean_square(k, e) + EPS) * gk_ref[0]
    if latent:
        cos, sin = cos_ref[...], sin_ref[...]
        qn = jnp.concatenate([_rope(qn[:, i:i + LANES], cos, sin) for i in range(0, ATTN_W, LANES)], axis=1)
        kn = _rope(kn, cos, sin)
        _store_kv_heads(ck_s, cv_s, ck_ref[0, 0], cv_ref[0, 0])
    else:
        for b in range(nb):
            kb, vb = kn[b * L:(b + 1) * L], v[b * L:(b + 1) * L]
            if aliased:
                k_out[b, 0] = kb
                v_out[b, 0] = vb
            else:
                for slot in range(DEPTH):
                    k_out[b, slot] = kb if slot == layer else jnp.zeros_like(kb)
                    v_out[b, slot] = vb if slot == layer else jnp.zeros_like(vb)
    qb = (qn * ATTN_SCALE).astype(BF16)
    for hd in range(N_Q_HEADS):
        j, g = divmod(hd, Q_PER_KV)
        q_s[j * Q_PER_KV + (g % 2) * HEAD_PAIRS + g // 2] = qb[:, hd * HEAD_DIM:(hd + 1) * HEAD_DIM]
    _store_kv_heads(k_s, v_s, kn, v)

    contract_last = (((1,), (1,)), ((), ()))
    chunks_per_seq = L // qc
    lane = lax.broadcasted_iota(jnp.int32, (qc, LANES), 1)
    groups = [(j, e) for j in range(N_KV_HEADS) for e in range(2)]

    def chunk(c, carry):
        r0 = pl.multiple_of(c * qc, qc)
        k0 = pl.multiple_of((c // chunks_per_seq) * L, L)
        for idx, (j, e) in enumerate(groups):
            slot = j * Q_PER_KV + e * HEAD_PAIRS
            qh = q_s[slot:slot + HEAD_PAIRS, pl.ds(r0, qc), :].reshape(HEAD_PAIRS * qc, HEAD_DIM)
            s = lax.dot_general(qh, k_s[j, pl.ds(k0, L), :], contract_last, preferred_element_type=F32)
            mx = jnp.max(s, axis=-1, keepdims=True)
            if latent:
                s0 = lax.dot_general(qh, ck_s[j], contract_last, preferred_element_type=F32)
                mx = jnp.maximum(mx, jnp.max(s0, axis=-1, keepdims=True))
                s_s[idx, :, 0:past] = s0
                s_s[idx, :, past:] = s
            else:
                s_s[idx] = s
            mx_s[idx] = jnp.broadcast_to(mx, (HEAD_PAIRS * qc, LANES))
        for idx, (j, e) in enumerate(groups):
            p = jnp.exp(s_s[idx] - jnp.tile(mx_s[idx], (1, (past + L) // LANES))).astype(BF16)
            if latent:
                acc = _dot(p[:, :past], cv_s[j, e]) + _dot(p[:, past:], v_s[j, e, pl.ds(k0, L), :])
            else:
                acc = _dot(p, v_s[j, e, pl.ds(k0, L), :])
            acc_s[idx] = acc
        b_idx = c // chunks_per_seq
        row0 = pl.multiple_of((c % chunks_per_seq) * qc, qc)
        for j in range(N_KV_HEADS):
            for t in range(HEAD_PAIRS):
                a_even = acc_s[2 * j, t * qc:(t + 1) * qc, :]
                a_odd = acc_s[2 * j + 1, t * qc:(t + 1) * qc, :]
                out = jnp.where(lane < HEAD_DIM, a_even, a_odd)
                den = pltpu.roll(jnp.where(lane < HEAD_DIM, a_odd, a_even), HEAD_DIM, 1)
                blk = slice((j * HEAD_PAIRS + t) * LANES, (j * HEAD_PAIRS + t + 1) * LANES)
                gate = _silu(g_s[pl.ds(r0, qc), blk])
                mix_out[b_idx, pl.ds(row0, qc), blk] = (out / den * gate).astype(BF16)
        return carry

    lax.fori_loop(0, nb * chunks_per_seq, chunk, 0)


def _attention_branch(latent, layer, nb, x, mod, norm_g, w_in_b, gq, gk, e_avg, rope=None, cache=None, kv_prev=None):
    B, L, _ = x.shape
    rows = nb * L
    assert nb == 1 or not latent
    full = lambda shape: pl.BlockSpec(shape, lambda b: (0,) * len(shape))
    per_b = lambda shape: pl.BlockSpec((nb,) + shape, lambda b: (b,) + (0,) * len(shape))
    per_layer = lambda shape: pl.BlockSpec((1,) + shape, lambda b: (layer,) + (0,) * len(shape))
    mod_spec = pl.BlockSpec((1, 1, 3, D_MODEL), (lambda b: (layer, 1 + b, 0, 0)) if latent else (lambda b: (layer, 0, 0, 0)))
    in_specs = [per_b((L, D_MODEL)), mod_spec, per_layer((1, D_MODEL)), per_layer((D_MODEL, ATTN_COLS)),
                per_layer((1, ATTN_W)), per_layer((1, KV_W)), full((MXU_DIM, MXU_DIM))]
    args = [x, mod, norm_g, w_in_b, gq, gk, e_avg]
    out_specs = [per_b((L, D_MODEL)), per_b((L, ATTN_W))]
    out_shape = [jax.ShapeDtypeStruct((B, L, D_MODEL), BF16), jax.ShapeDtypeStruct((B, L, ATTN_W), BF16)]
    qc = Q_CHUNK_LATENT if latent else Q_CHUNK
    past = cache[0].shape[2] if latent else 0
    assert B % nb == 0 and L % qc == 0 and L % LANES == 0 and past % LANES == 0
    n_groups = 2 * N_KV_HEADS
    scratch = [pltpu.VMEM((N_Q_HEADS, rows, HEAD_DIM), BF16), pltpu.VMEM((N_KV_HEADS, rows, HEAD_DIM), BF16),
               pltpu.VMEM((N_KV_HEADS, 2, rows, 2 * HEAD_DIM), BF16),
               pltpu.VMEM((rows, ATTN_W), F32),
               pltpu.VMEM((n_groups, HEAD_PAIRS * qc, past + L), F32),
               pltpu.VMEM((n_groups, HEAD_PAIRS * qc, LANES), F32),
               pltpu.VMEM((n_groups, HEAD_PAIRS * qc, 2 * HEAD_DIM), F32)]
    aliases = {}
    if latent:
        cache_k, cache_v = cache
        cache_spec = pl.BlockSpec((1, 1, past, KV_W), lambda b: (b, layer, 0, 0))
        in_specs += [full((L, LANES)), full((L, LANES)), cache_spec, cache_spec]
        args += [rope[0], rope[1], cache_k, cache_v]
        scratch += [pltpu.VMEM((N_KV_HEADS, past, HEAD_DIM), BF16),
                    pltpu.VMEM((N_KV_HEADS, 2, past, 2 * HEAD_DIM), BF16)]
    else:
        if kv_prev is not None:
            kv_spec = pl.BlockSpec((nb, 1, L, KV_W), lambda b: (b, layer, 0, 0))
        else:
            kv_spec = pl.BlockSpec((nb, DEPTH, L, KV_W), lambda b: (b, 0, 0, 0))
        kv_shape = jax.ShapeDtypeStruct((B, DEPTH, L, KV_W), F32)
        out_specs += [kv_spec, kv_spec]
        out_shape += [kv_shape, kv_shape]
        if kv_prev is not None:
            in_specs += [pl.BlockSpec(memory_space=pl.ANY)] * 2
            aliases = {len(args): 2, len(args) + 1: 3}
            args += list(kv_prev)
    return pl.pallas_call(
        functools.partial(_attn_kernel, latent, kv_prev is not None, layer, L, nb, qc, past),
        grid=(B // nb,),
        in_specs=in_specs,
        out_specs=out_specs,
        out_shape=out_shape,
        scratch_shapes=scratch,
        input_output_aliases=aliases,
        compiler_params=_cparams(1),
        name=f"attn_branch_L{L}",
    )(*args)


def _shifted(x, offset, period):
    rows = x.shape[0]
    t = lax.broadcasted_iota(jnp.int32, x.shape, 0)
    if rows != period:
        t = t % period
    rolled = pltpu.roll(x, (-offset) % rows, 0)
    valid = (t >= -offset) if offset < 0 else (t < period - offset)
    return jnp.where(valid, rolled, 0.0)


def _hyena_kernel(L, nb, h_ref, w0_ref, w1_ref, w2_ref, wg_ref, sw0_ref, sw1_ref, sw2_ref, sb0_ref, sb1_ref, sb2_ref,
                  fa_ref, fb_ref, fd_ref, bias_ref, fwd_ref, inv_ref, mix_out):
    hb = h_ref[...].reshape(nb * L, D_MODEL)

    def short_conv(w_ref, sw_ref, sb_ref):
        xs = _dot(hb, w_ref[0])
        w = sw_ref[0]
        return sb_ref[0] + _shifted(xs, -1, L) * w[0:1] + xs * w[1:2] + _shifted(xs, 1, L) * w[2:3]

    x0 = short_conv(w0_ref, sw0_ref, sb0_ref)
    x1 = short_conv(w1_ref, sw1_ref, sb1_ref)
    hv = short_conv(w2_ref, sw2_ref, sb2_ref)
    gated = x0 * _silu(_dot(hb, wg_ref[0]))
    z = x1 * hv
    zb = z.astype(BF16)
    fa, fb, fd = fa_ref[0], fb_ref[0], fd_ref[0]
    cb = z.shape[1]
    z_cat = jnp.concatenate([zb[b * L:(b + 1) * L] for b in range(nb)], axis=1)
    zf = _dot(fwd_ref[...], z_cat)
    y_re, y_im = [], []
    for b in range(nb):
        re, im = zf[:L, b * cb:(b + 1) * cb], zf[L:, b * cb:(b + 1) * cb]
        y_re.append((re * fa - im * fb).astype(BF16))
        y_im.append((re * fb + im * fd).astype(BF16))
    y_all = (_dot(inv_ref[:, :L], jnp.concatenate(y_re, axis=1))
             + _dot(inv_ref[:, L:], jnp.concatenate(y_im, axis=1)))
    for b in range(nb):
        rs = slice(b * L, (b + 1) * L)
        y = y_all[:, b * cb:(b + 1) * cb] + z[rs] * bias_ref[0]
        mix_out[b] = (y * gated[rs]).astype(BF16)


def _hyena_branch(layer, nb, hmod, w_in_b, short_w, short_b, filt, bias, fwd, inv):
    B, L, _ = hmod.shape
    cb = COL_BLOCK
    nt = HY_W // cb
    fa, fb, fd = filt
    assert B % nb == 0 and L % LANES == 0
    w_spec = lambda part: pl.BlockSpec((1, D_MODEL, cb), lambda i, b: (layer, 0, HY_BLOCK0 + part * nt + i))
    sw_spec = lambda part: pl.BlockSpec((1, 3, cb), lambda i, b: (layer, 0, part * nt + i))
    sb_spec = lambda part: pl.BlockSpec((1, 1, cb), lambda i, b: (layer, 0, part * nt + i))
    filt_spec = pl.BlockSpec((1, L, cb), lambda i, b: (layer, 0, i))
    return pl.pallas_call(
        functools.partial(_hyena_kernel, L, nb),
        grid=(nt, B // nb),
        in_specs=[
            pl.BlockSpec((nb, L, D_MODEL), lambda i, b: (b, 0, 0)),
            w_spec(0), w_spec(1), w_spec(2), w_spec(3),
            sw_spec(0), sw_spec(1), sw_spec(2),
            sb_spec(0), sb_spec(1), sb_spec(2),
            filt_spec, filt_spec, filt_spec,
            pl.BlockSpec((1, 1, cb), lambda i, b: (layer, 0, i)),
            pl.BlockSpec((2 * L, L), lambda i, b: (0, 0)),
            pl.BlockSpec((L, 2 * L), lambda i, b: (0, 0)),
        ],
        out_specs=pl.BlockSpec((nb, L, cb), lambda i, b: (b, 0, i)),
        out_shape=jax.ShapeDtypeStruct((B, L, HY_W), BF16),
        compiler_params=_cparams(2),
        name=f"hyena_branch_L{L}",
    )(hmod, w_in_b, w_in_b, w_in_b, w_in_b, short_w, short_w, short_w, short_b, short_b, short_b,
      fa, fb, fd, bias, fwd, inv)


def _lru_kernel(latent, aliased, final, layer, L, nb, *refs):
    (h_ref, wx0_ref, wx1_ref, wg0_ref, wg1_ref, cw_ref, cb_ref, wa_ref, ba_ref, wxg_ref, bx_ref, lam_ref,
     x_ref, ma_ref, mh_ref, wo_ref, mod_ref, fg_ref) = refs[:18]
    refs = refs[18:]
    if latent:
        h0_ref, x_out, gate_w_s, a_s, b_s, y_s, g_s, part_s = refs
    else:
        if aliased:
            refs = refs[1:]
        x_out, hl_out, gate_w_s, a_s, b_s, y_s, g_s, part_s = refs
    half_w = LRU_W // 2
    blocks_per_half = half_w // LRU_BS
    n_tiles = L // SUBLANES

    @pl.when(pl.program_id(0) == 0)
    def _build_gate_weights():
        gate_w_s[...] = jnp.zeros(gate_w_s.shape, BF16)
        for half in range(2):
            for kind, (w_ref, d) in enumerate(((wa_ref, 0), (wxg_ref, 0), (wa_ref, 1), (wxg_ref, 1))):
                for j in range(blocks_per_half):
                    blk = (0.5 * w_ref[0, d, half * blocks_per_half + j]).astype(BF16)
                    gate_w_s[half, j * LRU_BS:(j + 1) * LRU_BS,
                             kind * half_w + j * LRU_BS:kind * half_w + (j + 1) * LRU_BS] = blk

    part_s[...] = (_dot(ma_ref[...].reshape(nb * L, ATTN_W), wo_ref[0, 0:ATTN_W, :])
                   + _dot(mh_ref[...].reshape(nb * L, HY_W), wo_ref[0, ATTN_W:ATTN_W + HY_W, :]))

    hb = h_ref[...].reshape(nb * L, D_MODEL)
    xs = jnp.concatenate([_dot(hb, wx0_ref[0]), _dot(hb, wx1_ref[0])], axis=1)
    g_s[:, :half_w] = _dot(hb, wg0_ref[0])
    g_s[:, half_w:] = _dot(hb, wg1_ref[0])
    cw = cw_ref[0]
    xc = (cb_ref[0] + _shifted(xs, -1, L) * cw[0:1] + xs * cw[1:2] + _shifted(xs, 1, L) * cw[2:3]
          + _shifted(xs, 2, L) * cw[3:4])
    xcb = xc.astype(BF16)
    c_half = (-0.5 * LRU_C) * jax.nn.softplus(-lam_ref[0])
    ba, bx = 0.5 * ba_ref[0], 0.5 * bx_ref[0]
    for half in range(2):
        cols = slice(half * half_w, (half + 1) * half_w)
        bias = jnp.concatenate([ba[0:1, cols], bx[0:1, cols], ba[1:2, cols], bx[1:2, cols]], axis=1)
        gates = _dot(xcb[:, cols], gate_w_s[half]) + bias
        xh_half = 0.5 * xc[:, cols]
        for d in range(2):
            t_r = jnp.tanh(gates[:, (2 * d) * half_w:(2 * d + 1) * half_w])
            t_i = jnp.tanh(gates[:, (2 * d + 1) * half_w:(2 * d + 2) * half_w])
            ch = c_half[d:d + 1, cols]
            log_a = ch * t_r + ch
            a = jnp.exp(log_a)
            y = jnp.tanh(log_a) * (-1.0 - a * a)
            root = jnp.where(y > 0.0, y * lax.rsqrt(y), 0.0)
            bcoef = root * (xh_half * t_i + xh_half)
            a_s[d, :, :, cols] = a.reshape(nb * n_tiles, SUBLANES, half_w)
            b_s[d, :, :, cols] = bcoef.reshape(nb * n_tiles, SUBLANES, half_w)

    if latent:
        init = []
        for b in range(nb):
            h0 = h0_ref[b, 0]
            init += [h0[0:1], h0[1:2]]
    else:
        init = [jnp.zeros((1, LRU_W), F32)] * (2 * nb)

    def advance(d, tile, rows_g, h):
        a = [a_s[d, tile, r:r + 1, :] for r in rows_g]
        c = [b_s[d, tile, r:r + 1, :] for r in rows_g]
        a01, c01 = a[1] * a[0], a[1] * c[0] + c[1]
        hs = [a[0] * h + c[0], a01 * h + c01]
        if len(rows_g) == 4:
            a23, c23 = a[3] * a[2], a[3] * c[2] + c[3]
            hs += [a[2] * hs[1] + c[2], (a23 * a01) * h + (a23 * c01 + c23)]
        for r, v in zip(rows_g, hs):
            y_s[d, tile, r:r + 1, :] = v
        return hs[-1]

    group = 4 if nb == 1 else 2

    def tile_step(i, carry):
        carry = list(carry)
        for g0 in range(0, SUBLANES, group):
            fwd_rows = list(range(g0, g0 + group))
            bwd_rows = [SUBLANES - 1 - r for r in fwd_rows]
            for b in range(nb):
                carry[2 * b] = advance(0, b * n_tiles + i, fwd_rows, carry[2 * b])
                carry[2 * b + 1] = advance(1, b * n_tiles + (n_tiles - 1 - i), bwd_rows, carry[2 * b + 1])
        return tuple(carry)

    def two_tiles(i, carry):
        return tile_step(2 * i + 1, tile_step(2 * i, carry))

    last = lax.fori_loop(0, n_tiles // 2, two_tiles, tuple(init))
    if not latent:
        if not aliased:
            hl_out[...] = jnp.zeros(hl_out.shape, F32)
        slot = 0 if aliased else layer
        for b in range(nb):
            hl_out[b, slot, 0:1, :] = last[2 * b]
            hl_out[b, slot, 1:2, :] = last[2 * b + 1]
    y = (y_s[0] + y_s[1]).reshape(nb * L, LRU_W)
    mix_l = (y * _silu(g_s[...])).astype(BF16)
    out = part_s[...] + _dot(mix_l, wo_ref[0, ATTN_W + HY_W:, :])
    xn = x_ref[...].reshape(nb * L, D_MODEL) + mod_ref[0, 0][2:3] * out
    if final:
        ms = jnp.mean(xn * xn, axis=-1, keepdims=True)
        xn = xn * lax.rsqrt(ms + EPS) * fg_ref[...]
    x_out[...] = xn.reshape(nb, L, D_MODEL)


def _lru_out_branch(latent, layer, nb, hmod, w_in_b, conv_w, conv_b, wa, ba, wx, bx, lam, x, mix_a, mix_h, w_out_b,
                    mod, final_g, state=None, hl_prev=None):
    B, L, _ = hmod.shape
    assert B % nb == 0 and L % (2 * SUBLANES) == 0
    final = layer == DEPTH - 1
    cb = COL_BLOCK
    half_w = LRU_W // 2
    rows = nb * L
    per_layer = lambda shape: pl.BlockSpec((1,) + shape, lambda b: (layer,) + (0,) * len(shape))
    w_spec = lambda j: pl.BlockSpec((1, D_MODEL, cb), lambda b: (layer, 0, LRU_BLOCK0 + j))
    gate_blocks = (2, LRU_BLOCKS, LRU_BS, LRU_BS)
    in_specs = [pl.BlockSpec((nb, L, D_MODEL), lambda b: (b, 0, 0)), w_spec(0), w_spec(1), w_spec(2), w_spec(3),
                per_layer((4, LRU_W)), per_layer((1, LRU_W)), per_layer(gate_blocks), per_layer((2, LRU_W)),
                per_layer(gate_blocks), per_layer((2, LRU_W)), per_layer((2, LRU_W))]
    per_b = lambda w: pl.BlockSpec((nb, L, w), lambda b: (b, 0, 0))
    mod_row = (lambda b: (layer, 1 + b, 0, 0)) if latent else (lambda b: (layer, 0, 0, 0))
    in_specs += [per_b(D_MODEL), per_b(ATTN_W), per_b(HY_W), per_layer((D_MIX, D_MODEL)),
                 pl.BlockSpec((1, 1, 3, D_MODEL), mod_row), pl.BlockSpec((1, D_MODEL), lambda b: (0, 0))]
    args = [hmod, w_in_b, w_in_b, w_in_b, w_in_b, conv_w, conv_b, wa, ba, wx, bx, lam,
            x, mix_a, mix_h, w_out_b, mod, final_g]
    out_specs = [per_b(D_MODEL)]
    out_shape = [jax.ShapeDtypeStruct((B, L, D_MODEL), F32)]
    aliases = {}
    if latent:
        in_specs.append(pl.BlockSpec((nb, 1, 2, LRU_W), lambda b: (b, layer, 0, 0)))
        args.append(state)
    else:
        if hl_prev is not None:
            out_specs.append(pl.BlockSpec((nb, 1, 2, LRU_W), lambda b: (b, layer, 0, 0)))
        else:
            out_specs.append(pl.BlockSpec((nb, DEPTH, 2, LRU_W), lambda b: (b, 0, 0, 0)))
        out_shape.append(jax.ShapeDtypeStruct((B, DEPTH, 2, LRU_W), F32))
        if hl_prev is not None:
            in_specs.append(pl.BlockSpec(memory_space=pl.ANY))
            aliases = {len(args): 1}
            args.append(hl_prev)
    scan_buf = pltpu.VMEM((2, rows // SUBLANES, SUBLANES, LRU_W), F32)
    return pl.pallas_call(
        functools.partial(_lru_kernel, latent, hl_prev is not None, final, layer, L, nb),
        grid=(B // nb,),
        in_specs=in_specs,
        out_specs=out_specs,
        out_shape=out_shape,
        scratch_shapes=[pltpu.VMEM((2, half_w, 4 * half_w), BF16), scan_buf, scan_buf, scan_buf,
                        pltpu.VMEM((rows, LRU_W), F32), pltpu.VMEM((rows, D_MODEL), F32)],
        input_output_aliases=aliases,
        compiler_params=_cparams(1),
        name=f"lru_out_L{L}",
    )(*args)


def _mixer_pass(latent, x, p, filt, dft, rope=None, cache=None, state=None):
    B, L, _ = x.shape
    fwd, inv = dft
    nb = 1 if latent else 4
    nb_hyena = min(B, 2) if latent else 4
    nb_lru = 1 if latent else 2
    kv, hl = None, None
    for l in range(DEPTH):
        outs = _attention_branch(latent, l, nb, x, p['mod'], p['norm_g'], p['w_in'], p['gq'], p['gk'], p['e_avg'],
                                 rope, cache, kv)
        hmod, mix_a = outs[0], outs[1]
        if not latent:
            kv = (outs[2], outs[3])
        mix_h = _hyena_branch(l, nb_hyena, hmod, p['w_in'], p['short_w'], p['short_b'], filt, p['hy_bias'], fwd, inv)
        outs = _lru_out_branch(latent, l, nb_lru, hmod, p['w_in'], p['conv_w'], p['conv_b'], p['wa'], p['ba'], p['wx'],
                               p['bx'], p['lam'], x, mix_a, mix_h, p['w_out'], p['mod'], p['final_g'], state, hl)
        x = outs[0]
        if not latent:
            hl = outs[1]
    return x, kv, hl


def kernel(x_prompt, x_sample, cache_k, cache_v, state_lru, c, c_ctx, norm_g, w_ada, b_ada, w_in, q_norm_g, k_norm_g,
           hy_short_w, hy_short_b, hy_filt_w1, hy_filt_b1, hy_filt_w2, hy_filt_b2, hy_filt_w3, hy_filt_freq,
           hy_filt_decay, hy_bias, lru_conv_w, lru_conv_b, lru_wa, lru_ba, lru_wx, lru_bx, lru_lambda, w_out, final_g):
    batch, seq, _ = x_prompt.shape
    dec_batch, dec_seq, _ = x_sample.shape
    past = cache_k.shape[2]

    assert 1 + dec_batch <= MOD_ROWS
    cvecs_t = jnp.concatenate([c_ctx[:, None], c.T, jnp.zeros((D_MODEL, MOD_ROWS - 1 - dec_batch), F32)], axis=1)
    mod = _modulation(1 + dec_batch, cvecs_t, w_ada, b_ada).reshape(DEPTH, MOD_ROWS, 3, D_MODEL)

    w1p = jnp.pad(hy_filt_w1, ((0, 0), (0, FEAT_PAD - HY_EMB), (0, 0)))
    dft, filt = {}, {}
    for L in (seq, dec_seq):
        dft[L] = tuple(jnp.asarray(m).astype(BF16) for m in _dft_tables(L))
        filt[L] = _filter_spectrum(L, jnp.asarray(_hyena_feats(L)), dft[L][0], w1p, hy_filt_b1[:, None, :],
                                   hy_filt_w2, hy_filt_b2[:, None, :], hy_filt_w3, hy_filt_freq,
                                   hy_filt_decay[:, None, :])

    params = {
        'mod': mod,
        'norm_g': norm_g[:, None, :],
        'w_in': w_in.astype(BF16),
        'gq': jnp.tile(q_norm_g, (1, N_Q_HEADS))[:, None, :],
        'gk': jnp.tile(k_norm_g, (1, N_KV_HEADS))[:, None, :],
        'e_avg': jnp.asarray(_head_average_matrix()).astype(BF16),
        'short_w': hy_short_w, 'short_b': hy_short_b[:, None, :], 'hy_bias': hy_bias[:, None, :],
        'conv_w': lru_conv_w, 'conv_b': lru_conv_b[:, None, :],
        'wa': lru_wa, 'ba': lru_ba, 'wx': lru_wx, 'bx': lru_bx, 'lam': lru_lambda,
        'w_out': w_out.astype(BF16),
        'final_g': final_g[None, :],
    }

    y_prompt, kv, new_lru = _mixer_pass(False, x_prompt, params, filt[seq], dft[seq])
    new_k = kv[0].reshape(batch, DEPTH, seq, N_KV_HEADS, HEAD_DIM)
    new_v = kv[1].reshape(batch, DEPTH, seq, N_KV_HEADS, HEAD_DIM)

    rope = tuple(jnp.asarray(t) for t in _rope_tables(dec_seq))
    cache = (cache_k.reshape(dec_batch, DEPTH, past, KV_W), cache_v.reshape(dec_batch, DEPTH, past, KV_W))
    y_sample, _, _ = _mixer_pass(True, x_sample, params, filt[dec_seq], dft[dec_seq], rope, cache, state_lru)
    return (y_prompt, y_sample, new_k, new_v, new_lru)
```

```python
import functools
import math

import numpy as np
import jax
import jax.numpy as jnp
from jax import lax
from jax.experimental import pallas as pl
from jax.experimental.pallas import tpu as pltpu

F32 = jnp.float32
BF16 = jnp.bfloat16

D_MODEL = 1024
DEPTH = 2
GRID_W = 64
HEAD_DIM = 64
N_Q_HEADS = 8
N_KV_HEADS = 2
Q_PER_KV = N_Q_HEADS // N_KV_HEADS
ATTN_W = N_Q_HEADS * HEAD_DIM
KV_W = N_KV_HEADS * HEAD_DIM
ROPE_THETA = 10000.0
ATTN_SCALE = HEAD_DIM ** -0.5
HY_W = 512
HY_BANDS = 16
HY_EMB = 2 * HY_BANDS + 1
HY_FH = 64
LRU_W = 512
LRU_BLOCKS = 8
LRU_BS = LRU_W // LRU_BLOCKS
LRU_C = 8.0
EPS = 1e-6

ATTN_COLS = ATTN_W + 2 * KV_W + ATTN_W
D_MIX = ATTN_W + HY_W + LRU_W

LANES = 128
SUBLANES = 8
MXU_DIM = 256
COL_BLOCK = MXU_DIM
HY_BLOCK0 = ATTN_COLS // COL_BLOCK
LRU_BLOCK0 = (ATTN_COLS + 4 * HY_W) // COL_BLOCK
Q_CHUNK = 256
Q_CHUNK_LATENT = 128
HEAD_PAIRS = Q_PER_KV // 2
FEAT_PAD = 128
MOD_ROWS = 8
MOD_SLAB_ROWS = 256
MOD_COL_CHUNK = 1024
VMEM_LIMIT = 56 * 1024 * 1024


def _cparams(n_axes):
    return pltpu.CompilerParams(dimension_semantics=("arbitrary",) * n_axes, vmem_limit_bytes=VMEM_LIMIT)


def _split_bf16(a):
    hi = a.astype(BF16)
    lo = (a - hi.astype(F32)).astype(BF16)
    return hi, lo


def _dot(a, b):
    return jnp.dot(a, b, preferred_element_type=F32)


def _dot3(a, b):
    a_hi, a_lo = _split_bf16(a)
    b_hi, b_lo = _split_bf16(b)
    return _dot(a_hi, b_hi) + _dot(a_hi, b_lo) + _dot(a_lo, b_hi)


def _silu(x):
    half = 0.5 * x
    return half * jnp.tanh(half) + half


@functools.lru_cache(maxsize=None)
def _dft_tables(L):
    n = 2 * L
    t = np.arange(L)
    kt = (t[:, None] * t[None, :]) % n
    ang = 2.0 * np.pi * kt / n
    cosm, sinm = np.cos(ang), np.sin(ang)
    alt = (-1.0) ** t
    f_s = -sinm
    f_s[0, :] = alt
    fwd = np.concatenate([cosm, f_s], axis=0)
    g_c = 2.0 * cosm.T / n
    g_c[:, 0] = 1.0 / n
    g_s = -2.0 * sinm.T / n
    g_s[:, 0] = alt / n
    inv = np.concatenate([g_c, g_s], axis=1)
    return fwd.astype(np.float32), inv.astype(np.float32)


@functools.lru_cache(maxsize=None)
def _hyena_feats(L):
    t = np.linspace(0.0, 1.0, L)[:, None]
    w = 2.0 * math.pi * np.arange(L)[:, None] / L
    f = np.linspace(1e-4, HY_BANDS - 1, HY_BANDS)[None, :]
    out = np.zeros((L, FEAT_PAD), np.float32)
    out[:, :HY_EMB] = np.concatenate([t, np.cos(f * w), -np.sin(f * w)], axis=-1)
    return out


@functools.lru_cache(maxsize=None)
def _rope_tables(L):
    rows = L // GRID_W
    row = np.repeat(np.arange(rows), GRID_W).astype(np.float64)
    col = np.tile(np.arange(GRID_W), rows).astype(np.float64)
    n_freq = HEAD_DIM // 4
    inv = ROPE_THETA ** (-np.arange(n_freq) / n_freq)
    ang = np.concatenate([row[:, None] * inv, col[:, None] * inv], axis=-1)
    cos = np.repeat(np.cos(ang), 2, axis=-1)
    sin = np.repeat(np.sin(ang), 2, axis=-1) * np.tile(np.array([-1.0, 1.0]), HEAD_DIM // 2)
    reps = LANES // HEAD_DIM
    return np.tile(cos, (1, reps)).astype(np.float32), np.tile(sin, (1, reps)).astype(np.float32)


@functools.lru_cache(maxsize=None)
def _head_average_matrix():
    return np.kron(np.eye(MXU_DIM // HEAD_DIM), np.full((HEAD_DIM, HEAD_DIM), 1.0 / HEAD_DIM)).astype(np.float32)


def _mod_kernel(n_rows, ct_ref, *refs):
    w_refs, b_ref, o_ref = refs[:-2], refs[-2], refs[-1]
    s = _silu(ct_ref[...])
    tk = w_refs[0].shape[1]

    @pl.when(pl.program_id(1) == 0)
    def _start_from_bias():
        o_ref[0] = jnp.broadcast_to(b_ref[0], o_ref.shape[1:])

    for i, w_ref in enumerate(w_refs):
        cols = slice(i * MOD_COL_CHUNK, (i + 1) * MOD_COL_CHUNK)
        acc = [jnp.zeros((SUBLANES, MOD_COL_CHUNK), F32)] * n_rows
        for g0 in range(0, tk, SUBLANES):
            wg = w_ref[0, g0:g0 + SUBLANES, :]
            sg = s[g0:g0 + SUBLANES]
            acc = [acc[r] + sg[:, r:r + 1] * wg for r in range(n_rows)]
        for r in range(n_rows):
            o_ref[0, r:r + 1, cols] += jnp.sum(acc[r], axis=0, keepdims=True)


def _modulation(n_rows, cvecs_t, w_ada, b_ada):
    tk = MOD_SLAB_ROWS
    n_chunks = 3 * D_MODEL // MOD_COL_CHUNK
    w_specs = [pl.BlockSpec((1, tk, MOD_COL_CHUNK), functools.partial(lambda l, k, i: (l, k, i), i=i))
               for i in range(n_chunks)]
    return pl.pallas_call(
        functools.partial(_mod_kernel, n_rows),
        grid=(DEPTH, D_MODEL // tk),
        in_specs=[pl.BlockSpec((tk, MOD_ROWS), lambda l, k: (k, 0))] + w_specs
        + [pl.BlockSpec((1, 1, 3 * D_MODEL), lambda l, k: (l, 0, 0))],
        out_specs=pl.BlockSpec((1, MOD_ROWS, 3 * D_MODEL), lambda l, k: (l, 0, 0)),
        out_shape=jax.ShapeDtypeStruct((DEPTH, MOD_ROWS, 3 * D_MODEL), F32),
        compiler_params=_cparams(2),
        name="adaln_mod",
    )(cvecs_t, *([w_ada] * n_chunks), b_ada.reshape(DEPTH, 1, 3 * D_MODEL))


def _filter_kernel(L, feats_ref, w1_ref, b1_ref, w2_ref, b2_ref, w3_ref, freq_ref, decay_ref,
                   fwd_ref, a_ref, b_ref, d_ref):
    feats = feats_ref[...]
    t = feats[:, 0:1]
    freq = freq_ref[0]
    hdn = jnp.sin(freq[0:1] * (_dot3(feats, w1_ref[0]) + b1_ref[0]))
    hdn = jnp.sin(freq[1:2] * (_dot3(hdn, w2_ref[0]) + b2_ref[0]))
    h = _dot3(hdn, w3_ref[0]) * jnp.exp(-t * decay_ref[0])
    row = lax.broadcasted_iota(jnp.int32, (L, HY_W), 0)
    h_fwd = h[:, :HY_W]
    h_bwd = jnp.where(row >= 1, h[:, HY_W:], 0.0)
    inv_norm = 1.0 / jnp.sum(jnp.abs(h_fwd) + jnp.abs(h_bwd), axis=0, keepdims=True)
    even = h_fwd + h_bwd
    odd = h_fwd - h_bwd
    h_re = _dot(fwd_ref[0:L, :], even.astype(BF16)) * inv_norm
    h_im = _dot(fwd_ref[L:2 * L, :], odd.astype(BF16)) * inv_norm
    alt = jnp.where(row % 2 == 0, 1.0, -1.0)
    nyq = jnp.sum(even * alt, axis=0, keepdims=True) * inv_norm
    a_ref[0] = h_re
    b_ref[0] = jnp.where(row >= 1, h_im, 0.0)
    d_ref[0] = jnp.where(row >= 1, h_re, nyq)


def _filter_spectrum(L, feats, fwd, w1p, b1, w2, b2, w3, freq, decay):
    full = lambda shape: pl.BlockSpec(shape, lambda l: (0,) * len(shape))
    per_layer = lambda shape: pl.BlockSpec((1,) + shape, lambda l: (l,) + (0,) * len(shape))
    out = jax.ShapeDtypeStruct((DEPTH, L, HY_W), F32)
    return pl.pallas_call(
        functools.partial(_filter_kernel, L),
        grid=(DEPTH,),
        in_specs=[
            full((L, FEAT_PAD)),
            per_layer((FEAT_PAD, HY_FH)), per_layer((1, HY_FH)),
            per_layer((HY_FH, HY_FH)), per_layer((1, HY_FH)),
            per_layer((HY_FH, 2 * HY_W)), per_layer((2, HY_FH)), per_layer((1, 2 * HY_W)),
            full((2 * L, L)),
        ],
        out_specs=[per_layer((L, HY_W))] * 3,
        out_shape=[out, out, out],
        compiler_params=_cparams(1),
        name=f"hyena_filter_L{L}",
    )(feats, w1p, b1, w2, b2, w3, freq, decay, fwd)


def _rope(x, cos, sin):
    lane = lax.broadcasted_iota(jnp.int32, x.shape, 1)
    partner = jnp.where(lane % 2 == 0, pltpu.roll(x, LANES - 1, 1), pltpu.roll(x, 1, 1))
    return x * cos + partner * sin


def _head_mean_square(x, e):
    sq = (x * x).astype(BF16)
    width = x.shape[1]
    if width <= MXU_DIM:
        return _dot(sq, e[:width, :width])
    parts = [_dot(sq[:, i:i + MXU_DIM], e) for i in range(0, width, MXU_DIM)]
    return jnp.concatenate(parts, axis=1)


def _store_kv_heads(k_dst, v_dst, k, v):
    ones = jnp.ones((v.shape[0], HEAD_DIM), BF16)
    for j in range(N_KV_HEADS):
        js = slice(j * HEAD_DIM, (j + 1) * HEAD_DIM)
        vj = v[:, js].astype(BF16)
        k_dst[j] = k[:, js].astype(BF16)
        v_dst[j, 0, :, 0:HEAD_DIM] = vj
        v_dst[j, 0, :, HEAD_DIM:2 * HEAD_DIM] = ones
        v_dst[j, 1, :, 0:HEAD_DIM] = ones
        v_dst[j, 1, :, HEAD_DIM:2 * HEAD_DIM] = vj


def _attn_kernel(latent, aliased, layer, L, nb, qc, past, *refs):
    x_ref, mod_ref, ng_ref, w_ref, gq_ref, gk_ref, e_ref = refs[:7]
    refs = refs[7:]
    if latent:
        (cos_ref, sin_ref, ck_ref, cv_ref, h_out, mix_out,
         q_s, k_s, v_s, g_s, s_s, mx_s, acc_s, ck_s, cv_s) = refs
    else:
        if aliased:
            refs = refs[2:]
        h_out, mix_out, k_out, v_out, q_s, k_s, v_s, g_s, s_s, mx_s, acc_s = refs
    rows = nb * L

    x = x_ref[...].reshape(rows, D_MODEL)
    m = mod_ref[0, 0]
    shift, scale = m[0:1], m[1:2]
    ms = jnp.mean(x * x, axis=-1, keepdims=True)
    h = (x * lax.rsqrt(ms + EPS)) * (ng_ref[0] * (1.0 + scale)) + shift
    hb = h.astype(BF16)
    h_out[...] = hb.reshape(nb, L, D_MODEL)

    u = _dot(hb, w_ref[0])
    q = u[:, :ATTN_W]
    k = u[:, ATTN_W:ATTN_W + KV_W]
    v = u[:, ATTN_W + KV_W:ATTN_W + 2 * KV_W]
    g_s[...] = u[:, ATTN_W + 2 * KV_W:]
    e = e_ref[...]
    qn = q * lax.rsqrt(_head_mean_square(q, e) + EPS) * gq_ref[0]
    kn = k * lax.rsqrt(_head_mean_square(k, e) + EPS) * gk_ref[0]
    if latent:
        cos, sin = cos_ref[...], sin_ref[...]
        qn = jnp.concatenate([_rope(qn[:, i:i + LANES], cos, sin) for i in range(0, ATTN_W, LANES)], axis=1)
        kn = _rope(kn, cos, sin)
        _store_kv_heads(ck_s, cv_s, ck_ref[0, 0], cv_ref[0, 0])
    else:
        for b in range(nb):
            kb, vb = kn[b * L:(b + 1) * L], v[b * L:(b + 1) * L]
            if aliased:
                k_out[b, 0] = kb
                v_out[b, 0] = vb
            else:
                for slot in range(DEPTH):
                    k_out[b, slot] = kb if slot == layer else jnp.zeros_like(kb)
                    v_out[b, slot] = vb if slot == layer else jnp.zeros_like(vb)
    qb = (qn * ATTN_SCALE).astype(BF16)
    for hd in range(N_Q_HEADS):
        j, g = divmod(hd, Q_PER_KV)
        q_s[j * Q_PER_KV + (g % 2) * HEAD_PAIRS + g // 2] = qb[:, hd * HEAD_DIM:(hd + 1) * HEAD_DIM]
    _store_kv_heads(k_s, v_s, kn, v)

    contract_last = (((1,), (1,)), ((), ()))
    chunks_per_seq = L // qc
    lane = lax.broadcasted_iota(jnp.int32, (qc, LANES), 1)
    groups = [(j, e) for j in range(N_KV_HEADS) for e in range(2)]

    def chunk(c, carry):
        r0 = pl.multiple_of(c * qc, qc)
        k0 = pl.multiple_of((c // chunks_per_seq) * L, L)
        for idx, (j, e) in enumerate(groups):
            slot = j * Q_PER_KV + e * HEAD_PAIRS
            qh = q_s[slot:slot + HEAD_PAIRS, pl.ds(r0, qc), :].reshape(HEAD_PAIRS * qc, HEAD_DIM)
            s = lax.dot_general(qh, k_s[j, pl.ds(k0, L), :], contract_last, preferred_element_type=F32)
            mx = jnp.max(s, axis=-1, keepdims=True)
            if latent:
                s0 = lax.dot_general(qh, ck_s[j], contract_last, preferred_element_type=F32)
                mx = jnp.maximum(mx, jnp.max(s0, axis=-1, keepdims=True))
                s_s[idx, :, 0:past] = s0
                s_s[idx, :, past:] = s
            else:
                s_s[idx] = s
            mx_s[idx] = jnp.broadcast_to(mx, (HEAD_PAIRS * qc, LANES))
        for idx, (j, e) in enumerate(groups):
            p = jnp.exp(s_s[idx] - jnp.tile(mx_s[idx], (1, (past + L) // LANES))).astype(BF16)
            if latent:
                acc = _dot(p[:, :past], cv_s[j, e]) + _dot(p[:, past:], v_s[j, e, pl.ds(k0, L), :])
            else:
                acc = _dot(p, v_s[j, e, pl.ds(k0, L), :])
            acc_s[idx] = acc
        b_idx = c // chunks_per_seq
        row0 = pl.multiple_of((c % chunks_per_seq) * qc, qc)
        for j in range(N_KV_HEADS):
            for t in range(HEAD_PAIRS):
                a_even = acc_s[2 * j, t * qc:(t + 1) * qc, :]
                a_odd = acc_s[2 * j + 1, t * qc:(t + 1) * qc, :]
                out = jnp.where(lane < HEAD_DIM, a_even, a_odd)
                den = pltpu.roll(jnp.where(lane < HEAD_DIM, a_odd, a_even), HEAD_DIM, 1)
                blk = slice((j * HEAD_PAIRS + t) * LANES, (j * HEAD_PAIRS + t + 1) * LANES)
                gate = _silu(g_s[pl.ds(r0, qc), blk])
                mix_out[b_idx, pl.ds(row0, qc), blk] = (out / den * gate).astype(BF16)
        return carry

    lax.fori_loop(0, nb * chunks_per_seq, chunk, 0)


def _attention_branch(latent, layer, nb, x, mod, norm_g, w_in_b, gq, gk, e_avg, rope=None, cache=None, kv_prev=None):
    B, L, _ = x.shape
    rows = nb * L
    assert nb == 1 or not latent
    full = lambda shape: pl.BlockSpec(shape, lambda b: (0,) * len(shape))
    per_b = lambda shape: pl.BlockSpec((nb,) + shape, lambda b: (b,) + (0,) * len(shape))
    per_layer = lambda shape: pl.BlockSpec((1,) + shape, lambda b: (layer,) + (0,) * len(shape))
    mod_spec = pl.BlockSpec((1, 1, 3, D_MODEL), (lambda b: (layer, 1 + b, 0, 0)) if latent else (lambda b: (layer, 0, 0, 0)))
    in_specs = [per_b((L, D_MODEL)), mod_spec, per_layer((1, D_MODEL)), per_layer((D_MODEL, ATTN_COLS)),
                per_layer((1, ATTN_W)), per_layer((1, KV_W)), full((MXU_DIM, MXU_DIM))]
    args = [x, mod, norm_g, w_in_b, gq, gk, e_avg]
    out_specs = [per_b((L, D_MODEL)), per_b((L, ATTN_W))]
    out_shape = [jax.ShapeDtypeStruct((B, L, D_MODEL), BF16), jax.ShapeDtypeStruct((B, L, ATTN_W), BF16)]
    qc = Q_CHUNK_LATENT if latent else Q_CHUNK
    past = cache[0].shape[2] if latent else 0
    assert B % nb == 0 and L % qc == 0 and L % LANES == 0 and past % LANES == 0
    n_groups = 2 * N_KV_HEADS
    scratch = [pltpu.VMEM((N_Q_HEADS, rows, HEAD_DIM), BF16), pltpu.VMEM((N_KV_HEADS, rows, HEAD_DIM), BF16),
               pltpu.VMEM((N_KV_HEADS, 2, rows, 2 * HEAD_DIM), BF16),
               pltpu.VMEM((rows, ATTN_W), F32),
               pltpu.VMEM((n_groups, HEAD_PAIRS * qc, past + L), F32),
               pltpu.VMEM((n_groups, HEAD_PAIRS * qc, LANES), F32),
               pltpu.VMEM((n_groups, HEAD_PAIRS * qc, 2 * HEAD_DIM), F32)]
    aliases = {}
    if latent:
        cache_k, cache_v = cache
        cache_spec = pl.BlockSpec((1, 1, past, KV_W), lambda b: (b, layer, 0, 0))
        in_specs += [full((L, LANES)), full((L, LANES)), cache_spec, cache_spec]
        args += [rope[0], rope[1], cache_k, cache_v]
        scratch += [pltpu.VMEM((N_KV_HEADS, past, HEAD_DIM), BF16),
                    pltpu.VMEM((N_KV_HEADS, 2, past, 2 * HEAD_DIM), BF16)]
    else:
        if kv_prev is not None:
            kv_spec = pl.BlockSpec((nb, 1, L, KV_W), lambda b: (b, layer, 0, 0))
        else:
            kv_spec = pl.BlockSpec((nb, DEPTH, L, KV_W), lambda b: (b, 0, 0, 0))
        kv_shape = jax.ShapeDtypeStruct((B, DEPTH, L, KV_W), F32)
        out_specs += [kv_spec, kv_spec]
        out_shape += [kv_shape, kv_shape]
        if kv_prev is not None:
            in_specs += [pl.BlockSpec(memory_space=pl.ANY)] * 2
            aliases = {len(args): 2, len(args) + 1: 3}
            args += list(kv_prev)
    return pl.pallas_call(
        functools.partial(_attn_kernel, latent, kv_prev is not None, layer, L, nb, qc, past),
        grid=(B // nb,),
        in_specs=in_specs,
        out_specs=out_specs,
        out_shape=out_shape,
        scratch_shapes=scratch,
        input_output_aliases=aliases,
        compiler_params=_cparams(1),
        name=f"attn_branch_L{L}",
    )(*args)


def _shifted(x, offset, period):
    rows = x.shape[0]
    t = lax.broadcasted_iota(jnp.int32, x.shape, 0)
    if rows != period:
        t = t % period
    rolled = pltpu.roll(x, (-offset) % rows, 0)
    valid = (t >= -offset) if offset < 0 else (t < period - offset)
    return jnp.where(valid, rolled, 0.0)


def _hyena_kernel(L, nb, h_ref, w0_ref, w1_ref, w2_ref, wg_ref, sw0_ref, sw1_ref, sw2_ref, sb0_ref, sb1_ref, sb2_ref,
                  fa_ref, fb_ref, fd_ref, bias_ref, fwd_ref, inv_ref, mix_out):
    hb = h_ref[...].reshape(nb * L, D_MODEL)

    def short_conv(w_ref, sw_ref, sb_ref):
        xs = _dot(hb, w_ref[0])
        w = sw_ref[0]
        return sb_ref[0] + _shifted(xs, -1, L) * w[0:1] + xs * w[1:2] + _shifted(xs, 1, L) * w[2:3]

    x0 = short_conv(w0_ref, sw0_ref, sb0_ref)
    x1 = short_conv(w1_ref, sw1_ref, sb1_ref)
    hv = short_conv(w2_ref, sw2_ref, sb2_ref)
    gated = x0 * _silu(_dot(hb, wg_ref[0]))
    z = x1 * hv
    zb = z.astype(BF16)
    fa, fb, fd = fa_ref[0], fb_ref[0], fd_ref[0]
    cb = z.shape[1]
    z_cat = jnp.concatenate([zb[b * L:(b + 1) * L] for b in range(nb)], axis=1)
    zf = _dot(fwd_ref[...], z_cat)
    y_re, y_im = [], []
    for b in range(nb):
        re, im = zf[:L, b * cb:(b + 1) * cb], zf[L:, b * cb:(b + 1) * cb]
        y_re.append((re * fa - im * fb).astype(BF16))
        y_im.append((re * fb + im * fd).astype(BF16))
    y_all = (_dot(inv_ref[:, :L], jnp.concatenate(y_re, axis=1))
             + _dot(inv_ref[:, L:], jnp.concatenate(y_im, axis=1)))
    for b in range(nb):
        rs = slice(b * L, (b + 1) * L)
        y = y_all[:, b * cb:(b + 1) * cb] + z[rs] * bias_ref[0]
        mix_out[b] = (y * gated[rs]).astype(BF16)


def _hyena_branch(layer, nb, hmod, w_in_b, short_w, short_b, filt, bias, fwd, inv):
    B, L, _ = hmod.shape
    cb = COL_BLOCK
    nt = HY_W // cb
    fa, fb, fd = filt
    assert B % nb == 0 and L % LANES == 0
    w_spec = lambda part: pl.BlockSpec((1, D_MODEL, cb), lambda i, b: (layer, 0, HY_BLOCK0 + part * nt + i))
    sw_spec = lambda part: pl.BlockSpec((1, 3, cb), lambda i, b: (layer, 0, part * nt + i))
    sb_spec = lambda part: pl.BlockSpec((1, 1, cb), lambda i, b: (layer, 0, part * nt + i))
    filt_spec = pl.BlockSpec((1, L, cb), lambda i, b: (layer, 0, i))
    return pl.pallas_call(
        functools.partial(_hyena_kernel, L, nb),
        grid=(nt, B // nb),
        in_specs=[
            pl.BlockSpec((nb, L, D_MODEL), lambda i, b: (b, 0, 0)),
            w_spec(0), w_spec(1), w_spec(2), w_spec(3),
            sw_spec(0), sw_spec(1), sw_spec(2),
            sb_spec(0), sb_spec(1), sb_spec(2),
            filt_spec, filt_spec, filt_spec,
            pl.BlockSpec((1, 1, cb), lambda i, b: (layer, 0, i)),
            pl.BlockSpec((2 * L, L), lambda i, b: (0, 0)),
            pl.BlockSpec((L, 2 * L), lambda i, b: (0, 0)),
        ],
        out_specs=pl.BlockSpec((nb, L, cb), lambda i, b: (b, 0, i)),
        out_shape=jax.ShapeDtypeStruct((B, L, HY_W), BF16),
        compiler_params=_cparams(2),
        name=f"hyena_branch_L{L}",
    )(hmod, w_in_b, w_in_b, w_in_b, w_in_b, short_w, short_w, short_w, short_b, short_b, short_b,
      fa, fb, fd, bias, fwd, inv)


def _lru_kernel(latent, aliased, final, layer, L, nb, *refs):
    (h_ref, wx0_ref, wx1_ref, wg0_ref, wg1_ref, cw_ref, cb_ref, wa_ref, ba_ref, wxg_ref, bx_ref, lam_ref,
     x_ref, ma_ref, mh_ref, wo_ref, mod_ref, fg_ref) = refs[:18]
    refs = refs[18:]
    if latent:
        h0_ref, x_out, gate_w_s, a_s, b_s, y_s, g_s, part_s = refs
    else:
        if aliased:
            refs = refs[1:]
        x_out, hl_out, gate_w_s, a_s, b_s, y_s, g_s, part_s = refs
    half_w = LRU_W // 2
    blocks_per_half = half_w // LRU_BS
    n_tiles = L // SUBLANES

    @pl.when(pl.program_id(0) == 0)
    def _build_gate_weights():
        gate_w_s[...] = jnp.zeros(gate_w_s.shape, BF16)
        for half in range(2):
            for kind, (w_ref, d) in enumerate(((wa_ref, 0), (wxg_ref, 0), (wa_ref, 1), (wxg_ref, 1))):
                for j in range(blocks_per_half):
                    blk = (0.5 * w_ref[0, d, half * blocks_per_half + j]).astype(BF16)
                    gate_w_s[half, j * LRU_BS:(j + 1) * LRU_BS,
                             kind * half_w + j * LRU_BS:kind * half_w + (j + 1) * LRU_BS] = blk

    part_s[...] = (_dot(ma_ref[...].reshape(nb * L, ATTN_W), wo_ref[0, 0:ATTN_W, :])
                   + _dot(mh_ref[...].reshape(nb * L, HY_W), wo_ref[0, ATTN_W:ATTN_W + HY_W, :]))

    hb = h_ref[...].reshape(nb * L, D_MODEL)
    xs = jnp.concatenate([_dot(hb, wx0_ref[0]), _dot(hb, wx1_ref[0])], axis=1)
    g_s[:, :half_w] = _dot(hb, wg0_ref[0])
    g_s[:, half_w:] = _dot(hb, wg1_ref[0])
    cw = cw_ref[0]
    xc = (cb_ref[0] + _shifted(xs, -1, L) * cw[0:1] + xs * cw[1:2] + _shifted(xs, 1, L) * cw[2:3]
          + _shifted(xs, 2, L) * cw[3:4])
    xcb = xc.astype(BF16)
    c_half = (-0.5 * LRU_C) * jax.nn.softplus(-lam_ref[0])
    ba, bx = 0.5 * ba_ref[0], 0.5 * bx_ref[0]
    for half in range(2):
        cols = slice(half * half_w, (half + 1) * half_w)
        bias = jnp.concatenate([ba[0:1, cols], bx[0:1, cols], ba[1:2, cols], bx[1:2, cols]], axis=1)
        gates = _dot(xcb[:, cols], gate_w_s[half]) + bias
        xh_half = 0.5 * xc[:, cols]
        for d in range(2):
            t_r = jnp.tanh(gates[:, (2 * d) * half_w:(2 * d + 1) * half_w])
            t_i = jnp.tanh(gates[:, (2 * d + 1) * half_w:(2 * d + 2) * half_w])
            ch = c_half[d:d + 1, cols]
            log_a = ch * t_r + ch
            a = jnp.exp(log_a)
            y = jnp.tanh(log_a) * (-1.0 - a * a)
            root = jnp.where(y > 0.0, y * lax.rsqrt(y), 0.0)
            bcoef = root * (xh_half * t_i + xh_half)
            a_s[d, :, :, cols] = a.reshape(nb * n_tiles, SUBLANES, half_w)
            b_s[d, :, :, cols] = bcoef.reshape(nb * n_tiles, SUBLANES, half_w)

    if latent:
        init = []
        for b in range(nb):
            h0 = h0_ref[b, 0]
            init += [h0[0:1], h0[1:2]]
    else:
        init = [jnp.zeros((1, LRU_W), F32)] * (2 * nb)

    def advance(d, tile, rows_g, h):
        a = [a_s[d, tile, r:r + 1, :] for r in rows_g]
        c = [b_s[d, tile, r:r + 1, :] for r in rows_g]
        a01, c01 = a[1] * a[0], a[1] * c[0] + c[1]
        hs = [a[0] * h + c[0], a01 * h + c01]
        if len(rows_g) == 4:
            a23, c23 = a[3] * a[2], a[3] * c[2] + c[3]
            hs += [a[2] * hs[1] + c[2], (a23 * a01) * h + (a23 * c01 + c23)]
        for r, v in zip(rows_g, hs):
            y_s[d, tile, r:r + 1, :] = v
        return hs[-1]

    group = 4 if nb == 1 else 2

    def tile_step(i, carry):
        carry = list(carry)
        for g0 in range(0, SUBLANES, group):
            fwd_rows = list(range(g0, g0 + group))
            bwd_rows = [SUBLANES - 1 - r for r in fwd_rows]
            for b in range(nb):
                carry[2 * b] = advance(0, b * n_tiles + i, fwd_rows, carry[2 * b])
                carry[2 * b + 1] = advance(1, b * n_tiles + (n_tiles - 1 - i), bwd_rows, carry[2 * b + 1])
        return tuple(carry)

    def two_tiles(i, carry):
        return tile_step(2 * i + 1, tile_step(2 * i, carry))

    last = lax.fori_loop(0, n_tiles // 2, two_tiles, tuple(init))
    if not latent:
        if not aliased:
            hl_out[...] = jnp.zeros(hl_out.shape, F32)
        slot = 0 if aliased else layer
        for b in range(nb):
            hl_out[b, slot, 0:1, :] = last[2 * b]
            hl_out[b, slot, 1:2, :] = last[2 * b + 1]
    y = (y_s[0] + y_s[1]).reshape(nb * L, LRU_W)
    mix_l = (y * _silu(g_s[...])).astype(BF16)
    out = part_s[...] + _dot(mix_l, wo_ref[0, ATTN_W + HY_W:, :])
    xn = x_ref[...].reshape(nb * L, D_MODEL) + mod_ref[0, 0][2:3] * out
    if final:
        ms = jnp.mean(xn * xn, axis=-1, keepdims=True)
        xn = xn * lax.rsqrt(ms + EPS) * fg_ref[...]
    x_out[...] = xn.reshape(nb, L, D_MODEL)


def _lru_out_branch(latent, layer, nb, hmod, w_in_b, conv_w, conv_b, wa, ba, wx, bx, lam, x, mix_a, mix_h, w_out_b,
                    mod, final_g, state=None, hl_prev=None):
    B, L, _ = hmod.shape
    assert B % nb == 0 and L % (2 * SUBLANES) == 0
    final = layer == DEPTH - 1
    cb = COL_BLOCK
    half_w = LRU_W // 2
    rows = nb * L
    per_layer = lambda shape: pl.BlockSpec((1,) + shape, lambda b: (layer,) + (0,) * len(shape))
    w_spec = lambda j: pl.BlockSpec((1, D_MODEL, cb), lambda b: (layer, 0, LRU_BLOCK0 + j))
    gate_blocks = (2, LRU_BLOCKS, LRU_BS, LRU_BS)
    in_specs = [pl.BlockSpec((nb, L, D_MODEL), lambda b: (b, 0, 0)), w_spec(0), w_spec(1), w_spec(2), w_spec(3),
                per_layer((4, LRU_W)), per_layer((1, LRU_W)), per_layer(gate_blocks), per_layer((2, LRU_W)),
                per_layer(gate_blocks), per_layer((2, LRU_W)), per_layer((2, LRU_W))]
    per_b = lambda w: pl.BlockSpec((nb, L, w), lambda b: (b, 0, 0))
    mod_row = (lambda b: (layer, 1 + b, 0, 0)) if latent else (lambda b: (layer, 0, 0, 0))
    in_specs += [per_b(D_MODEL), per_b(ATTN_W), per_b(HY_W), per_layer((D_MIX, D_MODEL)),
                 pl.BlockSpec((1, 1, 3, D_MODEL), mod_row), pl.BlockSpec((1, D_MODEL), lambda b: (0, 0))]
    args = [hmod, w_in_b, w_in_b, w_in_b, w_in_b, conv_w, conv_b, wa, ba, wx, bx, lam,
            x, mix_a, mix_h, w_out_b, mod, final_g]
    out_specs = [per_b(D_MODEL)]
    out_shape = [jax.ShapeDtypeStruct((B, L, D_MODEL), F32)]
    aliases = {}
    if latent:
        in_specs.append(pl.BlockSpec((nb, 1, 2, LRU_W), lambda b: (b, layer, 0, 0)))
        args.append(state)
    else:
        if hl_prev is not None:
            out_specs.append(pl.BlockSpec((nb, 1, 2, LRU_W), lambda b: (b, layer, 0, 0)))
        else:
            out_specs.append(pl.BlockSpec((nb, DEPTH, 2, LRU_W), lambda b: (b, 0, 0, 0)))
        out_shape.append(jax.ShapeDtypeStruct((B, DEPTH, 2, LRU_W), F32))
        if hl_prev is not None:
            in_specs.append(pl.BlockSpec(memory_space=pl.ANY))
            aliases = {len(args): 1}
            args.append(hl_prev)
    scan_buf = pltpu.VMEM((2, rows // SUBLANES, SUBLANES, LRU_W), F32)
    return pl.pallas_call(
        functools.partial(_lru_kernel, latent, hl_prev is not None, final, layer, L, nb),
        grid=(B // nb,),
        in_specs=in_specs,
        out_specs=out_specs,
        out_shape=out_shape,
        scratch_shapes=[pltpu.VMEM((2, half_w, 4 * half_w), BF16), scan_buf, scan_buf, scan_buf,
                        pltpu.VMEM((rows, LRU_W), F32), pltpu.VMEM((rows, D_MODEL), F32)],
        input_output_aliases=aliases,
        compiler_params=_cparams(1),
        name=f"lru_out_L{L}",
    )(*args)


def _mixer_pass(latent, x, p, filt, dft, rope=None, cache=None, state=None):
    B, L, _ = x.shape
    fwd, inv = dft
    nb = 1 if latent else 4
    nb_hyena = min(B, 2) if latent else 4
    nb_lru = 1 if latent else 2
    kv, hl = None, None
    for l in range(DEPTH):
        outs = _attention_branch(latent, l, nb, x, p['mod'], p['norm_g'], p['w_in'], p['gq'], p['gk'], p['e_avg'],
                                 rope, cache, kv)
        hmod, mix_a = outs[0], outs[1]
        if not latent:
            kv = (outs[2], outs[3])
        mix_h = _hyena_branch(l, nb_hyena, hmod, p['w_in'], p['short_w'], p['short_b'], filt, p['hy_bias'], fwd, inv)
        outs = _lru_out_branch(latent, l, nb_lru, hmod, p['w_in'], p['conv_w'], p['conv_b'], p['wa'], p['ba'], p['wx'],
                               p['bx'], p['lam'], x, mix_a, mix_h, p['w_out'], p['mod'], p['final_g'], state, hl)
        x = outs[0]
        if not latent:
            hl = outs[1]
    return x, kv, hl


def kernel(x_prompt, x_sample, cache_k, cache_v, state_lru, c, c_ctx, norm_g, w_ada, b_ada, w_in, q_norm_g, k_norm_g,
           hy_short_w, hy_short_b, hy_filt_w1, hy_filt_b1, hy_filt_w2, hy_filt_b2, hy_filt_w3, hy_filt_freq,
           hy_filt_decay, hy_bias, lru_conv_w, lru_conv_b, lru_wa, lru_ba, lru_wx, lru_bx, lru_lambda, w_out, final_g):
    batch, seq, _ = x_prompt.shape
    dec_batch, dec_seq, _ = x_sample.shape
    past = cache_k.shape[2]

    assert 1 + dec_batch <= MOD_ROWS
    cvecs_t = jnp.concatenate([c_ctx[:, None], c.T, jnp.zeros((D_MODEL, MOD_ROWS - 1 - dec_batch), F32)], axis=1)
    mod = _modulation(1 + dec_batch, cvecs_t, w_ada, b_ada).reshape(DEPTH, MOD_ROWS, 3, D_MODEL)

    w1p = jnp.pad(hy_filt_w1, ((0, 0), (0, FEAT_PAD - HY_EMB), (0, 0)))
    dft, filt = {}, {}
    for L in (seq, dec_seq):
        dft[L] = tuple(jnp.asarray(m).astype(BF16) for m in _dft_tables(L))
        filt[L] = _filter_spectrum(L, jnp.asarray(_hyena_feats(L)), dft[L][0], w1p, hy_filt_b1[:, None, :],
                                   hy_filt_w2, hy_filt_b2[:, None, :], hy_filt_w3, hy_filt_freq,
                                   hy_filt_decay[:, None, :])

    params = {
        'mod': mod,
        'norm_g': norm_g[:, None, :],
        'w_in': w_in.astype(BF16),
        'gq': jnp.tile(q_norm_g, (1, N_Q_HEADS))[:, None, :],
        'gk': jnp.tile(k_norm_g, (1, N_KV_HEADS))[:, None, :],
        'e_avg': jnp.asarray(_head_average_matrix()).astype(BF16),
        'short_w': hy_short_w, 'short_b': hy_short_b[:, None, :], 'hy_bias': hy_bias[:, None, :],
        'conv_w': lru_conv_w, 'conv_b': lru_conv_b[:, None, :],
        'wa': lru_wa, 'ba': lru_ba, 'wx': lru_wx, 'bx': lru_bx, 'lam': lru_lambda,
        'w_out': w_out.astype(BF16),
        'final_g': final_g[None, :],
    }

    y_prompt, kv, new_lru = _mixer_pass(False, x_prompt, params, filt[seq], dft[seq])
    new_k = kv[0].reshape(batch, DEPTH, seq, N_KV_HEADS, HEAD_DIM)
    new_v = kv[1].reshape(batch, DEPTH, seq, N_KV_HEADS, HEAD_DIM)

    rope = tuple(jnp.asarray(t) for t in _rope_tables(dec_seq))
    cache = (cache_k.reshape(dec_batch, DEPTH, past, KV_W), cache_v.reshape(dec_batch, DEPTH, past, KV_W))
    y_sample, _, _ = _mixer_pass(True, x_sample, params, filt[dec_seq], dft[dec_seq], rope, cache, state_lru)
    return (y_prompt, y_sample, new_k, new_v, new_lru)
```

```python
import functools
import math

import numpy as np
import jax
import jax.numpy as jnp
from jax import lax
from jax.experimental import pallas as pl
from jax.experimental.pallas import tpu as pltpu

F32 = jnp.float32
BF16 = jnp.bfloat16

D_MODEL = 1024
DEPTH = 2
GRID_W = 64
HEAD_DIM = 64
N_Q_HEADS = 8
N_KV_HEADS = 2
Q_PER_KV = N_Q_HEADS // N_KV_HEADS
ATTN_W = N_Q_HEADS * HEAD_DIM
KV_W = N_KV_HEADS * HEAD_DIM
ROPE_THETA = 10000.0
ATTN_SCALE = HEAD_DIM ** -0.5
HY_W = 512
HY_BANDS = 16
HY_EMB = 2 * HY_BANDS + 1
HY_FH = 64
LRU_W = 512
LRU_BLOCKS = 8
LRU_BS = LRU_W // LRU_BLOCKS
LRU_C = 8.0
EPS = 1e-6

ATTN_COLS = ATTN_W + 2 * KV_W + ATTN_W
D_MIX = ATTN_W + HY_W + LRU_W

LANES = 128
SUBLANES = 8
MXU_DIM = 256
COL_BLOCK = MXU_DIM
HY_BLOCK0 = ATTN_COLS // COL_BLOCK
LRU_BLOCK0 = (ATTN_COLS + 4 * HY_W) // COL_BLOCK
Q_CHUNK = 256
Q_CHUNK_LATENT = 256
HEAD_PAIRS = Q_PER_KV // 2
FEAT_PAD = 128
MOD_ROWS = 8
VMEM_LIMIT = 56 * 1024 * 1024


def _cparams(n_axes):
    return pltpu.CompilerParams(dimension_semantics=("arbitrary",) * n_axes, vmem_limit_bytes=VMEM_LIMIT)


def _split_bf16(a):
    hi = a.astype(BF16)
    lo = (a - hi.astype(F32)).astype(BF16)
    return hi, lo


def _dot(a, b):
    return jnp.dot(a, b, preferred_element_type=F32)


def _dot3(a, b):
    a_hi, a_lo = _split_bf16(a)
    b_hi, b_lo = _split_bf16(b)
    return _dot(a_hi, b_hi) + _dot(a_hi, b_lo) + _dot(a_lo, b_hi)


def _silu(x):
    half = 0.5 * x
    return half * jnp.tanh(half) + half


@functools.lru_cache(maxsize=None)
def _dft_tables(L):
    n = 2 * L
    t = np.arange(L)
    kt = (t[:, None] * t[None, :]) % n
    ang = 2.0 * np.pi * kt / n
    cosm, sinm = np.cos(ang), np.sin(ang)
    alt = (-1.0) ** t
    f_s = -sinm
    f_s[0, :] = alt
    fwd = np.concatenate([cosm, f_s], axis=0)
    g_c = 2.0 * cosm.T / n
    g_c[:, 0] = 1.0 / n
    g_s = -2.0 * sinm.T / n
    g_s[:, 0] = alt / n
    inv = np.concatenate([g_c, g_s], axis=1)
    return fwd.astype(np.float32), inv.astype(np.float32)


@functools.lru_cache(maxsize=None)
def _hyena_feats(L):
    t = np.linspace(0.0, 1.0, L)[:, None]
    w = 2.0 * math.pi * np.arange(L)[:, None] / L
    f = np.linspace(1e-4, HY_BANDS - 1, HY_BANDS)[None, :]
    out = np.zeros((L, FEAT_PAD), np.float32)
    out[:, :HY_EMB] = np.concatenate([t, np.cos(f * w), -np.sin(f * w)], axis=-1)
    return out


@functools.lru_cache(maxsize=None)
def _rope_tables(L):
    rows = L // GRID_W
    row = np.repeat(np.arange(rows), GRID_W).astype(np.float64)
    col = np.tile(np.arange(GRID_W), rows).astype(np.float64)
    n_freq = HEAD_DIM // 4
    inv = ROPE_THETA ** (-np.arange(n_freq) / n_freq)
    ang = np.concatenate([row[:, None] * inv, col[:, None] * inv], axis=-1)
    cos = np.repeat(np.cos(ang), 2, axis=-1)
    sin = np.repeat(np.sin(ang), 2, axis=-1) * np.tile(np.array([-1.0, 1.0]), HEAD_DIM // 2)
    reps = LANES // HEAD_DIM
    return np.tile(cos, (1, reps)).astype(np.float32), np.tile(sin, (1, reps)).astype(np.float32)


@functools.lru_cache(maxsize=None)
def _head_average_matrix():
    return np.kron(np.eye(MXU_DIM // HEAD_DIM), np.full((HEAD_DIM, HEAD_DIM), 1.0 / HEAD_DIM)).astype(np.float32)


def _mod_kernel(c_ref, w_ref, b_ref, o_ref):
    s = _silu(c_ref[...])
    o_ref[0] = _dot3(s, w_ref[0]) + b_ref[0]


def _modulation(cvecs, w_ada, b_ada):
    tn = D_MODEL
    return pl.pallas_call(
        _mod_kernel,
        grid=(DEPTH, 3 * D_MODEL // tn),
        in_specs=[
            pl.BlockSpec((MOD_ROWS, D_MODEL), lambda l, j: (0, 0)),
            pl.BlockSpec((1, D_MODEL, tn), lambda l, j: (l, 0, j)),
            pl.BlockSpec((1, 1, tn), lambda l, j: (l, 0, j)),
        ],
        out_specs=pl.BlockSpec((1, MOD_ROWS, tn), lambda l, j: (l, 0, j)),
        out_shape=jax.ShapeDtypeStruct((DEPTH, MOD_ROWS, 3 * D_MODEL), F32),
        compiler_params=_cparams(2),
        name="adaln_mod",
    )(cvecs, w_ada, b_ada.reshape(DEPTH, 1, 3 * D_MODEL))


def _filter_kernel(L, feats_ref, w1_ref, b1_ref, w2_ref, b2_ref, w3_ref, freq_ref, decay_ref,
                   fwd_ref, a_ref, b_ref, d_ref):
    feats = feats_ref[...]
    t = feats[:, 0:1]
    freq = freq_ref[0]
    hdn = jnp.sin(freq[0:1] * (_dot3(feats, w1_ref[0]) + b1_ref[0]))
    hdn = jnp.sin(freq[1:2] * (_dot3(hdn, w2_ref[0]) + b2_ref[0]))
    h = _dot3(hdn, w3_ref[0]) * jnp.exp(-t * decay_ref[0])
    row = lax.broadcasted_iota(jnp.int32, (L, HY_W), 0)
    h_fwd = h[:, :HY_W]
    h_bwd = jnp.where(row >= 1, h[:, HY_W:], 0.0)
    inv_norm = 1.0 / jnp.sum(jnp.abs(h_fwd) + jnp.abs(h_bwd), axis=0, keepdims=True)
    even = h_fwd + h_bwd
    odd = h_fwd - h_bwd
    h_re = _dot(fwd_ref[0:L, :], even.astype(BF16)) * inv_norm
    h_im = _dot(fwd_ref[L:2 * L, :], odd.astype(BF16)) * inv_norm
    alt = jnp.where(row % 2 == 0, 1.0, -1.0)
    nyq = jnp.sum(even * alt, axis=0, keepdims=True) * inv_norm
    a_ref[0] = h_re
    b_ref[0] = jnp.where(row >= 1, h_im, 0.0)
    d_ref[0] = jnp.where(row >= 1, h_re, nyq)


def _filter_spectrum(L, feats, fwd, w1p, b1, w2, b2, w3, freq, decay):
    full = lambda shape: pl.BlockSpec(shape, lambda l: (0,) * len(shape))
    per_layer = lambda shape: pl.BlockSpec((1,) + shape, lambda l: (l,) + (0,) * len(shape))
    out = jax.ShapeDtypeStruct((DEPTH, L, HY_W), F32)
    return pl.pallas_call(
        functools.partial(_filter_kernel, L),
        grid=(DEPTH,),
        in_specs=[
            full((L, FEAT_PAD)),
            per_layer((FEAT_PAD, HY_FH)), per_layer((1, HY_FH)),
            per_layer((HY_FH, HY_FH)), per_layer((1, HY_FH)),
            per_layer((HY_FH, 2 * HY_W)), per_layer((2, HY_FH)), per_layer((1, 2 * HY_W)),
            full((2 * L, L)),
        ],
        out_specs=[per_layer((L, HY_W))] * 3,
        out_shape=[out, out, out],
        compiler_params=_cparams(1),
        name=f"hyena_filter_L{L}",
    )(feats, w1p, b1, w2, b2, w3, freq, decay, fwd)


def _rope(x, cos, sin):
    lane = lax.broadcasted_iota(jnp.int32, x.shape, 1)
    partner = jnp.where(lane % 2 == 0, pltpu.roll(x, LANES - 1, 1), pltpu.roll(x, 1, 1))
    return x * cos + partner * sin


def _head_mean_square(x, e):
    sq = (x * x).astype(BF16)
    width = x.shape[1]
    if width <= MXU_DIM:
        return _dot(sq, e[:width, :width])
    parts = [_dot(sq[:, i:i + MXU_DIM], e) for i in range(0, width, MXU_DIM)]
    return jnp.concatenate(parts, axis=1)


def _store_kv_heads(k_dst, v_dst, k, v):
    ones = jnp.ones((v.shape[0], HEAD_DIM), BF16)
    for j in range(N_KV_HEADS):
        js = slice(j * HEAD_DIM, (j + 1) * HEAD_DIM)
        vj = v[:, js].astype(BF16)
        k_dst[j] = k[:, js].astype(BF16)
        v_dst[j, 0, :, 0:HEAD_DIM] = vj
        v_dst[j, 0, :, HEAD_DIM:2 * HEAD_DIM] = ones
        v_dst[j, 1, :, 0:HEAD_DIM] = ones
        v_dst[j, 1, :, HEAD_DIM:2 * HEAD_DIM] = vj


def _attn_kernel(latent, aliased, layer, L, nb, qc, past, *refs):
    x_ref, mod_ref, ng_ref, w_ref, gq_ref, gk_ref, e_ref = refs[:7]
    refs = refs[7:]
    if latent:
        (cos_ref, sin_ref, ck_ref, cv_ref, h_out, mix_out,
         q_s, k_s, v_s, g_s, s_s, mx_s, acc_s, ck_s, cv_s) = refs
    else:
        if aliased:
            refs = refs[2:]
        h_out, mix_out, k_out, v_out, q_s, k_s, v_s, g_s, s_s, mx_s, acc_s = refs
    rows = nb * L

    x = x_ref[...].reshape(rows, D_MODEL)
    m = mod_ref[0, 0]
    shift, scale = m[0:1], m[1:2]
    ms = jnp.mean(x * x, axis=-1, keepdims=True)
    h = (x * lax.rsqrt(ms + EPS)) * (ng_ref[0] * (1.0 + scale)) + shift
    hb = h.astype(BF16)
    h_out[...] = hb.reshape(nb, L, D_MODEL)

    u = _dot(hb, w_ref[0])
    q = u[:, :ATTN_W]
    k = u[:, ATTN_W:ATTN_W + KV_W]
    v = u[:, ATTN_W + KV_W:ATTN_W + 2 * KV_W]
    g_s[...] = u[:, ATTN_W + 2 * KV_W:]
    e = e_ref[...]
    qn = q * lax.rsqrt(_head_mean_square(q, e) + EPS) * gq_ref[0]
    kn = k * lax.rsqrt(_head_mean_square(k, e) + EPS) * gk_ref[0]
    if latent:
        cos, sin = cos_ref[...], sin_ref[...]
        qn = jnp.concatenate([_rope(qn[:, i:i + LANES], cos, sin) for i in range(0, ATTN_W, LANES)], axis=1)
        kn = _rope(kn, cos, sin)
        _store_kv_heads(ck_s, cv_s, ck_ref[0, 0], cv_ref[0, 0])
    else:
        for b in range(nb):
            kb, vb = kn[b * L:(b + 1) * L], v[b * L:(b + 1) * L]
            if aliased:
                k_out[b, 0] = kb
                v_out[b, 0] = vb
            else:
                for slot in range(DEPTH):
                    k_out[b, slot] = kb if slot == layer else jnp.zeros_like(kb)
                    v_out[b, slot] = vb if slot == layer else jnp.zeros_like(vb)
    qb = (qn * ATTN_SCALE).astype(BF16)
    for hd in range(N_Q_HEADS):
        j, g = divmod(hd, Q_PER_KV)
        q_s[j * Q_PER_KV + (g % 2) * HEAD_PAIRS + g // 2] = qb[:, hd * HEAD_DIM:(hd + 1) * HEAD_DIM]
    _store_kv_heads(k_s, v_s, kn, v)

    contract_last = (((1,), (1,)), ((), ()))
    chunks_per_seq = L // qc
    lane = lax.broadcasted_iota(jnp.int32, (qc, LANES), 1)
    groups = [(j, e) for j in range(N_KV_HEADS) for e in range(2)]

    def chunk(c, carry):
        r0 = pl.multiple_of(c * qc, qc)
        k0 = pl.multiple_of((c // chunks_per_seq) * L, L)
        for idx, (j, e) in enumerate(groups):
            slot = j * Q_PER_KV + e * HEAD_PAIRS
            qh = q_s[slot:slot + HEAD_PAIRS, pl.ds(r0, qc), :].reshape(HEAD_PAIRS * qc, HEAD_DIM)
            s = lax.dot_general(qh, k_s[j, pl.ds(k0, L), :], contract_last, preferred_element_type=F32)
            mx = jnp.max(s, axis=-1, keepdims=True)
            if latent:
                s0 = lax.dot_general(qh, ck_s[j], contract_last, preferred_element_type=F32)
                mx = jnp.maximum(mx, jnp.max(s0, axis=-1, keepdims=True))
                s_s[idx, :, 0:past] = s0
                s_s[idx, :, past:] = s
            else:
                s_s[idx] = s
            mx_s[idx] = jnp.broadcast_to(mx, (HEAD_PAIRS * qc, LANES))
        for idx, (j, e) in enumerate(groups):
            p = jnp.exp(s_s[idx] - jnp.tile(mx_s[idx], (1, (past + L) // LANES))).astype(BF16)
            if latent:
                acc = _dot(p[:, :past], cv_s[j, e]) + _dot(p[:, past:], v_s[j, e, pl.ds(k0, L), :])
            else:
                acc = _dot(p, v_s[j, e, pl.ds(k0, L), :])
            acc_s[idx] = acc
        b_idx = c // chunks_per_seq
        row0 = pl.multiple_of((c % chunks_per_seq) * qc, qc)
        for j in range(N_KV_HEADS):
            for t in range(HEAD_PAIRS):
                a_even = acc_s[2 * j, t * qc:(t + 1) * qc, :]
                a_odd = acc_s[2 * j + 1, t * qc:(t + 1) * qc, :]
                out = jnp.where(lane < HEAD_DIM, a_even, a_odd)
                den = pltpu.roll(jnp.where(lane < HEAD_DIM, a_odd, a_even), HEAD_DIM, 1)
                blk = slice((j * HEAD_PAIRS + t) * LANES, (j * HEAD_PAIRS + t + 1) * LANES)
                gate = _silu(g_s[pl.ds(r0, qc), blk])
                mix_out[b_idx, pl.ds(row0, qc), blk] = (out / den * gate).astype(BF16)
        return carry

    lax.fori_loop(0, nb * chunks_per_seq, chunk, 0)


def _attention_branch(latent, layer, nb, x, mod, norm_g, w_in_b, gq, gk, e_avg, rope=None, cache=None, kv_prev=None):
    B, L, _ = x.shape
    rows = nb * L
    assert nb == 1 or not latent
    full = lambda shape: pl.BlockSpec(shape, lambda b: (0,) * len(shape))
    per_b = lambda shape: pl.BlockSpec((nb,) + shape, lambda b: (b,) + (0,) * len(shape))
    per_layer = lambda shape: pl.BlockSpec((1,) + shape, lambda b: (layer,) + (0,) * len(shape))
    mod_spec = pl.BlockSpec((1, 1, 3, D_MODEL), (lambda b: (layer, 1 + b, 0, 0)) if latent else (lambda b: (layer, 0, 0, 0)))
    in_specs = [per_b((L, D_MODEL)), mod_spec, per_layer((1, D_MODEL)), per_layer((D_MODEL, ATTN_COLS)),
                per_layer((1, ATTN_W)), per_layer((1, KV_W)), full((MXU_DIM, MXU_DIM))]
    args = [x, mod, norm_g, w_in_b, gq, gk, e_avg]
    out_specs = [per_b((L, D_MODEL)), per_b((L, ATTN_W))]
    out_shape = [jax.ShapeDtypeStruct((B, L, D_MODEL), BF16), jax.ShapeDtypeStruct((B, L, ATTN_W), BF16)]
    qc = Q_CHUNK_LATENT if latent else Q_CHUNK
    past = cache[0].shape[2] if latent else 0
    assert B % nb == 0 and L % qc == 0 and L % LANES == 0 and past % LANES == 0
    n_groups = 2 * N_KV_HEADS
    scratch = [pltpu.VMEM((N_Q_HEADS, rows, HEAD_DIM), BF16), pltpu.VMEM((N_KV_HEADS, rows, HEAD_DIM), BF16),
               pltpu.VMEM((N_KV_HEADS, 2, rows, 2 * HEAD_DIM), BF16),
               pltpu.VMEM((rows, ATTN_W), F32),
               pltpu.VMEM((n_groups, HEAD_PAIRS * qc, past + L), F32),
               pltpu.VMEM((n_groups, HEAD_PAIRS * qc, LANES), F32),
               pltpu.VMEM((n_groups, HEAD_PAIRS * qc, 2 * HEAD_DIM), F32)]
    aliases = {}
    if latent:
        cache_k, cache_v = cache
        cache_spec = pl.BlockSpec((1, 1, past, KV_W), lambda b: (b, layer, 0, 0))
        in_specs += [full((L, LANES)), full((L, LANES)), cache_spec, cache_spec]
        args += [rope[0], rope[1], cache_k, cache_v]
        scratch += [pltpu.VMEM((N_KV_HEADS, past, HEAD_DIM), BF16),
                    pltpu.VMEM((N_KV_HEADS, 2, past, 2 * HEAD_DIM), BF16)]
    else:
        if kv_prev is not None:
            kv_spec = pl.BlockSpec((nb, 1, L, KV_W), lambda b: (b, layer, 0, 0))
        else:
            kv_spec = pl.BlockSpec((nb, DEPTH, L, KV_W), lambda b: (b, 0, 0, 0))
        kv_shape = jax.ShapeDtypeStruct((B, DEPTH, L, KV_W), F32)
        out_specs += [kv_spec, kv_spec]
        out_shape += [kv_shape, kv_shape]
        if kv_prev is not None:
            in_specs += [pl.BlockSpec(memory_space=pl.ANY)] * 2
            aliases = {len(args): 2, len(args) + 1: 3}
            args += list(kv_prev)
    return pl.pallas_call(
        functools.partial(_attn_kernel, latent, kv_prev is not None, layer, L, nb, qc, past),
        grid=(B // nb,),
        in_specs=in_specs,
        out_specs=out_specs,
        out_shape=out_shape,
        scratch_shapes=scratch,
        input_output_aliases=aliases,
        compiler_params=_cparams(1),
        name=f"attn_branch_L{L}",
    )(*args)


def _shifted(x, offset, period):
    rows = x.shape[0]
    t = lax.broadcasted_iota(jnp.int32, x.shape, 0)
    if rows != period:
        t = t % period
    rolled = pltpu.roll(x, (-offset) % rows, 0)
    valid = (t >= -offset) if offset < 0 else (t < period - offset)
    return jnp.where(valid, rolled, 0.0)


def _hyena_kernel(L, nb, h_ref, w0_ref, w1_ref, w2_ref, wg_ref, sw0_ref, sw1_ref, sw2_ref, sb0_ref, sb1_ref, sb2_ref,
                  fa_ref, fb_ref, fd_ref, bias_ref, fwd_ref, inv_ref, mix_out):
    hb = h_ref[...].reshape(nb * L, D_MODEL)

    def short_conv(w_ref, sw_ref, sb_ref):
        xs = _dot(hb, w_ref[0])
        w = sw_ref[0]
        return sb_ref[0] + _shifted(xs, -1, L) * w[0:1] + xs * w[1:2] + _shifted(xs, 1, L) * w[2:3]

    x0 = short_conv(w0_ref, sw0_ref, sb0_ref)
    x1 = short_conv(w1_ref, sw1_ref, sb1_ref)
    hv = short_conv(w2_ref, sw2_ref, sb2_ref)
    gated = x0 * _silu(_dot(hb, wg_ref[0]))
    z = x1 * hv
    zb = z.astype(BF16)
    fa, fb, fd = fa_ref[0], fb_ref[0], fd_ref[0]
    cb = z.shape[1]
    z_cat = jnp.concatenate([zb[b * L:(b + 1) * L] for b in range(nb)], axis=1)
    zf = _dot(fwd_ref[...], z_cat)
    y_re, y_im = [], []
    for b in range(nb):
        re, im = zf[:L, b * cb:(b + 1) * cb], zf[L:, b * cb:(b + 1) * cb]
        y_re.append((re * fa - im * fb).astype(BF16))
        y_im.append((re * fb + im * fd).astype(BF16))
    y_all = (_dot(inv_ref[:, :L], jnp.concatenate(y_re, axis=1))
             + _dot(inv_ref[:, L:], jnp.concatenate(y_im, axis=1)))
    for b in range(nb):
        rs = slice(b * L, (b + 1) * L)
        y = y_all[:, b * cb:(b + 1) * cb] + z[rs] * bias_ref[0]
        mix_out[b] = (y * gated[rs]).astype(BF16)


def _hyena_branch(layer, nb, hmod, w_in_b, short_w, short_b, filt, bias, fwd, inv):
    B, L, _ = hmod.shape
    cb = COL_BLOCK
    nt = HY_W // cb
    fa, fb, fd = filt
    assert B % nb == 0 and L % LANES == 0
    w_spec = lambda part: pl.BlockSpec((1, D_MODEL, cb), lambda i, b: (layer, 0, HY_BLOCK0 + part * nt + i))
    sw_spec = lambda part: pl.BlockSpec((1, 3, cb), lambda i, b: (layer, 0, part * nt + i))
    sb_spec = lambda part: pl.BlockSpec((1, 1, cb), lambda i, b: (layer, 0, part * nt + i))
    filt_spec = pl.BlockSpec((1, L, cb), lambda i, b: (layer, 0, i))
    return pl.pallas_call(
        functools.partial(_hyena_kernel, L, nb),
        grid=(nt, B // nb),
        in_specs=[
            pl.BlockSpec((nb, L, D_MODEL), lambda i, b: (b, 0, 0)),
            w_spec(0), w_spec(1), w_spec(2), w_spec(3),
            sw_spec(0), sw_spec(1), sw_spec(2),
            sb_spec(0), sb_spec(1), sb_spec(2),
            filt_spec, filt_spec, filt_spec,
            pl.BlockSpec((1, 1, cb), lambda i, b: (layer, 0, i)),
            pl.BlockSpec((2 * L, L), lambda i, b: (0, 0)),
            pl.BlockSpec((L, 2 * L), lambda i, b: (0, 0)),
        ],
        out_specs=pl.BlockSpec((nb, L, cb), lambda i, b: (b, 0, i)),
        out_shape=jax.ShapeDtypeStruct((B, L, HY_W), BF16),
        compiler_params=_cparams(2),
        name=f"hyena_branch_L{L}",
    )(hmod, w_in_b, w_in_b, w_in_b, w_in_b, short_w, short_w, short_w, short_b, short_b, short_b,
      fa, fb, fd, bias, fwd, inv)


def _lru_kernel(latent, aliased, final, layer, L, nb, *refs):
    (h_ref, wx0_ref, wx1_ref, wg0_ref, wg1_ref, cw_ref, cb_ref, wa_ref, ba_ref, wxg_ref, bx_ref, lam_ref,
     x_ref, ma_ref, mh_ref, wo_ref, mod_ref, fg_ref) = refs[:18]
    refs = refs[18:]
    if latent:
        h0_ref, x_out, gate_w_s, a_s, b_s, y_s, g_s, part_s = refs
    else:
        if aliased:
            refs = refs[1:]
        x_out, hl_out, gate_w_s, a_s, b_s, y_s, g_s, part_s = refs
    half_w = LRU_W // 2
    blocks_per_half = half_w // LRU_BS
    n_tiles = L // SUBLANES

    @pl.when(pl.program_id(0) == 0)
    def _build_gate_weights():
        gate_w_s[...] = jnp.zeros(gate_w_s.shape, BF16)
        for half in range(2):
            for kind, (w_ref, d) in enumerate(((wa_ref, 0), (wxg_ref, 0), (wa_ref, 1), (wxg_ref, 1))):
                for j in range(blocks_per_half):
                    blk = (0.5 * w_ref[0, d, half * blocks_per_half + j]).astype(BF16)
                    gate_w_s[half, j * LRU_BS:(j + 1) * LRU_BS,
                             kind * half_w + j * LRU_BS:kind * half_w + (j + 1) * LRU_BS] = blk

    part_s[...] = (_dot(ma_ref[...].reshape(nb * L, ATTN_W), wo_ref[0, 0:ATTN_W, :])
                   + _dot(mh_ref[...].reshape(nb * L, HY_W), wo_ref[0, ATTN_W:ATTN_W + HY_W, :]))

    hb = h_ref[...].reshape(nb * L, D_MODEL)
    xs = jnp.concatenate([_dot(hb, wx0_ref[0]), _dot(hb, wx1_ref[0])], axis=1)
    g_s[:, :half_w] = _dot(hb, wg0_ref[0])
    g_s[:, half_w:] = _dot(hb, wg1_ref[0])
    cw = cw_ref[0]
    xc = (cb_ref[0] + _shifted(xs, -1, L) * cw[0:1] + xs * cw[1:2] + _shifted(xs, 1, L) * cw[2:3]
          + _shifted(xs, 2, L) * cw[3:4])
    xcb = xc.astype(BF16)
    c_half = (-0.5 * LRU_C) * jax.nn.softplus(-lam_ref[0])
    ba, bx = 0.5 * ba_ref[0], 0.5 * bx_ref[0]
    for half in range(2):
        cols = slice(half * half_w, (half + 1) * half_w)
        bias = jnp.concatenate([ba[0:1, cols], bx[0:1, cols], ba[1:2, cols], bx[1:2, cols]], axis=1)
        gates = _dot(xcb[:, cols], gate_w_s[half]) + bias
        xh_half = 0.5 * xc[:, cols]
        for d in range(2):
            t_r = jnp.tanh(gates[:, (2 * d) * half_w:(2 * d + 1) * half_w])
            t_i = jnp.tanh(gates[:, (2 * d + 1) * half_w:(2 * d + 2) * half_w])
            ch = c_half[d:d + 1, cols]
            log_a = ch * t_r + ch
            a = jnp.exp(log_a)
            y = jnp.tanh(log_a) * (-1.0 - a * a)
            root = jnp.where(y > 0.0, y * lax.rsqrt(y), 0.0)
            bcoef = root * (xh_half * t_i + xh_half)
            a_s[d, :, :, cols] = a.reshape(nb * n_tiles, SUBLANES, half_w)
            b_s[d, :, :, cols] = bcoef.reshape(nb * n_tiles, SUBLANES, half_w)

    if latent:
        init = []
        for b in range(nb):
            h0 = h0_ref[b, 0]
            init += [h0[0:1], h0[1:2]]
    else:
        init = [jnp.zeros((1, LRU_W), F32)] * (2 * nb)

    def advance(d, tile, rows_g, h):
        a = [a_s[d, tile, r:r + 1, :] for r in rows_g]
        c = [b_s[d, tile, r:r + 1, :] for r in rows_g]
        a01, c01 = a[1] * a[0], a[1] * c[0] + c[1]
        hs = [a[0] * h + c[0], a01 * h + c01]
        if len(rows_g) == 4:
            a23, c23 = a[3] * a[2], a[3] * c[2] + c[3]
            hs += [a[2] * hs[1] + c[2], (a23 * a01) * h + (a23 * c01 + c23)]
        for r, v in zip(rows_g, hs):
            y_s[d, tile, r:r + 1, :] = v
        return hs[-1]

    group = 4 if nb == 1 else 2

    def tile_step(i, carry):
        carry = list(carry)
        for g0 in range(0, SUBLANES, group):
            fwd_rows = list(range(g0, g0 + group))
            bwd_rows = [SUBLANES - 1 - r for r in fwd_rows]
            for b in range(nb):
                carry[2 * b] = advance(0, b * n_tiles + i, fwd_rows, carry[2 * b])
                carry[2 * b + 1] = advance(1, b * n_tiles + (n_tiles - 1 - i), bwd_rows, carry[2 * b + 1])
        return tuple(carry)

    def two_tiles(i, carry):
        return tile_step(2 * i + 1, tile_step(2 * i, carry))

    last = lax.fori_loop(0, n_tiles // 2, two_tiles, tuple(init))
    if not latent:
        if not aliased:
            hl_out[...] = jnp.zeros(hl_out.shape, F32)
        slot = 0 if aliased else layer
        for b in range(nb):
            hl_out[b, slot, 0:1, :] = last[2 * b]
            hl_out[b, slot, 1:2, :] = last[2 * b + 1]
    y = (y_s[0] + y_s[1]).reshape(nb * L, LRU_W)
    mix_l = (y * _silu(g_s[...])).astype(BF16)
    out = part_s[...] + _dot(mix_l, wo_ref[0, ATTN_W + HY_W:, :])
    xn = x_ref[...].reshape(nb * L, D_MODEL) + mod_ref[0, 0][2:3] * out
    if final:
        ms = jnp.mean(xn * xn, axis=-1, keepdims=True)
        xn = xn * lax.rsqrt(ms + EPS) * fg_ref[...]
    x_out[...] = xn.reshape(nb, L, D_MODEL)


def _lru_out_branch(latent, layer, nb, hmod, w_in_b, conv_w, conv_b, wa, ba, wx, bx, lam, x, mix_a, mix_h, w_out_b,
                    mod, final_g, state=None, hl_prev=None):
    B, L, _ = hmod.shape
    assert B % nb == 0 and L % (2 * SUBLANES) == 0
    final = layer == DEPTH - 1
    cb = COL_BLOCK
    half_w = LRU_W // 2
    rows = nb * L
    per_layer = lambda shape: pl.BlockSpec((1,) + shape, lambda b: (layer,) + (0,) * len(shape))
    w_spec = lambda j: pl.BlockSpec((1, D_MODEL, cb), lambda b: (layer, 0, LRU_BLOCK0 + j))
    gate_blocks = (2, LRU_BLOCKS, LRU_BS, LRU_BS)
    in_specs = [pl.BlockSpec((nb, L, D_MODEL), lambda b: (b, 0, 0)), w_spec(0), w_spec(1), w_spec(2), w_spec(3),
                per_layer((4, LRU_W)), per_layer((1, LRU_W)), per_layer(gate_blocks), per_layer((2, LRU_W)),
                per_layer(gate_blocks), per_layer((2, LRU_W)), per_layer((2, LRU_W))]
    per_b = lambda w: pl.BlockSpec((nb, L, w), lambda b: (b, 0, 0))
    mod_row = (lambda b: (layer, 1 + b, 0, 0)) if latent else (lambda b: (layer, 0, 0, 0))
    in_specs += [per_b(D_MODEL), per_b(ATTN_W), per_b(HY_W), per_layer((D_MIX, D_MODEL)),
                 pl.BlockSpec((1, 1, 3, D_MODEL), mod_row), pl.BlockSpec((1, D_MODEL), lambda b: (0, 0))]
    args = [hmod, w_in_b, w_in_b, w_in_b, w_in_b, conv_w, conv_b, wa, ba, wx, bx, lam,
            x, mix_a, mix_h, w_out_b, mod, final_g]
    out_specs = [per_b(D_MODEL)]
    out_shape = [jax.ShapeDtypeStruct((B, L, D_MODEL), F32)]
    aliases = {}
    if latent:
        in_specs.append(pl.BlockSpec((nb, 1, 2, LRU_W), lambda b: (b, layer, 0, 0)))
        args.append(state)
    else:
        if hl_prev is not None:
            out_specs.append(pl.BlockSpec((nb, 1, 2, LRU_W), lambda b: (b, layer, 0, 0)))
        else:
            out_specs.append(pl.BlockSpec((nb, DEPTH, 2, LRU_W), lambda b: (b, 0, 0, 0)))
        out_shape.append(jax.ShapeDtypeStruct((B, DEPTH, 2, LRU_W), F32))
        if hl_prev is not None:
            in_specs.append(pl.BlockSpec(memory_space=pl.ANY))
            aliases = {len(args): 1}
            args.append(hl_prev)
    scan_buf = pltpu.VMEM((2, rows // SUBLANES, SUBLANES, LRU_W), F32)
    return pl.pallas_call(
        functools.partial(_lru_kernel, latent, hl_prev is not None, final, layer, L, nb),
        grid=(B // nb,),
        in_specs=in_specs,
        out_specs=out_specs,
        out_shape=out_shape,
        scratch_shapes=[pltpu.VMEM((2, half_w, 4 * half_w), BF16), scan_buf, scan_buf, scan_buf,
                        pltpu.VMEM((rows, LRU_W), F32), pltpu.VMEM((rows, D_MODEL), F32)],
        input_output_aliases=aliases,
        compiler_params=_cparams(1),
        name=f"lru_out_L{L}",
    )(*args)


def _mixer_pass(latent, x, p, filt, dft, rope=None, cache=None, state=None):
    B, L, _ = x.shape
    fwd, inv = dft
    nb = 1 if latent else 4
    nb_hyena = min(B, 2) if latent else 4
    nb_lru = 1 if latent else 2
    kv, hl = None, None
    for l in range(DEPTH):
        outs = _attention_branch(latent, l, nb, x, p['mod'], p['norm_g'], p['w_in'], p['gq'], p['gk'], p['e_avg'],
                                 rope, cache, kv)
        hmod, mix_a = outs[0], outs[1]
        if not latent:
            kv = (outs[2], outs[3])
        mix_h = _hyena_branch(l, nb_hyena, hmod, p['w_in'], p['short_w'], p['short_b'], filt, p['hy_bias'], fwd, inv)
        outs = _lru_out_branch(latent, l, nb_lru, hmod, p['w_in'], p['conv_w'], p['conv_b'], p['wa'], p['ba'], p['wx'],
                               p['bx'], p['lam'], x, mix_a, mix_h, p['w_out'], p['mod'], p['final_g'], state, hl)
        x = outs[0]
        if not latent:
            hl = outs[1]
    return x, kv, hl


def kernel(x_prompt, x_sample, cache_k, cache_v, state_lru, c, c_ctx, norm_g, w_ada, b_ada, w_in, q_norm_g, k_norm_g,
           hy_short_w, hy_short_b, hy_filt_w1, hy_filt_b1, hy_filt_w2, hy_filt_b2, hy_filt_w3, hy_filt_freq,
           hy_filt_decay, hy_bias, lru_conv_w, lru_conv_b, lru_wa, lru_ba, lru_wx, lru_bx, lru_lambda, w_out, final_g):
    batch, seq, _ = x_prompt.shape
    dec_batch, dec_seq, _ = x_sample.shape
    past = cache_k.shape[2]

    cvecs = jnp.concatenate([c_ctx[None, :], c, jnp.zeros((MOD_ROWS - 1 - dec_batch, D_MODEL), F32)], axis=0)
    mod = _modulation(cvecs, w_ada, b_ada).reshape(DEPTH, MOD_ROWS, 3, D_MODEL)

    w1p = jnp.pad(hy_filt_w1, ((0, 0), (0, FEAT_PAD - HY_EMB), (0, 0)))
    dft, filt = {}, {}
    for L in (seq, dec_seq):
        dft[L] = tuple(jnp.asarray(m).astype(BF16) for m in _dft_tables(L))
        filt[L] = _filter_spectrum(L, jnp.asarray(_hyena_feats(L)), dft[L][0], w1p, hy_filt_b1[:, None, :],
                                   hy_filt_w2, hy_filt_b2[:, None, :], hy_filt_w3, hy_filt_freq,
                                   hy_filt_decay[:, None, :])

    params = {
        'mod': mod,
        'norm_g': norm_g[:, None, :],
        'w_in': w_in.astype(BF16),
        'gq': jnp.tile(q_norm_g, (1, N_Q_HEADS))[:, None, :],
        'gk': jnp.tile(k_norm_g, (1, N_KV_HEADS))[:, None, :],
        'e_avg': jnp.asarray(_head_average_matrix()).astype(BF16),
        'short_w': hy_short_w, 'short_b': hy_short_b[:, None, :], 'hy_bias': hy_bias[:, None, :],
        'conv_w': lru_conv_w, 'conv_b': lru_conv_b[:, None, :],
        'wa': lru_wa, 'ba': lru_ba, 'wx': lru_wx, 'bx': lru_bx, 'lam': lru_lambda,
        'w_out': w_out.astype(BF16),
        'final_g': final_g[None, :],
    }

    y_prompt, kv, new_lru = _mixer_pass(False, x_prompt, params, filt[seq], dft[seq])
    new_k = kv[0].reshape(batch, DEPTH, seq, N_KV_HEADS, HEAD_DIM)
    new_v = kv[1].reshape(batch, DEPTH, seq, N_KV_HEADS, HEAD_DIM)

    rope = tuple(jnp.asarray(t) for t in _rope_tables(dec_seq))
    cache = (cache_k.reshape(dec_batch, DEPTH, past, KV_W), cache_v.reshape(dec_batch, DEPTH, past, KV_W))
    y_sample, _, _ = _mixer_pass(True, x_sample, params, filt[dec_seq], dft[dec_seq], rope, cache, state_lru)
    return (y_prompt, y_sample, new_k, new_v, new_lru)
```

```python
import functools
import math

import numpy as np
import jax
import jax.numpy as jnp
from jax import lax
from jax.experimental import pallas as pl
from jax.experimental.pallas import tpu as pltpu

F32 = jnp.float32
BF16 = jnp.bfloat16

D_MODEL = 1024
DEPTH = 2
GRID_W = 64
HEAD_DIM = 64
N_Q_HEADS = 8
N_KV_HEADS = 2
Q_PER_KV = N_Q_HEADS // N_KV_HEADS
ATTN_W = N_Q_HEADS * HEAD_DIM
KV_W = N_KV_HEADS * HEAD_DIM
ROPE_THETA = 10000.0
ATTN_SCALE = HEAD_DIM ** -0.5
HY_W = 512
HY_BANDS = 16
HY_EMB = 2 * HY_BANDS + 1
HY_FH = 64
LRU_W = 512
LRU_BLOCKS = 8
LRU_BS = LRU_W // LRU_BLOCKS
LRU_C = 8.0
EPS = 1e-6

ATTN_COLS = ATTN_W + 2 * KV_W + ATTN_W
D_MIX = ATTN_W + HY_W + LRU_W

LANES = 128
SUBLANES = 8
MXU_DIM = 256
COL_BLOCK = MXU_DIM
HY_BLOCK0 = ATTN_COLS // COL_BLOCK
LRU_BLOCK0 = (ATTN_COLS + 4 * HY_W) // COL_BLOCK
Q_CHUNK = 256
Q_CHUNK_LATENT = 256
HEAD_PAIRS = Q_PER_KV // 2
FEAT_PAD = 128
MOD_ROWS = 8
VMEM_LIMIT = 56 * 1024 * 1024


def _cparams(n_axes):
    return pltpu.CompilerParams(dimension_semantics=("arbitrary",) * n_axes, vmem_limit_bytes=VMEM_LIMIT)


def _split_bf16(a):
    hi = a.astype(BF16)
    lo = (a - hi.astype(F32)).astype(BF16)
    return hi, lo


def _dot(a, b):
    return jnp.dot(a, b, preferred_element_type=F32)


def _dot3(a, b):
    a_hi, a_lo = _split_bf16(a)
    b_hi, b_lo = _split_bf16(b)
    return _dot(a_hi, b_hi) + _dot(a_hi, b_lo) + _dot(a_lo, b_hi)


def _silu(x):
    half = 0.5 * x
    return half * jnp.tanh(half) + half


@functools.lru_cache(maxsize=None)
def _dft_tables(L):
    n = 2 * L
    t = np.arange(L)
    kt = (t[:, None] * t[None, :]) % n
    ang = 2.0 * np.pi * kt / n
    cosm, sinm = np.cos(ang), np.sin(ang)
    alt = (-1.0) ** t
    f_s = -sinm
    f_s[0, :] = alt
    fwd = np.concatenate([cosm, f_s], axis=0)
    g_c = 2.0 * cosm.T / n
    g_c[:, 0] = 1.0 / n
    g_s = -2.0 * sinm.T / n
    g_s[:, 0] = alt / n
    inv = np.concatenate([g_c, g_s], axis=1)
    return fwd.astype(np.float32), inv.astype(np.float32)


@functools.lru_cache(maxsize=None)
def _hyena_feats(L):
    t = np.linspace(0.0, 1.0, L)[:, None]
    w = 2.0 * math.pi * np.arange(L)[:, None] / L
    f = np.linspace(1e-4, HY_BANDS - 1, HY_BANDS)[None, :]
    out = np.zeros((L, FEAT_PAD), np.float32)
    out[:, :HY_EMB] = np.concatenate([t, np.cos(f * w), -np.sin(f * w)], axis=-1)
    return out


@functools.lru_cache(maxsize=None)
def _rope_tables(L):
    rows = L // GRID_W
    row = np.repeat(np.arange(rows), GRID_W).astype(np.float64)
    col = np.tile(np.arange(GRID_W), rows).astype(np.float64)
    n_freq = HEAD_DIM // 4
    inv = ROPE_THETA ** (-np.arange(n_freq) / n_freq)
    ang = np.concatenate([row[:, None] * inv, col[:, None] * inv], axis=-1)
    cos = np.repeat(np.cos(ang), 2, axis=-1)
    sin = np.repeat(np.sin(ang), 2, axis=-1) * np.tile(np.array([-1.0, 1.0]), HEAD_DIM // 2)
    reps = LANES // HEAD_DIM
    return np.tile(cos, (1, reps)).astype(np.float32), np.tile(sin, (1, reps)).astype(np.float32)


@functools.lru_cache(maxsize=None)
def _head_average_matrix():
    return np.kron(np.eye(MXU_DIM // HEAD_DIM), np.full((HEAD_DIM, HEAD_DIM), 1.0 / HEAD_DIM)).astype(np.float32)


def _mod_kernel(c_ref, w_ref, b_ref, o_ref):
    s = _silu(c_ref[...])
    o_ref[0] = _dot3(s, w_ref[0]) + b_ref[0]


def _modulation(cvecs, w_ada, b_ada):
    tn = D_MODEL
    return pl.pallas_call(
        _mod_kernel,
        grid=(DEPTH, 3 * D_MODEL // tn),
        in_specs=[
            pl.BlockSpec((MOD_ROWS, D_MODEL), lambda l, j: (0, 0)),
            pl.BlockSpec((1, D_MODEL, tn), lambda l, j: (l, 0, j)),
            pl.BlockSpec((1, 1, tn), lambda l, j: (l, 0, j)),
        ],
        out_specs=pl.BlockSpec((1, MOD_ROWS, tn), lambda l, j: (l, 0, j)),
        out_shape=jax.ShapeDtypeStruct((DEPTH, MOD_ROWS, 3 * D_MODEL), F32),
        compiler_params=_cparams(2),
        name="adaln_mod",
    )(cvecs, w_ada, b_ada.reshape(DEPTH, 1, 3 * D_MODEL))


def _filter_kernel(L, feats_ref, w1_ref, b1_ref, w2_ref, b2_ref, w3_ref, freq_ref, decay_ref,
                   fwd_ref, a_ref, b_ref, d_ref):
    feats = feats_ref[...]
    t = feats[:, 0:1]
    freq = freq_ref[0]
    hdn = jnp.sin(freq[0:1] * (_dot3(feats, w1_ref[0]) + b1_ref[0]))
    hdn = jnp.sin(freq[1:2] * (_dot3(hdn, w2_ref[0]) + b2_ref[0]))
    h = _dot3(hdn, w3_ref[0]) * jnp.exp(-t * decay_ref[0])
    row = lax.broadcasted_iota(jnp.int32, (L, HY_W), 0)
    h_fwd = h[:, :HY_W]
    h_bwd = jnp.where(row >= 1, h[:, HY_W:], 0.0)
    inv_norm = 1.0 / jnp.sum(jnp.abs(h_fwd) + jnp.abs(h_bwd), axis=0, keepdims=True)
    even = h_fwd + h_bwd
    odd = h_fwd - h_bwd
    h_re = _dot(fwd_ref[0:L, :], even.astype(BF16)) * inv_norm
    h_im = _dot(fwd_ref[L:2 * L, :], odd.astype(BF16)) * inv_norm
    alt = jnp.where(row % 2 == 0, 1.0, -1.0)
    nyq = jnp.sum(even * alt, axis=0, keepdims=True) * inv_norm
    a_ref[0] = h_re
    b_ref[0] = jnp.where(row >= 1, h_im, 0.0)
    d_ref[0] = jnp.where(row >= 1, h_re, nyq)


def _filter_spectrum(L, feats, fwd, w1p, b1, w2, b2, w3, freq, decay):
    full = lambda shape: pl.BlockSpec(shape, lambda l: (0,) * len(shape))
    per_layer = lambda shape: pl.BlockSpec((1,) + shape, lambda l: (l,) + (0,) * len(shape))
    out = jax.ShapeDtypeStruct((DEPTH, L, HY_W), F32)
    return pl.pallas_call(
        functools.partial(_filter_kernel, L),
        grid=(DEPTH,),
        in_specs=[
            full((L, FEAT_PAD)),
            per_layer((FEAT_PAD, HY_FH)), per_layer((1, HY_FH)),
            per_layer((HY_FH, HY_FH)), per_layer((1, HY_FH)),
            per_layer((HY_FH, 2 * HY_W)), per_layer((2, HY_FH)), per_layer((1, 2 * HY_W)),
            full((2 * L, L)),
        ],
        out_specs=[per_layer((L, HY_W))] * 3,
        out_shape=[out, out, out],
        compiler_params=_cparams(1),
        name=f"hyena_filter_L{L}",
    )(feats, w1p, b1, w2, b2, w3, freq, decay, fwd)


def _rope(x, cos, sin):
    lane = lax.broadcasted_iota(jnp.int32, x.shape, 1)
    partner = jnp.where(lane % 2 == 0, pltpu.roll(x, LANES - 1, 1), pltpu.roll(x, 1, 1))
    return x * cos + partner * sin


def _head_mean_square(x, e):
    sq = (x * x).astype(BF16)
    width = x.shape[1]
    if width <= MXU_DIM:
        return _dot(sq, e[:width, :width])
    parts = [_dot(sq[:, i:i + MXU_DIM], e) for i in range(0, width, MXU_DIM)]
    return jnp.concatenate(parts, axis=1)


def _store_kv_heads(k_dst, v_dst, k, v):
    ones = jnp.ones((v.shape[0], HEAD_DIM), BF16)
    for j in range(N_KV_HEADS):
        js = slice(j * HEAD_DIM, (j + 1) * HEAD_DIM)
        vj = v[:, js].astype(BF16)
        k_dst[j] = k[:, js].astype(BF16)
        v_dst[j, 0, :, 0:HEAD_DIM] = vj
        v_dst[j, 0, :, HEAD_DIM:2 * HEAD_DIM] = ones
        v_dst[j, 1, :, 0:HEAD_DIM] = ones
        v_dst[j, 1, :, HEAD_DIM:2 * HEAD_DIM] = vj


def _attn_kernel(latent, aliased, layer, L, nb, qc, past, *refs):
    x_ref, mod_ref, ng_ref, w_ref, gq_ref, gk_ref, e_ref = refs[:7]
    refs = refs[7:]
    if latent:
        (cos_ref, sin_ref, ck_ref, cv_ref, h_out, mix_out,
         q_s, k_s, v_s, g_s, s_s, mx_s, acc_s, ck_s, cv_s) = refs
    else:
        if aliased:
            refs = refs[2:]
        h_out, mix_out, k_out, v_out, q_s, k_s, v_s, g_s, s_s, mx_s, acc_s = refs
    rows = nb * L

    x = x_ref[...].reshape(rows, D_MODEL)
    m = mod_ref[0, 0]
    shift, scale = m[0:1], m[1:2]
    ms = jnp.mean(x * x, axis=-1, keepdims=True)
    h = (x * lax.rsqrt(ms + EPS)) * (ng_ref[0] * (1.0 + scale)) + shift
    hb = h.astype(BF16)
    h_out[...] = hb.reshape(nb, L, D_MODEL)

    u = _dot(hb, w_ref[0])
    q = u[:, :ATTN_W]
    k = u[:, ATTN_W:ATTN_W + KV_W]
    v = u[:, ATTN_W + KV_W:ATTN_W + 2 * KV_W]
    g_s[...] = u[:, ATTN_W + 2 * KV_W:]
    e = e_ref[...]
    qn = q * lax.rsqrt(_head_mean_square(q, e) + EPS) * gq_ref[0]
    kn = k * lax.rsqrt(_head_mean_square(k, e) + EPS) * gk_ref[0]
    if latent:
        cos, sin = cos_ref[...], sin_ref[...]
        qn = jnp.concatenate([_rope(qn[:, i:i + LANES], cos, sin) for i in range(0, ATTN_W, LANES)], axis=1)
        kn = _rope(kn, cos, sin)
        _store_kv_heads(ck_s, cv_s, ck_ref[0, 0], cv_ref[0, 0])
    else:
        for b in range(nb):
            kb, vb = kn[b * L:(b + 1) * L], v[b * L:(b + 1) * L]
            if aliased:
                k_out[b, 0] = kb
                v_out[b, 0] = vb
            else:
                for slot in range(DEPTH):
                    k_out[b, slot] = kb if slot == layer else jnp.zeros_like(kb)
                    v_out[b, slot] = vb if slot == layer else jnp.zeros_like(vb)
    qb = (qn * ATTN_SCALE).astype(BF16)
    for hd in range(N_Q_HEADS):
        j, g = divmod(hd, Q_PER_KV)
        q_s[j * Q_PER_KV + (g % 2) * HEAD_PAIRS + g // 2] = qb[:, hd * HEAD_DIM:(hd + 1) * HEAD_DIM]
    _store_kv_heads(k_s, v_s, kn, v)

    contract_last = (((1,), (1,)), ((), ()))
    chunks_per_seq = L // qc
    lane = lax.broadcasted_iota(jnp.int32, (qc, LANES), 1)
    groups = [(j, e) for j in range(N_KV_HEADS) for e in range(2)]

    def chunk(c, carry):
        r0 = pl.multiple_of(c * qc, qc)
        k0 = pl.multiple_of((c // chunks_per_seq) * L, L)
        for idx, (j, e) in enumerate(groups):
            slot = j * Q_PER_KV + e * HEAD_PAIRS
            qh = q_s[slot:slot + HEAD_PAIRS, pl.ds(r0, qc), :].reshape(HEAD_PAIRS * qc, HEAD_DIM)
            s = lax.dot_general(qh, k_s[j, pl.ds(k0, L), :], contract_last, preferred_element_type=F32)
            mx = jnp.max(s, axis=-1, keepdims=True)
            if latent:
                s0 = lax.dot_general(qh, ck_s[j], contract_last, preferred_element_type=F32)
                mx = jnp.maximum(mx, jnp.max(s0, axis=-1, keepdims=True))
                s_s[idx, :, 0:past] = s0
                s_s[idx, :, past:] = s
            else:
                s_s[idx] = s
            mx_s[idx] = jnp.broadcast_to(mx, (HEAD_PAIRS * qc, LANES))
        for idx, (j, e) in enumerate(groups):
            p = jnp.exp(s_s[idx] - jnp.tile(mx_s[idx], (1, (past + L) // LANES))).astype(BF16)
            if latent:
                acc = _dot(p[:, :past], cv_s[j, e]) + _dot(p[:, past:], v_s[j, e, pl.ds(k0, L), :])
            else:
                acc = _dot(p, v_s[j, e, pl.ds(k0, L), :])
            acc_s[idx] = acc
        b_idx = c // chunks_per_seq
        row0 = pl.multiple_of((c % chunks_per_seq) * qc, qc)
        for j in range(N_KV_HEADS):
            for t in range(HEAD_PAIRS):
                a_even = acc_s[2 * j, t * qc:(t + 1) * qc, :]
                a_odd = acc_s[2 * j + 1, t * qc:(t + 1) * qc, :]
                out = jnp.where(lane < HEAD_DIM, a_even, a_odd)
                den = pltpu.roll(jnp.where(lane < HEAD_DIM, a_odd, a_even), HEAD_DIM, 1)
                blk = slice((j * HEAD_PAIRS + t) * LANES, (j * HEAD_PAIRS + t + 1) * LANES)
                gate = _silu(g_s[pl.ds(r0, qc), blk])
                mix_out[b_idx, pl.ds(row0, qc), blk] = (out / den * gate).astype(BF16)
        return carry

    lax.fori_loop(0, nb * chunks_per_seq, chunk, 0)


def _attention_branch(latent, layer, nb, x, mod, norm_g, w_in_b, gq, gk, e_avg, rope=None, cache=None, kv_prev=None):
    B, L, _ = x.shape
    rows = nb * L
    assert nb == 1 or not latent
    full = lambda shape: pl.BlockSpec(shape, lambda b: (0,) * len(shape))
    per_b = lambda shape: pl.BlockSpec((nb,) + shape, lambda b: (b,) + (0,) * len(shape))
    per_layer = lambda shape: pl.BlockSpec((1,) + shape, lambda b: (layer,) + (0,) * len(shape))
    mod_spec = pl.BlockSpec((1, 1, 3, D_MODEL), (lambda b: (layer, 1 + b, 0, 0)) if latent else (lambda b: (layer, 0, 0, 0)))
    in_specs = [per_b((L, D_MODEL)), mod_spec, per_layer((1, D_MODEL)), per_layer((D_MODEL, ATTN_COLS)),
                per_layer((1, ATTN_W)), per_layer((1, KV_W)), full((MXU_DIM, MXU_DIM))]
    args = [x, mod, norm_g, w_in_b, gq, gk, e_avg]
    out_specs = [per_b((L, D_MODEL)), per_b((L, ATTN_W))]
    out_shape = [jax.ShapeDtypeStruct((B, L, D_MODEL), BF16), jax.ShapeDtypeStruct((B, L, ATTN_W), BF16)]
    qc = Q_CHUNK_LATENT if latent else Q_CHUNK
    past = cache[0].shape[2] if latent else 0
    assert B % nb == 0 and L % qc == 0 and L % LANES == 0 and past % LANES == 0
    n_groups = 2 * N_KV_HEADS
    scratch = [pltpu.VMEM((N_Q_HEADS, rows, HEAD_DIM), BF16), pltpu.VMEM((N_KV_HEADS, rows, HEAD_DIM), BF16),
               pltpu.VMEM((N_KV_HEADS, 2, rows, 2 * HEAD_DIM), BF16),
               pltpu.VMEM((rows, ATTN_W), F32),
               pltpu.VMEM((n_groups, HEAD_PAIRS * qc, past + L), F32),
               pltpu.VMEM((n_groups, HEAD_PAIRS * qc, LANES), F32),
               pltpu.VMEM((n_groups, HEAD_PAIRS * qc, 2 * HEAD_DIM), F32)]
    aliases = {}
    if latent:
        cache_k, cache_v = cache
        cache_spec = pl.BlockSpec((1, 1, past, KV_W), lambda b: (b, layer, 0, 0))
        in_specs += [full((L, LANES)), full((L, LANES)), cache_spec, cache_spec]
        args += [rope[0], rope[1], cache_k, cache_v]
        scratch += [pltpu.VMEM((N_KV_HEADS, past, HEAD_DIM), BF16),
                    pltpu.VMEM((N_KV_HEADS, 2, past, 2 * HEAD_DIM), BF16)]
    else:
        if kv_prev is not None:
            kv_spec = pl.BlockSpec((nb, 1, L, KV_W), lambda b: (b, layer, 0, 0))
        else:
            kv_spec = pl.BlockSpec((nb, DEPTH, L, KV_W), lambda b: (b, 0, 0, 0))
        kv_shape = jax.ShapeDtypeStruct((B, DEPTH, L, KV_W), F32)
        out_specs += [kv_spec, kv_spec]
        out_shape += [kv_shape, kv_shape]
        if kv_prev is not None:
            in_specs += [pl.BlockSpec(memory_space=pl.ANY)] * 2
            aliases = {len(args): 2, len(args) + 1: 3}
            args += list(kv_prev)
    return pl.pallas_call(
        functools.partial(_attn_kernel, latent, kv_prev is not None, layer, L, nb, qc, past),
        grid=(B // nb,),
        in_specs=in_specs,
        out_specs=out_specs,
        out_shape=out_shape,
        scratch_shapes=scratch,
        input_output_aliases=aliases,
        compiler_params=_cparams(1),
        name=f"attn_branch_L{L}",
    )(*args)


def _shifted(x, offset, period):
    rows = x.shape[0]
    t = lax.broadcasted_iota(jnp.int32, x.shape, 0)
    if rows != period:
        t = t % period
    rolled = pltpu.roll(x, (-offset) % rows, 0)
    valid = (t >= -offset) if offset < 0 else (t < period - offset)
    return jnp.where(valid, rolled, 0.0)


def _hyena_kernel(L, nb, h_ref, w0_ref, w1_ref, w2_ref, wg_ref, sw0_ref, sw1_ref, sw2_ref, sb0_ref, sb1_ref, sb2_ref,
                  fa_ref, fb_ref, fd_ref, bias_ref, fwd_ref, inv_ref, mix_out):
    hb = h_ref[...].reshape(nb * L, D_MODEL)

    def short_conv(w_ref, sw_ref, sb_ref):
        xs = _dot(hb, w_ref[0])
        w = sw_ref[0]
        return sb_ref[0] + _shifted(xs, -1, L) * w[0:1] + xs * w[1:2] + _shifted(xs, 1, L) * w[2:3]

    x0 = short_conv(w0_ref, sw0_ref, sb0_ref)
    x1 = short_conv(w1_ref, sw1_ref, sb1_ref)
    hv = short_conv(w2_ref, sw2_ref, sb2_ref)
    gated = x0 * _silu(_dot(hb, wg_ref[0]))
    z = x1 * hv
    zb = z.astype(BF16)
    fa, fb, fd = fa_ref[0], fb_ref[0], fd_ref[0]
    cb = z.shape[1]
    z_cat = jnp.concatenate([zb[b * L:(b + 1) * L] for b in range(nb)], axis=1)
    zf = _dot(fwd_ref[...], z_cat)
    y_re, y_im = [], []
    for b in range(nb):
        re, im = zf[:L, b * cb:(b + 1) * cb], zf[L:, b * cb:(b + 1) * cb]
        y_re.append((re * fa - im * fb).astype(BF16))
        y_im.append((re * fb + im * fd).astype(BF16))
    y_all = (_dot(inv_ref[:, :L], jnp.concatenate(y_re, axis=1))
             + _dot(inv_ref[:, L:], jnp.concatenate(y_im, axis=1)))
    for b in range(nb):
        rs = slice(b * L, (b + 1) * L)
        y = y_all[:, b * cb:(b + 1) * cb] + z[rs] * bias_ref[0]
        mix_out[b] = (y * gated[rs]).astype(BF16)


def _hyena_branch(layer, nb, hmod, w_in_b, short_w, short_b, filt, bias, fwd, inv):
    B, L, _ = hmod.shape
    cb = COL_BLOCK
    nt = HY_W // cb
    fa, fb, fd = filt
    assert B % nb == 0 and L % LANES == 0
    w_spec = lambda part: pl.BlockSpec((1, D_MODEL, cb), lambda i, b: (layer, 0, HY_BLOCK0 + part * nt + i))
    sw_spec = lambda part: pl.BlockSpec((1, 3, cb), lambda i, b: (layer, 0, part * nt + i))
    sb_spec = lambda part: pl.BlockSpec((1, 1, cb), lambda i, b: (layer, 0, part * nt + i))
    filt_spec = pl.BlockSpec((1, L, cb), lambda i, b: (layer, 0, i))
    return pl.pallas_call(
        functools.partial(_hyena_kernel, L, nb),
        grid=(nt, B // nb),
        in_specs=[
            pl.BlockSpec((nb, L, D_MODEL), lambda i, b: (b, 0, 0)),
            w_spec(0), w_spec(1), w_spec(2), w_spec(3),
            sw_spec(0), sw_spec(1), sw_spec(2),
            sb_spec(0), sb_spec(1), sb_spec(2),
            filt_spec, filt_spec, filt_spec,
            pl.BlockSpec((1, 1, cb), lambda i, b: (layer, 0, i)),
            pl.BlockSpec((2 * L, L), lambda i, b: (0, 0)),
            pl.BlockSpec((L, 2 * L), lambda i, b: (0, 0)),
        ],
        out_specs=pl.BlockSpec((nb, L, cb), lambda i, b: (b, 0, i)),
        out_shape=jax.ShapeDtypeStruct((B, L, HY_W), BF16),
        compiler_params=_cparams(2),
        name=f"hyena_branch_L{L}",
    )(hmod, w_in_b, w_in_b, w_in_b, w_in_b, short_w, short_w, short_w, short_b, short_b, short_b,
      fa, fb, fd, bias, fwd, inv)


def _lru_kernel(latent, aliased, final, layer, L, nb, *refs):
    (h_ref, wx0_ref, wx1_ref, wg0_ref, wg1_ref, cw_ref, cb_ref, wa_ref, ba_ref, wxg_ref, bx_ref, lam_ref,
     x_ref, ma_ref, mh_ref, wo_ref, mod_ref, fg_ref) = refs[:18]
    refs = refs[18:]
    if latent:
        h0_ref, x_out, gate_w_s, a_s, b_s, y_s, g_s, part_s = refs
    else:
        if aliased:
            refs = refs[1:]
        x_out, hl_out, gate_w_s, a_s, b_s, y_s, g_s, part_s = refs
    half_w = LRU_W // 2
    blocks_per_half = half_w // LRU_BS
    n_tiles = L // SUBLANES

    @pl.when(pl.program_id(0) == 0)
    def _build_gate_weights():
        gate_w_s[...] = jnp.zeros(gate_w_s.shape, BF16)
        for half in range(2):
            for kind, (w_ref, d) in enumerate(((wa_ref, 0), (wxg_ref, 0), (wa_ref, 1), (wxg_ref, 1))):
                for j in range(blocks_per_half):
                    blk = (0.5 * w_ref[0, d, half * blocks_per_half + j]).astype(BF16)
                    gate_w_s[half, j * LRU_BS:(j + 1) * LRU_BS,
                             kind * half_w + j * LRU_BS:kind * half_w + (j + 1) * LRU_BS] = blk

    part_s[...] = (_dot(ma_ref[...].reshape(nb * L, ATTN_W), wo_ref[0, 0:ATTN_W, :])
                   + _dot(mh_ref[...].reshape(nb * L, HY_W), wo_ref[0, ATTN_W:ATTN_W + HY_W, :]))

    hb = h_ref[...].reshape(nb * L, D_MODEL)
    xs = jnp.concatenate([_dot(hb, wx0_ref[0]), _dot(hb, wx1_ref[0])], axis=1)
    g_s[:, :half_w] = _dot(hb, wg0_ref[0])
    g_s[:, half_w:] = _dot(hb, wg1_ref[0])
    cw = cw_ref[0]
    xc = (cb_ref[0] + _shifted(xs, -1, L) * cw[0:1] + xs * cw[1:2] + _shifted(xs, 1, L) * cw[2:3]
          + _shifted(xs, 2, L) * cw[3:4])
    xcb = xc.astype(BF16)
    c_half = (-0.5 * LRU_C) * jax.nn.softplus(-lam_ref[0])
    ba, bx = 0.5 * ba_ref[0], 0.5 * bx_ref[0]
    for half in range(2):
        cols = slice(half * half_w, (half + 1) * half_w)
        bias = jnp.concatenate([ba[0:1, cols], bx[0:1, cols], ba[1:2, cols], bx[1:2, cols]], axis=1)
        gates = _dot(xcb[:, cols], gate_w_s[half]) + bias
        xh_half = 0.5 * xc[:, cols]
        for d in range(2):
            t_r = jnp.tanh(gates[:, (2 * d) * half_w:(2 * d + 1) * half_w])
            t_i = jnp.tanh(gates[:, (2 * d + 1) * half_w:(2 * d + 2) * half_w])
            ch = c_half[d:d + 1, cols]
            log_a = ch * t_r + ch
            a = jnp.exp(log_a)
            y = jnp.tanh(log_a) * (-1.0 - a * a)
            root = jnp.where(y > 0.0, y * lax.rsqrt(y), 0.0)
            bcoef = root * (xh_half * t_i + xh_half)
            a_s[d, :, :, cols] = a.reshape(nb * n_tiles, SUBLANES, half_w)
            b_s[d, :, :, cols] = bcoef.reshape(nb * n_tiles, SUBLANES, half_w)

    if latent:
        init = []
        for b in range(nb):
            h0 = h0_ref[b, 0]
            init += [h0[0:1], h0[1:2]]
    else:
        init = [jnp.zeros((1, LRU_W), F32)] * (2 * nb)

    def advance(d, tile, rows_g, h):
        a = [a_s[d, tile, r:r + 1, :] for r in rows_g]
        c = [b_s[d, tile, r:r + 1, :] for r in rows_g]
        a01, c01 = a[1] * a[0], a[1] * c[0] + c[1]
        hs = [a[0] * h + c[0], a01 * h + c01]
        if len(rows_g) == 4:
            a23, c23 = a[3] * a[2], a[3] * c[2] + c[3]
            hs += [a[2] * hs[1] + c[2], (a23 * a01) * h + (a23 * c01 + c23)]
        for r, v in zip(rows_g, hs):
            y_s[d, tile, r:r + 1, :] = v
        return hs[-1]

    group = 4 if nb == 1 else 2

    def tile_step(i, carry):
        carry = list(carry)
        for g0 in range(0, SUBLANES, group):
            fwd_rows = list(range(g0, g0 + group))
            bwd_rows = [SUBLANES - 1 - r for r in fwd_rows]
            for b in range(nb):
                carry[2 * b] = advance(0, b * n_tiles + i, fwd_rows, carry[2 * b])
                carry[2 * b + 1] = advance(1, b * n_tiles + (n_tiles - 1 - i), bwd_rows, carry[2 * b + 1])
        return tuple(carry)

    def two_tiles(i, carry):
        return tile_step(2 * i + 1, tile_step(2 * i, carry))

    last = lax.fori_loop(0, n_tiles // 2, two_tiles, tuple(init))
    if not latent:
        if not aliased:
            hl_out[...] = jnp.zeros(hl_out.shape, F32)
        slot = 0 if aliased else layer
        for b in range(nb):
            hl_out[b, slot, 0:1, :] = last[2 * b]
            hl_out[b, slot, 1:2, :] = last[2 * b + 1]
    y = (y_s[0] + y_s[1]).reshape(nb * L, LRU_W)
    mix_l = (y * _silu(g_s[...])).astype(BF16)
    out = part_s[...] + _dot(mix_l, wo_ref[0, ATTN_W + HY_W:, :])
    xn = x_ref[...].reshape(nb * L, D_MODEL) + mod_ref[0, 0][2:3] * out
    if final:
        ms = jnp.mean(xn * xn, axis=-1, keepdims=True)
        xn = xn * lax.rsqrt(ms + EPS) * fg_ref[...]
    x_out[...] = xn.reshape(nb, L, D_MODEL)


def _lru_out_branch(latent, layer, nb, hmod, w_in_b, conv_w, conv_b, wa, ba, wx, bx, lam, x, mix_a, mix_h, w_out_b,
                    mod, final_g, state=None, hl_prev=None):
    B, L, _ = hmod.shape
    assert B % nb == 0 and L % (2 * SUBLANES) == 0
    final = layer == DEPTH - 1
    cb = COL_BLOCK
    half_w = LRU_W // 2
    rows = nb * L
    per_layer = lambda shape: pl.BlockSpec((1,) + shape, lambda b: (layer,) + (0,) * len(shape))
    w_spec = lambda j: pl.BlockSpec((1, D_MODEL, cb), lambda b: (layer, 0, LRU_BLOCK0 + j))
    gate_blocks = (2, LRU_BLOCKS, LRU_BS, LRU_BS)
    in_specs = [pl.BlockSpec((nb, L, D_MODEL), lambda b: (b, 0, 0)), w_spec(0), w_spec(1), w_spec(2), w_spec(3),
                per_layer((4, LRU_W)), per_layer((1, LRU_W)), per_layer(gate_blocks), per_layer((2, LRU_W)),
                per_layer(gate_blocks), per_layer((2, LRU_W)), per_layer((2, LRU_W))]
    per_b = lambda w: pl.BlockSpec((nb, L, w), lambda b: (b, 0, 0))
    mod_row = (lambda b: (layer, 1 + b, 0, 0)) if latent else (lambda b: (layer, 0, 0, 0))
    in_specs += [per_b(D_MODEL), per_b(ATTN_W), per_b(HY_W), per_layer((D_MIX, D_MODEL)),
                 pl.BlockSpec((1, 1, 3, D_MODEL), mod_row), pl.BlockSpec((1, D_MODEL), lambda b: (0, 0))]
    args = [hmod, w_in_b, w_in_b, w_in_b, w_in_b, conv_w, conv_b, wa, ba, wx, bx, lam,
            x, mix_a, mix_h, w_out_b, mod, final_g]
    out_specs = [per_b(D_MODEL)]
    out_shape = [jax.ShapeDtypeStruct((B, L, D_MODEL), F32)]
    aliases = {}
    if latent:
        in_specs.append(pl.BlockSpec((nb, 1, 2, LRU_W), lambda b: (b, layer, 0, 0)))
        args.append(state)
    else:
        if hl_prev is not None:
            out_specs.append(pl.BlockSpec((nb, 1, 2, LRU_W), lambda b: (b, layer, 0, 0)))
        else:
            out_specs.append(pl.BlockSpec((nb, DEPTH, 2, LRU_W), lambda b: (b, 0, 0, 0)))
        out_shape.append(jax.ShapeDtypeStruct((B, DEPTH, 2, LRU_W), F32))
        if hl_prev is not None:
            in_specs.append(pl.BlockSpec(memory_space=pl.ANY))
            aliases = {len(args): 1}
            args.append(hl_prev)
    scan_buf = pltpu.VMEM((2, rows // SUBLANES, SUBLANES, LRU_W), F32)
    return pl.pallas_call(
        functools.partial(_lru_kernel, latent, hl_prev is not None, final, layer, L, nb),
        grid=(B // nb,),
        in_specs=in_specs,
        out_specs=out_specs,
        out_shape=out_shape,
        scratch_shapes=[pltpu.VMEM((2, half_w, 4 * half_w), BF16), scan_buf, scan_buf, scan_buf,
                        pltpu.VMEM((rows, LRU_W), F32), pltpu.VMEM((rows, D_MODEL), F32)],
        input_output_aliases=aliases,
        compiler_params=_cparams(1),
        name=f"lru_out_L{L}",
    )(*args)


def _mixer_pass(latent, x, p, filt, dft, rope=None, cache=None, state=None):
    B, L, _ = x.shape
    fwd, inv = dft
    nb = 1 if latent else 4
    nb_hyena = min(B, 2) if latent else 4
    nb_lru = 1 if latent else 4
    kv, hl = None, None
    for l in range(DEPTH):
        outs = _attention_branch(latent, l, nb, x, p['mod'], p['norm_g'], p['w_in'], p['gq'], p['gk'], p['e_avg'],
                                 rope, cache, kv)
        hmod, mix_a = outs[0], outs[1]
        if not latent:
            kv = (outs[2], outs[3])
        mix_h = _hyena_branch(l, nb_hyena, hmod, p['w_in'], p['short_w'], p['short_b'], filt, p['hy_bias'], fwd, inv)
        outs = _lru_out_branch(latent, l, nb_lru, hmod, p['w_in'], p['conv_w'], p['conv_b'], p['wa'], p['ba'], p['wx'],
                               p['bx'], p['lam'], x, mix_a, mix_h, p['w_out'], p['mod'], p['final_g'], state, hl)
        x = outs[0]
        if not latent:
            hl = outs[1]
    return x, kv, hl


def kernel(x_prompt, x_sample, cache_k, cache_v, state_lru, c, c_ctx, norm_g, w_ada, b_ada, w_in, q_norm_g, k_norm_g,
           hy_short_w, hy_short_b, hy_filt_w1, hy_filt_b1, hy_filt_w2, hy_filt_b2, hy_filt_w3, hy_filt_freq,
           hy_filt_decay, hy_bias, lru_conv_w, lru_conv_b, lru_wa, lru_ba, lru_wx, lru_bx, lru_lambda, w_out, final_g):
    batch, seq, _ = x_prompt.shape
    dec_batch, dec_seq, _ = x_sample.shape
    past = cache_k.shape[2]

    cvecs = jnp.concatenate([c_ctx[None, :], c, jnp.zeros((MOD_ROWS - 1 - dec_batch, D_MODEL), F32)], axis=0)
    mod = _modulation(cvecs, w_ada, b_ada).reshape(DEPTH, MOD_ROWS, 3, D_MODEL)

    w1p = jnp.pad(hy_filt_w1, ((0, 0), (0, FEAT_PAD - HY_EMB), (0, 0)))
    dft, filt = {}, {}
    for L in (seq, dec_seq):
        dft[L] = tuple(jnp.asarray(m).astype(BF16) for m in _dft_tables(L))
        filt[L] = _filter_spectrum(L, jnp.asarray(_hyena_feats(L)), dft[L][0], w1p, hy_filt_b1[:, None, :],
                                   hy_filt_w2, hy_filt_b2[:, None, :], hy_filt_w3, hy_filt_freq,
                                   hy_filt_decay[:, None, :])

    params = {
        'mod': mod,
        'norm_g': norm_g[:, None, :],
        'w_in': w_in.astype(BF16),
        'gq': jnp.tile(q_norm_g, (1, N_Q_HEADS))[:, None, :],
        'gk': jnp.tile(k_norm_g, (1, N_KV_HEADS))[:, None, :],
        'e_avg': jnp.asarray(_head_average_matrix()).astype(BF16),
        'short_w': hy_short_w, 'short_b': hy_short_b[:, None, :], 'hy_bias': hy_bias[:, None, :],
        'conv_w': lru_conv_w, 'conv_b': lru_conv_b[:, None, :],
        'wa': lru_wa, 'ba': lru_ba, 'wx': lru_wx, 'bx': lru_bx, 'lam': lru_lambda,
        'w_out': w_out.astype(BF16),
        'final_g': final_g[None, :],
    }

    y_prompt, kv, new_lru = _mixer_pass(False, x_prompt, params, filt[seq], dft[seq])
    new_k = kv[0].reshape(batch, DEPTH, seq, N_KV_HEADS, HEAD_DIM)
    new_v = kv[1].reshape(batch, DEPTH, seq, N_KV_HEADS, HEAD_DIM)

    rope = tuple(jnp.asarray(t) for t in _rope_tables(dec_seq))
    cache = (cache_k.reshape(dec_batch, DEPTH, past, KV_W), cache_v.reshape(dec_batch, DEPTH, past, KV_W))
    y_sample, _, _ = _mixer_pass(True, x_sample, params, filt[dec_seq], dft[dec_seq], rope, cache, state_lru)
    return (y_prompt, y_sample, new_k, new_v, new_lru)
```

```python
import functools
import math

import numpy as np
import jax
import jax.numpy as jnp
from jax import lax
from jax.experimental import pallas as pl
from jax.experimental.pallas import tpu as pltpu

F32 = jnp.float32
BF16 = jnp.bfloat16

D_MODEL = 1024
DEPTH = 2
GRID_W = 64
HEAD_DIM = 64
N_Q_HEADS = 8
N_KV_HEADS = 2
Q_PER_KV = N_Q_HEADS // N_KV_HEADS
ATTN_W = N_Q_HEADS * HEAD_DIM
KV_W = N_KV_HEADS * HEAD_DIM
ROPE_THETA = 10000.0
ATTN_SCALE = HEAD_DIM ** -0.5
HY_W = 512
HY_BANDS = 16
HY_EMB = 2 * HY_BANDS + 1
HY_FH = 64
LRU_W = 512
LRU_BLOCKS = 8
LRU_BS = LRU_W // LRU_BLOCKS
LRU_C = 8.0
EPS = 1e-6

ATTN_COLS = ATTN_W + 2 * KV_W + ATTN_W
D_MIX = ATTN_W + HY_W + LRU_W

LANES = 128
SUBLANES = 8
MXU_DIM = 256
COL_BLOCK = MXU_DIM
HY_BLOCK0 = ATTN_COLS // COL_BLOCK
LRU_BLOCK0 = (ATTN_COLS + 4 * HY_W) // COL_BLOCK
Q_CHUNK = 256
Q_CHUNK_LATENT = 256
HEAD_PAIRS = Q_PER_KV // 2
FEAT_PAD = 128
MOD_ROWS = 8
VMEM_LIMIT = 56 * 1024 * 1024


def _cparams(n_axes):
    return pltpu.CompilerParams(dimension_semantics=("arbitrary",) * n_axes, vmem_limit_bytes=VMEM_LIMIT)


def _split_bf16(a):
    hi = a.astype(BF16)
    lo = (a - hi.astype(F32)).astype(BF16)
    return hi, lo


def _dot(a, b):
    return jnp.dot(a, b, preferred_element_type=F32)


def _dot3(a, b):
    a_hi, a_lo = _split_bf16(a)
    b_hi, b_lo = _split_bf16(b)
    return _dot(a_hi, b_hi) + _dot(a_hi, b_lo) + _dot(a_lo, b_hi)


def _silu(x):
    half = 0.5 * x
    return half * jnp.tanh(half) + half


@functools.lru_cache(maxsize=None)
def _dft_tables(L):
    n = 2 * L
    t = np.arange(L)
    kt = (t[:, None] * t[None, :]) % n
    ang = 2.0 * np.pi * kt / n
    cosm, sinm = np.cos(ang), np.sin(ang)
    alt = (-1.0) ** t
    f_s = -sinm
    f_s[0, :] = alt
    fwd = np.concatenate([cosm, f_s], axis=0)
    g_c = 2.0 * cosm.T / n
    g_c[:, 0] = 1.0 / n
    g_s = -2.0 * sinm.T / n
    g_s[:, 0] = alt / n
    inv = np.concatenate([g_c, g_s], axis=1)
    return fwd.astype(np.float32), inv.astype(np.float32)


@functools.lru_cache(maxsize=None)
def _hyena_feats(L):
    t = np.linspace(0.0, 1.0, L)[:, None]
    w = 2.0 * math.pi * np.arange(L)[:, None] / L
    f = np.linspace(1e-4, HY_BANDS - 1, HY_BANDS)[None, :]
    out = np.zeros((L, FEAT_PAD), np.float32)
    out[:, :HY_EMB] = np.concatenate([t, np.cos(f * w), -np.sin(f * w)], axis=-1)
    return out


@functools.lru_cache(maxsize=None)
def _rope_tables(L):
    rows = L // GRID_W
    row = np.repeat(np.arange(rows), GRID_W).astype(np.float64)
    col = np.tile(np.arange(GRID_W), rows).astype(np.float64)
    n_freq = HEAD_DIM // 4
    inv = ROPE_THETA ** (-np.arange(n_freq) / n_freq)
    ang = np.concatenate([row[:, None] * inv, col[:, None] * inv], axis=-1)
    cos = np.repeat(np.cos(ang), 2, axis=-1)
    sin = np.repeat(np.sin(ang), 2, axis=-1) * np.tile(np.array([-1.0, 1.0]), HEAD_DIM // 2)
    reps = LANES // HEAD_DIM
    return np.tile(cos, (1, reps)).astype(np.float32), np.tile(sin, (1, reps)).astype(np.float32)


@functools.lru_cache(maxsize=None)
def _head_average_matrix():
    return np.kron(np.eye(MXU_DIM // HEAD_DIM), np.full((HEAD_DIM, HEAD_DIM), 1.0 / HEAD_DIM)).astype(np.float32)


def _mod_kernel(c_ref, w_ref, b_ref, o_ref):
    s = _silu(c_ref[...])
    o_ref[0] = _dot3(s, w_ref[0]) + b_ref[0]


def _modulation(cvecs, w_ada, b_ada):
    tn = D_MODEL
    return pl.pallas_call(
        _mod_kernel,
        grid=(DEPTH, 3 * D_MODEL // tn),
        in_specs=[
            pl.BlockSpec((MOD_ROWS, D_MODEL), lambda l, j: (0, 0)),
            pl.BlockSpec((1, D_MODEL, tn), lambda l, j: (l, 0, j)),
            pl.BlockSpec((1, 1, tn), lambda l, j: (l, 0, j)),
        ],
        out_specs=pl.BlockSpec((1, MOD_ROWS, tn), lambda l, j: (l, 0, j)),
        out_shape=jax.ShapeDtypeStruct((DEPTH, MOD_ROWS, 3 * D_MODEL), F32),
        compiler_params=_cparams(2),
        name="adaln_mod",
    )(cvecs, w_ada, b_ada.reshape(DEPTH, 1, 3 * D_MODEL))


def _filter_kernel(L, feats_ref, w1_ref, b1_ref, w2_ref, b2_ref, w3_ref, freq_ref, decay_ref,
                   fwd_ref, a_ref, b_ref, d_ref):
    feats = feats_ref[...]
    t = feats[:, 0:1]
    freq = freq_ref[0]
    hdn = jnp.sin(freq[0:1] * (_dot3(feats, w1_ref[0]) + b1_ref[0]))
    hdn = jnp.sin(freq[1:2] * (_dot3(hdn, w2_ref[0]) + b2_ref[0]))
    h = _dot3(hdn, w3_ref[0]) * jnp.exp(-t * decay_ref[0])
    row = lax.broadcasted_iota(jnp.int32, (L, HY_W), 0)
    h_fwd = h[:, :HY_W]
    h_bwd = jnp.where(row >= 1, h[:, HY_W:], 0.0)
    inv_norm = 1.0 / jnp.sum(jnp.abs(h_fwd) + jnp.abs(h_bwd), axis=0, keepdims=True)
    even = h_fwd + h_bwd
    odd = h_fwd - h_bwd
    h_re = _dot(fwd_ref[0:L, :], even.astype(BF16)) * inv_norm
    h_im = _dot(fwd_ref[L:2 * L, :], odd.astype(BF16)) * inv_norm
    alt = jnp.where(row % 2 == 0, 1.0, -1.0)
    nyq = jnp.sum(even * alt, axis=0, keepdims=True) * inv_norm
    a_ref[0] = h_re
    b_ref[0] = jnp.where(row >= 1, h_im, 0.0)
    d_ref[0] = jnp.where(row >= 1, h_re, nyq)


def _filter_spectrum(L, feats, fwd, w1p, b1, w2, b2, w3, freq, decay):
    full = lambda shape: pl.BlockSpec(shape, lambda l: (0,) * len(shape))
    per_layer = lambda shape: pl.BlockSpec((1,) + shape, lambda l: (l,) + (0,) * len(shape))
    out = jax.ShapeDtypeStruct((DEPTH, L, HY_W), F32)
    return pl.pallas_call(
        functools.partial(_filter_kernel, L),
        grid=(DEPTH,),
        in_specs=[
            full((L, FEAT_PAD)),
            per_layer((FEAT_PAD, HY_FH)), per_layer((1, HY_FH)),
            per_layer((HY_FH, HY_FH)), per_layer((1, HY_FH)),
            per_layer((HY_FH, 2 * HY_W)), per_layer((2, HY_FH)), per_layer((1, 2 * HY_W)),
            full((2 * L, L)),
        ],
        out_specs=[per_layer((L, HY_W))] * 3,
        out_shape=[out, out, out],
        compiler_params=_cparams(1),
        name=f"hyena_filter_L{L}",
    )(feats, w1p, b1, w2, b2, w3, freq, decay, fwd)


def _rope(x, cos, sin):
    lane = lax.broadcasted_iota(jnp.int32, x.shape, 1)
    partner = jnp.where(lane % 2 == 0, pltpu.roll(x, LANES - 1, 1), pltpu.roll(x, 1, 1))
    return x * cos + partner * sin


def _head_mean_square(x, e):
    sq = (x * x).astype(BF16)
    width = x.shape[1]
    if width <= MXU_DIM:
        return _dot(sq, e[:width, :width])
    parts = [_dot(sq[:, i:i + MXU_DIM], e) for i in range(0, width, MXU_DIM)]
    return jnp.concatenate(parts, axis=1)


def _store_kv_heads(k_dst, v_dst, k, v):
    ones = jnp.ones((v.shape[0], HEAD_DIM), BF16)
    for j in range(N_KV_HEADS):
        js = slice(j * HEAD_DIM, (j + 1) * HEAD_DIM)
        vj = v[:, js].astype(BF16)
        k_dst[j] = k[:, js].astype(BF16)
        v_dst[j, 0, :, 0:HEAD_DIM] = vj
        v_dst[j, 0, :, HEAD_DIM:2 * HEAD_DIM] = ones
        v_dst[j, 1, :, 0:HEAD_DIM] = ones
        v_dst[j, 1, :, HEAD_DIM:2 * HEAD_DIM] = vj


def _attn_kernel(latent, aliased, layer, L, nb, qc, past, *refs):
    x_ref, mod_ref, ng_ref, w_ref, gq_ref, gk_ref, e_ref = refs[:7]
    refs = refs[7:]
    if latent:
        (cos_ref, sin_ref, ck_ref, cv_ref, h_out, mix_out,
         q_s, k_s, v_s, g_s, s_s, mx_s, acc_s, ck_s, cv_s) = refs
    else:
        if aliased:
            refs = refs[2:]
        h_out, mix_out, k_out, v_out, q_s, k_s, v_s, g_s, s_s, mx_s, acc_s = refs
    rows = nb * L

    x = x_ref[...].reshape(rows, D_MODEL)
    m = mod_ref[0, 0]
    shift, scale = m[0:1], m[1:2]
    ms = jnp.mean(x * x, axis=-1, keepdims=True)
    h = (x * lax.rsqrt(ms + EPS)) * (ng_ref[0] * (1.0 + scale)) + shift
    hb = h.astype(BF16)
    h_out[...] = hb.reshape(nb, L, D_MODEL)

    u = _dot(hb, w_ref[0])
    q = u[:, :ATTN_W]
    k = u[:, ATTN_W:ATTN_W + KV_W]
    v = u[:, ATTN_W + KV_W:ATTN_W + 2 * KV_W]
    g_s[...] = u[:, ATTN_W + 2 * KV_W:]
    e = e_ref[...]
    qn = q * lax.rsqrt(_head_mean_square(q, e) + EPS) * gq_ref[0]
    kn = k * lax.rsqrt(_head_mean_square(k, e) + EPS) * gk_ref[0]
    if latent:
        cos, sin = cos_ref[...], sin_ref[...]
        qn = jnp.concatenate([_rope(qn[:, i:i + LANES], cos, sin) for i in range(0, ATTN_W, LANES)], axis=1)
        kn = _rope(kn, cos, sin)
        _store_kv_heads(ck_s, cv_s, ck_ref[0, 0], cv_ref[0, 0])
    else:
        for b in range(nb):
            kb, vb = kn[b * L:(b + 1) * L], v[b * L:(b + 1) * L]
            if aliased:
                k_out[b, 0] = kb
                v_out[b, 0] = vb
            else:
                for slot in range(DEPTH):
                    k_out[b, slot] = kb if slot == layer else jnp.zeros_like(kb)
                    v_out[b, slot] = vb if slot == layer else jnp.zeros_like(vb)
    qb = (qn * ATTN_SCALE).astype(BF16)
    for hd in range(N_Q_HEADS):
        j, g = divmod(hd, Q_PER_KV)
        q_s[j * Q_PER_KV + (g % 2) * HEAD_PAIRS + g // 2] = qb[:, hd * HEAD_DIM:(hd + 1) * HEAD_DIM]
    _store_kv_heads(k_s, v_s, kn, v)

    contract_last = (((1,), (1,)), ((), ()))
    chunks_per_seq = L // qc
    lane = lax.broadcasted_iota(jnp.int32, (qc, LANES), 1)
    groups = [(j, e) for j in range(N_KV_HEADS) for e in range(2)]

    def chunk(c, carry):
        r0 = pl.multiple_of(c * qc, qc)
        k0 = pl.multiple_of((c // chunks_per_seq) * L, L)
        for idx, (j, e) in enumerate(groups):
            slot = j * Q_PER_KV + e * HEAD_PAIRS
            qh = q_s[slot:slot + HEAD_PAIRS, pl.ds(r0, qc), :].reshape(HEAD_PAIRS * qc, HEAD_DIM)
            s = lax.dot_general(qh, k_s[j, pl.ds(k0, L), :], contract_last, preferred_element_type=F32)
            mx = jnp.max(s, axis=-1, keepdims=True)
            if latent:
                s0 = lax.dot_general(qh, ck_s[j], contract_last, preferred_element_type=F32)
                mx = jnp.maximum(mx, jnp.max(s0, axis=-1, keepdims=True))
                s_s[idx, :, 0:past] = s0
                s_s[idx, :, past:] = s
            else:
                s_s[idx] = s
            mx_s[idx] = jnp.broadcast_to(mx, (HEAD_PAIRS * qc, LANES))
        for idx, (j, e) in enumerate(groups):
            p = jnp.exp(s_s[idx] - jnp.tile(mx_s[idx], (1, (past + L) // LANES))).astype(BF16)
            if latent:
                acc = _dot(p[:, :past], cv_s[j, e]) + _dot(p[:, past:], v_s[j, e, pl.ds(k0, L), :])
            else:
                acc = _dot(p, v_s[j, e, pl.ds(k0, L), :])
            acc_s[idx] = acc
        b_idx = c // chunks_per_seq
        row0 = pl.multiple_of((c % chunks_per_seq) * qc, qc)
        for j in range(N_KV_HEADS):
            for t in range(HEAD_PAIRS):
                a_even = acc_s[2 * j, t * qc:(t + 1) * qc, :]
                a_odd = acc_s[2 * j + 1, t * qc:(t + 1) * qc, :]
                out = jnp.where(lane < HEAD_DIM, a_even, a_odd)
                den = pltpu.roll(jnp.where(lane < HEAD_DIM, a_odd, a_even), HEAD_DIM, 1)
                blk = slice((j * HEAD_PAIRS + t) * LANES, (j * HEAD_PAIRS + t + 1) * LANES)
                gate = _silu(g_s[pl.ds(r0, qc), blk])
                mix_out[b_idx, pl.ds(row0, qc), blk] = (out / den * gate).astype(BF16)
        return carry

    lax.fori_loop(0, nb * chunks_per_seq, chunk, 0)


def _attention_branch(latent, layer, nb, x, mod, norm_g, w_in_b, gq, gk, e_avg, rope=None, cache=None, kv_prev=None):
    B, L, _ = x.shape
    rows = nb * L
    assert nb == 1 or not latent
    full = lambda shape: pl.BlockSpec(shape, lambda b: (0,) * len(shape))
    per_b = lambda shape: pl.BlockSpec((nb,) + shape, lambda b: (b,) + (0,) * len(shape))
    per_layer = lambda shape: pl.BlockSpec((1,) + shape, lambda b: (layer,) + (0,) * len(shape))
    mod_spec = pl.BlockSpec((1, 1, 3, D_MODEL), (lambda b: (layer, 1 + b, 0, 0)) if latent else (lambda b: (layer, 0, 0, 0)))
    in_specs = [per_b((L, D_MODEL)), mod_spec, per_layer((1, D_MODEL)), per_layer((D_MODEL, ATTN_COLS)),
                per_layer((1, ATTN_W)), per_layer((1, KV_W)), full((MXU_DIM, MXU_DIM))]
    args = [x, mod, norm_g, w_in_b, gq, gk, e_avg]
    out_specs = [per_b((L, D_MODEL)), per_b((L, ATTN_W))]
    out_shape = [jax.ShapeDtypeStruct((B, L, D_MODEL), BF16), jax.ShapeDtypeStruct((B, L, ATTN_W), BF16)]
    qc = Q_CHUNK_LATENT if latent else Q_CHUNK
    past = cache[0].shape[2] if latent else 0
    assert B % nb == 0 and L % qc == 0 and L % LANES == 0 and past % LANES == 0
    n_groups = 2 * N_KV_HEADS
    scratch = [pltpu.VMEM((N_Q_HEADS, rows, HEAD_DIM), BF16), pltpu.VMEM((N_KV_HEADS, rows, HEAD_DIM), BF16),
               pltpu.VMEM((N_KV_HEADS, 2, rows, 2 * HEAD_DIM), BF16),
               pltpu.VMEM((rows, ATTN_W), F32),
               pltpu.VMEM((n_groups, HEAD_PAIRS * qc, past + L), F32),
               pltpu.VMEM((n_groups, HEAD_PAIRS * qc, LANES), F32),
               pltpu.VMEM((n_groups, HEAD_PAIRS * qc, 2 * HEAD_DIM), F32)]
    aliases = {}
    if latent:
        cache_k, cache_v = cache
        cache_spec = pl.BlockSpec((1, 1, past, KV_W), lambda b: (b, layer, 0, 0))
        in_specs += [full((L, LANES)), full((L, LANES)), cache_spec, cache_spec]
        args += [rope[0], rope[1], cache_k, cache_v]
        scratch += [pltpu.VMEM((N_KV_HEADS, past, HEAD_DIM), BF16),
                    pltpu.VMEM((N_KV_HEADS, 2, past, 2 * HEAD_DIM), BF16)]
    else:
        if kv_prev is not None:
            kv_spec = pl.BlockSpec((nb, 1, L, KV_W), lambda b: (b, layer, 0, 0))
        else:
            kv_spec = pl.BlockSpec((nb, DEPTH, L, KV_W), lambda b: (b, 0, 0, 0))
        kv_shape = jax.ShapeDtypeStruct((B, DEPTH, L, KV_W), F32)
        out_specs += [kv_spec, kv_spec]
        out_shape += [kv_shape, kv_shape]
        if kv_prev is not None:
            in_specs += [pl.BlockSpec(memory_space=pl.ANY)] * 2
            aliases = {len(args): 2, len(args) + 1: 3}
            args += list(kv_prev)
    return pl.pallas_call(
        functools.partial(_attn_kernel, latent, kv_prev is not None, layer, L, nb, qc, past),
        grid=(B // nb,),
        in_specs=in_specs,
        out_specs=out_specs,
        out_shape=out_shape,
        scratch_shapes=scratch,
        input_output_aliases=aliases,
        compiler_params=_cparams(1),
        name=f"attn_branch_L{L}",
    )(*args)


def _shifted(x, offset, period):
    rows = x.shape[0]
    t = lax.broadcasted_iota(jnp.int32, x.shape, 0)
    if rows != period:
        t = t % period
    rolled = pltpu.roll(x, (-offset) % rows, 0)
    valid = (t >= -offset) if offset < 0 else (t < period - offset)
    return jnp.where(valid, rolled, 0.0)


def _hyena_kernel(L, nb, nw, h_ref, *refs):
    w_refs = refs[:4 * nw]
    (sw0_ref, sw1_ref, sw2_ref, sb0_ref, sb1_ref, sb2_ref,
     fa_ref, fb_ref, fd_ref, bias_ref, fwd_ref, inv_ref, mix_out) = refs[4 * nw:]
    hb = h_ref[...].reshape(nb * L, D_MODEL)

    def project(part):
        cols = [_dot(hb, w_refs[part * nw + i][0]) for i in range(nw)]
        return cols[0] if nw == 1 else jnp.concatenate(cols, axis=1)

    def short_conv(part, sw_ref, sb_ref):
        xs = project(part)
        w = sw_ref[0]
        return sb_ref[0] + _shifted(xs, -1, L) * w[0:1] + xs * w[1:2] + _shifted(xs, 1, L) * w[2:3]

    x0 = short_conv(0, sw0_ref, sb0_ref)
    x1 = short_conv(1, sw1_ref, sb1_ref)
    hv = short_conv(2, sw2_ref, sb2_ref)
    gated = x0 * _silu(project(3))
    z = x1 * hv
    zb = z.astype(BF16)
    fa, fb, fd = fa_ref[0], fb_ref[0], fd_ref[0]
    cb = z.shape[1]
    z_cat = jnp.concatenate([zb[b * L:(b + 1) * L] for b in range(nb)], axis=1)
    zf = _dot(fwd_ref[...], z_cat)
    y_re, y_im = [], []
    for b in range(nb):
        re, im = zf[:L, b * cb:(b + 1) * cb], zf[L:, b * cb:(b + 1) * cb]
        y_re.append((re * fa - im * fb).astype(BF16))
        y_im.append((re * fb + im * fd).astype(BF16))
    y_all = (_dot(inv_ref[:, :L], jnp.concatenate(y_re, axis=1))
             + _dot(inv_ref[:, L:], jnp.concatenate(y_im, axis=1)))
    for b in range(nb):
        rs = slice(b * L, (b + 1) * L)
        y = y_all[:, b * cb:(b + 1) * cb] + z[rs] * bias_ref[0]
        mix_out[b] = (y * gated[rs]).astype(BF16)


def _hyena_branch(layer, nb, nw, hmod, w_in_b, short_w, short_b, filt, bias, fwd, inv):
    B, L, _ = hmod.shape
    cb = COL_BLOCK
    width = nw * cb
    nt = HY_W // width
    blocks_per_part = HY_W // cb
    fa, fb, fd = filt
    assert B % nb == 0 and L % LANES == 0 and HY_W % width == 0
    w_specs = [pl.BlockSpec((1, D_MODEL, cb),
                            functools.partial(lambda i, b, off: (layer, 0, off + i * nw),
                                              off=HY_BLOCK0 + part * blocks_per_part + j))
               for part in range(4) for j in range(nw)]
    sw_spec = lambda part: pl.BlockSpec((1, 3, width), lambda i, b: (layer, 0, part * nt + i))
    sb_spec = lambda part: pl.BlockSpec((1, 1, width), lambda i, b: (layer, 0, part * nt + i))
    filt_spec = pl.BlockSpec((1, L, width), lambda i, b: (layer, 0, i))
    return pl.pallas_call(
        functools.partial(_hyena_kernel, L, nb, nw),
        grid=(nt, B // nb),
        in_specs=[pl.BlockSpec((nb, L, D_MODEL), lambda i, b: (b, 0, 0))] + w_specs + [
            sw_spec(0), sw_spec(1), sw_spec(2),
            sb_spec(0), sb_spec(1), sb_spec(2),
            filt_spec, filt_spec, filt_spec,
            pl.BlockSpec((1, 1, width), lambda i, b: (layer, 0, i)),
            pl.BlockSpec((2 * L, L), lambda i, b: (0, 0)),
            pl.BlockSpec((L, 2 * L), lambda i, b: (0, 0)),
        ],
        out_specs=pl.BlockSpec((nb, L, width), lambda i, b: (b, 0, i)),
        out_shape=jax.ShapeDtypeStruct((B, L, HY_W), BF16),
        compiler_params=_cparams(2),
        name=f"hyena_branch_L{L}",
    )(hmod, *([w_in_b] * (4 * nw)), short_w, short_w, short_w, short_b, short_b, short_b,
      fa, fb, fd, bias, fwd, inv)


def _lru_kernel(latent, aliased, final, layer, L, nb, *refs):
    (h_ref, wx0_ref, wx1_ref, wg0_ref, wg1_ref, cw_ref, cb_ref, wa_ref, ba_ref, wxg_ref, bx_ref, lam_ref,
     x_ref, ma_ref, mh_ref, wo_ref, mod_ref, fg_ref) = refs[:18]
    refs = refs[18:]
    if latent:
        h0_ref, x_out, gate_w_s, a_s, b_s, y_s, g_s, part_s = refs
    else:
        if aliased:
            refs = refs[1:]
        x_out, hl_out, gate_w_s, a_s, b_s, y_s, g_s, part_s = refs
    half_w = LRU_W // 2
    blocks_per_half = half_w // LRU_BS
    n_tiles = L // SUBLANES

    @pl.when(pl.program_id(0) == 0)
    def _build_gate_weights():
        gate_w_s[...] = jnp.zeros(gate_w_s.shape, BF16)
        for half in range(2):
            for kind, (w_ref, d) in enumerate(((wa_ref, 0), (wxg_ref, 0), (wa_ref, 1), (wxg_ref, 1))):
                for j in range(blocks_per_half):
                    blk = (0.5 * w_ref[0, d, half * blocks_per_half + j]).astype(BF16)
                    gate_w_s[half, j * LRU_BS:(j + 1) * LRU_BS,
                             kind * half_w + j * LRU_BS:kind * half_w + (j + 1) * LRU_BS] = blk

    part_s[...] = (_dot(ma_ref[...].reshape(nb * L, ATTN_W), wo_ref[0, 0:ATTN_W, :])
                   + _dot(mh_ref[...].reshape(nb * L, HY_W), wo_ref[0, ATTN_W:ATTN_W + HY_W, :]))

    hb = h_ref[...].reshape(nb * L, D_MODEL)
    xs = jnp.concatenate([_dot(hb, wx0_ref[0]), _dot(hb, wx1_ref[0])], axis=1)
    g_s[:, :half_w] = _dot(hb, wg0_ref[0])
    g_s[:, half_w:] = _dot(hb, wg1_ref[0])
    cw = cw_ref[0]
    xc = (cb_ref[0] + _shifted(xs, -1, L) * cw[0:1] + xs * cw[1:2] + _shifted(xs, 1, L) * cw[2:3]
          + _shifted(xs, 2, L) * cw[3:4])
    xcb = xc.astype(BF16)
    c_half = (-0.5 * LRU_C) * jax.nn.softplus(-lam_ref[0])
    ba, bx = 0.5 * ba_ref[0], 0.5 * bx_ref[0]
    for half in range(2):
        cols = slice(half * half_w, (half + 1) * half_w)
        bias = jnp.concatenate([ba[0:1, cols], bx[0:1, cols], ba[1:2, cols], bx[1:2, cols]], axis=1)
        gates = _dot(xcb[:, cols], gate_w_s[half]) + bias
        xh_half = 0.5 * xc[:, cols]
        for d in range(2):
            t_r = jnp.tanh(gates[:, (2 * d) * half_w:(2 * d + 1) * half_w])
            t_i = jnp.tanh(gates[:, (2 * d + 1) * half_w:(2 * d + 2) * half_w])
            ch = c_half[d:d + 1, cols]
            log_a = ch * t_r + ch
            a = jnp.exp(log_a)
            y = jnp.tanh(log_a) * (-1.0 - a * a)
            root = jnp.where(y > 0.0, y * lax.rsqrt(y), 0.0)
            bcoef = root * (xh_half * t_i + xh_half)
            a_s[d, :, :, cols] = a.reshape(nb * n_tiles, SUBLANES, half_w)
            b_s[d, :, :, cols] = bcoef.reshape(nb * n_tiles, SUBLANES, half_w)

    if latent:
        init = []
        for b in range(nb):
            h0 = h0_ref[b, 0]
            init += [h0[0:1], h0[1:2]]
    else:
        init = [jnp.zeros((1, LRU_W), F32)] * (2 * nb)

    def advance(d, tile, rows_g, h):
        a = [a_s[d, tile, r:r + 1, :] for r in rows_g]
        c = [b_s[d, tile, r:r + 1, :] for r in rows_g]
        a01, c01 = a[1] * a[0], a[1] * c[0] + c[1]
        hs = [a[0] * h + c[0], a01 * h + c01]
        if len(rows_g) == 4:
            a23, c23 = a[3] * a[2], a[3] * c[2] + c[3]
            hs += [a[2] * hs[1] + c[2], (a23 * a01) * h + (a23 * c01 + c23)]
        for r, v in zip(rows_g, hs):
            y_s[d, tile, r:r + 1, :] = v
        return hs[-1]

    group = 4 if nb == 1 else 2

    def tile_step(i, carry):
        carry = list(carry)
        for g0 in range(0, SUBLANES, group):
            fwd_rows = list(range(g0, g0 + group))
            bwd_rows = [SUBLANES - 1 - r for r in fwd_rows]
            for b in range(nb):
                carry[2 * b] = advance(0, b * n_tiles + i, fwd_rows, carry[2 * b])
                carry[2 * b + 1] = advance(1, b * n_tiles + (n_tiles - 1 - i), bwd_rows, carry[2 * b + 1])
        return tuple(carry)

    def two_tiles(i, carry):
        return tile_step(2 * i + 1, tile_step(2 * i, carry))

    last = lax.fori_loop(0, n_tiles // 2, two_tiles, tuple(init))
    if not latent:
        if not aliased:
            hl_out[...] = jnp.zeros(hl_out.shape, F32)
        slot = 0 if aliased else layer
        for b in range(nb):
            hl_out[b, slot, 0:1, :] = last[2 * b]
            hl_out[b, slot, 1:2, :] = last[2 * b + 1]
    y = (y_s[0] + y_s[1]).reshape(nb * L, LRU_W)
    mix_l = (y * _silu(g_s[...])).astype(BF16)
    out = part_s[...] + _dot(mix_l, wo_ref[0, ATTN_W + HY_W:, :])
    xn = x_ref[...].reshape(nb * L, D_MODEL) + mod_ref[0, 0][2:3] * out
    if final:
        ms = jnp.mean(xn * xn, axis=-1, keepdims=True)
        xn = xn * lax.rsqrt(ms + EPS) * fg_ref[...]
    x_out[...] = xn.reshape(nb, L, D_MODEL)


def _lru_out_branch(latent, layer, nb, hmod, w_in_b, conv_w, conv_b, wa, ba, wx, bx, lam, x, mix_a, mix_h, w_out_b,
                    mod, final_g, state=None, hl_prev=None):
    B, L, _ = hmod.shape
    assert B % nb == 0 and L % (2 * SUBLANES) == 0
    final = layer == DEPTH - 1
    cb = COL_BLOCK
    half_w = LRU_W // 2
    rows = nb * L
    per_layer = lambda shape: pl.BlockSpec((1,) + shape, lambda b: (layer,) + (0,) * len(shape))
    w_spec = lambda j: pl.BlockSpec((1, D_MODEL, cb), lambda b: (layer, 0, LRU_BLOCK0 + j))
    gate_blocks = (2, LRU_BLOCKS, LRU_BS, LRU_BS)
    in_specs = [pl.BlockSpec((nb, L, D_MODEL), lambda b: (b, 0, 0)), w_spec(0), w_spec(1), w_spec(2), w_spec(3),
                per_layer((4, LRU_W)), per_layer((1, LRU_W)), per_layer(gate_blocks), per_layer((2, LRU_W)),
                per_layer(gate_blocks), per_layer((2, LRU_W)), per_layer((2, LRU_W))]
    per_b = lambda w: pl.BlockSpec((nb, L, w), lambda b: (b, 0, 0))
    mod_row = (lambda b: (layer, 1 + b, 0, 0)) if latent else (lambda b: (layer, 0, 0, 0))
    in_specs += [per_b(D_MODEL), per_b(ATTN_W), per_b(HY_W), per_layer((D_MIX, D_MODEL)),
                 pl.BlockSpec((1, 1, 3, D_MODEL), mod_row), pl.BlockSpec((1, D_MODEL), lambda b: (0, 0))]
    args = [hmod, w_in_b, w_in_b, w_in_b, w_in_b, conv_w, conv_b, wa, ba, wx, bx, lam,
            x, mix_a, mix_h, w_out_b, mod, final_g]
    out_specs = [per_b(D_MODEL)]
    out_shape = [jax.ShapeDtypeStruct((B, L, D_MODEL), F32)]
    aliases = {}
    if latent:
        in_specs.append(pl.BlockSpec((nb, 1, 2, LRU_W), lambda b: (b, layer, 0, 0)))
        args.append(state)
    else:
        if hl_prev is not None:
            out_specs.append(pl.BlockSpec((nb, 1, 2, LRU_W), lambda b: (b, layer, 0, 0)))
        else:
            out_specs.append(pl.BlockSpec((nb, DEPTH, 2, LRU_W), lambda b: (b, 0, 0, 0)))
        out_shape.append(jax.ShapeDtypeStruct((B, DEPTH, 2, LRU_W), F32))
        if hl_prev is not None:
            in_specs.append(pl.BlockSpec(memory_space=pl.ANY))
            aliases = {len(args): 1}
            args.append(hl_prev)
    scan_buf = pltpu.VMEM((2, rows // SUBLANES, SUBLANES, LRU_W), F32)
    return pl.pallas_call(
        functools.partial(_lru_kernel, latent, hl_prev is not None, final, layer, L, nb),
        grid=(B // nb,),
        in_specs=in_specs,
        out_specs=out_specs,
        out_shape=out_shape,
        scratch_shapes=[pltpu.VMEM((2, half_w, 4 * half_w), BF16), scan_buf, scan_buf, scan_buf,
                        pltpu.VMEM((rows, LRU_W), F32), pltpu.VMEM((rows, D_MODEL), F32)],
        input_output_aliases=aliases,
        compiler_params=_cparams(1),
        name=f"lru_out_L{L}",
    )(*args)


def _mixer_pass(latent, x, p, filt, dft, rope=None, cache=None, state=None):
    B, L, _ = x.shape
    fwd, inv = dft
    nb = 1 if latent else 4
    nb_hyena = min(B, 2) if latent else 4
    nw_hyena = 1 if latent else HY_W // COL_BLOCK
    nb_lru = 1 if latent else 2
    kv, hl = None, None
    for l in range(DEPTH):
        outs = _attention_branch(latent, l, nb, x, p['mod'], p['norm_g'], p['w_in'], p['gq'], p['gk'], p['e_avg'],
                                 rope, cache, kv)
        hmod, mix_a = outs[0], outs[1]
        if not latent:
            kv = (outs[2], outs[3])
        mix_h = _hyena_branch(l, nb_hyena, nw_hyena, hmod, p['w_in'], p['short_w'], p['short_b'], filt, p['hy_bias'],
                              fwd, inv)
        outs = _lru_out_branch(latent, l, nb_lru, hmod, p['w_in'], p['conv_w'], p['conv_b'], p['wa'], p['ba'], p['wx'],
                               p['bx'], p['lam'], x, mix_a, mix_h, p['w_out'], p['mod'], p['final_g'], state, hl)
        x = outs[0]
        if not latent:
            hl = outs[1]
    return x, kv, hl


def kernel(x_prompt, x_sample, cache_k, cache_v, state_lru, c, c_ctx, norm_g, w_ada, b_ada, w_in, q_norm_g, k_norm_g,
           hy_short_w, hy_short_b, hy_filt_w1, hy_filt_b1, hy_filt_w2, hy_filt_b2, hy_filt_w3, hy_filt_freq,
           hy_filt_decay, hy_bias, lru_conv_w, lru_conv_b, lru_wa, lru_ba, lru_wx, lru_bx, lru_lambda, w_out, final_g):
    batch, seq, _ = x_prompt.shape
    dec_batch, dec_seq, _ = x_sample.shape
    past = cache_k.shape[2]

    cvecs = jnp.concatenate([c_ctx[None, :], c, jnp.zeros((MOD_ROWS - 1 - dec_batch, D_MODEL), F32)], axis=0)
    mod = _modulation(cvecs, w_ada, b_ada).reshape(DEPTH, MOD_ROWS, 3, D_MODEL)

    w1p = jnp.pad(hy_filt_w1, ((0, 0), (0, FEAT_PAD - HY_EMB), (0, 0)))
    dft, filt = {}, {}
    for L in (seq, dec_seq):
        dft[L] = tuple(jnp.asarray(m).astype(BF16) for m in _dft_tables(L))
        filt[L] = _filter_spectrum(L, jnp.asarray(_hyena_feats(L)), dft[L][0], w1p, hy_filt_b1[:, None, :],
                                   hy_filt_w2, hy_filt_b2[:, None, :], hy_filt_w3, hy_filt_freq,
                                   hy_filt_decay[:, None, :])

    params = {
        'mod': mod,
        'norm_g': norm_g[:, None, :],
        'w_in': w_in.astype(BF16),
        'gq': jnp.tile(q_norm_g, (1, N_Q_HEADS))[:, None, :],
        'gk': jnp.tile(k_norm_g, (1, N_KV_HEADS))[:, None, :],
        'e_avg': jnp.asarray(_head_average_matrix()).astype(BF16),
        'short_w': hy_short_w, 'short_b': hy_short_b[:, None, :], 'hy_bias': hy_bias[:, None, :],
        'conv_w': lru_conv_w, 'conv_b': lru_conv_b[:, None, :],
        'wa': lru_wa, 'ba': lru_ba, 'wx': lru_wx, 'bx': lru_bx, 'lam': lru_lambda,
        'w_out': w_out.astype(BF16),
        'final_g': final_g[None, :],
    }

    y_prompt, kv, new_lru = _mixer_pass(False, x_prompt, params, filt[seq], dft[seq])
    new_k = kv[0].reshape(batch, DEPTH, seq, N_KV_HEADS, HEAD_DIM)
    new_v = kv[1].reshape(batch, DEPTH, seq, N_KV_HEADS, HEAD_DIM)

    rope = tuple(jnp.asarray(t) for t in _rope_tables(dec_seq))
    cache = (cache_k.reshape(dec_batch, DEPTH, past, KV_W), cache_v.reshape(dec_batch, DEPTH, past, KV_W))
    y_sample, _, _ = _mixer_pass(True, x_sample, params, filt[dec_seq], dft[dec_seq], rope, cache, state_lru)
    return (y_prompt, y_sample, new_k, new_v, new_lru)
```

```python
import functools
import math

import numpy as np
import jax
import jax.numpy as jnp
from jax import lax
from jax.experimental import pallas as pl
from jax.experimental.pallas import tpu as pltpu

F32 = jnp.float32
BF16 = jnp.bfloat16

D_MODEL = 1024
DEPTH = 2
GRID_W = 64
HEAD_DIM = 64
N_Q_HEADS = 8
N_KV_HEADS = 2
Q_PER_KV = N_Q_HEADS // N_KV_HEADS
ATTN_W = N_Q_HEADS * HEAD_DIM
KV_W = N_KV_HEADS * HEAD_DIM
ROPE_THETA = 10000.0
ATTN_SCALE = HEAD_DIM ** -0.5
HY_W = 512
HY_BANDS = 16
HY_EMB = 2 * HY_BANDS + 1
HY_FH = 64
LRU_W = 512
LRU_BLOCKS = 8
LRU_BS = LRU_W // LRU_BLOCKS
LRU_C = 8.0
EPS = 1e-6

ATTN_COLS = ATTN_W + 2 * KV_W + ATTN_W
D_MIX = ATTN_W + HY_W + LRU_W

LANES = 128
SUBLANES = 8
MXU_DIM = 256
COL_BLOCK = MXU_DIM
HY_BLOCK0 = ATTN_COLS // COL_BLOCK
LRU_BLOCK0 = (ATTN_COLS + 4 * HY_W) // COL_BLOCK
Q_CHUNK = 256
Q_CHUNK_LATENT = 256
HEAD_PAIRS = Q_PER_KV // 2
FEAT_PAD = 128
MOD_ROWS = 8
VMEM_LIMIT = 56 * 1024 * 1024


def _cparams(n_axes):
    return pltpu.CompilerParams(dimension_semantics=("arbitrary",) * n_axes, vmem_limit_bytes=VMEM_LIMIT)


def _split_bf16(a):
    hi = a.astype(BF16)
    lo = (a - hi.astype(F32)).astype(BF16)
    return hi, lo


def _dot(a, b):
    return jnp.dot(a, b, preferred_element_type=F32)


def _dot3(a, b):
    a_hi, a_lo = _split_bf16(a)
    b_hi, b_lo = _split_bf16(b)
    return _dot(a_hi, b_hi) + _dot(a_hi, b_lo) + _dot(a_lo, b_hi)


def _silu(x):
    half = 0.5 * x
    return half * jnp.tanh(half) + half


@functools.lru_cache(maxsize=None)
def _dft_tables(L):
    n = 2 * L
    t = np.arange(L)
    kt = (t[:, None] * t[None, :]) % n
    ang = 2.0 * np.pi * kt / n
    cosm, sinm = np.cos(ang), np.sin(ang)
    alt = (-1.0) ** t
    f_s = -sinm
    f_s[0, :] = alt
    fwd = np.concatenate([cosm, f_s], axis=0)
    g_c = 2.0 * cosm.T / n
    g_c[:, 0] = 1.0 / n
    g_s = -2.0 * sinm.T / n
    g_s[:, 0] = alt / n
    inv = np.concatenate([g_c, g_s], axis=1)
    return fwd.astype(np.float32), inv.astype(np.float32)


@functools.lru_cache(maxsize=None)
def _hyena_feats(L):
    t = np.linspace(0.0, 1.0, L)[:, None]
    w = 2.0 * math.pi * np.arange(L)[:, None] / L
    f = np.linspace(1e-4, HY_BANDS - 1, HY_BANDS)[None, :]
    out = np.zeros((L, FEAT_PAD), np.float32)
    out[:, :HY_EMB] = np.concatenate([t, np.cos(f * w), -np.sin(f * w)], axis=-1)
    return out


@functools.lru_cache(maxsize=None)
def _rope_tables(L):
    rows = L // GRID_W
    row = np.repeat(np.arange(rows), GRID_W).astype(np.float64)
    col = np.tile(np.arange(GRID_W), rows).astype(np.float64)
    n_freq = HEAD_DIM // 4
    inv = ROPE_THETA ** (-np.arange(n_freq) / n_freq)
    ang = np.concatenate([row[:, None] * inv, col[:, None] * inv], axis=-1)
    cos = np.repeat(np.cos(ang), 2, axis=-1)
    sin = np.repeat(np.sin(ang), 2, axis=-1) * np.tile(np.array([-1.0, 1.0]), HEAD_DIM // 2)
    reps = LANES // HEAD_DIM
    return np.tile(cos, (1, reps)).astype(np.float32), np.tile(sin, (1, reps)).astype(np.float32)


@functools.lru_cache(maxsize=None)
def _head_average_matrix():
    return np.kron(np.eye(MXU_DIM // HEAD_DIM), np.full((HEAD_DIM, HEAD_DIM), 1.0 / HEAD_DIM)).astype(np.float32)


def _mod_kernel(c_ref, w_ref, b_ref, o_ref):
    s = _silu(c_ref[...])
    o_ref[0] = _dot3(s, w_ref[0]) + b_ref[0]


def _modulation(cvecs, w_ada, b_ada):
    tn = D_MODEL
    return pl.pallas_call(
        _mod_kernel,
        grid=(DEPTH, 3 * D_MODEL // tn),
        in_specs=[
            pl.BlockSpec((MOD_ROWS, D_MODEL), lambda l, j: (0, 0)),
            pl.BlockSpec((1, D_MODEL, tn), lambda l, j: (l, 0, j)),
            pl.BlockSpec((1, 1, tn), lambda l, j: (l, 0, j)),
        ],
        out_specs=pl.BlockSpec((1, MOD_ROWS, tn), lambda l, j: (l, 0, j)),
        out_shape=jax.ShapeDtypeStruct((DEPTH, MOD_ROWS, 3 * D_MODEL), F32),
        compiler_params=_cparams(2),
        name="adaln_mod",
    )(cvecs, w_ada, b_ada.reshape(DEPTH, 1, 3 * D_MODEL))


def _filter_kernel(L, feats_ref, w1_ref, b1_ref, w2_ref, b2_ref, w3_ref, freq_ref, decay_ref,
                   fwd_ref, a_ref, b_ref, d_ref):
    feats = feats_ref[...]
    t = feats[:, 0:1]
    freq = freq_ref[0]
    hdn = jnp.sin(freq[0:1] * (_dot3(feats, w1_ref[0]) + b1_ref[0]))
    hdn = jnp.sin(freq[1:2] * (_dot3(hdn, w2_ref[0]) + b2_ref[0]))
    h = _dot3(hdn, w3_ref[0]) * jnp.exp(-t * decay_ref[0])
    row = lax.broadcasted_iota(jnp.int32, (L, HY_W), 0)
    h_fwd = h[:, :HY_W]
    h_bwd = jnp.where(row >= 1, h[:, HY_W:], 0.0)
    inv_norm = 1.0 / jnp.sum(jnp.abs(h_fwd) + jnp.abs(h_bwd), axis=0, keepdims=True)
    even = h_fwd + h_bwd
    odd = h_fwd - h_bwd
    h_re = _dot(fwd_ref[0:L, :], even.astype(BF16)) * inv_norm
    h_im = _dot(fwd_ref[L:2 * L, :], odd.astype(BF16)) * inv_norm
    alt = jnp.where(row % 2 == 0, 1.0, -1.0)
    nyq = jnp.sum(even * alt, axis=0, keepdims=True) * inv_norm
    a_ref[0] = h_re
    b_ref[0] = jnp.where(row >= 1, h_im, 0.0)
    d_ref[0] = jnp.where(row >= 1, h_re, nyq)


def _filter_spectrum(L, feats, fwd, w1p, b1, w2, b2, w3, freq, decay):
    full = lambda shape: pl.BlockSpec(shape, lambda l: (0,) * len(shape))
    per_layer = lambda shape: pl.BlockSpec((1,) + shape, lambda l: (l,) + (0,) * len(shape))
    out = jax.ShapeDtypeStruct((DEPTH, L, HY_W), F32)
    return pl.pallas_call(
        functools.partial(_filter_kernel, L),
        grid=(DEPTH,),
        in_specs=[
            full((L, FEAT_PAD)),
            per_layer((FEAT_PAD, HY_FH)), per_layer((1, HY_FH)),
            per_layer((HY_FH, HY_FH)), per_layer((1, HY_FH)),
            per_layer((HY_FH, 2 * HY_W)), per_layer((2, HY_FH)), per_layer((1, 2 * HY_W)),
            full((2 * L, L)),
        ],
        out_specs=[per_layer((L, HY_W))] * 3,
        out_shape=[out, out, out],
        compiler_params=_cparams(1),
        name=f"hyena_filter_L{L}",
    )(feats, w1p, b1, w2, b2, w3, freq, decay, fwd)


def _rope(x, cos, sin):
    lane = lax.broadcasted_iota(jnp.int32, x.shape, 1)
    partner = jnp.where(lane % 2 == 0, pltpu.roll(x, LANES - 1, 1), pltpu.roll(x, 1, 1))
    return x * cos + partner * sin


def _head_mean_square(x, e):
    sq = (x * x).astype(BF16)
    width = x.shape[1]
    if width <= MXU_DIM:
        return _dot(sq, e[:width, :width])
    parts = [_dot(sq[:, i:i + MXU_DIM], e) for i in range(0, width, MXU_DIM)]
    return jnp.concatenate(parts, axis=1)


def _store_kv_heads(k_dst, v_dst, k, v):
    ones = jnp.ones((v.shape[0], HEAD_DIM), BF16)
    for j in range(N_KV_HEADS):
        js = slice(j * HEAD_DIM, (j + 1) * HEAD_DIM)
        vj = v[:, js].astype(BF16)
        k_dst[j] = k[:, js].astype(BF16)
        v_dst[j, 0, :, 0:HEAD_DIM] = vj
        v_dst[j, 0, :, HEAD_DIM:2 * HEAD_DIM] = ones
        v_dst[j, 1, :, 0:HEAD_DIM] = ones
        v_dst[j, 1, :, HEAD_DIM:2 * HEAD_DIM] = vj


def _attn_kernel(latent, aliased, layer, L, nb, qc, past, *refs):
    x_ref, mod_ref, ng_ref, w_ref, gq_ref, gk_ref, e_ref = refs[:7]
    refs = refs[7:]
    if latent:
        (cos_ref, sin_ref, ck_ref, cv_ref, h_out, mix_out,
         q_s, k_s, v_s, g_s, s_s, mx_s, acc_s, ck_s, cv_s) = refs
    else:
        if aliased:
            refs = refs[2:]
        h_out, mix_out, k_out, v_out, q_s, k_s, v_s, g_s, s_s, mx_s, acc_s = refs
    rows = nb * L

    x = x_ref[...].reshape(rows, D_MODEL)
    m = mod_ref[0, 0]
    shift, scale = m[0:1], m[1:2]
    ms = jnp.mean(x * x, axis=-1, keepdims=True)
    h = (x * lax.rsqrt(ms + EPS)) * (ng_ref[0] * (1.0 + scale)) + shift
    hb = h.astype(BF16)
    h_out[...] = hb.reshape(nb, L, D_MODEL)

    u = _dot(hb, w_ref[0])
    q = u[:, :ATTN_W]
    k = u[:, ATTN_W:ATTN_W + KV_W]
    v = u[:, ATTN_W + KV_W:ATTN_W + 2 * KV_W]
    g_s[...] = u[:, ATTN_W + 2 * KV_W:]
    e = e_ref[...]
    qn = q * lax.rsqrt(_head_mean_square(q, e) + EPS) * gq_ref[0]
    kn = k * lax.rsqrt(_head_mean_square(k, e) + EPS) * gk_ref[0]
    if latent:
        cos, sin = cos_ref[...], sin_ref[...]
        qn = jnp.concatenate([_rope(qn[:, i:i + LANES], cos, sin) for i in range(0, ATTN_W, LANES)], axis=1)
        kn = _rope(kn, cos, sin)
        _store_kv_heads(ck_s, cv_s, ck_ref[0, 0], cv_ref[0, 0])
    else:
        for b in range(nb):
            kb, vb = kn[b * L:(b + 1) * L], v[b * L:(b + 1) * L]
            if aliased:
                k_out[b, 0] = kb
                v_out[b, 0] = vb
            else:
                for slot in range(DEPTH):
                    k_out[b, slot] = kb if slot == layer else jnp.zeros_like(kb)
                    v_out[b, slot] = vb if slot == layer else jnp.zeros_like(vb)
    qb = (qn * ATTN_SCALE).astype(BF16)
    for hd in range(N_Q_HEADS):
        j, g = divmod(hd, Q_PER_KV)
        q_s[j * Q_PER_KV + (g % 2) * HEAD_PAIRS + g // 2] = qb[:, hd * HEAD_DIM:(hd + 1) * HEAD_DIM]
    _store_kv_heads(k_s, v_s, kn, v)

    contract_last = (((1,), (1,)), ((), ()))
    chunks_per_seq = L // qc
    lane = lax.broadcasted_iota(jnp.int32, (qc, LANES), 1)
    groups = [(j, e) for j in range(N_KV_HEADS) for e in range(2)]

    def chunk(c, carry):
        r0 = pl.multiple_of(c * qc, qc)
        k0 = pl.multiple_of((c // chunks_per_seq) * L, L)
        for idx, (j, e) in enumerate(groups):
            slot = j * Q_PER_KV + e * HEAD_PAIRS
            qh = q_s[slot:slot + HEAD_PAIRS, pl.ds(r0, qc), :].reshape(HEAD_PAIRS * qc, HEAD_DIM)
            s = lax.dot_general(qh, k_s[j, pl.ds(k0, L), :], contract_last, preferred_element_type=F32)
            mx = jnp.max(s, axis=-1, keepdims=True)
            if latent:
                s0 = lax.dot_general(qh, ck_s[j], contract_last, preferred_element_type=F32)
                mx = jnp.maximum(mx, jnp.max(s0, axis=-1, keepdims=True))
                s_s[idx, :, 0:past] = s0
                s_s[idx, :, past:] = s
            else:
                s_s[idx] = s
            mx_s[idx] = jnp.broadcast_to(mx, (HEAD_PAIRS * qc, LANES))
        for idx, (j, e) in enumerate(groups):
            p = jnp.exp(s_s[idx] - jnp.tile(mx_s[idx], (1, (past + L) // LANES))).astype(BF16)
            if latent:
                acc = _dot(p[:, :past], cv_s[j, e]) + _dot(p[:, past:], v_s[j, e, pl.ds(k0, L), :])
            else:
                acc = _dot(p, v_s[j, e, pl.ds(k0, L), :])
            acc_s[idx] = acc
        b_idx = c // chunks_per_seq
        row0 = pl.multiple_of((c % chunks_per_seq) * qc, qc)
        for j in range(N_KV_HEADS):
            for t in range(HEAD_PAIRS):
                a_even = acc_s[2 * j, t * qc:(t + 1) * qc, :]
                a_odd = acc_s[2 * j + 1, t * qc:(t + 1) * qc, :]
                out = jnp.where(lane < HEAD_DIM, a_even, a_odd)
                den = pltpu.roll(jnp.where(lane < HEAD_DIM, a_odd, a_even), HEAD_DIM, 1)
                blk = slice((j * HEAD_PAIRS + t) * LANES, (j * HEAD_PAIRS + t + 1) * LANES)
                gate = _silu(g_s[pl.ds(r0, qc), blk])
                mix_out[b_idx, pl.ds(row0, qc), blk] = (out / den * gate).astype(BF16)
        return carry

    lax.fori_loop(0, nb * chunks_per_seq, chunk, 0)


def _attention_branch(latent, layer, nb, x, mod, norm_g, w_in_b, gq, gk, e_avg, rope=None, cache=None, kv_prev=None):
    B, L, _ = x.shape
    rows = nb * L
    assert nb == 1 or not latent
    full = lambda shape: pl.BlockSpec(shape, lambda b: (0,) * len(shape))
    per_b = lambda shape: pl.BlockSpec((nb,) + shape, lambda b: (b,) + (0,) * len(shape))
    per_layer = lambda shape: pl.BlockSpec((1,) + shape, lambda b: (layer,) + (0,) * len(shape))
    mod_spec = pl.BlockSpec((1, 1, 3, D_MODEL), (lambda b: (layer, 1 + b, 0, 0)) if latent else (lambda b: (layer, 0, 0, 0)))
    in_specs = [per_b((L, D_MODEL)), mod_spec, per_layer((1, D_MODEL)), per_layer((D_MODEL, ATTN_COLS)),
                per_layer((1, ATTN_W)), per_layer((1, KV_W)), full((MXU_DIM, MXU_DIM))]
    args = [x, mod, norm_g, w_in_b, gq, gk, e_avg]
    out_specs = [per_b((L, D_MODEL)), per_b((L, ATTN_W))]
    out_shape = [jax.ShapeDtypeStruct((B, L, D_MODEL), BF16), jax.ShapeDtypeStruct((B, L, ATTN_W), BF16)]
    qc = Q_CHUNK_LATENT if latent else Q_CHUNK
    past = cache[0].shape[2] if latent else 0
    assert B % nb == 0 and L % qc == 0 and L % LANES == 0 and past % LANES == 0
    n_groups = 2 * N_KV_HEADS
    scratch = [pltpu.VMEM((N_Q_HEADS, rows, HEAD_DIM), BF16), pltpu.VMEM((N_KV_HEADS, rows, HEAD_DIM), BF16),
               pltpu.VMEM((N_KV_HEADS, 2, rows, 2 * HEAD_DIM), BF16),
               pltpu.VMEM((rows, ATTN_W), F32),
               pltpu.VMEM((n_groups, HEAD_PAIRS * qc, past + L), F32),
               pltpu.VMEM((n_groups, HEAD_PAIRS * qc, LANES), F32),
               pltpu.VMEM((n_groups, HEAD_PAIRS * qc, 2 * HEAD_DIM), F32)]
    aliases = {}
    if latent:
        cache_k, cache_v = cache
        cache_spec = pl.BlockSpec((1, 1, past, KV_W), lambda b: (b, layer, 0, 0))
        in_specs += [full((L, LANES)), full((L, LANES)), cache_spec, cache_spec]
        args += [rope[0], rope[1], cache_k, cache_v]
        scratch += [pltpu.VMEM((N_KV_HEADS, past, HEAD_DIM), BF16),
                    pltpu.VMEM((N_KV_HEADS, 2, past, 2 * HEAD_DIM), BF16)]
    else:
        if kv_prev is not None:
            kv_spec = pl.BlockSpec((nb, 1, L, KV_W), lambda b: (b, layer, 0, 0))
        else:
            kv_spec = pl.BlockSpec((nb, DEPTH, L, KV_W), lambda b: (b, 0, 0, 0))
        kv_shape = jax.ShapeDtypeStruct((B, DEPTH, L, KV_W), F32)
        out_specs += [kv_spec, kv_spec]
        out_shape += [kv_shape, kv_shape]
        if kv_prev is not None:
            in_specs += [pl.BlockSpec(memory_space=pl.ANY)] * 2
            aliases = {len(args): 2, len(args) + 1: 3}
            args += list(kv_prev)
    return pl.pallas_call(
        functools.partial(_attn_kernel, latent, kv_prev is not None, layer, L, nb, qc, past),
        grid=(B // nb,),
        in_specs=in_specs,
        out_specs=out_specs,
        out_shape=out_shape,
        scratch_shapes=scratch,
        input_output_aliases=aliases,
        compiler_params=_cparams(1),
        name=f"attn_branch_L{L}",
    )(*args)


def _shifted(x, offset, period):
    rows = x.shape[0]
    t = lax.broadcasted_iota(jnp.int32, x.shape, 0)
    if rows != period:
        t = t % period
    rolled = pltpu.roll(x, (-offset) % rows, 0)
    valid = (t >= -offset) if offset < 0 else (t < period - offset)
    return jnp.where(valid, rolled, 0.0)


def _hyena_kernel(L, nb, nw, h_ref, *refs):
    w_refs = refs[:4 * nw]
    (sw0_ref, sw1_ref, sw2_ref, sb0_ref, sb1_ref, sb2_ref,
     fa_ref, fb_ref, fd_ref, bias_ref, fwd_ref, inv_ref, mix_out) = refs[4 * nw:]
    hb = h_ref[...].reshape(nb * L, D_MODEL)

    def project(part):
        cols = [_dot(hb, w_refs[part * nw + i][0]) for i in range(nw)]
        return cols[0] if nw == 1 else jnp.concatenate(cols, axis=1)

    def short_conv(part, sw_ref, sb_ref):
        xs = project(part)
        w = sw_ref[0]
        return sb_ref[0] + _shifted(xs, -1, L) * w[0:1] + xs * w[1:2] + _shifted(xs, 1, L) * w[2:3]

    x0 = short_conv(0, sw0_ref, sb0_ref)
    x1 = short_conv(1, sw1_ref, sb1_ref)
    hv = short_conv(2, sw2_ref, sb2_ref)
    gated = x0 * _silu(project(3))
    z = x1 * hv
    zb = z.astype(BF16)
    fa, fb, fd = fa_ref[0], fb_ref[0], fd_ref[0]
    cb = z.shape[1]
    z_cat = jnp.concatenate([zb[b * L:(b + 1) * L] for b in range(nb)], axis=1)
    zf = _dot(fwd_ref[...], z_cat)
    y_re, y_im = [], []
    for b in range(nb):
        re, im = zf[:L, b * cb:(b + 1) * cb], zf[L:, b * cb:(b + 1) * cb]
        y_re.append((re * fa - im * fb).astype(BF16))
        y_im.append((re * fb + im * fd).astype(BF16))
    y_all = (_dot(inv_ref[:, :L], jnp.concatenate(y_re, axis=1))
             + _dot(inv_ref[:, L:], jnp.concatenate(y_im, axis=1)))
    for b in range(nb):
        rs = slice(b * L, (b + 1) * L)
        y = y_all[:, b * cb:(b + 1) * cb] + z[rs] * bias_ref[0]
        mix_out[b] = (y * gated[rs]).astype(BF16)


def _hyena_branch(layer, nb, nw, hmod, w_in_b, short_w, short_b, filt, bias, fwd, inv):
    B, L, _ = hmod.shape
    cb = COL_BLOCK
    width = nw * cb
    nt = HY_W // width
    blocks_per_part = HY_W // cb
    fa, fb, fd = filt
    assert B % nb == 0 and L % LANES == 0 and HY_W % width == 0
    w_specs = [pl.BlockSpec((1, D_MODEL, cb),
                            functools.partial(lambda i, b, off: (layer, 0, off + i * nw),
                                              off=HY_BLOCK0 + part * blocks_per_part + j))
               for part in range(4) for j in range(nw)]
    sw_spec = lambda part: pl.BlockSpec((1, 3, width), lambda i, b: (layer, 0, part * nt + i))
    sb_spec = lambda part: pl.BlockSpec((1, 1, width), lambda i, b: (layer, 0, part * nt + i))
    filt_spec = pl.BlockSpec((1, L, width), lambda i, b: (layer, 0, i))
    return pl.pallas_call(
        functools.partial(_hyena_kernel, L, nb, nw),
        grid=(nt, B // nb),
        in_specs=[pl.BlockSpec((nb, L, D_MODEL), lambda i, b: (b, 0, 0))] + w_specs + [
            sw_spec(0), sw_spec(1), sw_spec(2),
            sb_spec(0), sb_spec(1), sb_spec(2),
            filt_spec, filt_spec, filt_spec,
            pl.BlockSpec((1, 1, width), lambda i, b: (layer, 0, i)),
            pl.BlockSpec((2 * L, L), lambda i, b: (0, 0)),
            pl.BlockSpec((L, 2 * L), lambda i, b: (0, 0)),
        ],
        out_specs=pl.BlockSpec((nb, L, width), lambda i, b: (b, 0, i)),
        out_shape=jax.ShapeDtypeStruct((B, L, HY_W), BF16),
        compiler_params=_cparams(2),
        name=f"hyena_branch_L{L}",
    )(hmod, *([w_in_b] * (4 * nw)), short_w, short_w, short_w, short_b, short_b, short_b,
      fa, fb, fd, bias, fwd, inv)


def _lru_kernel(latent, aliased, final, layer, L, nb, *refs):
    (h_ref, wx0_ref, wx1_ref, wg0_ref, wg1_ref, cw_ref, cb_ref, wa_ref, ba_ref, wxg_ref, bx_ref, lam_ref,
     x_ref, ma_ref, mh_ref, wo_ref, mod_ref, fg_ref) = refs[:18]
    refs = refs[18:]
    if latent:
        h0_ref, x_out, gate_w_s, a_s, b_s, y_s, g_s, part_s = refs
    else:
        if aliased:
            refs = refs[1:]
        x_out, hl_out, gate_w_s, a_s, b_s, y_s, g_s, part_s = refs
    half_w = LRU_W // 2
    blocks_per_half = half_w // LRU_BS
    n_tiles = L // SUBLANES

    @pl.when(pl.program_id(0) == 0)
    def _build_gate_weights():
        gate_w_s[...] = jnp.zeros(gate_w_s.shape, BF16)
        for half in range(2):
            for kind, (w_ref, d) in enumerate(((wa_ref, 0), (wxg_ref, 0), (wa_ref, 1), (wxg_ref, 1))):
                for j in range(blocks_per_half):
                    blk = (0.5 * w_ref[0, d, half * blocks_per_half + j]).astype(BF16)
                    gate_w_s[half, j * LRU_BS:(j + 1) * LRU_BS,
                             kind * half_w + j * LRU_BS:kind * half_w + (j + 1) * LRU_BS] = blk

    part_s[...] = (_dot(ma_ref[...].reshape(nb * L, ATTN_W), wo_ref[0, 0:ATTN_W, :])
                   + _dot(mh_ref[...].reshape(nb * L, HY_W), wo_ref[0, ATTN_W:ATTN_W + HY_W, :]))

    hb = h_ref[...].reshape(nb * L, D_MODEL)
    xs = jnp.concatenate([_dot(hb, wx0_ref[0]), _dot(hb, wx1_ref[0])], axis=1)
    g_s[:, :half_w] = _dot(hb, wg0_ref[0])
    g_s[:, half_w:] = _dot(hb, wg1_ref[0])
    cw = cw_ref[0]
    xc = (cb_ref[0] + _shifted(xs, -1, L) * cw[0:1] + xs * cw[1:2] + _shifted(xs, 1, L) * cw[2:3]
          + _shifted(xs, 2, L) * cw[3:4])
    xcb = xc.astype(BF16)
    c_half = (-0.5 * LRU_C) * jax.nn.softplus(-lam_ref[0])
    ba, bx = 0.5 * ba_ref[0], 0.5 * bx_ref[0]
    for half in range(2):
        cols = slice(half * half_w, (half + 1) * half_w)
        bias = jnp.concatenate([ba[0:1, cols], bx[0:1, cols], ba[1:2, cols], bx[1:2, cols]], axis=1)
        gates = _dot(xcb[:, cols], gate_w_s[half]) + bias
        xh_half = 0.5 * xc[:, cols]
        for d in range(2):
            t_r = jnp.tanh(gates[:, (2 * d) * half_w:(2 * d + 1) * half_w])
            t_i = jnp.tanh(gates[:, (2 * d + 1) * half_w:(2 * d + 2) * half_w])
            ch = c_half[d:d + 1, cols]
            log_a = ch * t_r + ch
            a = jnp.exp(log_a)
            y = jnp.tanh(log_a) * (-1.0 - a * a)
            root = jnp.where(y > 0.0, y * lax.rsqrt(y), 0.0)
            bcoef = root * (xh_half * t_i + xh_half)
            a_s[d, :, :, cols] = a.reshape(nb * n_tiles, SUBLANES, half_w)
            b_s[d, :, :, cols] = bcoef.reshape(nb * n_tiles, SUBLANES, half_w)

    if latent:
        init = []
        for b in range(nb):
            h0 = h0_ref[b, 0]
            init += [h0[0:1], h0[1:2]]
    else:
        init = [jnp.zeros((1, LRU_W), F32)] * (2 * nb)

    def advance(d, tile, rows_g, h):
        a = [a_s[d, tile, r:r + 1, :] for r in rows_g]
        c = [b_s[d, tile, r:r + 1, :] for r in rows_g]
        a01, c01 = a[1] * a[0], a[1] * c[0] + c[1]
        hs = [a[0] * h + c[0], a01 * h + c01]
        if len(rows_g) == 4:
            a23, c23 = a[3] * a[2], a[3] * c[2] + c[3]
            hs += [a[2] * hs[1] + c[2], (a23 * a01) * h + (a23 * c01 + c23)]
        for r, v in zip(rows_g, hs):
            y_s[d, tile, r:r + 1, :] = v
        return hs[-1]

    group = 4 if nb == 1 else 2

    def tile_step(i, carry):
        carry = list(carry)
        for g0 in range(0, SUBLANES, group):
            fwd_rows = list(range(g0, g0 + group))
            bwd_rows = [SUBLANES - 1 - r for r in fwd_rows]
            for b in range(nb):
                carry[2 * b] = advance(0, b * n_tiles + i, fwd_rows, carry[2 * b])
                carry[2 * b + 1] = advance(1, b * n_tiles + (n_tiles - 1 - i), bwd_rows, carry[2 * b + 1])
        return tuple(carry)

    def two_tiles(i, carry):
        return tile_step(2 * i + 1, tile_step(2 * i, carry))

    last = lax.fori_loop(0, n_tiles // 2, two_tiles, tuple(init))
    if not latent:
        if not aliased:
            hl_out[...] = jnp.zeros(hl_out.shape, F32)
        slot = 0 if aliased else layer
        for b in range(nb):
            hl_out[b, slot, 0:1, :] = last[2 * b]
            hl_out[b, slot, 1:2, :] = last[2 * b + 1]
    y = (y_s[0] + y_s[1]).reshape(nb * L, LRU_W)
    mix_l = (y * _silu(g_s[...])).astype(BF16)
    out = part_s[...] + _dot(mix_l, wo_ref[0, ATTN_W + HY_W:, :])
    xn = x_ref[...].reshape(nb * L, D_MODEL) + mod_ref[0, 0][2:3] * out
    if final:
        ms = jnp.mean(xn * xn, axis=-1, keepdims=True)
        xn = xn * lax.rsqrt(ms + EPS) * fg_ref[...]
    x_out[...] = xn.reshape(nb, L, D_MODEL)


def _lru_out_branch(latent, layer, nb, hmod, w_in_b, conv_w, conv_b, wa, ba, wx, bx, lam, x, mix_a, mix_h, w_out_b,
                    mod, final_g, state=None, hl_prev=None):
    B, L, _ = hmod.shape
    assert B % nb == 0 and L % (2 * SUBLANES) == 0
    final = layer == DEPTH - 1
    cb = COL_BLOCK
    half_w = LRU_W // 2
    rows = nb * L
    per_layer = lambda shape: pl.BlockSpec((1,) + shape, lambda b: (layer,) + (0,) * len(shape))
    w_spec = lambda j: pl.BlockSpec((1, D_MODEL, cb), lambda b: (layer, 0, LRU_BLOCK0 + j))
    gate_blocks = (2, LRU_BLOCKS, LRU_BS, LRU_BS)
    in_specs = [pl.BlockSpec((nb, L, D_MODEL), lambda b: (b, 0, 0)), w_spec(0), w_spec(1), w_spec(2), w_spec(3),
                per_layer((4, LRU_W)), per_layer((1, LRU_W)), per_layer(gate_blocks), per_layer((2, LRU_W)),
                per_layer(gate_blocks), per_layer((2, LRU_W)), per_layer((2, LRU_W))]
    per_b = lambda w: pl.BlockSpec((nb, L, w), lambda b: (b, 0, 0))
    mod_row = (lambda b: (layer, 1 + b, 0, 0)) if latent else (lambda b: (layer, 0, 0, 0))
    in_specs += [per_b(D_MODEL), per_b(ATTN_W), per_b(HY_W), per_layer((D_MIX, D_MODEL)),
                 pl.BlockSpec((1, 1, 3, D_MODEL), mod_row), pl.BlockSpec((1, D_MODEL), lambda b: (0, 0))]
    args = [hmod, w_in_b, w_in_b, w_in_b, w_in_b, conv_w, conv_b, wa, ba, wx, bx, lam,
            x, mix_a, mix_h, w_out_b, mod, final_g]
    out_specs = [per_b(D_MODEL)]
    out_shape = [jax.ShapeDtypeStruct((B, L, D_MODEL), F32)]
    aliases = {}
    if latent:
        in_specs.append(pl.BlockSpec((nb, 1, 2, LRU_W), lambda b: (b, layer, 0, 0)))
        args.append(state)
    else:
        if hl_prev is not None:
            out_specs.append(pl.BlockSpec((nb, 1, 2, LRU_W), lambda b: (b, layer, 0, 0)))
        else:
            out_specs.append(pl.BlockSpec((nb, DEPTH, 2, LRU_W), lambda b: (b, 0, 0, 0)))
        out_shape.append(jax.ShapeDtypeStruct((B, DEPTH, 2, LRU_W), F32))
        if hl_prev is not None:
            in_specs.append(pl.BlockSpec(memory_space=pl.ANY))
            aliases = {len(args): 1}
            args.append(hl_prev)
    scan_buf = pltpu.VMEM((2, rows // SUBLANES, SUBLANES, LRU_W), F32)
    return pl.pallas_call(
        functools.partial(_lru_kernel, latent, hl_prev is not None, final, layer, L, nb),
        grid=(B // nb,),
        in_specs=in_specs,
        out_specs=out_specs,
        out_shape=out_shape,
        scratch_shapes=[pltpu.VMEM((2, half_w, 4 * half_w), BF16), scan_buf, scan_buf, scan_buf,
                        pltpu.VMEM((rows, LRU_W), F32), pltpu.VMEM((rows, D_MODEL), F32)],
        input_output_aliases=aliases,
        compiler_params=_cparams(1),
        name=f"lru_out_L{L}",
    )(*args)


def _mixer_pass(latent, x, p, filt, dft, rope=None, cache=None, state=None):
    B, L, _ = x.shape
    fwd, inv = dft
    nb = 1 if latent else 4
    nb_hyena = 1 if latent else 4
    nw_hyena = HY_W // COL_BLOCK
    nb_lru = 1 if latent else 2
    kv, hl = None, None
    for l in range(DEPTH):
        outs = _attention_branch(latent, l, nb, x, p['mod'], p['norm_g'], p['w_in'], p['gq'], p['gk'], p['e_avg'],
                                 rope, cache, kv)
        hmod, mix_a = outs[0], outs[1]
        if not latent:
            kv = (outs[2], outs[3])
        mix_h = _hyena_branch(l, nb_hyena, nw_hyena, hmod, p['w_in'], p['short_w'], p['short_b'], filt, p['hy_bias'],
                              fwd, inv)
        outs = _lru_out_branch(latent, l, nb_lru, hmod, p['w_in'], p['conv_w'], p['conv_b'], p['wa'], p['ba'], p['wx'],
                               p['bx'], p['lam'], x, mix_a, mix_h, p['w_out'], p['mod'], p['final_g'], state, hl)
        x = outs[0]
        if not latent:
            hl = outs[1]
    return x, kv, hl


def kernel(x_prompt, x_sample, cache_k, cache_v, state_lru, c, c_ctx, norm_g, w_ada, b_ada, w_in, q_norm_g, k_norm_g,
           hy_short_w, hy_short_b, hy_filt_w1, hy_filt_b1, hy_filt_w2, hy_filt_b2, hy_filt_w3, hy_filt_freq,
           hy_filt_decay, hy_bias, lru_conv_w, lru_conv_b, lru_wa, lru_ba, lru_wx, lru_bx, lru_lambda, w_out, final_g):
    batch, seq, _ = x_prompt.shape
    dec_batch, dec_seq, _ = x_sample.shape
    past = cache_k.shape[2]

    cvecs = jnp.concatenate([c_ctx[None, :], c, jnp.zeros((MOD_ROWS - 1 - dec_batch, D_MODEL), F32)], axis=0)
    mod = _modulation(cvecs, w_ada, b_ada).reshape(DEPTH, MOD_ROWS, 3, D_MODEL)

    w1p = jnp.pad(hy_filt_w1, ((0, 0), (0, FEAT_PAD - HY_EMB), (0, 0)))
    dft, filt = {}, {}
    for L in (seq, dec_seq):
        dft[L] = tuple(jnp.asarray(m).astype(BF16) for m in _dft_tables(L))
        filt[L] = _filter_spectrum(L, jnp.asarray(_hyena_feats(L)), dft[L][0], w1p, hy_filt_b1[:, None, :],
                                   hy_filt_w2, hy_filt_b2[:, None, :], hy_filt_w3, hy_filt_freq,
                                   hy_filt_decay[:, None, :])

    params = {
        'mod': mod,
        'norm_g': norm_g[:, None, :],
        'w_in': w_in.astype(BF16),
        'gq': jnp.tile(q_norm_g, (1, N_Q_HEADS))[:, None, :],
        'gk': jnp.tile(k_norm_g, (1, N_KV_HEADS))[:, None, :],
        'e_avg': jnp.asarray(_head_average_matrix()).astype(BF16),
        'short_w': hy_short_w, 'short_b': hy_short_b[:, None, :], 'hy_bias': hy_bias[:, None, :],
        'conv_w': lru_conv_w, 'conv_b': lru_conv_b[:, None, :],
        'wa': lru_wa, 'ba': lru_ba, 'wx': lru_wx, 'bx': lru_bx, 'lam': lru_lambda,
        'w_out': w_out.astype(BF16),
        'final_g': final_g[None, :],
    }

    y_prompt, kv, new_lru = _mixer_pass(False, x_prompt, params, filt[seq], dft[seq])
    new_k = kv[0].reshape(batch, DEPTH, seq, N_KV_HEADS, HEAD_DIM)
    new_v = kv[1].reshape(batch, DEPTH, seq, N_KV_HEADS, HEAD_DIM)

    rope = tuple(jnp.asarray(t) for t in _rope_tables(dec_seq))
    cache = (cache_k.reshape(dec_batch, DEPTH, past, KV_W), cache_v.reshape(dec_batch, DEPTH, past, KV_W))
    y_sample, _, _ = _mixer_pass(True, x_sample, params, filt[dec_seq], dft[dec_seq], rope, cache, state_lru)
    return (y_prompt, y_sample, new_k, new_v, new_lru)
```

```python
import functools
import math

import numpy as np
import jax
import jax.numpy as jnp
from jax import lax
from jax.experimental import pallas as pl
from jax.experimental.pallas import tpu as pltpu

F32 = jnp.float32
BF16 = jnp.bfloat16

D_MODEL = 1024
DEPTH = 2
GRID_W = 64
HEAD_DIM = 64
N_Q_HEADS = 8
N_KV_HEADS = 2
Q_PER_KV = N_Q_HEADS // N_KV_HEADS
ATTN_W = N_Q_HEADS * HEAD_DIM
KV_W = N_KV_HEADS * HEAD_DIM
ROPE_THETA = 10000.0
ATTN_SCALE = HEAD_DIM ** -0.5
HY_W = 512
HY_BANDS = 16
HY_EMB = 2 * HY_BANDS + 1
HY_FH = 64
LRU_W = 512
LRU_BLOCKS = 8
LRU_BS = LRU_W // LRU_BLOCKS
LRU_C = 8.0
EPS = 1e-6

ATTN_COLS = ATTN_W + 2 * KV_W + ATTN_W
D_MIX = ATTN_W + HY_W + LRU_W

LANES = 128
SUBLANES = 8
MXU_DIM = 256
COL_BLOCK = MXU_DIM
HY_BLOCK0 = ATTN_COLS // COL_BLOCK
LRU_BLOCK0 = (ATTN_COLS + 4 * HY_W) // COL_BLOCK
Q_CHUNK = 256
Q_CHUNK_LATENT = 256
HEAD_PAIRS = Q_PER_KV // 2
FEAT_PAD = 128
MOD_ROWS = 8
VMEM_LIMIT = 56 * 1024 * 1024


def _cparams(n_axes):
    return pltpu.CompilerParams(dimension_semantics=("arbitrary",) * n_axes, vmem_limit_bytes=VMEM_LIMIT)


def _split_bf16(a):
    hi = a.astype(BF16)
    lo = (a - hi.astype(F32)).astype(BF16)
    return hi, lo


def _dot(a, b):
    return jnp.dot(a, b, preferred_element_type=F32)


def _dot3(a, b):
    a_hi, a_lo = _split_bf16(a)
    b_hi, b_lo = _split_bf16(b)
    return _dot(a_hi, b_hi) + _dot(a_hi, b_lo) + _dot(a_lo, b_hi)


def _silu(x):
    half = 0.5 * x
    return half * jnp.tanh(half) + half


@functools.lru_cache(maxsize=None)
def _dft_tables(L):
    n = 2 * L
    t = np.arange(L)
    kt = (t[:, None] * t[None, :]) % n
    ang = 2.0 * np.pi * kt / n
    cosm, sinm = np.cos(ang), np.sin(ang)
    alt = (-1.0) ** t
    f_s = -sinm
    f_s[0, :] = alt
    fwd = np.concatenate([cosm, f_s], axis=0)
    g_c = 2.0 * cosm.T / n
    g_c[:, 0] = 1.0 / n
    g_s = -2.0 * sinm.T / n
    g_s[:, 0] = alt / n
    inv = np.concatenate([g_c, g_s], axis=1)
    return fwd.astype(np.float32), inv.astype(np.float32)


@functools.lru_cache(maxsize=None)
def _hyena_feats(L):
    t = np.linspace(0.0, 1.0, L)[:, None]
    w = 2.0 * math.pi * np.arange(L)[:, None] / L
    f = np.linspace(1e-4, HY_BANDS - 1, HY_BANDS)[None, :]
    out = np.zeros((L, FEAT_PAD), np.float32)
    out[:, :HY_EMB] = np.concatenate([t, np.cos(f * w), -np.sin(f * w)], axis=-1)
    return out


@functools.lru_cache(maxsize=None)
def _rope_tables(L):
    rows = L // GRID_W
    row = np.repeat(np.arange(rows), GRID_W).astype(np.float64)
    col = np.tile(np.arange(GRID_W), rows).astype(np.float64)
    n_freq = HEAD_DIM // 4
    inv = ROPE_THETA ** (-np.arange(n_freq) / n_freq)
    ang = np.concatenate([row[:, None] * inv, col[:, None] * inv], axis=-1)
    cos = np.repeat(np.cos(ang), 2, axis=-1)
    sin = np.repeat(np.sin(ang), 2, axis=-1) * np.tile(np.array([-1.0, 1.0]), HEAD_DIM // 2)
    reps = LANES // HEAD_DIM
    return np.tile(cos, (1, reps)).astype(np.float32), np.tile(sin, (1, reps)).astype(np.float32)


@functools.lru_cache(maxsize=None)
def _head_average_matrix():
    return np.kron(np.eye(MXU_DIM // HEAD_DIM), np.full((HEAD_DIM, HEAD_DIM), 1.0 / HEAD_DIM)).astype(np.float32)


def _mod_kernel(c_ref, w_ref, b_ref, o_ref):
    s = _silu(c_ref[...])
    o_ref[0] = _dot3(s, w_ref[0]) + b_ref[0]


def _modulation(cvecs, w_ada, b_ada):
    tn = D_MODEL
    return pl.pallas_call(
        _mod_kernel,
        grid=(DEPTH, 3 * D_MODEL // tn),
        in_specs=[
            pl.BlockSpec((MOD_ROWS, D_MODEL), lambda l, j: (0, 0)),
            pl.BlockSpec((1, D_MODEL, tn), lambda l, j: (l, 0, j)),
            pl.BlockSpec((1, 1, tn), lambda l, j: (l, 0, j)),
        ],
        out_specs=pl.BlockSpec((1, MOD_ROWS, tn), lambda l, j: (l, 0, j)),
        out_shape=jax.ShapeDtypeStruct((DEPTH, MOD_ROWS, 3 * D_MODEL), F32),
        compiler_params=_cparams(2),
        name="adaln_mod",
    )(cvecs, w_ada, b_ada.reshape(DEPTH, 1, 3 * D_MODEL))


def _filter_kernel(L, feats_ref, w1_ref, b1_ref, w2_ref, b2_ref, w3_ref, freq_ref, decay_ref,
                   fwd_ref, a_ref, b_ref, d_ref):
    feats = feats_ref[...]
    t = feats[:, 0:1]
    freq = freq_ref[0]
    hdn = jnp.sin(freq[0:1] * (_dot3(feats, w1_ref[0]) + b1_ref[0]))
    hdn = jnp.sin(freq[1:2] * (_dot3(hdn, w2_ref[0]) + b2_ref[0]))
    h = _dot3(hdn, w3_ref[0]) * jnp.exp(-t * decay_ref[0])
    row = lax.broadcasted_iota(jnp.int32, (L, HY_W), 0)
    h_fwd = h[:, :HY_W]
    h_bwd = jnp.where(row >= 1, h[:, HY_W:], 0.0)
    inv_norm = 1.0 / jnp.sum(jnp.abs(h_fwd) + jnp.abs(h_bwd), axis=0, keepdims=True)
    even = h_fwd + h_bwd
    odd = h_fwd - h_bwd
    h_re = _dot(fwd_ref[0:L, :], even.astype(BF16)) * inv_norm
    h_im = _dot(fwd_ref[L:2 * L, :], odd.astype(BF16)) * inv_norm
    alt = jnp.where(row % 2 == 0, 1.0, -1.0)
    nyq = jnp.sum(even * alt, axis=0, keepdims=True) * inv_norm
    a_ref[0] = h_re
    b_ref[0] = jnp.where(row >= 1, h_im, 0.0)
    d_ref[0] = jnp.where(row >= 1, h_re, nyq)


def _filter_spectrum(L, feats, fwd, w1p, b1, w2, b2, w3, freq, decay):
    full = lambda shape: pl.BlockSpec(shape, lambda l: (0,) * len(shape))
    per_layer = lambda shape: pl.BlockSpec((1,) + shape, lambda l: (l,) + (0,) * len(shape))
    out = jax.ShapeDtypeStruct((DEPTH, L, HY_W), F32)
    return pl.pallas_call(
        functools.partial(_filter_kernel, L),
        grid=(DEPTH,),
        in_specs=[
            full((L, FEAT_PAD)),
            per_layer((FEAT_PAD, HY_FH)), per_layer((1, HY_FH)),
            per_layer((HY_FH, HY_FH)), per_layer((1, HY_FH)),
            per_layer((HY_FH, 2 * HY_W)), per_layer((2, HY_FH)), per_layer((1, 2 * HY_W)),
            full((2 * L, L)),
        ],
        out_specs=[per_layer((L, HY_W))] * 3,
        out_shape=[out, out, out],
        compiler_params=_cparams(1),
        name=f"hyena_filter_L{L}",
    )(feats, w1p, b1, w2, b2, w3, freq, decay, fwd)


def _rope(x, cos, sin):
    lane = lax.broadcasted_iota(jnp.int32, x.shape, 1)
    partner = jnp.where(lane % 2 == 0, pltpu.roll(x, LANES - 1, 1), pltpu.roll(x, 1, 1))
    return x * cos + partner * sin


def _head_mean_square(x, e):
    sq = (x * x).astype(BF16)
    width = x.shape[1]
    if width <= MXU_DIM:
        return _dot(sq, e[:width, :width])
    parts = [_dot(sq[:, i:i + MXU_DIM], e) for i in range(0, width, MXU_DIM)]
    return jnp.concatenate(parts, axis=1)


def _store_kv_heads(k_dst, v_dst, k, v):
    ones = jnp.ones((v.shape[0], HEAD_DIM), BF16)
    for j in range(N_KV_HEADS):
        js = slice(j * HEAD_DIM, (j + 1) * HEAD_DIM)
        vj = v[:, js].astype(BF16)
        k_dst[j] = k[:, js].astype(BF16)
        v_dst[j, 0, :, 0:HEAD_DIM] = vj
        v_dst[j, 0, :, HEAD_DIM:2 * HEAD_DIM] = ones
        v_dst[j, 1, :, 0:HEAD_DIM] = ones
        v_dst[j, 1, :, HEAD_DIM:2 * HEAD_DIM] = vj


def _attn_kernel(latent, aliased, layer, L, nb, qc, past, *refs):
    x_ref, mod_ref, ng_ref, w_ref, gq_ref, gk_ref, e_ref = refs[:7]
    refs = refs[7:]
    if latent:
        (cos_ref, sin_ref, ck_ref, cv_ref, h_out, mix_out,
         q_s, k_s, v_s, g_s, s_s, mx_s, acc_s, ck_s, cv_s) = refs
    else:
        if aliased:
            refs = refs[2:]
        h_out, mix_out, k_out, v_out, q_s, k_s, v_s, g_s, s_s, mx_s, acc_s = refs
    rows = nb * L

    x = x_ref[...].reshape(rows, D_MODEL)
    m = mod_ref[0, 0]
    shift, scale = m[0:1], m[1:2]
    ms = jnp.mean(x * x, axis=-1, keepdims=True)
    h = (x * lax.rsqrt(ms + EPS)) * (ng_ref[0] * (1.0 + scale)) + shift
    hb = h.astype(BF16)
    h_out[...] = hb.reshape(nb, L, D_MODEL)

    u = _dot(hb, w_ref[0])
    q = u[:, :ATTN_W]
    k = u[:, ATTN_W:ATTN_W + KV_W]
    v = u[:, ATTN_W + KV_W:ATTN_W + 2 * KV_W]
    g_s[...] = u[:, ATTN_W + 2 * KV_W:]
    e = e_ref[...]
    qn = q * lax.rsqrt(_head_mean_square(q, e) + EPS) * gq_ref[0]
    kn = k * lax.rsqrt(_head_mean_square(k, e) + EPS) * gk_ref[0]
    if latent:
        cos, sin = cos_ref[...], sin_ref[...]
        qn = jnp.concatenate([_rope(qn[:, i:i + LANES], cos, sin) for i in range(0, ATTN_W, LANES)], axis=1)
        kn = _rope(kn, cos, sin)
        _store_kv_heads(ck_s, cv_s, ck_ref[0, 0], cv_ref[0, 0])
    else:
        for b in range(nb):
            kb, vb = kn[b * L:(b + 1) * L], v[b * L:(b + 1) * L]
            if aliased:
                k_out[b, 0] = kb
                v_out[b, 0] = vb
            else:
                for slot in range(DEPTH):
                    k_out[b, slot] = kb if slot == layer else jnp.zeros_like(kb)
                    v_out[b, slot] = vb if slot == layer else jnp.zeros_like(vb)
    qb = (qn * ATTN_SCALE).astype(BF16)
    for hd in range(N_Q_HEADS):
        j, g = divmod(hd, Q_PER_KV)
        q_s[j * Q_PER_KV + (g % 2) * HEAD_PAIRS + g // 2] = qb[:, hd * HEAD_DIM:(hd + 1) * HEAD_DIM]
    _store_kv_heads(k_s, v_s, kn, v)

    contract_last = (((1,), (1,)), ((), ()))
    chunks_per_seq = L // qc
    lane = lax.broadcasted_iota(jnp.int32, (qc, LANES), 1)
    groups = [(j, e) for j in range(N_KV_HEADS) for e in range(2)]

    def chunk(c, carry):
        r0 = pl.multiple_of(c * qc, qc)
        k0 = pl.multiple_of((c // chunks_per_seq) * L, L)
        for idx, (j, e) in enumerate(groups):
            slot = j * Q_PER_KV + e * HEAD_PAIRS
            qh = q_s[slot:slot + HEAD_PAIRS, pl.ds(r0, qc), :].reshape(HEAD_PAIRS * qc, HEAD_DIM)
            s = lax.dot_general(qh, k_s[j, pl.ds(k0, L), :], contract_last, preferred_element_type=F32)
            mx = jnp.max(s, axis=-1, keepdims=True)
            if latent:
                s0 = lax.dot_general(qh, ck_s[j], contract_last, preferred_element_type=F32)
                mx = jnp.maximum(mx, jnp.max(s0, axis=-1, keepdims=True))
                s_s[idx, :, 0:past] = s0
                s_s[idx, :, past:] = s
            else:
                s_s[idx] = s
            mx_s[idx] = jnp.broadcast_to(mx, (HEAD_PAIRS * qc, LANES))
        for idx, (j, e) in enumerate(groups):
            p = jnp.exp(s_s[idx] - jnp.tile(mx_s[idx], (1, (past + L) // LANES))).astype(BF16)
            if latent:
                acc = _dot(p[:, :past], cv_s[j, e]) + _dot(p[:, past:], v_s[j, e, pl.ds(k0, L), :])
            else:
                acc = _dot(p, v_s[j, e, pl.ds(k0, L), :])
            acc_s[idx] = acc
        b_idx = c // chunks_per_seq
        row0 = pl.multiple_of((c % chunks_per_seq) * qc, qc)
        for j in range(N_KV_HEADS):
            for t in range(HEAD_PAIRS):
                a_even = acc_s[2 * j, t * qc:(t + 1) * qc, :]
                a_odd = acc_s[2 * j + 1, t * qc:(t + 1) * qc, :]
                out = jnp.where(lane < HEAD_DIM, a_even, a_odd)
                den = pltpu.roll(jnp.where(lane < HEAD_DIM, a_odd, a_even), HEAD_DIM, 1)
                blk = slice((j * HEAD_PAIRS + t) * LANES, (j * HEAD_PAIRS + t + 1) * LANES)
                gate = _silu(g_s[pl.ds(r0, qc), blk])
                mix_out[b_idx, pl.ds(row0, qc), blk] = (out / den * gate).astype(BF16)
        return carry

    lax.fori_loop(0, nb * chunks_per_seq, chunk, 0, unroll=4)


def _attention_branch(latent, layer, nb, x, mod, norm_g, w_in_b, gq, gk, e_avg, rope=None, cache=None, kv_prev=None):
    B, L, _ = x.shape
    rows = nb * L
    assert nb == 1 or not latent
    full = lambda shape: pl.BlockSpec(shape, lambda b: (0,) * len(shape))
    per_b = lambda shape: pl.BlockSpec((nb,) + shape, lambda b: (b,) + (0,) * len(shape))
    per_layer = lambda shape: pl.BlockSpec((1,) + shape, lambda b: (layer,) + (0,) * len(shape))
    mod_spec = pl.BlockSpec((1, 1, 3, D_MODEL), (lambda b: (layer, 1 + b, 0, 0)) if latent else (lambda b: (layer, 0, 0, 0)))
    in_specs = [per_b((L, D_MODEL)), mod_spec, per_layer((1, D_MODEL)), per_layer((D_MODEL, ATTN_COLS)),
                per_layer((1, ATTN_W)), per_layer((1, KV_W)), full((MXU_DIM, MXU_DIM))]
    args = [x, mod, norm_g, w_in_b, gq, gk, e_avg]
    out_specs = [per_b((L, D_MODEL)), per_b((L, ATTN_W))]
    out_shape = [jax.ShapeDtypeStruct((B, L, D_MODEL), BF16), jax.ShapeDtypeStruct((B, L, ATTN_W), BF16)]
    qc = Q_CHUNK_LATENT if latent else Q_CHUNK
    past = cache[0].shape[2] if latent else 0
    assert B % nb == 0 and L % qc == 0 and L % LANES == 0 and past % LANES == 0
    n_groups = 2 * N_KV_HEADS
    scratch = [pltpu.VMEM((N_Q_HEADS, rows, HEAD_DIM), BF16), pltpu.VMEM((N_KV_HEADS, rows, HEAD_DIM), BF16),
               pltpu.VMEM((N_KV_HEADS, 2, rows, 2 * HEAD_DIM), BF16),
               pltpu.VMEM((rows, ATTN_W), F32),
               pltpu.VMEM((n_groups, HEAD_PAIRS * qc, past + L), F32),
               pltpu.VMEM((n_groups, HEAD_PAIRS * qc, LANES), F32),
               pltpu.VMEM((n_groups, HEAD_PAIRS * qc, 2 * HEAD_DIM), F32)]
    aliases = {}
    if latent:
        cache_k, cache_v = cache
        cache_spec = pl.BlockSpec((1, 1, past, KV_W), lambda b: (b, layer, 0, 0))
        in_specs += [full((L, LANES)), full((L, LANES)), cache_spec, cache_spec]
        args += [rope[0], rope[1], cache_k, cache_v]
        scratch += [pltpu.VMEM((N_KV_HEADS, past, HEAD_DIM), BF16),
                    pltpu.VMEM((N_KV_HEADS, 2, past, 2 * HEAD_DIM), BF16)]
    else:
        if kv_prev is not None:
            kv_spec = pl.BlockSpec((nb, 1, L, KV_W), lambda b: (b, layer, 0, 0))
        else:
            kv_spec = pl.BlockSpec((nb, DEPTH, L, KV_W), lambda b: (b, 0, 0, 0))
        kv_shape = jax.ShapeDtypeStruct((B, DEPTH, L, KV_W), F32)
        out_specs += [kv_spec, kv_spec]
        out_shape += [kv_shape, kv_shape]
        if kv_prev is not None:
            in_specs += [pl.BlockSpec(memory_space=pl.ANY)] * 2
            aliases = {len(args): 2, len(args) + 1: 3}
            args += list(kv_prev)
    return pl.pallas_call(
        functools.partial(_attn_kernel, latent, kv_prev is not None, layer, L, nb, qc, past),
        grid=(B // nb,),
        in_specs=in_specs,
        out_specs=out_specs,
        out_shape=out_shape,
        scratch_shapes=scratch,
        input_output_aliases=aliases,
        compiler_params=_cparams(1),
        name=f"attn_branch_L{L}",
    )(*args)


def _shifted(x, offset, period):
    rows = x.shape[0]
    t = lax.broadcasted_iota(jnp.int32, x.shape, 0)
    if rows != period:
        t = t % period
    rolled = pltpu.roll(x, (-offset) % rows, 0)
    valid = (t >= -offset) if offset < 0 else (t < period - offset)
    return jnp.where(valid, rolled, 0.0)


def _hyena_kernel(L, nb, nw, h_ref, *refs):
    w_refs = refs[:4 * nw]
    (sw0_ref, sw1_ref, sw2_ref, sb0_ref, sb1_ref, sb2_ref,
     fa_ref, fb_ref, fd_ref, bias_ref, fwd_ref, inv_ref, mix_out) = refs[4 * nw:]
    hb = h_ref[...].reshape(nb * L, D_MODEL)

    def project(part):
        cols = [_dot(hb, w_refs[part * nw + i][0]) for i in range(nw)]
        return cols[0] if nw == 1 else jnp.concatenate(cols, axis=1)

    def short_conv(part, sw_ref, sb_ref):
        xs = project(part)
        w = sw_ref[0]
        return sb_ref[0] + _shifted(xs, -1, L) * w[0:1] + xs * w[1:2] + _shifted(xs, 1, L) * w[2:3]

    x0 = short_conv(0, sw0_ref, sb0_ref)
    x1 = short_conv(1, sw1_ref, sb1_ref)
    hv = short_conv(2, sw2_ref, sb2_ref)
    gated = x0 * _silu(project(3))
    z = x1 * hv
    zb = z.astype(BF16)
    fa, fb, fd = fa_ref[0], fb_ref[0], fd_ref[0]
    cb = z.shape[1]
    z_cat = jnp.concatenate([zb[b * L:(b + 1) * L] for b in range(nb)], axis=1)
    zf = _dot(fwd_ref[...], z_cat)
    y_re, y_im = [], []
    for b in range(nb):
        re, im = zf[:L, b * cb:(b + 1) * cb], zf[L:, b * cb:(b + 1) * cb]
        y_re.append((re * fa - im * fb).astype(BF16))
        y_im.append((re * fb + im * fd).astype(BF16))
    y_all = (_dot(inv_ref[:, :L], jnp.concatenate(y_re, axis=1))
             + _dot(inv_ref[:, L:], jnp.concatenate(y_im, axis=1)))
    for b in range(nb):
        rs = slice(b * L, (b + 1) * L)
        y = y_all[:, b * cb:(b + 1) * cb] + z[rs] * bias_ref[0]
        mix_out[b] = (y * gated[rs]).astype(BF16)


def _hyena_branch(layer, nb, nw, hmod, w_in_b, short_w, short_b, filt, bias, fwd, inv):
    B, L, _ = hmod.shape
    cb = COL_BLOCK
    width = nw * cb
    nt = HY_W // width
    blocks_per_part = HY_W // cb
    fa, fb, fd = filt
    assert B % nb == 0 and L % LANES == 0 and HY_W % width == 0
    w_specs = [pl.BlockSpec((1, D_MODEL, cb),
                            functools.partial(lambda i, b, off: (layer, 0, off + i * nw),
                                              off=HY_BLOCK0 + part * blocks_per_part + j))
               for part in range(4) for j in range(nw)]
    sw_spec = lambda part: pl.BlockSpec((1, 3, width), lambda i, b: (layer, 0, part * nt + i))
    sb_spec = lambda part: pl.BlockSpec((1, 1, width), lambda i, b: (layer, 0, part * nt + i))
    filt_spec = pl.BlockSpec((1, L, width), lambda i, b: (layer, 0, i))
    return pl.pallas_call(
        functools.partial(_hyena_kernel, L, nb, nw),
        grid=(nt, B // nb),
        in_specs=[pl.BlockSpec((nb, L, D_MODEL), lambda i, b: (b, 0, 0))] + w_specs + [
            sw_spec(0), sw_spec(1), sw_spec(2),
            sb_spec(0), sb_spec(1), sb_spec(2),
            filt_spec, filt_spec, filt_spec,
            pl.BlockSpec((1, 1, width), lambda i, b: (layer, 0, i)),
            pl.BlockSpec((2 * L, L), lambda i, b: (0, 0)),
            pl.BlockSpec((L, 2 * L), lambda i, b: (0, 0)),
        ],
        out_specs=pl.BlockSpec((nb, L, width), lambda i, b: (b, 0, i)),
        out_shape=jax.ShapeDtypeStruct((B, L, HY_W), BF16),
        compiler_params=_cparams(2),
        name=f"hyena_branch_L{L}",
    )(hmod, *([w_in_b] * (4 * nw)), short_w, short_w, short_w, short_b, short_b, short_b,
      fa, fb, fd, bias, fwd, inv)


def _lru_kernel(latent, aliased, final, layer, L, nb, *refs):
    (h_ref, wx0_ref, wx1_ref, wg0_ref, wg1_ref, cw_ref, cb_ref, wa_ref, ba_ref, wxg_ref, bx_ref, lam_ref,
     x_ref, ma_ref, mh_ref, wo_ref, mod_ref, fg_ref) = refs[:18]
    refs = refs[18:]
    if latent:
        h0_ref, x_out, gate_w_s, a_s, b_s, y_s, g_s, part_s = refs
    else:
        if aliased:
            refs = refs[1:]
        x_out, hl_out, gate_w_s, a_s, b_s, y_s, g_s, part_s = refs
    half_w = LRU_W // 2
    blocks_per_half = half_w // LRU_BS
    n_tiles = L // SUBLANES

    @pl.when(pl.program_id(0) == 0)
    def _build_gate_weights():
        gate_w_s[...] = jnp.zeros(gate_w_s.shape, BF16)
        for half in range(2):
            for kind, (w_ref, d) in enumerate(((wa_ref, 0), (wxg_ref, 0), (wa_ref, 1), (wxg_ref, 1))):
                for j in range(blocks_per_half):
                    blk = (0.5 * w_ref[0, d, half * blocks_per_half + j]).astype(BF16)
                    gate_w_s[half, j * LRU_BS:(j + 1) * LRU_BS,
                             kind * half_w + j * LRU_BS:kind * half_w + (j + 1) * LRU_BS] = blk

    part_s[...] = (_dot(ma_ref[...].reshape(nb * L, ATTN_W), wo_ref[0, 0:ATTN_W, :])
                   + _dot(mh_ref[...].reshape(nb * L, HY_W), wo_ref[0, ATTN_W:ATTN_W + HY_W, :]))

    hb = h_ref[...].reshape(nb * L, D_MODEL)
    xs = jnp.concatenate([_dot(hb, wx0_ref[0]), _dot(hb, wx1_ref[0])], axis=1)
    g_s[:, :half_w] = _dot(hb, wg0_ref[0])
    g_s[:, half_w:] = _dot(hb, wg1_ref[0])
    cw = cw_ref[0]
    xc = (cb_ref[0] + _shifted(xs, -1, L) * cw[0:1] + xs * cw[1:2] + _shifted(xs, 1, L) * cw[2:3]
          + _shifted(xs, 2, L) * cw[3:4])
    xcb = xc.astype(BF16)
    c_half = (-0.5 * LRU_C) * jax.nn.softplus(-lam_ref[0])
    ba, bx = 0.5 * ba_ref[0], 0.5 * bx_ref[0]
    for half in range(2):
        cols = slice(half * half_w, (half + 1) * half_w)
        bias = jnp.concatenate([ba[0:1, cols], bx[0:1, cols], ba[1:2, cols], bx[1:2, cols]], axis=1)
        gates = _dot(xcb[:, cols], gate_w_s[half]) + bias
        xh_half = 0.5 * xc[:, cols]
        for d in range(2):
            t_r = jnp.tanh(gates[:, (2 * d) * half_w:(2 * d + 1) * half_w])
            t_i = jnp.tanh(gates[:, (2 * d + 1) * half_w:(2 * d + 2) * half_w])
            ch = c_half[d:d + 1, cols]
            log_a = ch * t_r + ch
            a = jnp.exp(log_a)
            y = jnp.tanh(log_a) * (-1.0 - a * a)
            root = jnp.where(y > 0.0, y * lax.rsqrt(y), 0.0)
            bcoef = root * (xh_half * t_i + xh_half)
            a_s[d, :, :, cols] = a.reshape(nb * n_tiles, SUBLANES, half_w)
            b_s[d, :, :, cols] = bcoef.reshape(nb * n_tiles, SUBLANES, half_w)

    if latent:
        init = []
        for b in range(nb):
            h0 = h0_ref[b, 0]
            init += [h0[0:1], h0[1:2]]
    else:
        init = [jnp.zeros((1, LRU_W), F32)] * (2 * nb)

    def advance(d, tile, rows_g, h):
        a = [a_s[d, tile, r:r + 1, :] for r in rows_g]
        c = [b_s[d, tile, r:r + 1, :] for r in rows_g]
        a01, c01 = a[1] * a[0], a[1] * c[0] + c[1]
        hs = [a[0] * h + c[0], a01 * h + c01]
        if len(rows_g) == 4:
            a23, c23 = a[3] * a[2], a[3] * c[2] + c[3]
            hs += [a[2] * hs[1] + c[2], (a23 * a01) * h + (a23 * c01 + c23)]
        for r, v in zip(rows_g, hs):
            y_s[d, tile, r:r + 1, :] = v
        return hs[-1]

    group = 4 if nb == 1 else 2

    def tile_step(i, carry):
        carry = list(carry)
        for g0 in range(0, SUBLANES, group):
            fwd_rows = list(range(g0, g0 + group))
            bwd_rows = [SUBLANES - 1 - r for r in fwd_rows]
            for b in range(nb):
                carry[2 * b] = advance(0, b * n_tiles + i, fwd_rows, carry[2 * b])
                carry[2 * b + 1] = advance(1, b * n_tiles + (n_tiles - 1 - i), bwd_rows, carry[2 * b + 1])
        return tuple(carry)

    def two_tiles(i, carry):
        return tile_step(2 * i + 1, tile_step(2 * i, carry))

    last = lax.fori_loop(0, n_tiles // 2, two_tiles, tuple(init))
    if not latent:
        if not aliased:
            hl_out[...] = jnp.zeros(hl_out.shape, F32)
        slot = 0 if aliased else layer
        for b in range(nb):
            hl_out[b, slot, 0:1, :] = last[2 * b]
            hl_out[b, slot, 1:2, :] = last[2 * b + 1]
    y = (y_s[0] + y_s[1]).reshape(nb * L, LRU_W)
    mix_l = (y * _silu(g_s[...])).astype(BF16)
    out = part_s[...] + _dot(mix_l, wo_ref[0, ATTN_W + HY_W:, :])
    xn = x_ref[...].reshape(nb * L, D_MODEL) + mod_ref[0, 0][2:3] * out
    if final:
        ms = jnp.mean(xn * xn, axis=-1, keepdims=True)
        xn = xn * lax.rsqrt(ms + EPS) * fg_ref[...]
    x_out[...] = xn.reshape(nb, L, D_MODEL)


def _lru_out_branch(latent, layer, nb, hmod, w_in_b, conv_w, conv_b, wa, ba, wx, bx, lam, x, mix_a, mix_h, w_out_b,
                    mod, final_g, state=None, hl_prev=None):
    B, L, _ = hmod.shape
    assert B % nb == 0 and L % (2 * SUBLANES) == 0
    final = layer == DEPTH - 1
    cb = COL_BLOCK
    half_w = LRU_W // 2
    rows = nb * L
    per_layer = lambda shape: pl.BlockSpec((1,) + shape, lambda b: (layer,) + (0,) * len(shape))
    w_spec = lambda j: pl.BlockSpec((1, D_MODEL, cb), lambda b: (layer, 0, LRU_BLOCK0 + j))
    gate_blocks = (2, LRU_BLOCKS, LRU_BS, LRU_BS)
    in_specs = [pl.BlockSpec((nb, L, D_MODEL), lambda b: (b, 0, 0)), w_spec(0), w_spec(1), w_spec(2), w_spec(3),
                per_layer((4, LRU_W)), per_layer((1, LRU_W)), per_layer(gate_blocks), per_layer((2, LRU_W)),
                per_layer(gate_blocks), per_layer((2, LRU_W)), per_layer((2, LRU_W))]
    per_b = lambda w: pl.BlockSpec((nb, L, w), lambda b: (b, 0, 0))
    mod_row = (lambda b: (layer, 1 + b, 0, 0)) if latent else (lambda b: (layer, 0, 0, 0))
    in_specs += [per_b(D_MODEL), per_b(ATTN_W), per_b(HY_W), per_layer((D_MIX, D_MODEL)),
                 pl.BlockSpec((1, 1, 3, D_MODEL), mod_row), pl.BlockSpec((1, D_MODEL), lambda b: (0, 0))]
    args = [hmod, w_in_b, w_in_b, w_in_b, w_in_b, conv_w, conv_b, wa, ba, wx, bx, lam,
            x, mix_a, mix_h, w_out_b, mod, final_g]
    out_specs = [per_b(D_MODEL)]
    out_shape = [jax.ShapeDtypeStruct((B, L, D_MODEL), F32)]
    aliases = {}
    if latent:
        in_specs.append(pl.BlockSpec((nb, 1, 2, LRU_W), lambda b: (b, layer, 0, 0)))
        args.append(state)
    else:
        if hl_prev is not None:
            out_specs.append(pl.BlockSpec((nb, 1, 2, LRU_W), lambda b: (b, layer, 0, 0)))
        else:
            out_specs.append(pl.BlockSpec((nb, DEPTH, 2, LRU_W), lambda b: (b, 0, 0, 0)))
        out_shape.append(jax.ShapeDtypeStruct((B, DEPTH, 2, LRU_W), F32))
        if hl_prev is not None:
            in_specs.append(pl.BlockSpec(memory_space=pl.ANY))
            aliases = {len(args): 1}
            args.append(hl_prev)
    scan_buf = pltpu.VMEM((2, rows // SUBLANES, SUBLANES, LRU_W), F32)
    return pl.pallas_call(
        functools.partial(_lru_kernel, latent, hl_prev is not None, final, layer, L, nb),
        grid=(B // nb,),
        in_specs=in_specs,
        out_specs=out_specs,
        out_shape=out_shape,
        scratch_shapes=[pltpu.VMEM((2, half_w, 4 * half_w), BF16), scan_buf, scan_buf, scan_buf,
                        pltpu.VMEM((rows, LRU_W), F32), pltpu.VMEM((rows, D_MODEL), F32)],
        input_output_aliases=aliases,
        compiler_params=_cparams(1),
        name=f"lru_out_L{L}",
    )(*args)


def _mixer_pass(latent, x, p, filt, dft, rope=None, cache=None, state=None):
    B, L, _ = x.shape
    fwd, inv = dft
    nb = 1 if latent else 4
    nb_hyena = 1 if latent else 4
    nw_hyena = HY_W // COL_BLOCK
    nb_lru = 1 if latent else 2
    kv, hl = None, None
    for l in range(DEPTH):
        outs = _attention_branch(latent, l, nb, x, p['mod'], p['norm_g'], p['w_in'], p['gq'], p['gk'], p['e_avg'],
                                 rope, cache, kv)
        hmod, mix_a = outs[0], outs[1]
        if not latent:
            kv = (outs[2], outs[3])
        mix_h = _hyena_branch(l, nb_hyena, nw_hyena, hmod, p['w_in'], p['short_w'], p['short_b'], filt, p['hy_bias'],
                              fwd, inv)
        outs = _lru_out_branch(latent, l, nb_lru, hmod, p['w_in'], p['conv_w'], p['conv_b'], p['wa'], p['ba'], p['wx'],
                               p['bx'], p['lam'], x, mix_a, mix_h, p['w_out'], p['mod'], p['final_g'], state, hl)
        x = outs[0]
        if not latent:
            hl = outs[1]
    return x, kv, hl


def kernel(x_prompt, x_sample, cache_k, cache_v, state_lru, c, c_ctx, norm_g, w_ada, b_ada, w_in, q_norm_g, k_norm_g,
           hy_short_w, hy_short_b, hy_filt_w1, hy_filt_b1, hy_filt_w2, hy_filt_b2, hy_filt_w3, hy_filt_freq,
           hy_filt_decay, hy_bias, lru_conv_w, lru_conv_b, lru_wa, lru_ba, lru_wx, lru_bx, lru_lambda, w_out, final_g):
    batch, seq, _ = x_prompt.shape
    dec_batch, dec_seq, _ = x_sample.shape
    past = cache_k.shape[2]

    cvecs = jnp.concatenate([c_ctx[None, :], c, jnp.zeros((MOD_ROWS - 1 - dec_batch, D_MODEL), F32)], axis=0)
    mod = _modulation(cvecs, w_ada, b_ada).reshape(DEPTH, MOD_ROWS, 3, D_MODEL)

    w1p = jnp.pad(hy_filt_w1, ((0, 0), (0, FEAT_PAD - HY_EMB), (0, 0)))
    dft, filt = {}, {}
    for L in (seq, dec_seq):
        dft[L] = tuple(jnp.asarray(m).astype(BF16) for m in _dft_tables(L))
        filt[L] = _filter_spectrum(L, jnp.asarray(_hyena_feats(L)), dft[L][0], w1p, hy_filt_b1[:, None, :],
                                   hy_filt_w2, hy_filt_b2[:, None, :], hy_filt_w3, hy_filt_freq,
                                   hy_filt_decay[:, None, :])

    params = {
        'mod': mod,
        'norm_g': norm_g[:, None, :],
        'w_in': w_in.astype(BF16),
        'gq': jnp.tile(q_norm_g, (1, N_Q_HEADS))[:, None, :],
        'gk': jnp.tile(k_norm_g, (1, N_KV_HEADS))[:, None, :],
        'e_avg': jnp.asarray(_head_average_matrix()).astype(BF16),
        'short_w': hy_short_w, 'short_b': hy_short_b[:, None, :], 'hy_bias': hy_bias[:, None, :],
        'conv_w': lru_conv_w, 'conv_b': lru_conv_b[:, None, :],
        'wa': lru_wa, 'ba': lru_ba, 'wx': lru_wx, 'bx': lru_bx, 'lam': lru_lambda,
        'w_out': w_out.astype(BF16),
        'final_g': final_g[None, :],
    }

    y_prompt, kv, new_lru = _mixer_pass(False, x_prompt, params, filt[seq], dft[seq])
    new_k = kv[0].reshape(batch, DEPTH, seq, N_KV_HEADS, HEAD_DIM)
    new_v = kv[1].reshape(batch, DEPTH, seq, N_KV_HEADS, HEAD_DIM)

    rope = tuple(jnp.asarray(t) for t in _rope_tables(dec_seq))
    cache = (cache_k.reshape(dec_batch, DEPTH, past, KV_W), cache_v.reshape(dec_batch, DEPTH, past, KV_W))
    y_sample, _, _ = _mixer_pass(True, x_sample, params, filt[dec_seq], dft[dec_seq], rope, cache, state_lru)
    return (y_prompt, y_sample, new_k, new_v, new_lru)
```

```python
import functools
import math

import numpy as np
import jax
import jax.numpy as jnp
from jax import lax
from jax.experimental import pallas as pl
from jax.experimental.pallas import tpu as pltpu

F32 = jnp.float32
BF16 = jnp.bfloat16

D_MODEL = 1024
DEPTH = 2
GRID_W = 64
HEAD_DIM = 64
N_Q_HEADS = 8
N_KV_HEADS = 2
Q_PER_KV = N_Q_HEADS // N_KV_HEADS
ATTN_W = N_Q_HEADS * HEAD_DIM
KV_W = N_KV_HEADS * HEAD_DIM
ROPE_THETA = 10000.0
ATTN_SCALE = HEAD_DIM ** -0.5
HY_W = 512
HY_BANDS = 16
HY_EMB = 2 * HY_BANDS + 1
HY_FH = 64
LRU_W = 512
LRU_BLOCKS = 8
LRU_BS = LRU_W // LRU_BLOCKS
LRU_C = 8.0
EPS = 1e-6

ATTN_COLS = ATTN_W + 2 * KV_W + ATTN_W
D_MIX = ATTN_W + HY_W + LRU_W

LANES = 128
SUBLANES = 8
MXU_DIM = 256
COL_BLOCK = MXU_DIM
HY_BLOCK0 = ATTN_COLS // COL_BLOCK
LRU_BLOCK0 = (ATTN_COLS + 4 * HY_W) // COL_BLOCK
Q_CHUNK = 256
Q_CHUNK_LATENT = 256
HEAD_PAIRS = Q_PER_KV // 2
FEAT_PAD = 128
MOD_ROWS = 8
VMEM_LIMIT = 56 * 1024 * 1024


def _cparams(n_axes):
    return pltpu.CompilerParams(dimension_semantics=("arbitrary",) * n_axes, vmem_limit_bytes=VMEM_LIMIT)


def _split_bf16(a):
    hi = a.astype(BF16)
    lo = (a - hi.astype(F32)).astype(BF16)
    return hi, lo


def _dot(a, b):
    return jnp.dot(a, b, preferred_element_type=F32)


def _dot3(a, b):
    a_hi, a_lo = _split_bf16(a)
    b_hi, b_lo = _split_bf16(b)
    return _dot(a_hi, b_hi) + _dot(a_hi, b_lo) + _dot(a_lo, b_hi)


def _silu(x):
    half = 0.5 * x
    return half * jnp.tanh(half) + half


@functools.lru_cache(maxsize=None)
def _dft_tables(L):
    n = 2 * L
    t = np.arange(L)
    kt = (t[:, None] * t[None, :]) % n
    ang = 2.0 * np.pi * kt / n
    cosm, sinm = np.cos(ang), np.sin(ang)
    alt = (-1.0) ** t
    f_s = -sinm
    f_s[0, :] = alt
    fwd = np.concatenate([cosm, f_s], axis=0)
    g_c = 2.0 * cosm.T / n
    g_c[:, 0] = 1.0 / n
    g_s = -2.0 * sinm.T / n
    g_s[:, 0] = alt / n
    inv = np.concatenate([g_c, g_s], axis=1)
    return fwd.astype(np.float32), inv.astype(np.float32)


@functools.lru_cache(maxsize=None)
def _hyena_feats(L):
    t = np.linspace(0.0, 1.0, L)[:, None]
    w = 2.0 * math.pi * np.arange(L)[:, None] / L
    f = np.linspace(1e-4, HY_BANDS - 1, HY_BANDS)[None, :]
    out = np.zeros((L, FEAT_PAD), np.float32)
    out[:, :HY_EMB] = np.concatenate([t, np.cos(f * w), -np.sin(f * w)], axis=-1)
    return out


@functools.lru_cache(maxsize=None)
def _rope_tables(L):
    rows = L // GRID_W
    row = np.repeat(np.arange(rows), GRID_W).astype(np.float64)
    col = np.tile(np.arange(GRID_W), rows).astype(np.float64)
    n_freq = HEAD_DIM // 4
    inv = ROPE_THETA ** (-np.arange(n_freq) / n_freq)
    ang = np.concatenate([row[:, None] * inv, col[:, None] * inv], axis=-1)
    cos = np.repeat(np.cos(ang), 2, axis=-1)
    sin = np.repeat(np.sin(ang), 2, axis=-1) * np.tile(np.array([-1.0, 1.0]), HEAD_DIM // 2)
    reps = LANES // HEAD_DIM
    return np.tile(cos, (1, reps)).astype(np.float32), np.tile(sin, (1, reps)).astype(np.float32)


@functools.lru_cache(maxsize=None)
def _head_average_matrix():
    return np.kron(np.eye(MXU_DIM // HEAD_DIM), np.full((HEAD_DIM, HEAD_DIM), 1.0 / HEAD_DIM)).astype(np.float32)


def _mod_kernel(c_ref, w_ref, b_ref, o_ref):
    s = _silu(c_ref[...])
    o_ref[0] = _dot3(s, w_ref[0]) + b_ref[0]


def _modulation(cvecs, w_ada, b_ada):
    tn = D_MODEL
    return pl.pallas_call(
        _mod_kernel,
        grid=(DEPTH, 3 * D_MODEL // tn),
        in_specs=[
            pl.BlockSpec((MOD_ROWS, D_MODEL), lambda l, j: (0, 0)),
            pl.BlockSpec((1, D_MODEL, tn), lambda l, j: (l, 0, j)),
            pl.BlockSpec((1, 1, tn), lambda l, j: (l, 0, j)),
        ],
        out_specs=pl.BlockSpec((1, MOD_ROWS, tn), lambda l, j: (l, 0, j)),
        out_shape=jax.ShapeDtypeStruct((DEPTH, MOD_ROWS, 3 * D_MODEL), F32),
        compiler_params=_cparams(2),
        name="adaln_mod",
    )(cvecs, w_ada, b_ada.reshape(DEPTH, 1, 3 * D_MODEL))


def _filter_kernel(L, feats_ref, w1_ref, b1_ref, w2_ref, b2_ref, w3_ref, freq_ref, decay_ref,
                   fwd_ref, a_ref, b_ref, d_ref):
    feats = feats_ref[...]
    t = feats[:, 0:1]
    freq = freq_ref[0]
    hdn = jnp.sin(freq[0:1] * (_dot3(feats, w1_ref[0]) + b1_ref[0]))
    hdn = jnp.sin(freq[1:2] * (_dot3(hdn, w2_ref[0]) + b2_ref[0]))
    h = _dot3(hdn, w3_ref[0]) * jnp.exp(-t * decay_ref[0])
    row = lax.broadcasted_iota(jnp.int32, (L, HY_W), 0)
    h_fwd = h[:, :HY_W]
    h_bwd = jnp.where(row >= 1, h[:, HY_W:], 0.0)
    inv_norm = 1.0 / jnp.sum(jnp.abs(h_fwd) + jnp.abs(h_bwd), axis=0, keepdims=True)
    even = h_fwd + h_bwd
    odd = h_fwd - h_bwd
    h_re = _dot(fwd_ref[0:L, :], even.astype(BF16)) * inv_norm
    h_im = _dot(fwd_ref[L:2 * L, :], odd.astype(BF16)) * inv_norm
    alt = jnp.where(row % 2 == 0, 1.0, -1.0)
    nyq = jnp.sum(even * alt, axis=0, keepdims=True) * inv_norm
    a_ref[0] = h_re
    b_ref[0] = jnp.where(row >= 1, h_im, 0.0)
    d_ref[0] = jnp.where(row >= 1, h_re, nyq)


def _filter_spectrum(L, feats, fwd, w1p, b1, w2, b2, w3, freq, decay):
    full = lambda shape: pl.BlockSpec(shape, lambda l: (0,) * len(shape))
    per_layer = lambda shape: pl.BlockSpec((1,) + shape, lambda l: (l,) + (0,) * len(shape))
    out = jax.ShapeDtypeStruct((DEPTH, L, HY_W), F32)
    return pl.pallas_call(
        functools.partial(_filter_kernel, L),
        grid=(DEPTH,),
        in_specs=[
            full((L, FEAT_PAD)),
            per_layer((FEAT_PAD, HY_FH)), per_layer((1, HY_FH)),
            per_layer((HY_FH, HY_FH)), per_layer((1, HY_FH)),
            per_layer((HY_FH, 2 * HY_W)), per_layer((2, HY_FH)), per_layer((1, 2 * HY_W)),
            full((2 * L, L)),
        ],
        out_specs=[per_layer((L, HY_W))] * 3,
        out_shape=[out, out, out],
        compiler_params=_cparams(1),
        name=f"hyena_filter_L{L}",
    )(feats, w1p, b1, w2, b2, w3, freq, decay, fwd)


def _rope(x, cos, sin):
    lane = lax.broadcasted_iota(jnp.int32, x.shape, 1)
    partner = jnp.where(lane % 2 == 0, pltpu.roll(x, LANES - 1, 1), pltpu.roll(x, 1, 1))
    return x * cos + partner * sin


def _head_mean_square(x, e):
    sq = (x * x).astype(BF16)
    width = x.shape[1]
    if width <= MXU_DIM:
        return _dot(sq, e[:width, :width])
    parts = [_dot(sq[:, i:i + MXU_DIM], e) for i in range(0, width, MXU_DIM)]
    return jnp.concatenate(parts, axis=1)


def _store_kv_heads(k_dst, v_dst, k, v):
    ones = jnp.ones((v.shape[0], HEAD_DIM), BF16)
    for j in range(N_KV_HEADS):
        js = slice(j * HEAD_DIM, (j + 1) * HEAD_DIM)
        vj = v[:, js].astype(BF16)
        k_dst[j] = k[:, js].astype(BF16)
        v_dst[j, 0, :, 0:HEAD_DIM] = vj
        v_dst[j, 0, :, HEAD_DIM:2 * HEAD_DIM] = ones
        v_dst[j, 1, :, 0:HEAD_DIM] = ones
        v_dst[j, 1, :, HEAD_DIM:2 * HEAD_DIM] = vj


def _attn_kernel(latent, aliased, layer, L, nb, qc, past, *refs):
    x_ref, mod_ref, ng_ref, w_ref, gq_ref, gk_ref, e_ref = refs[:7]
    refs = refs[7:]
    if latent:
        (cos_ref, sin_ref, ck_ref, cv_ref, h_out, mix_out,
         q_s, k_s, v_s, g_s, s_s, mx_s, acc_s, ck_s, cv_s) = refs
    else:
        if aliased:
            refs = refs[2:]
        h_out, mix_out, k_out, v_out, q_s, k_s, v_s, g_s, s_s, mx_s, acc_s = refs
    rows = nb * L

    x = x_ref[...].reshape(rows, D_MODEL)
    m = mod_ref[0, 0]
    shift, scale = m[0:1], m[1:2]
    ms = jnp.mean(x * x, axis=-1, keepdims=True)
    h = (x * lax.rsqrt(ms + EPS)) * (ng_ref[0] * (1.0 + scale)) + shift
    hb = h.astype(BF16)
    h_out[...] = hb.reshape(nb, L, D_MODEL)

    u = _dot(hb, w_ref[0])
    q = u[:, :ATTN_W]
    k = u[:, ATTN_W:ATTN_W + KV_W]
    v = u[:, ATTN_W + KV_W:ATTN_W + 2 * KV_W]
    g_s[...] = u[:, ATTN_W + 2 * KV_W:]
    e = e_ref[...]
    qn = q * lax.rsqrt(_head_mean_square(q, e) + EPS) * gq_ref[0]
    kn = k * lax.rsqrt(_head_mean_square(k, e) + EPS) * gk_ref[0]
    if latent:
        cos, sin = cos_ref[...], sin_ref[...]
        qn = jnp.concatenate([_rope(qn[:, i:i + LANES], cos, sin) for i in range(0, ATTN_W, LANES)], axis=1)
        kn = _rope(kn, cos, sin)
        _store_kv_heads(ck_s, cv_s, ck_ref[0, 0], cv_ref[0, 0])
    else:
        for b in range(nb):
            kb, vb = kn[b * L:(b + 1) * L], v[b * L:(b + 1) * L]
            if aliased:
                k_out[b, 0] = kb
                v_out[b, 0] = vb
            else:
                for slot in range(DEPTH):
                    k_out[b, slot] = kb if slot == layer else jnp.zeros_like(kb)
                    v_out[b, slot] = vb if slot == layer else jnp.zeros_like(vb)
    qb = (qn * ATTN_SCALE).astype(BF16)
    for hd in range(N_Q_HEADS):
        j, g = divmod(hd, Q_PER_KV)
        q_s[j * Q_PER_KV + (g % 2) * HEAD_PAIRS + g // 2] = qb[:, hd * HEAD_DIM:(hd + 1) * HEAD_DIM]
    _store_kv_heads(k_s, v_s, kn, v)

    contract_last = (((1,), (1,)), ((), ()))
    chunks_per_seq = L // qc
    lane = lax.broadcasted_iota(jnp.int32, (qc, LANES), 1)
    groups = [(j, e) for j in range(N_KV_HEADS) for e in range(2)]

    def chunk(c, carry):
        r0 = pl.multiple_of(c * qc, qc)
        k0 = pl.multiple_of((c // chunks_per_seq) * L, L)
        for idx, (j, e) in enumerate(groups):
            slot = j * Q_PER_KV + e * HEAD_PAIRS
            qh = q_s[slot:slot + HEAD_PAIRS, pl.ds(r0, qc), :].reshape(HEAD_PAIRS * qc, HEAD_DIM)
            s = lax.dot_general(qh, k_s[j, pl.ds(k0, L), :], contract_last, preferred_element_type=F32)
            mx = jnp.max(s, axis=-1, keepdims=True)
            if latent:
                s0 = lax.dot_general(qh, ck_s[j], contract_last, preferred_element_type=F32)
                mx = jnp.maximum(mx, jnp.max(s0, axis=-1, keepdims=True))
                s_s[idx, :, 0:past] = s0
                s_s[idx, :, past:] = s
            else:
                s_s[idx] = s
            mx_s[idx] = jnp.broadcast_to(mx, (HEAD_PAIRS * qc, LANES))
        for idx, (j, e) in enumerate(groups):
            p = jnp.exp(s_s[idx] - jnp.tile(mx_s[idx], (1, (past + L) // LANES))).astype(BF16)
            if latent:
                acc = _dot(p[:, :past], cv_s[j, e]) + _dot(p[:, past:], v_s[j, e, pl.ds(k0, L), :])
            else:
                acc = _dot(p, v_s[j, e, pl.ds(k0, L), :])
            acc_s[idx] = acc
        b_idx = c // chunks_per_seq
        row0 = pl.multiple_of((c % chunks_per_seq) * qc, qc)
        for j in range(N_KV_HEADS):
            for t in range(HEAD_PAIRS):
                a_even = acc_s[2 * j, t * qc:(t + 1) * qc, :]
                a_odd = acc_s[2 * j + 1, t * qc:(t + 1) * qc, :]
                out = jnp.where(lane < HEAD_DIM, a_even, a_odd)
                den = pltpu.roll(jnp.where(lane < HEAD_DIM, a_odd, a_even), HEAD_DIM, 1)
                blk = slice((j * HEAD_PAIRS + t) * LANES, (j * HEAD_PAIRS + t + 1) * LANES)
                gate = _silu(g_s[pl.ds(r0, qc), blk])
                mix_out[b_idx, pl.ds(row0, qc), blk] = (out / den * gate).astype(BF16)
        return carry

    lax.fori_loop(0, nb * chunks_per_seq, chunk, 0, unroll=4)


def _attention_branch(latent, layer, nb, x, mod, norm_g, w_in_b, gq, gk, e_avg, rope=None, cache=None, kv_prev=None):
    B, L, _ = x.shape
    rows = nb * L
    assert nb == 1 or not latent
    full = lambda shape: pl.BlockSpec(shape, lambda b: (0,) * len(shape))
    per_b = lambda shape: pl.BlockSpec((nb,) + shape, lambda b: (b,) + (0,) * len(shape))
    per_layer = lambda shape: pl.BlockSpec((1,) + shape, lambda b: (layer,) + (0,) * len(shape))
    mod_spec = pl.BlockSpec((1, 1, 3, D_MODEL), (lambda b: (layer, 1 + b, 0, 0)) if latent else (lambda b: (layer, 0, 0, 0)))
    in_specs = [per_b((L, D_MODEL)), mod_spec, per_layer((1, D_MODEL)), per_layer((D_MODEL, ATTN_COLS)),
                per_layer((1, ATTN_W)), per_layer((1, KV_W)), full((MXU_DIM, MXU_DIM))]
    args = [x, mod, norm_g, w_in_b, gq, gk, e_avg]
    out_specs = [per_b((L, D_MODEL)), per_b((L, ATTN_W))]
    out_shape = [jax.ShapeDtypeStruct((B, L, D_MODEL), BF16), jax.ShapeDtypeStruct((B, L, ATTN_W), BF16)]
    qc = Q_CHUNK_LATENT if latent else Q_CHUNK
    past = cache[0].shape[2] if latent else 0
    assert B % nb == 0 and L % qc == 0 and L % LANES == 0 and past % LANES == 0
    n_groups = 2 * N_KV_HEADS
    scratch = [pltpu.VMEM((N_Q_HEADS, rows, HEAD_DIM), BF16), pltpu.VMEM((N_KV_HEADS, rows, HEAD_DIM), BF16),
               pltpu.VMEM((N_KV_HEADS, 2, rows, 2 * HEAD_DIM), BF16),
               pltpu.VMEM((rows, ATTN_W), F32),
               pltpu.VMEM((n_groups, HEAD_PAIRS * qc, past + L), F32),
               pltpu.VMEM((n_groups, HEAD_PAIRS * qc, LANES), F32),
               pltpu.VMEM((n_groups, HEAD_PAIRS * qc, 2 * HEAD_DIM), F32)]
    aliases = {}
    if latent:
        cache_k, cache_v = cache
        cache_spec = pl.BlockSpec((1, 1, past, KV_W), lambda b: (b, layer, 0, 0))
        in_specs += [full((L, LANES)), full((L, LANES)), cache_spec, cache_spec]
        args += [rope[0], rope[1], cache_k, cache_v]
        scratch += [pltpu.VMEM((N_KV_HEADS, past, HEAD_DIM), BF16),
                    pltpu.VMEM((N_KV_HEADS, 2, past, 2 * HEAD_DIM), BF16)]
    else:
        if kv_prev is not None:
            kv_spec = pl.BlockSpec((nb, 1, L, KV_W), lambda b: (b, layer, 0, 0))
        else:
            kv_spec = pl.BlockSpec((nb, DEPTH, L, KV_W), lambda b: (b, 0, 0, 0))
        kv_shape = jax.ShapeDtypeStruct((B, DEPTH, L, KV_W), F32)
        out_specs += [kv_spec, kv_spec]
        out_shape += [kv_shape, kv_shape]
        if kv_prev is not None:
            in_specs += [pl.BlockSpec(memory_space=pl.ANY)] * 2
            aliases = {len(args): 2, len(args) + 1: 3}
            args += list(kv_prev)
    return pl.pallas_call(
        functools.partial(_attn_kernel, latent, kv_prev is not None, layer, L, nb, qc, past),
        grid=(B // nb,),
        in_specs=in_specs,
        out_specs=out_specs,
        out_shape=out_shape,
        scratch_shapes=scratch,
        input_output_aliases=aliases,
        compiler_params=_cparams(1),
        name=f"attn_branch_L{L}",
    )(*args)


def _shifted(x, offset, period):
    rows = x.shape[0]
    t = lax.broadcasted_iota(jnp.int32, x.shape, 0)
    if rows != period:
        t = t % period
    rolled = pltpu.roll(x, (-offset) % rows, 0)
    valid = (t >= -offset) if offset < 0 else (t < period - offset)
    return jnp.where(valid, rolled, 0.0)


def _hyena_kernel(L, nb, nw, h_ref, *refs):
    w_refs = refs[:4 * nw]
    (sw0_ref, sw1_ref, sw2_ref, sb0_ref, sb1_ref, sb2_ref,
     fa_ref, fb_ref, fd_ref, bias_ref, fwd_ref, inv_ref, mix_out) = refs[4 * nw:]
    hb = h_ref[...].reshape(nb * L, D_MODEL)

    def project(part):
        cols = [_dot(hb, w_refs[part * nw + i][0]) for i in range(nw)]
        return cols[0] if nw == 1 else jnp.concatenate(cols, axis=1)

    def short_conv(part, sw_ref, sb_ref):
        xs = project(part)
        w = sw_ref[0]
        return sb_ref[0] + _shifted(xs, -1, L) * w[0:1] + xs * w[1:2] + _shifted(xs, 1, L) * w[2:3]

    x0 = short_conv(0, sw0_ref, sb0_ref)
    x1 = short_conv(1, sw1_ref, sb1_ref)
    hv = short_conv(2, sw2_ref, sb2_ref)
    gated = x0 * _silu(project(3))
    z = x1 * hv
    zb = z.astype(BF16)
    fa, fb, fd = fa_ref[0], fb_ref[0], fd_ref[0]
    cb = z.shape[1]
    z_cat = jnp.concatenate([zb[b * L:(b + 1) * L] for b in range(nb)], axis=1)
    zf = _dot(fwd_ref[...], z_cat)
    y_re, y_im = [], []
    for b in range(nb):
        re, im = zf[:L, b * cb:(b + 1) * cb], zf[L:, b * cb:(b + 1) * cb]
        y_re.append((re * fa - im * fb).astype(BF16))
        y_im.append((re * fb + im * fd).astype(BF16))
    y_all = (_dot(inv_ref[:, :L], jnp.concatenate(y_re, axis=1))
             + _dot(inv_ref[:, L:], jnp.concatenate(y_im, axis=1)))
    for b in range(nb):
        rs = slice(b * L, (b + 1) * L)
        y = y_all[:, b * cb:(b + 1) * cb] + z[rs] * bias_ref[0]
        mix_out[b] = (y * gated[rs]).astype(BF16)


def _hyena_branch(layer, nb, nw, hmod, w_in_b, short_w, short_b, filt, bias, fwd, inv):
    B, L, _ = hmod.shape
    cb = COL_BLOCK
    width = nw * cb
    nt = HY_W // width
    blocks_per_part = HY_W // cb
    fa, fb, fd = filt
    assert B % nb == 0 and L % LANES == 0 and HY_W % width == 0
    w_specs = [pl.BlockSpec((1, D_MODEL, cb),
                            functools.partial(lambda i, b, off: (layer, 0, off + i * nw),
                                              off=HY_BLOCK0 + part * blocks_per_part + j))
               for part in range(4) for j in range(nw)]
    sw_spec = lambda part: pl.BlockSpec((1, 3, width), lambda i, b: (layer, 0, part * nt + i))
    sb_spec = lambda part: pl.BlockSpec((1, 1, width), lambda i, b: (layer, 0, part * nt + i))
    filt_spec = pl.BlockSpec((1, L, width), lambda i, b: (layer, 0, i))
    return pl.pallas_call(
        functools.partial(_hyena_kernel, L, nb, nw),
        grid=(nt, B // nb),
        in_specs=[pl.BlockSpec((nb, L, D_MODEL), lambda i, b: (b, 0, 0))] + w_specs + [
            sw_spec(0), sw_spec(1), sw_spec(2),
            sb_spec(0), sb_spec(1), sb_spec(2),
            filt_spec, filt_spec, filt_spec,
            pl.BlockSpec((1, 1, width), lambda i, b: (layer, 0, i)),
            pl.BlockSpec((2 * L, L), lambda i, b: (0, 0)),
            pl.BlockSpec((L, 2 * L), lambda i, b: (0, 0)),
        ],
        out_specs=pl.BlockSpec((nb, L, width), lambda i, b: (b, 0, i)),
        out_shape=jax.ShapeDtypeStruct((B, L, HY_W), BF16),
        compiler_params=_cparams(2),
        name=f"hyena_branch_L{L}",
    )(hmod, *([w_in_b] * (4 * nw)), short_w, short_w, short_w, short_b, short_b, short_b,
      fa, fb, fd, bias, fwd, inv)


def _lru_kernel(latent, aliased, final, layer, L, nb, *refs):
    (h_ref, wx0_ref, wx1_ref, wg0_ref, wg1_ref, cw_ref, cb_ref, wa_ref, ba_ref, wxg_ref, bx_ref, lam_ref,
     x_ref, ma_ref, mh_ref, wo_ref, mod_ref, fg_ref) = refs[:18]
    refs = refs[18:]
    if latent:
        h0_ref, x_out, gate_w_s, a_s, b_s, y_s, g_s, part_s, x_s, x_sem = refs
    else:
        if aliased:
            refs = refs[1:]
        x_out, hl_out, gate_w_s, a_s, b_s, y_s, g_s, part_s, x_s, x_sem = refs
    half_w = LRU_W // 2
    blocks_per_half = half_w // LRU_BS
    n_tiles = L // SUBLANES

    x_copy = pltpu.make_async_copy(x_ref.at[pl.ds(pl.program_id(0) * nb, nb)], x_s, x_sem)
    x_copy.start()

    @pl.when(pl.program_id(0) == 0)
    def _build_gate_weights():
        gate_w_s[...] = jnp.zeros(gate_w_s.shape, BF16)
        for half in range(2):
            for kind, (w_ref, d) in enumerate(((wa_ref, 0), (wxg_ref, 0), (wa_ref, 1), (wxg_ref, 1))):
                for j in range(blocks_per_half):
                    blk = (0.5 * w_ref[0, d, half * blocks_per_half + j]).astype(BF16)
                    gate_w_s[half, j * LRU_BS:(j + 1) * LRU_BS,
                             kind * half_w + j * LRU_BS:kind * half_w + (j + 1) * LRU_BS] = blk

    part_s[...] = (_dot(ma_ref[...].reshape(nb * L, ATTN_W), wo_ref[0, 0:ATTN_W, :])
                   + _dot(mh_ref[...].reshape(nb * L, HY_W), wo_ref[0, ATTN_W:ATTN_W + HY_W, :]))

    hb = h_ref[...].reshape(nb * L, D_MODEL)
    xs = jnp.concatenate([_dot(hb, wx0_ref[0]), _dot(hb, wx1_ref[0])], axis=1)
    g_s[:, :half_w] = _dot(hb, wg0_ref[0])
    g_s[:, half_w:] = _dot(hb, wg1_ref[0])
    cw = cw_ref[0]
    xc = (cb_ref[0] + _shifted(xs, -1, L) * cw[0:1] + xs * cw[1:2] + _shifted(xs, 1, L) * cw[2:3]
          + _shifted(xs, 2, L) * cw[3:4])
    xcb = xc.astype(BF16)
    c_half = (-0.5 * LRU_C) * jax.nn.softplus(-lam_ref[0])
    ba, bx = 0.5 * ba_ref[0], 0.5 * bx_ref[0]
    for half in range(2):
        cols = slice(half * half_w, (half + 1) * half_w)
        bias = jnp.concatenate([ba[0:1, cols], bx[0:1, cols], ba[1:2, cols], bx[1:2, cols]], axis=1)
        gates = _dot(xcb[:, cols], gate_w_s[half]) + bias
        xh_half = 0.5 * xc[:, cols]
        for d in range(2):
            t_r = jnp.tanh(gates[:, (2 * d) * half_w:(2 * d + 1) * half_w])
            t_i = jnp.tanh(gates[:, (2 * d + 1) * half_w:(2 * d + 2) * half_w])
            ch = c_half[d:d + 1, cols]
            log_a = ch * t_r + ch
            a = jnp.exp(log_a)
            y = jnp.tanh(log_a) * (-1.0 - a * a)
            root = jnp.where(y > 0.0, y * lax.rsqrt(y), 0.0)
            bcoef = root * (xh_half * t_i + xh_half)
            a_s[d, :, :, cols] = a.reshape(nb * n_tiles, SUBLANES, half_w)
            b_s[d, :, :, cols] = bcoef.reshape(nb * n_tiles, SUBLANES, half_w)

    if latent:
        init = []
        for b in range(nb):
            h0 = h0_ref[b, 0]
            init += [h0[0:1], h0[1:2]]
    else:
        init = [jnp.zeros((1, LRU_W), F32)] * (2 * nb)

    def advance(d, tile, rows_g, h):
        a = [a_s[d, tile, r:r + 1, :] for r in rows_g]
        c = [b_s[d, tile, r:r + 1, :] for r in rows_g]
        a01, c01 = a[1] * a[0], a[1] * c[0] + c[1]
        hs = [a[0] * h + c[0], a01 * h + c01]
        if len(rows_g) == 4:
            a23, c23 = a[3] * a[2], a[3] * c[2] + c[3]
            hs += [a[2] * hs[1] + c[2], (a23 * a01) * h + (a23 * c01 + c23)]
        for r, v in zip(rows_g, hs):
            y_s[d, tile, r:r + 1, :] = v
        return hs[-1]

    group = 4 if nb == 1 else 2

    def tile_step(i, carry):
        carry = list(carry)
        for g0 in range(0, SUBLANES, group):
            fwd_rows = list(range(g0, g0 + group))
            bwd_rows = [SUBLANES - 1 - r for r in fwd_rows]
            for b in range(nb):
                carry[2 * b] = advance(0, b * n_tiles + i, fwd_rows, carry[2 * b])
                carry[2 * b + 1] = advance(1, b * n_tiles + (n_tiles - 1 - i), bwd_rows, carry[2 * b + 1])
        return tuple(carry)

    def two_tiles(i, carry):
        return tile_step(2 * i + 1, tile_step(2 * i, carry))

    last = lax.fori_loop(0, n_tiles // 2, two_tiles, tuple(init))
    if not latent:
        if not aliased:
            hl_out[...] = jnp.zeros(hl_out.shape, F32)
        slot = 0 if aliased else layer
        for b in range(nb):
            hl_out[b, slot, 0:1, :] = last[2 * b]
            hl_out[b, slot, 1:2, :] = last[2 * b + 1]
    y = (y_s[0] + y_s[1]).reshape(nb * L, LRU_W)
    mix_l = (y * _silu(g_s[...])).astype(BF16)
    out = part_s[...] + _dot(mix_l, wo_ref[0, ATTN_W + HY_W:, :])
    x_copy.wait()
    xn = x_s[...].reshape(nb * L, D_MODEL) + mod_ref[0, 0][2:3] * out
    if final:
        ms = jnp.mean(xn * xn, axis=-1, keepdims=True)
        xn = xn * lax.rsqrt(ms + EPS) * fg_ref[...]
    x_out[...] = xn.reshape(nb, L, D_MODEL)


def _lru_out_branch(latent, layer, nb, hmod, w_in_b, conv_w, conv_b, wa, ba, wx, bx, lam, x, mix_a, mix_h, w_out_b,
                    mod, final_g, state=None, hl_prev=None):
    B, L, _ = hmod.shape
    assert B % nb == 0 and L % (2 * SUBLANES) == 0
    final = layer == DEPTH - 1
    cb = COL_BLOCK
    half_w = LRU_W // 2
    rows = nb * L
    per_layer = lambda shape: pl.BlockSpec((1,) + shape, lambda b: (layer,) + (0,) * len(shape))
    w_spec = lambda j: pl.BlockSpec((1, D_MODEL, cb), lambda b: (layer, 0, LRU_BLOCK0 + j))
    gate_blocks = (2, LRU_BLOCKS, LRU_BS, LRU_BS)
    in_specs = [pl.BlockSpec((nb, L, D_MODEL), lambda b: (b, 0, 0)), w_spec(0), w_spec(1), w_spec(2), w_spec(3),
                per_layer((4, LRU_W)), per_layer((1, LRU_W)), per_layer(gate_blocks), per_layer((2, LRU_W)),
                per_layer(gate_blocks), per_layer((2, LRU_W)), per_layer((2, LRU_W))]
    per_b = lambda w: pl.BlockSpec((nb, L, w), lambda b: (b, 0, 0))
    mod_row = (lambda b: (layer, 1 + b, 0, 0)) if latent else (lambda b: (layer, 0, 0, 0))
    in_specs += [pl.BlockSpec(memory_space=pl.ANY), per_b(ATTN_W), per_b(HY_W), per_layer((D_MIX, D_MODEL)),
                 pl.BlockSpec((1, 1, 3, D_MODEL), mod_row), pl.BlockSpec((1, D_MODEL), lambda b: (0, 0))]
    args = [hmod, w_in_b, w_in_b, w_in_b, w_in_b, conv_w, conv_b, wa, ba, wx, bx, lam,
            x, mix_a, mix_h, w_out_b, mod, final_g]
    out_specs = [per_b(D_MODEL)]
    out_shape = [jax.ShapeDtypeStruct((B, L, D_MODEL), F32)]
    aliases = {}
    if latent:
        in_specs.append(pl.BlockSpec((nb, 1, 2, LRU_W), lambda b: (b, layer, 0, 0)))
        args.append(state)
    else:
        if hl_prev is not None:
            out_specs.append(pl.BlockSpec((nb, 1, 2, LRU_W), lambda b: (b, layer, 0, 0)))
        else:
            out_specs.append(pl.BlockSpec((nb, DEPTH, 2, LRU_W), lambda b: (b, 0, 0, 0)))
        out_shape.append(jax.ShapeDtypeStruct((B, DEPTH, 2, LRU_W), F32))
        if hl_prev is not None:
            in_specs.append(pl.BlockSpec(memory_space=pl.ANY))
            aliases = {len(args): 1}
            args.append(hl_prev)
    scan_buf = pltpu.VMEM((2, rows // SUBLANES, SUBLANES, LRU_W), F32)
    return pl.pallas_call(
        functools.partial(_lru_kernel, latent, hl_prev is not None, final, layer, L, nb),
        grid=(B // nb,),
        in_specs=in_specs,
        out_specs=out_specs,
        out_shape=out_shape,
        scratch_shapes=[pltpu.VMEM((2, half_w, 4 * half_w), BF16), scan_buf, scan_buf, scan_buf,
                        pltpu.VMEM((rows, LRU_W), F32), pltpu.VMEM((rows, D_MODEL), F32),
                        pltpu.VMEM((nb, L, D_MODEL), F32), pltpu.SemaphoreType.DMA],
        input_output_aliases=aliases,
        compiler_params=_cparams(1),
        name=f"lru_out_L{L}",
    )(*args)


def _mixer_pass(latent, x, p, filt, dft, rope=None, cache=None, state=None):
    B, L, _ = x.shape
    fwd, inv = dft
    nb = 1 if latent else 4
    nb_hyena = 1 if latent else 4
    nw_hyena = HY_W // COL_BLOCK
    nb_lru = 1 if latent else 2
    kv, hl = None, None
    for l in range(DEPTH):
        outs = _attention_branch(latent, l, nb, x, p['mod'], p['norm_g'], p['w_in'], p['gq'], p['gk'], p['e_avg'],
                                 rope, cache, kv)
        hmod, mix_a = outs[0], outs[1]
        if not latent:
            kv = (outs[2], outs[3])
        mix_h = _hyena_branch(l, nb_hyena, nw_hyena, hmod, p['w_in'], p['short_w'], p['short_b'], filt, p['hy_bias'],
                              fwd, inv)
        outs = _lru_out_branch(latent, l, nb_lru, hmod, p['w_in'], p['conv_w'], p['conv_b'], p['wa'], p['ba'], p['wx'],
                               p['bx'], p['lam'], x, mix_a, mix_h, p['w_out'], p['mod'], p['final_g'], state, hl)
        x = outs[0]
        if not latent:
            hl = outs[1]
    return x, kv, hl


def kernel(x_prompt, x_sample, cache_k, cache_v, state_lru, c, c_ctx, norm_g, w_ada, b_ada, w_in, q_norm_g, k_norm_g,
           hy_short_w, hy_short_b, hy_filt_w1, hy_filt_b1, hy_filt_w2, hy_filt_b2, hy_filt_w3, hy_filt_freq,
           hy_filt_decay, hy_bias, lru_conv_w, lru_conv_b, lru_wa, lru_ba, lru_wx, lru_bx, lru_lambda, w_out, final_g):
    batch, seq, _ = x_prompt.shape
    dec_batch, dec_seq, _ = x_sample.shape
    past = cache_k.shape[2]

    cvecs = jnp.concatenate([c_ctx[None, :], c, jnp.zeros((MOD_ROWS - 1 - dec_batch, D_MODEL), F32)], axis=0)
    mod = _modulation(cvecs, w_ada, b_ada).reshape(DEPTH, MOD_ROWS, 3, D_MODEL)

    w1p = jnp.pad(hy_filt_w1, ((0, 0), (0, FEAT_PAD - HY_EMB), (0, 0)))
    dft, filt = {}, {}
    for L in (seq, dec_seq):
        dft[L] = tuple(jnp.asarray(m).astype(BF16) for m in _dft_tables(L))
        filt[L] = _filter_spectrum(L, jnp.asarray(_hyena_feats(L)), dft[L][0], w1p, hy_filt_b1[:, None, :],
                                   hy_filt_w2, hy_filt_b2[:, None, :], hy_filt_w3, hy_filt_freq,
                                   hy_filt_decay[:, None, :])

    params = {
        'mod': mod,
        'norm_g': norm_g[:, None, :],
        'w_in': w_in.astype(BF16),
        'gq': jnp.tile(q_norm_g, (1, N_Q_HEADS))[:, None, :],
        'gk': jnp.tile(k_norm_g, (1, N_KV_HEADS))[:, None, :],
        'e_avg': jnp.asarray(_head_average_matrix()).astype(BF16),
        'short_w': hy_short_w, 'short_b': hy_short_b[:, None, :], 'hy_bias': hy_bias[:, None, :],
        'conv_w': lru_conv_w, 'conv_b': lru_conv_b[:, None, :],
        'wa': lru_wa, 'ba': lru_ba, 'wx': lru_wx, 'bx': lru_bx, 'lam': lru_lambda,
        'w_out': w_out.astype(BF16),
        'final_g': final_g[None, :],
    }

    y_prompt, kv, new_lru = _mixer_pass(False, x_prompt, params, filt[seq], dft[seq])
    new_k = kv[0].reshape(batch, DEPTH, seq, N_KV_HEADS, HEAD_DIM)
    new_v = kv[1].reshape(batch, DEPTH, seq, N_KV_HEADS, HEAD_DIM)

    rope = tuple(jnp.asarray(t) for t in _rope_tables(dec_seq))
    cache = (cache_k.reshape(dec_batch, DEPTH, past, KV_W), cache_v.reshape(dec_batch, DEPTH, past, KV_W))
    y_sample, _, _ = _mixer_pass(True, x_sample, params, filt[dec_seq], dft[dec_seq], rope, cache, state_lru)
    return (y_prompt, y_sample, new_k, new_v, new_lru)
```

```python
import functools
import math

import numpy as np
import jax
import jax.numpy as jnp
from jax import lax
from jax.experimental import pallas as pl
from jax.experimental.pallas import tpu as pltpu

F32 = jnp.float32
BF16 = jnp.bfloat16

D_MODEL = 1024
DEPTH = 2
GRID_W = 64
HEAD_DIM = 64
N_Q_HEADS = 8
N_KV_HEADS = 2
Q_PER_KV = N_Q_HEADS // N_KV_HEADS
ATTN_W = N_Q_HEADS * HEAD_DIM
KV_W = N_KV_HEADS * HEAD_DIM
ROPE_THETA = 10000.0
ATTN_SCALE = HEAD_DIM ** -0.5
HY_W = 512
HY_BANDS = 16
HY_EMB = 2 * HY_BANDS + 1
HY_FH = 64
LRU_W = 512
LRU_BLOCKS = 8
LRU_BS = LRU_W // LRU_BLOCKS
LRU_C = 8.0
EPS = 1e-6

ATTN_COLS = ATTN_W + 2 * KV_W + ATTN_W
D_MIX = ATTN_W + HY_W + LRU_W

LANES = 128
SUBLANES = 8
MXU_DIM = 256
COL_BLOCK = MXU_DIM
HY_BLOCK0 = ATTN_COLS // COL_BLOCK
LRU_BLOCK0 = (ATTN_COLS + 4 * HY_W) // COL_BLOCK
Q_CHUNK = 256
Q_CHUNK_LATENT = 256
HEAD_PAIRS = Q_PER_KV // 2
FEAT_PAD = 128
MOD_ROWS = 8
VMEM_LIMIT = 56 * 1024 * 1024


def _cparams(n_axes):
    return pltpu.CompilerParams(dimension_semantics=("arbitrary",) * n_axes, vmem_limit_bytes=VMEM_LIMIT)


def _split_bf16(a):
    hi = a.astype(BF16)
    lo = (a - hi.astype(F32)).astype(BF16)
    return hi, lo


def _dot(a, b):
    return jnp.dot(a, b, preferred_element_type=F32)


def _dot3(a, b):
    a_hi, a_lo = _split_bf16(a)
    b_hi, b_lo = _split_bf16(b)
    return _dot(a_hi, b_hi) + _dot(a_hi, b_lo) + _dot(a_lo, b_hi)


def _silu(x):
    half = 0.5 * x
    return half * jnp.tanh(half) + half


@functools.lru_cache(maxsize=None)
def _dft_tables(L):
    n = 2 * L
    t = np.arange(L)
    kt = (t[:, None] * t[None, :]) % n
    ang = 2.0 * np.pi * kt / n
    cosm, sinm = np.cos(ang), np.sin(ang)
    alt = (-1.0) ** t
    f_s = -sinm
    f_s[0, :] = alt
    fwd = np.concatenate([cosm, f_s], axis=0)
    g_c = 2.0 * cosm.T / n
    g_c[:, 0] = 1.0 / n
    g_s = -2.0 * sinm.T / n
    g_s[:, 0] = alt / n
    inv = np.concatenate([g_c, g_s], axis=1)
    return fwd.astype(np.float32), inv.astype(np.float32)


@functools.lru_cache(maxsize=None)
def _hyena_feats(L):
    t = np.linspace(0.0, 1.0, L)[:, None]
    w = 2.0 * math.pi * np.arange(L)[:, None] / L
    f = np.linspace(1e-4, HY_BANDS - 1, HY_BANDS)[None, :]
    out = np.zeros((L, FEAT_PAD), np.float32)
    out[:, :HY_EMB] = np.concatenate([t, np.cos(f * w), -np.sin(f * w)], axis=-1)
    return out


@functools.lru_cache(maxsize=None)
def _rope_tables(L):
    rows = L // GRID_W
    row = np.repeat(np.arange(rows), GRID_W).astype(np.float64)
    col = np.tile(np.arange(GRID_W), rows).astype(np.float64)
    n_freq = HEAD_DIM // 4
    inv = ROPE_THETA ** (-np.arange(n_freq) / n_freq)
    ang = np.concatenate([row[:, None] * inv, col[:, None] * inv], axis=-1)
    cos = np.repeat(np.cos(ang), 2, axis=-1)
    sin = np.repeat(np.sin(ang), 2, axis=-1) * np.tile(np.array([-1.0, 1.0]), HEAD_DIM // 2)
    reps = LANES // HEAD_DIM
    return np.tile(cos, (1, reps)).astype(np.float32), np.tile(sin, (1, reps)).astype(np.float32)


@functools.lru_cache(maxsize=None)
def _head_average_matrix():
    return np.kron(np.eye(MXU_DIM // HEAD_DIM), np.full((HEAD_DIM, HEAD_DIM), 1.0 / HEAD_DIM)).astype(np.float32)


def _mod_kernel(c_ref, w_ref, b_ref, o_ref):
    s = _silu(c_ref[...])
    o_ref[0] = _dot3(s, w_ref[0]) + b_ref[0]


def _modulation(cvecs, w_ada, b_ada):
    tn = D_MODEL
    return pl.pallas_call(
        _mod_kernel,
        grid=(DEPTH, 3 * D_MODEL // tn),
        in_specs=[
            pl.BlockSpec((MOD_ROWS, D_MODEL), lambda l, j: (0, 0)),
            pl.BlockSpec((1, D_MODEL, tn), lambda l, j: (l, 0, j)),
            pl.BlockSpec((1, 1, tn), lambda l, j: (l, 0, j)),
        ],
        out_specs=pl.BlockSpec((1, MOD_ROWS, tn), lambda l, j: (l, 0, j)),
        out_shape=jax.ShapeDtypeStruct((DEPTH, MOD_ROWS, 3 * D_MODEL), F32),
        compiler_params=_cparams(2),
        name="adaln_mod",
    )(cvecs, w_ada, b_ada.reshape(DEPTH, 1, 3 * D_MODEL))


def _filter_kernel(L, feats_ref, w1_ref, b1_ref, w2_ref, b2_ref, w3_ref, freq_ref, decay_ref,
                   fwd_ref, a_ref, b_ref, d_ref):
    feats = feats_ref[...]
    t = feats[:, 0:1]
    freq = freq_ref[0]
    hdn = jnp.sin(freq[0:1] * (_dot3(feats, w1_ref[0]) + b1_ref[0]))
    hdn = jnp.sin(freq[1:2] * (_dot3(hdn, w2_ref[0]) + b2_ref[0]))
    h = _dot3(hdn, w3_ref[0]) * jnp.exp(-t * decay_ref[0])
    row = lax.broadcasted_iota(jnp.int32, (L, HY_W), 0)
    h_fwd = h[:, :HY_W]
    h_bwd = jnp.where(row >= 1, h[:, HY_W:], 0.0)
    inv_norm = 1.0 / jnp.sum(jnp.abs(h_fwd) + jnp.abs(h_bwd), axis=0, keepdims=True)
    even = h_fwd + h_bwd
    odd = h_fwd - h_bwd
    h_re = _dot(fwd_ref[0:L, :], even.astype(BF16)) * inv_norm
    h_im = _dot(fwd_ref[L:2 * L, :], odd.astype(BF16)) * inv_norm
    alt = jnp.where(row % 2 == 0, 1.0, -1.0)
    nyq = jnp.sum(even * alt, axis=0, keepdims=True) * inv_norm
    a_ref[0] = h_re
    b_ref[0] = jnp.where(row >= 1, h_im, 0.0)
    d_ref[0] = jnp.where(row >= 1, h_re, nyq)


def _filter_spectrum(L, feats, fwd, w1p, b1, w2, b2, w3, freq, decay):
    full = lambda shape: pl.BlockSpec(shape, lambda l: (0,) * len(shape))
    per_layer = lambda shape: pl.BlockSpec((1,) + shape, lambda l: (l,) + (0,) * len(shape))
    out = jax.ShapeDtypeStruct((DEPTH, L, HY_W), F32)
    return pl.pallas_call(
        functools.partial(_filter_kernel, L),
        grid=(DEPTH,),
        in_specs=[
            full((L, FEAT_PAD)),
            per_layer((FEAT_PAD, HY_FH)), per_layer((1, HY_FH)),
            per_layer((HY_FH, HY_FH)), per_layer((1, HY_FH)),
            per_layer((HY_FH, 2 * HY_W)), per_layer((2, HY_FH)), per_layer((1, 2 * HY_W)),
            full((2 * L, L)),
        ],
        out_specs=[per_layer((L, HY_W))] * 3,
        out_shape=[out, out, out],
        compiler_params=_cparams(1),
        name=f"hyena_filter_L{L}",
    )(feats, w1p, b1, w2, b2, w3, freq, decay, fwd)


def _rope(x, cos, sin):
    lane = lax.broadcasted_iota(jnp.int32, x.shape, 1)
    partner = jnp.where(lane % 2 == 0, pltpu.roll(x, LANES - 1, 1), pltpu.roll(x, 1, 1))
    return x * cos + partner * sin


def _head_mean_square(x, e):
    sq = (x * x).astype(BF16)
    width = x.shape[1]
    if width <= MXU_DIM:
        return _dot(sq, e[:width, :width])
    parts = [_dot(sq[:, i:i + MXU_DIM], e) for i in range(0, width, MXU_DIM)]
    return jnp.concatenate(parts, axis=1)


def _store_kv_heads(k_dst, v_dst, k, v):
    ones = jnp.ones((v.shape[0], HEAD_DIM), BF16)
    for j in range(N_KV_HEADS):
        js = slice(j * HEAD_DIM, (j + 1) * HEAD_DIM)
        vj = v[:, js].astype(BF16)
        k_dst[j] = k[:, js].astype(BF16)
        v_dst[j, 0, :, 0:HEAD_DIM] = vj
        v_dst[j, 0, :, HEAD_DIM:2 * HEAD_DIM] = ones
        v_dst[j, 1, :, 0:HEAD_DIM] = ones
        v_dst[j, 1, :, HEAD_DIM:2 * HEAD_DIM] = vj


def _attn_kernel(latent, aliased, layer, L, nb, qc, past, *refs):
    x_ref, mod_ref, ng_ref, w_ref, gq_ref, gk_ref, e_ref = refs[:7]
    refs = refs[7:]
    if latent:
        (cos_ref, sin_ref, ck_ref, cv_ref, h_out, mix_out,
         q_s, k_s, v_s, g_s, s_s, mx_s, acc_s, ck_s, cv_s) = refs
    else:
        if aliased:
            refs = refs[2:]
        h_out, mix_out, k_out, v_out, q_s, k_s, v_s, g_s, s_s, mx_s, acc_s = refs
    rows = nb * L

    x = x_ref[...].reshape(rows, D_MODEL)
    m = mod_ref[0, 0]
    shift, scale = m[0:1], m[1:2]
    ms = jnp.mean(x * x, axis=-1, keepdims=True)
    h = (x * lax.rsqrt(ms + EPS)) * (ng_ref[0] * (1.0 + scale)) + shift
    hb = h.astype(BF16)
    h_out[...] = hb.reshape(nb, L, D_MODEL)

    u = _dot(hb, w_ref[0])
    q = u[:, :ATTN_W]
    k = u[:, ATTN_W:ATTN_W + KV_W]
    v = u[:, ATTN_W + KV_W:ATTN_W + 2 * KV_W]
    g_s[...] = u[:, ATTN_W + 2 * KV_W:]
    e = e_ref[...]
    qn = q * lax.rsqrt(_head_mean_square(q, e) + EPS) * gq_ref[0]
    kn = k * lax.rsqrt(_head_mean_square(k, e) + EPS) * gk_ref[0]
    if latent:
        cos, sin = cos_ref[...], sin_ref[...]
        qn = jnp.concatenate([_rope(qn[:, i:i + LANES], cos, sin) for i in range(0, ATTN_W, LANES)], axis=1)
        kn = _rope(kn, cos, sin)
        _store_kv_heads(ck_s, cv_s, ck_ref[0, 0], cv_ref[0, 0])
    else:
        for b in range(nb):
            kb, vb = kn[b * L:(b + 1) * L], v[b * L:(b + 1) * L]
            if aliased:
                k_out[b, 0] = kb
                v_out[b, 0] = vb
            else:
                for slot in range(DEPTH):
                    k_out[b, slot] = kb if slot == layer else jnp.zeros_like(kb)
                    v_out[b, slot] = vb if slot == layer else jnp.zeros_like(vb)
    qb = (qn * ATTN_SCALE).astype(BF16)
    for hd in range(N_Q_HEADS):
        j, g = divmod(hd, Q_PER_KV)
        q_s[j * Q_PER_KV + (g % 2) * HEAD_PAIRS + g // 2] = qb[:, hd * HEAD_DIM:(hd + 1) * HEAD_DIM]
    _store_kv_heads(k_s, v_s, kn, v)

    contract_last = (((1,), (1,)), ((), ()))
    chunks_per_seq = L // qc
    lane = lax.broadcasted_iota(jnp.int32, (qc, LANES), 1)
    groups = [(j, e) for j in range(N_KV_HEADS) for e in range(2)]

    def chunk(c, carry):
        r0 = pl.multiple_of(c * qc, qc)
        k0 = pl.multiple_of((c // chunks_per_seq) * L, L)
        for idx, (j, e) in enumerate(groups):
            slot = j * Q_PER_KV + e * HEAD_PAIRS
            qh = q_s[slot:slot + HEAD_PAIRS, pl.ds(r0, qc), :].reshape(HEAD_PAIRS * qc, HEAD_DIM)
            s = lax.dot_general(qh, k_s[j, pl.ds(k0, L), :], contract_last, preferred_element_type=F32)
            mx = jnp.max(s, axis=-1, keepdims=True)
            if latent:
                s0 = lax.dot_general(qh, ck_s[j], contract_last, preferred_element_type=F32)
                mx = jnp.maximum(mx, jnp.max(s0, axis=-1, keepdims=True))
                s_s[idx, :, 0:past] = s0
                s_s[idx, :, past:] = s
            else:
                s_s[idx] = s
            mx_s[idx] = jnp.broadcast_to(mx, (HEAD_PAIRS * qc, LANES))
        for idx, (j, e) in enumerate(groups):
            p = jnp.exp(s_s[idx] - jnp.tile(mx_s[idx], (1, (past + L) // LANES))).astype(BF16)
            if latent:
                acc = _dot(p[:, :past], cv_s[j, e]) + _dot(p[:, past:], v_s[j, e, pl.ds(k0, L), :])
            else:
                acc = _dot(p, v_s[j, e, pl.ds(k0, L), :])
            acc_s[idx] = acc
        b_idx = c // chunks_per_seq
        row0 = pl.multiple_of((c % chunks_per_seq) * qc, qc)
        for j in range(N_KV_HEADS):
            for t in range(HEAD_PAIRS):
                a_even = acc_s[2 * j, t * qc:(t + 1) * qc, :]
                a_odd = acc_s[2 * j + 1, t * qc:(t + 1) * qc, :]
                out = jnp.where(lane < HEAD_DIM, a_even, a_odd)
                den = pltpu.roll(jnp.where(lane < HEAD_DIM, a_odd, a_even), HEAD_DIM, 1)
                blk = slice((j * HEAD_PAIRS + t) * LANES, (j * HEAD_PAIRS + t + 1) * LANES)
                gate = _silu(g_s[pl.ds(r0, qc), blk])
                mix_out[b_idx, pl.ds(row0, qc), blk] = (out / den * gate).astype(BF16)
        return carry

    lax.fori_loop(0, nb * chunks_per_seq, chunk, 0, unroll=4)


def _attention_branch(latent, layer, nb, x, mod, norm_g, w_in_b, gq, gk, e_avg, rope=None, cache=None, kv_prev=None):
    B, L, _ = x.shape
    rows = nb * L
    assert nb == 1 or not latent
    full = lambda shape: pl.BlockSpec(shape, lambda b: (0,) * len(shape))
    per_b = lambda shape: pl.BlockSpec((nb,) + shape, lambda b: (b,) + (0,) * len(shape))
    per_layer = lambda shape: pl.BlockSpec((1,) + shape, lambda b: (layer,) + (0,) * len(shape))
    mod_spec = pl.BlockSpec((1, 1, 3, D_MODEL), (lambda b: (layer, 1 + b, 0, 0)) if latent else (lambda b: (layer, 0, 0, 0)))
    in_specs = [per_b((L, D_MODEL)), mod_spec, per_layer((1, D_MODEL)), per_layer((D_MODEL, ATTN_COLS)),
                per_layer((1, ATTN_W)), per_layer((1, KV_W)), full((MXU_DIM, MXU_DIM))]
    args = [x, mod, norm_g, w_in_b, gq, gk, e_avg]
    out_specs = [per_b((L, D_MODEL)), per_b((L, ATTN_W))]
    out_shape = [jax.ShapeDtypeStruct((B, L, D_MODEL), BF16), jax.ShapeDtypeStruct((B, L, ATTN_W), BF16)]
    qc = Q_CHUNK_LATENT if latent else Q_CHUNK
    past = cache[0].shape[2] if latent else 0
    assert B % nb == 0 and L % qc == 0 and L % LANES == 0 and past % LANES == 0
    n_groups = 2 * N_KV_HEADS
    scratch = [pltpu.VMEM((N_Q_HEADS, rows, HEAD_DIM), BF16), pltpu.VMEM((N_KV_HEADS, rows, HEAD_DIM), BF16),
               pltpu.VMEM((N_KV_HEADS, 2, rows, 2 * HEAD_DIM), BF16),
               pltpu.VMEM((rows, ATTN_W), F32),
               pltpu.VMEM((n_groups, HEAD_PAIRS * qc, past + L), F32),
               pltpu.VMEM((n_groups, HEAD_PAIRS * qc, LANES), F32),
               pltpu.VMEM((n_groups, HEAD_PAIRS * qc, 2 * HEAD_DIM), F32)]
    aliases = {}
    if latent:
        cache_k, cache_v = cache
        cache_spec = pl.BlockSpec((1, 1, past, KV_W), lambda b: (b, layer, 0, 0))
        in_specs += [full((L, LANES)), full((L, LANES)), cache_spec, cache_spec]
        args += [rope[0], rope[1], cache_k, cache_v]
        scratch += [pltpu.VMEM((N_KV_HEADS, past, HEAD_DIM), BF16),
                    pltpu.VMEM((N_KV_HEADS, 2, past, 2 * HEAD_DIM), BF16)]
    else:
        if kv_prev is not None:
            kv_spec = pl.BlockSpec((nb, 1, L, KV_W), lambda b: (b, layer, 0, 0))
        else:
            kv_spec = pl.BlockSpec((nb, DEPTH, L, KV_W), lambda b: (b, 0, 0, 0))
        kv_shape = jax.ShapeDtypeStruct((B, DEPTH, L, KV_W), F32)
        out_specs += [kv_spec, kv_spec]
        out_shape += [kv_shape, kv_shape]
        if kv_prev is not None:
            in_specs += [pl.BlockSpec(memory_space=pl.ANY)] * 2
            aliases = {len(args): 2, len(args) + 1: 3}
            args += list(kv_prev)
    return pl.pallas_call(
        functools.partial(_attn_kernel, latent, kv_prev is not None, layer, L, nb, qc, past),
        grid=(B // nb,),
        in_specs=in_specs,
        out_specs=out_specs,
        out_shape=out_shape,
        scratch_shapes=scratch,
        input_output_aliases=aliases,
        compiler_params=_cparams(1),
        name=f"attn_branch_L{L}",
    )(*args)


def _shifted(x, offset, period):
    rows = x.shape[0]
    t = lax.broadcasted_iota(jnp.int32, x.shape, 0)
    if rows != period:
        t = t % period
    rolled = pltpu.roll(x, (-offset) % rows, 0)
    valid = (t >= -offset) if offset < 0 else (t < period - offset)
    return jnp.where(valid, rolled, 0.0)


def _hyena_kernel(L, nb, nw, h_ref, *refs):
    w_refs = refs[:4 * nw]
    (sw0_ref, sw1_ref, sw2_ref, sb0_ref, sb1_ref, sb2_ref,
     fa_ref, fb_ref, fd_ref, bias_ref, fwd_ref, inv_ref, mix_out) = refs[4 * nw:]
    hb = h_ref[...].reshape(nb * L, D_MODEL)

    def project(part):
        cols = [_dot(hb, w_refs[part * nw + i][0]) for i in range(nw)]
        return cols[0] if nw == 1 else jnp.concatenate(cols, axis=1)

    def short_conv(part, sw_ref, sb_ref):
        xs = project(part)
        w = sw_ref[0]
        return sb_ref[0] + _shifted(xs, -1, L) * w[0:1] + xs * w[1:2] + _shifted(xs, 1, L) * w[2:3]

    x0 = short_conv(0, sw0_ref, sb0_ref)
    x1 = short_conv(1, sw1_ref, sb1_ref)
    hv = short_conv(2, sw2_ref, sb2_ref)
    gated = x0 * _silu(project(3))
    z = x1 * hv
    zb = z.astype(BF16)
    fa, fb, fd = fa_ref[0], fb_ref[0], fd_ref[0]
    cb = z.shape[1]
    z_cat = jnp.concatenate([zb[b * L:(b + 1) * L] for b in range(nb)], axis=1)
    zf = _dot(fwd_ref[...], z_cat)
    y_re, y_im = [], []
    for b in range(nb):
        re, im = zf[:L, b * cb:(b + 1) * cb], zf[L:, b * cb:(b + 1) * cb]
        y_re.append((re * fa - im * fb).astype(BF16))
        y_im.append((re * fb + im * fd).astype(BF16))
    y_all = (_dot(inv_ref[:, :L], jnp.concatenate(y_re, axis=1))
             + _dot(inv_ref[:, L:], jnp.concatenate(y_im, axis=1)))
    for b in range(nb):
        rs = slice(b * L, (b + 1) * L)
        y = y_all[:, b * cb:(b + 1) * cb] + z[rs] * bias_ref[0]
        mix_out[b] = (y * gated[rs]).astype(BF16)


def _hyena_branch(layer, nb, nw, hmod, w_in_b, short_w, short_b, filt, bias, fwd, inv):
    B, L, _ = hmod.shape
    cb = COL_BLOCK
    width = nw * cb
    nt = HY_W // width
    blocks_per_part = HY_W // cb
    fa, fb, fd = filt
    assert B % nb == 0 and L % LANES == 0 and HY_W % width == 0
    w_specs = [pl.BlockSpec((1, D_MODEL, cb),
                            functools.partial(lambda i, b, off: (layer, 0, off + i * nw),
                                              off=HY_BLOCK0 + part * blocks_per_part + j))
               for part in range(4) for j in range(nw)]
    sw_spec = lambda part: pl.BlockSpec((1, 3, width), lambda i, b: (layer, 0, part * nt + i))
    sb_spec = lambda part: pl.BlockSpec((1, 1, width), lambda i, b: (layer, 0, part * nt + i))
    filt_spec = pl.BlockSpec((1, L, width), lambda i, b: (layer, 0, i))
    return pl.pallas_call(
        functools.partial(_hyena_kernel, L, nb, nw),
        grid=(nt, B // nb),
        in_specs=[pl.BlockSpec((nb, L, D_MODEL), lambda i, b: (b, 0, 0))] + w_specs + [
            sw_spec(0), sw_spec(1), sw_spec(2),
            sb_spec(0), sb_spec(1), sb_spec(2),
            filt_spec, filt_spec, filt_spec,
            pl.BlockSpec((1, 1, width), lambda i, b: (layer, 0, i)),
            pl.BlockSpec((2 * L, L), lambda i, b: (0, 0)),
            pl.BlockSpec((L, 2 * L), lambda i, b: (0, 0)),
        ],
        out_specs=pl.BlockSpec((nb, L, width), lambda i, b: (b, 0, i)),
        out_shape=jax.ShapeDtypeStruct((B, L, HY_W), BF16),
        compiler_params=_cparams(2),
        name=f"hyena_branch_L{L}",
    )(hmod, *([w_in_b] * (4 * nw)), short_w, short_w, short_w, short_b, short_b, short_b,
      fa, fb, fd, bias, fwd, inv)


def _lru_kernel(latent, aliased, final, layer, L, nb, *refs):
    (h_ref, wx0_ref, wx1_ref, wg0_ref, wg1_ref, cw_ref, cb_ref, wa_ref, ba_ref, wxg_ref, bx_ref, lam_ref,
     x_ref, ma_ref, mh_ref, wo_ref, mod_ref, fg_ref) = refs[:18]
    refs = refs[18:]
    if latent:
        h0_ref, x_out, gate_w_s, a_s, b_s, y_s, g_s, part_s, x_s, x_sem = refs
    else:
        if aliased:
            refs = refs[1:]
        x_out, hl_out, gate_w_s, a_s, b_s, y_s, g_s, part_s = refs
    half_w = LRU_W // 2
    blocks_per_half = half_w // LRU_BS
    n_tiles = L // SUBLANES

    if latent:
        x_copy = pltpu.make_async_copy(x_ref.at[pl.ds(pl.program_id(0) * nb, nb)], x_s, x_sem)
        x_copy.start()

    @pl.when(pl.program_id(0) == 0)
    def _build_gate_weights():
        gate_w_s[...] = jnp.zeros(gate_w_s.shape, BF16)
        for half in range(2):
            for kind, (w_ref, d) in enumerate(((wa_ref, 0), (wxg_ref, 0), (wa_ref, 1), (wxg_ref, 1))):
                for j in range(blocks_per_half):
                    blk = (0.5 * w_ref[0, d, half * blocks_per_half + j]).astype(BF16)
                    gate_w_s[half, j * LRU_BS:(j + 1) * LRU_BS,
                             kind * half_w + j * LRU_BS:kind * half_w + (j + 1) * LRU_BS] = blk

    part_s[...] = (_dot(ma_ref[...].reshape(nb * L, ATTN_W), wo_ref[0, 0:ATTN_W, :])
                   + _dot(mh_ref[...].reshape(nb * L, HY_W), wo_ref[0, ATTN_W:ATTN_W + HY_W, :]))

    hb = h_ref[...].reshape(nb * L, D_MODEL)
    xs = jnp.concatenate([_dot(hb, wx0_ref[0]), _dot(hb, wx1_ref[0])], axis=1)
    g_s[:, :half_w] = _dot(hb, wg0_ref[0])
    g_s[:, half_w:] = _dot(hb, wg1_ref[0])
    cw = cw_ref[0]
    xc = (cb_ref[0] + _shifted(xs, -1, L) * cw[0:1] + xs * cw[1:2] + _shifted(xs, 1, L) * cw[2:3]
          + _shifted(xs, 2, L) * cw[3:4])
    xcb = xc.astype(BF16)
    c_half = (-0.5 * LRU_C) * jax.nn.softplus(-lam_ref[0])
    ba, bx = 0.5 * ba_ref[0], 0.5 * bx_ref[0]
    for half in range(2):
        cols = slice(half * half_w, (half + 1) * half_w)
        bias = jnp.concatenate([ba[0:1, cols], bx[0:1, cols], ba[1:2, cols], bx[1:2, cols]], axis=1)
        gates = _dot(xcb[:, cols], gate_w_s[half]) + bias
        xh_half = 0.5 * xc[:, cols]
        for d in range(2):
            t_r = jnp.tanh(gates[:, (2 * d) * half_w:(2 * d + 1) * half_w])
            t_i = jnp.tanh(gates[:, (2 * d + 1) * half_w:(2 * d + 2) * half_w])
            ch = c_half[d:d + 1, cols]
            log_a = ch * t_r + ch
            a = jnp.exp(log_a)
            y = jnp.tanh(log_a) * (-1.0 - a * a)
            root = jnp.where(y > 0.0, y * lax.rsqrt(y), 0.0)
            bcoef = root * (xh_half * t_i + xh_half)
            a_s[d, :, :, cols] = a.reshape(nb * n_tiles, SUBLANES, half_w)
            b_s[d, :, :, cols] = bcoef.reshape(nb * n_tiles, SUBLANES, half_w)

    if latent:
        init = []
        for b in range(nb):
            h0 = h0_ref[b, 0]
            init += [h0[0:1], h0[1:2]]
    else:
        init = [jnp.zeros((1, LRU_W), F32)] * (2 * nb)

    def advance(d, tile, rows_g, h):
        a = [a_s[d, tile, r:r + 1, :] for r in rows_g]
        c = [b_s[d, tile, r:r + 1, :] for r in rows_g]
        a01, c01 = a[1] * a[0], a[1] * c[0] + c[1]
        hs = [a[0] * h + c[0], a01 * h + c01]
        if len(rows_g) == 4:
            a23, c23 = a[3] * a[2], a[3] * c[2] + c[3]
            hs += [a[2] * hs[1] + c[2], (a23 * a01) * h + (a23 * c01 + c23)]
        for r, v in zip(rows_g, hs):
            y_s[d, tile, r:r + 1, :] = v
        return hs[-1]

    group = 4 if nb == 1 else 2

    def tile_step(i, carry):
        carry = list(carry)
        for g0 in range(0, SUBLANES, group):
            fwd_rows = list(range(g0, g0 + group))
            bwd_rows = [SUBLANES - 1 - r for r in fwd_rows]
            for b in range(nb):
                carry[2 * b] = advance(0, b * n_tiles + i, fwd_rows, carry[2 * b])
                carry[2 * b + 1] = advance(1, b * n_tiles + (n_tiles - 1 - i), bwd_rows, carry[2 * b + 1])
        return tuple(carry)

    def two_tiles(i, carry):
        return tile_step(2 * i + 1, tile_step(2 * i, carry))

    last = lax.fori_loop(0, n_tiles // 2, two_tiles, tuple(init))
    if not latent:
        if not aliased:
            hl_out[...] = jnp.zeros(hl_out.shape, F32)
        slot = 0 if aliased else layer
        for b in range(nb):
            hl_out[b, slot, 0:1, :] = last[2 * b]
            hl_out[b, slot, 1:2, :] = last[2 * b + 1]
    y = (y_s[0] + y_s[1]).reshape(nb * L, LRU_W)
    mix_l = (y * _silu(g_s[...])).astype(BF16)
    out = part_s[...] + _dot(mix_l, wo_ref[0, ATTN_W + HY_W:, :])
    if latent:
        x_copy.wait()
        x = x_s[...]
    else:
        x = x_ref[...]
    xn = x.reshape(nb * L, D_MODEL) + mod_ref[0, 0][2:3] * out
    if final:
        ms = jnp.mean(xn * xn, axis=-1, keepdims=True)
        xn = xn * lax.rsqrt(ms + EPS) * fg_ref[...]
    x_out[...] = xn.reshape(nb, L, D_MODEL)


def _lru_out_branch(latent, layer, nb, hmod, w_in_b, conv_w, conv_b, wa, ba, wx, bx, lam, x, mix_a, mix_h, w_out_b,
                    mod, final_g, state=None, hl_prev=None):
    B, L, _ = hmod.shape
    assert B % nb == 0 and L % (2 * SUBLANES) == 0
    final = layer == DEPTH - 1
    cb = COL_BLOCK
    half_w = LRU_W // 2
    rows = nb * L
    per_layer = lambda shape: pl.BlockSpec((1,) + shape, lambda b: (layer,) + (0,) * len(shape))
    w_spec = lambda j: pl.BlockSpec((1, D_MODEL, cb), lambda b: (layer, 0, LRU_BLOCK0 + j))
    gate_blocks = (2, LRU_BLOCKS, LRU_BS, LRU_BS)
    in_specs = [pl.BlockSpec((nb, L, D_MODEL), lambda b: (b, 0, 0)), w_spec(0), w_spec(1), w_spec(2), w_spec(3),
                per_layer((4, LRU_W)), per_layer((1, LRU_W)), per_layer(gate_blocks), per_layer((2, LRU_W)),
                per_layer(gate_blocks), per_layer((2, LRU_W)), per_layer((2, LRU_W))]
    per_b = lambda w: pl.BlockSpec((nb, L, w), lambda b: (b, 0, 0))
    mod_row = (lambda b: (layer, 1 + b, 0, 0)) if latent else (lambda b: (layer, 0, 0, 0))
    x_spec = pl.BlockSpec(memory_space=pl.ANY) if latent else per_b(D_MODEL)
    in_specs += [x_spec, per_b(ATTN_W), per_b(HY_W), per_layer((D_MIX, D_MODEL)),
                 pl.BlockSpec((1, 1, 3, D_MODEL), mod_row), pl.BlockSpec((1, D_MODEL), lambda b: (0, 0))]
    args = [hmod, w_in_b, w_in_b, w_in_b, w_in_b, conv_w, conv_b, wa, ba, wx, bx, lam,
            x, mix_a, mix_h, w_out_b, mod, final_g]
    out_specs = [per_b(D_MODEL)]
    out_shape = [jax.ShapeDtypeStruct((B, L, D_MODEL), F32)]
    aliases = {}
    if latent:
        in_specs.append(pl.BlockSpec((nb, 1, 2, LRU_W), lambda b: (b, layer, 0, 0)))
        args.append(state)
    else:
        if hl_prev is not None:
            out_specs.append(pl.BlockSpec((nb, 1, 2, LRU_W), lambda b: (b, layer, 0, 0)))
        else:
            out_specs.append(pl.BlockSpec((nb, DEPTH, 2, LRU_W), lambda b: (b, 0, 0, 0)))
        out_shape.append(jax.ShapeDtypeStruct((B, DEPTH, 2, LRU_W), F32))
        if hl_prev is not None:
            in_specs.append(pl.BlockSpec(memory_space=pl.ANY))
            aliases = {len(args): 1}
            args.append(hl_prev)
    scan_buf = pltpu.VMEM((2, rows // SUBLANES, SUBLANES, LRU_W), F32)
    return pl.pallas_call(
        functools.partial(_lru_kernel, latent, hl_prev is not None, final, layer, L, nb),
        grid=(B // nb,),
        in_specs=in_specs,
        out_specs=out_specs,
        out_shape=out_shape,
        scratch_shapes=[pltpu.VMEM((2, half_w, 4 * half_w), BF16), scan_buf, scan_buf, scan_buf,
                        pltpu.VMEM((rows, LRU_W), F32), pltpu.VMEM((rows, D_MODEL), F32)]
        + ([pltpu.VMEM((nb, L, D_MODEL), F32), pltpu.SemaphoreType.DMA] if latent else []),
        input_output_aliases=aliases,
        compiler_params=_cparams(1),
        name=f"lru_out_L{L}",
    )(*args)


def _mixer_pass(latent, x, p, filt, dft, rope=None, cache=None, state=None):
    B, L, _ = x.shape
    fwd, inv = dft
    nb = 1 if latent else 4
    nb_hyena = 1 if latent else 4
    nw_hyena = HY_W // COL_BLOCK
    nb_lru = 1 if latent else 2
    kv, hl = None, None
    for l in range(DEPTH):
        outs = _attention_branch(latent, l, nb, x, p['mod'], p['norm_g'], p['w_in'], p['gq'], p['gk'], p['e_avg'],
                                 rope, cache, kv)
        hmod, mix_a = outs[0], outs[1]
        if not latent:
            kv = (outs[2], outs[3])
        mix_h = _hyena_branch(l, nb_hyena, nw_hyena, hmod, p['w_in'], p['short_w'], p['short_b'], filt, p['hy_bias'],
                              fwd, inv)
        outs = _lru_out_branch(latent, l, nb_lru, hmod, p['w_in'], p['conv_w'], p['conv_b'], p['wa'], p['ba'], p['wx'],
                               p['bx'], p['lam'], x, mix_a, mix_h, p['w_out'], p['mod'], p['final_g'], state, hl)
        x = outs[0]
        if not latent:
            hl = outs[1]
    return x, kv, hl


def kernel(x_prompt, x_sample, cache_k, cache_v, state_lru, c, c_ctx, norm_g, w_ada, b_ada, w_in, q_norm_g, k_norm_g,
           hy_short_w, hy_short_b, hy_filt_w1, hy_filt_b1, hy_filt_w2, hy_filt_b2, hy_filt_w3, hy_filt_freq,
           hy_filt_decay, hy_bias, lru_conv_w, lru_conv_b, lru_wa, lru_ba, lru_wx, lru_bx, lru_lambda, w_out, final_g):
    batch, seq, _ = x_prompt.shape
    dec_batch, dec_seq, _ = x_sample.shape
    past = cache_k.shape[2]

    cvecs = jnp.concatenate([c_ctx[None, :], c, jnp.zeros((MOD_ROWS - 1 - dec_batch, D_MODEL), F32)], axis=0)
    mod = _modulation(cvecs, w_ada, b_ada).reshape(DEPTH, MOD_ROWS, 3, D_MODEL)

    w1p = jnp.pad(hy_filt_w1, ((0, 0), (0, FEAT_PAD - HY_EMB), (0, 0)))
    dft, filt = {}, {}
    for L in (seq, dec_seq):
        dft[L] = tuple(jnp.asarray(m).astype(BF16) for m in _dft_tables(L))
        filt[L] = _filter_spectrum(L, jnp.asarray(_hyena_feats(L)), dft[L][0], w1p, hy_filt_b1[:, None, :],
                                   hy_filt_w2, hy_filt_b2[:, None, :], hy_filt_w3, hy_filt_freq,
                                   hy_filt_decay[:, None, :])

    params = {
        'mod': mod,
        'norm_g': norm_g[:, None, :],
        'w_in': w_in.astype(BF16),
        'gq': jnp.tile(q_norm_g, (1, N_Q_HEADS))[:, None, :],
        'gk': jnp.tile(k_norm_g, (1, N_KV_HEADS))[:, None, :],
        'e_avg': jnp.asarray(_head_average_matrix()).astype(BF16),
        'short_w': hy_short_w, 'short_b': hy_short_b[:, None, :], 'hy_bias': hy_bias[:, None, :],
        'conv_w': lru_conv_w, 'conv_b': lru_conv_b[:, None, :],
        'wa': lru_wa, 'ba': lru_ba, 'wx': lru_wx, 'bx': lru_bx, 'lam': lru_lambda,
        'w_out': w_out.astype(BF16),
        'final_g': final_g[None, :],
    }

    y_prompt, kv, new_lru = _mixer_pass(False, x_prompt, params, filt[seq], dft[seq])
    new_k = kv[0].reshape(batch, DEPTH, seq, N_KV_HEADS, HEAD_DIM)
    new_v = kv[1].reshape(batch, DEPTH, seq, N_KV_HEADS, HEAD_DIM)

    rope = tuple(jnp.asarray(t) for t in _rope_tables(dec_seq))
    cache = (cache_k.reshape(dec_batch, DEPTH, past, KV_W), cache_v.reshape(dec_batch, DEPTH, past, KV_W))
    y_sample, _, _ = _mixer_pass(True, x_sample, params, filt[dec_seq], dft[dec_seq], rope, cache, state_lru)
    return (y_prompt, y_sample, new_k, new_v, new_lru)
```
